```python
import math
import jax, jax.numpy as jnp
from jax import lax
import numpy as np

D_MODEL = 1024
BATCH = 8
SEQ = 2048
DEPTH = 1
DEC_BATCH = 32
DEC_SEQ = 4
PAST_LEN = 16384
PAGE_SIZE = 128

HEAD_DIM = 64
N_HEADS_A = 8
N_KV_A = 2
N_IDX_HEADS = 8
D_IDX = 64
TOPK_MAX = 256
N_HEADS_B = 8
D_A = N_HEADS_A * HEAD_DIM
D_B = N_HEADS_B * HEAD_DIM
D_MIX = D_A + D_B
D_PLE = 256
N_BUCKETS = 32
MAX_DISTANCE = 128
ROPE_BASE = 10000.0
RET_CHUNK = 128
Q_BLOCK = 128
EPS = 1e-6
SPLIT_SIZES = (D_A, N_KV_A * HEAD_DIM, N_KV_A * HEAD_DIM, N_IDX_HEADS * D_IDX, D_IDX, N_IDX_HEADS, D_A, D_B, D_B, D_B, D_B)
D_IN = D_A + 2 * N_KV_A * HEAD_DIM + N_IDX_HEADS * D_IDX + D_IDX + N_IDX_HEADS + D_A + 4 * D_B

kernel_name = 'hymba_dsa_retention_step'


def _rmsnorm(x, g):
    xf = x.astype(jnp.float32)
    y = xf * lax.rsqrt(jnp.mean(xf * xf, axis=-1, keepdims=True) + EPS) * g.astype(jnp.float32)
    return y.astype(x.dtype)


def _head_norm(o):
    mu = jnp.mean(o, axis=-1, keepdims=True)
    var = jnp.mean(jnp.square(o - mu), axis=-1, keepdims=True)
    return (o - mu) * lax.rsqrt(var + EPS)


def _rope(x, pos):
    half = x.shape[-1] // 2
    inv = ROPE_BASE ** (-jnp.arange(half, dtype=jnp.float32) / half)
    ang = pos.astype(jnp.float32)[:, None] * inv[None, :]
    cos = jnp.cos(ang)[:, None, :]
    sin = jnp.sin(ang)[:, None, :]
    xf = x.astype(jnp.float32)
    x1, x2 = xf[..., :half], xf[..., half:]
    return jnp.concatenate([x1 * cos - x2 * sin, x1 * sin + x2 * cos], axis=-1).astype(x.dtype)


def _t5_bucket(n):
    max_exact = N_BUCKETS // 2
    n = jnp.maximum(n, 0)
    nf = jnp.maximum(n, 1).astype(jnp.float32)
    large = max_exact + (jnp.log(nf / max_exact) / math.log(MAX_DISTANCE / max_exact)
                         * (N_BUCKETS - max_exact)).astype(jnp.int32)
    large = jnp.minimum(large, N_BUCKETS - 1)
    return jnp.where(n < max_exact, n, large)


def _project(h, w_in, pos):
    B, T, _ = h.shape
    z = jnp.einsum('btd,de->bte', h, w_in)
    offsets = np.cumsum(SPLIT_SIZES)[:-1].tolist()
    qa, ka, va, qi, ki, wi, ga, qb, kb, vb, gb = jnp.split(z, offsets, axis=-1)
    qa = qa.reshape(B, T, N_HEADS_A, HEAD_DIM)
    ka = ka.reshape(B, T, N_KV_A, HEAD_DIM)
    va = va.reshape(B, T, N_KV_A, HEAD_DIM)
    qi = qi.reshape(B, T, N_IDX_HEADS, D_IDX)
    wi = wi * (N_IDX_HEADS ** -0.5)
    qb = _rope(qb.reshape(B, T, N_HEADS_B, HEAD_DIM), pos)
    kb = _rope(kb.reshape(B, T, N_HEADS_B, HEAD_DIM), pos) * (HEAD_DIM ** -0.5)
    vb = vb.reshape(B, T, N_HEADS_B, HEAD_DIM)
    return qa, ka, va, qi, ki, wi, ga, qb, kb, vb, gb


def _index_select(qi, wi, ki, qpos, kpos, topk):
    s = jnp.einsum('bqhd,bsd->bqhs', qi.astype(jnp.float32), ki.astype(jnp.float32)) * (D_IDX ** -0.5)
    score = jnp.einsum('bqh,bqhs->bqs', wi.astype(jnp.float32), jax.nn.relu(s))
    valid = kpos[None, None, :] <= qpos[None, :, None]
    score = jnp.where(valid, score, -jnp.inf)
    top_val, top_idx = lax.top_k(score, topk)
    return top_idx, jnp.isfinite(top_val)


def _sparse_attend(q, kg, vg, qpos, sel_pos, sel_ok, rel_bias):
    B, Q = q.shape[:2]
    K = sel_pos.shape[-1]
    G = N_HEADS_A // N_KV_A
    qg = q.reshape(B, Q, N_KV_A, G, HEAD_DIM).astype(jnp.float32)
    logits = jnp.einsum('bqngd,bqknd->bqngk', qg, kg.astype(jnp.float32)) * (HEAD_DIM ** -0.5)
    bucket = _t5_bucket(qpos[None, :, None] - sel_pos)
    bias = rel_bias.astype(jnp.float32)[bucket]
    bias = bias.reshape(B, Q, K, N_KV_A, G).transpose(0, 1, 3, 4, 2)
    logits = jnp.where(sel_ok[:, :, None, None, :], logits + bias, -jnp.inf)
    probs = jax.nn.softmax(logits, axis=-1)
    o = jnp.einsum('bqngk,bqknd->bqngd', probs, vg.astype(jnp.float32))
    return o.reshape(B, Q, N_HEADS_A, HEAD_DIM)


def _gather_rows(a, idx):
    return jax.vmap(lambda ab, ib: ab[ib])(a, idx)


def _dsa_prompt(qa, ka, va, qi, ki, wi, rel_bias):
    B, T = qa.shape[:2]
    topk = min(TOPK_MAX, T // 4)
    blk = math.gcd(T, Q_BLOCK)
    kpos = jnp.arange(T)

    def one_block(t0):
        sl = lambda a: lax.dynamic_slice_in_dim(a, t0, blk, axis=1)
        qpos = t0 + jnp.arange(blk)
        idx, ok = _index_select(sl(qi), sl(wi), ki, qpos, kpos, topk)
        kg = _gather_rows(ka, idx)
        vg = _gather_rows(va, idx)
        return _sparse_attend(sl(qa), kg, vg, qpos, idx, ok, rel_bias)

    out = lax.map(one_block, jnp.arange(T // blk) * blk)
    return out.transpose(1, 0, 2, 3, 4).reshape(B, T, N_HEADS_A, HEAD_DIM)


def _dsa_sample(qa, ka, va, qi, ki, wi, cache_k, cache_v, cache_kidx, page_table, rel_bias):
    B, T = qa.shape[:2]
    n_pages = page_table.shape[1]
    past = n_pages * PAGE_SIZE
    L = past + T
    topk = min(TOPK_MAX, L // 4)
    ki_past = cache_kidx[page_table].reshape(B, past, D_IDX).astype(ki.dtype)
    ki_all = jnp.concatenate([ki_past, ki], axis=1)
    qpos = past + jnp.arange(T)
    kpos = jnp.arange(L)
    idx, ok = _index_select(qi, wi, ki_all, qpos, kpos, topk)
    is_past = idx < past
    pidx = jnp.minimum(idx, past - 1)
    phys_page = jnp.take_along_axis(page_table, (pidx // PAGE_SIZE).reshape(B, -1), axis=1).reshape(idx.shape)
    phys_row = phys_page * PAGE_SIZE + pidx % PAGE_SIZE
    nidx = jnp.clip(idx - past, 0, T - 1)
    flat_k = cache_k.reshape(-1, N_KV_A, HEAD_DIM)
    flat_v = cache_v.reshape(-1, N_KV_A, HEAD_DIM)
    kg = jnp.where(is_past[..., None, None], flat_k[phys_row].astype(ka.dtype), _gather_rows(ka, nidx))
    vg = jnp.where(is_past[..., None, None], flat_v[phys_row].astype(va.dtype), _gather_rows(va, nidx))
    return _sparse_attend(qa, kg, vg, qpos, idx, ok, rel_bias)


def _retention(q, k, v, state0):
    B, T, H, dk = q.shape
    dv = v.shape[-1]
    C = math.gcd(T, RET_CHUNK)
    n = T // C
    log_g = jnp.log1p(-jnp.exp2(-5.0 - jnp.arange(H, dtype=jnp.float32)))
    i = jnp.arange(C, dtype=jnp.float32)
    diff = i[:, None] - i[None, :]
    dmat = jnp.where(diff >= 0, jnp.exp(jnp.maximum(diff, 0.0)[None] * log_g[:, None, None]), 0.0)
    dec_in = jnp.exp((i + 1.0)[:, None] * log_g[None, :])
    dec_out = jnp.exp((C - 1.0 - i)[:, None] * log_g[None, :])
    dec_chunk = jnp.exp(C * log_g)

    def to_chunks(a):
        return a.astype(jnp.float32).reshape(B, n, C, H, a.shape[-1]).transpose(1, 0, 2, 3, 4)

    def step(S, inp):
        qc, kc, vc = inp
        inter = jnp.einsum('bchk,bhkv->bchv', qc * dec_in[None, :, :, None], S)
        scores = jnp.einsum('bchk,bshk->bhcs', qc, kc) * dmat[None]
        intra = jnp.einsum('bhcs,bshv->bchv', scores, vc)
        S = S * dec_chunk[None, :, None, None] + jnp.einsum('bchk,bchv->bhkv', kc * dec_out[None, :, :, None], vc)
        return S, inter + intra

    S, ys = lax.scan(step, state0.astype(jnp.float32), (to_chunks(q), to_chunks(k), to_chunks(v)))
    o = ys.transpose(1, 0, 2, 3, 4).reshape(B, T, H, dv)
    return o, S


def _merge(x, oa, ob, ga, gb, p, w_out, g_post, w_ple_up, w_ple_gate):
    B, T, _ = x.shape
    ya = jax.nn.silu(ga) * oa.reshape(B, T, D_A).astype(x.dtype)
    yb = jax.nn.silu(gb) * _head_norm(ob).reshape(B, T, D_B).astype(x.dtype)
    y = jnp.einsum('bte,ed->btd', jnp.concatenate([ya, yb], axis=-1), w_out)
    x = x + _rmsnorm(y, g_post)
    ple = jnp.einsum('btp,pd->btd', p.astype(x.dtype), w_ple_up)
    gate = jax.nn.sigmoid(jnp.einsum('btd,de->bte', x, w_ple_gate))
    return x + ple * gate


def setup_inputs(seed: int = 0) -> dict:
    key = jax.random.key(seed)
    ks = jax.random.split(key, 20)
    f32 = jnp.float32
    n_pages = PAST_LEN // PAGE_SIZE
    n_used = DEC_BATCH * n_pages
    n_pool = n_used + max(1, n_used // 4)
    perm = jax.random.permutation(ks[0], n_pool)[:n_used]
    page_table = perm.reshape(DEC_BATCH, n_pages).astype(jnp.int32)
    nrm = lambda k, shape, s: jax.random.normal(k, shape, f32) * s
    return {
        'x_prompt': nrm(ks[1], (BATCH, SEQ, D_MODEL), 1.0),
        'x_sample': nrm(ks[2], (DEC_BATCH, DEC_SEQ, D_MODEL), 1.0),
        'cache_k': nrm(ks[3], (DEPTH, n_pool, PAGE_SIZE, N_KV_A, HEAD_DIM), 1.0),
        'cache_v': nrm(ks[4], (DEPTH, n_pool, PAGE_SIZE, N_KV_A, HEAD_DIM), 1.0),
        'cache_kidx': nrm(ks[5], (DEPTH, n_pool, PAGE_SIZE, D_IDX), 1.0),
        'state_ret': nrm(ks[6], (DEPTH, DEC_BATCH, N_HEADS_B, HEAD_DIM, HEAD_DIM), 0.5),
        'page_table': page_table,
        'p_prompt': nrm(ks[7], (DEPTH, BATCH, SEQ, D_PLE), 1.0),
        'p_sample': nrm(ks[8], (DEPTH, DEC_BATCH, DEC_SEQ, D_PLE), 1.0),
        'rel_bias': nrm(ks[9], (N_BUCKETS, N_HEADS_A), 0.1),
        'w_in': nrm(ks[10], (DEPTH, D_MODEL, D_IN), D_MODEL ** -0.5),
        'w_out': nrm(ks[11], (DEPTH, D_MIX, D_MODEL), D_MIX ** -0.5),
        'g_pre': 1.0 + nrm(ks[12], (DEPTH, D_MODEL), 0.01),
        'g_post': 1.0 + nrm(ks[13], (DEPTH, D_MODEL), 0.01),
        'w_ple_up': nrm(ks[14], (DEPTH, D_PLE, D_MODEL), D_PLE ** -0.5),
        'w_ple_gate': nrm(ks[15], (DEPTH, D_MODEL, D_MODEL), D_MODEL ** -0.5),
    }


def reference(x_prompt, x_sample, cache_k, cache_v, cache_kidx, state_ret, page_table, p_prompt, p_sample,
              rel_bias, w_in, w_out, g_pre, g_post, w_ple_up, w_ple_gate):
    B, T = x_prompt.shape[:2]
    Ts = x_sample.shape[1]
    pos_p = jnp.arange(T)
    pos_s = PAST_LEN + jnp.arange(Ts)
    xp, xs = x_prompt, x_sample
    kp_l, vp_l, ip_l, sp_l = [], [], [], []
    ks_l, vs_l, is_l, ss_l = [], [], [], []
    for i in range(DEPTH):
        hp = _rmsnorm(xp, g_pre[i])
        qa, ka, va, qi, ki, wi, ga, qb, kb, vb, gb = _project(hp, w_in[i], pos_p)
        oa = _dsa_prompt(qa, ka, va, qi, ki, wi, rel_bias)
        ob, sp = _retention(qb, kb, vb, jnp.zeros((B, N_HEADS_B, HEAD_DIM, HEAD_DIM), jnp.float32))
        xp = _merge(xp, oa, ob, ga, gb, p_prompt[i], w_out[i], g_post[i], w_ple_up[i], w_ple_gate[i])
        kp_l.append(ka)
        vp_l.append(va)
        ip_l.append(ki)
        sp_l.append(sp.astype(state_ret.dtype))
        hs = _rmsnorm(xs, g_pre[i])
        qa, ka, va, qi, ki, wi, ga, qb, kb, vb, gb = _project(hs, w_in[i], pos_s)
        oa = _dsa_sample(qa, ka, va, qi, ki, wi, cache_k[i], cache_v[i], cache_kidx[i], page_table, rel_bias)
        ob, ss = _retention(qb, kb, vb, state_ret[i])
        xs = _merge(xs, oa, ob, ga, gb, p_sample[i], w_out[i], g_post[i], w_ple_up[i], w_ple_gate[i])
        ks_l.append(ka)
        vs_l.append(va)
        is_l.append(ki)
        ss_l.append(ss.astype(state_ret.dtype))
    k_prompt = jnp.stack(kp_l)
    v_prompt = jnp.stack(vp_l)
    kidx_prompt = jnp.stack(ip_l)
    ret_state_prompt = jnp.stack(sp_l)
    k_sample = jnp.stack(ks_l)
    v_sample = jnp.stack(vs_l)
    kidx_sample = jnp.stack(is_l)
    ret_state_sample = jnp.stack(ss_l)
    return (xp, xs, k_prompt, v_prompt, kidx_prompt, ret_state_prompt, k_sample, v_sample, kidx_sample, ret_state_sample)
```

```python
import functools
import math

import jax
import jax.numpy as jnp
import numpy as np
from jax import lax
from jax.experimental import pallas as pl
from jax.experimental.pallas import tpu as pltpu

F32 = jnp.float32
BF16 = jnp.bfloat16
I32 = jnp.int32

HEAD_DIM = 64
N_HEADS_A = 8
N_KV_A = 2
N_IDX_HEADS = 8
D_IDX = 64
TOPK_MAX = 256
N_HEADS_B = 8
D_A = N_HEADS_A * HEAD_DIM
D_B = N_HEADS_B * HEAD_DIM
N_BUCKETS = 32
MAX_DISTANCE = 128
ROPE_BASE = 10000.0
RET_CHUNK = 128
PAGE_SIZE = 128
EPS = 1e-6
SPLIT_SIZES = (D_A, N_KV_A * HEAD_DIM, N_KV_A * HEAD_DIM, N_IDX_HEADS * D_IDX, D_IDX, N_IDX_HEADS,
               D_A, D_B, D_B, D_B, D_B)

LANES = 128
SUBLANES = 8
BLK = 128
VMEM_LIMIT = 48 * 1024 * 1024

INT_MIN = -(2 ** 31)
KEY_NEG_FLT_MAX = INT_MIN + 0x00800000
NEG_BIG = -1e30

_C_QA, _C_KA, _C_VA, _C_QI, _C_GA, _C_QB, _C_KB, _C_VB, _C_GB, _C_KW, _C_END = (
    0, 512, 640, 768, 1280, 1792, 2304, 2816, 3328, 3840, 3968)


def _nt_dot(a, b):
    return lax.dot_general(a, b, (((1,), (1,)), ((), ())), preferred_element_type=F32)


def _dot(a, b):
    return jnp.dot(a, b, preferred_element_type=F32)


def _sigmoid(x):
    return 1.0 / (1.0 + jnp.exp(-x))


def _key_to_float(k):
    bits = jnp.where(k >= 0, k, k ^ 0x7FFFFFFF)
    return lax.bitcast_convert_type(bits, F32)


def _t5_bucket(n):
    max_exact = N_BUCKETS // 2
    n = jnp.maximum(n, 0)
    nf = jnp.maximum(n, 1).astype(F32)
    large = max_exact + (jnp.log(nf / max_exact) / math.log(MAX_DISTANCE / max_exact)
                         * (N_BUCKETS - max_exact)).astype(I32)
    large = jnp.minimum(large, N_BUCKETS - 1)
    return jnp.where(n < max_exact, n, large)


def _bias_from_bucket(bkt, rb_ref, h):
    out = jnp.zeros(bkt.shape, F32)
    for k in range(N_BUCKETS):
        out = jnp.where(bkt == k, rb_ref[k, h], out)
    return out


def _proj_kernel(x_ref, g_ref, w_ref, cos_ref, sin_ref,
                 qa_ref, ka_ref, va_ref, k16_ref, v16_ref, qi_ref, ga_ref, gb_ref,
                 qb_ref, kb_ref, vb_ref, kw_ref, kw16_ref, h_scr):
    x = x_ref[...]
    ms = jnp.mean(x * x, axis=-1, keepdims=True)
    h_scr[...] = (x * lax.rsqrt(ms + EPS) * g_ref[...]).astype(BF16)

    def mm(lo, hi):
        return _dot(h_scr[...], w_ref[:, lo:hi])

    qa_ref[...] = (mm(_C_QA, _C_KA) * (HEAD_DIM ** -0.5)).astype(BF16)
    ka = mm(_C_KA, _C_VA)
    ka_ref[...] = ka
    k16_ref[...] = ka.astype(BF16)
    va = mm(_C_VA, _C_QI)
    va_ref[...] = va
    v16_ref[...] = va.astype(BF16)
    qi_ref[...] = mm(_C_QI, _C_GA).astype(BF16)
    ga_ref[...] = mm(_C_GA, _C_QB)
    gb_ref[...] = mm(_C_GB, _C_KW)
    vb_ref[...] = mm(_C_VB, _C_GB).astype(BF16)

    cos = cos_ref[...]
    sin = sin_ref[...]
    lane = lax.broadcasted_iota(I32, cos.shape, 1)
    first_half = (lane % HEAD_DIM) < (HEAD_DIM // 2)

    def rope(z, scale):
        outs = []
        for g in range(z.shape[1] // LANES):
            zg = z[:, g * LANES:(g + 1) * LANES]
            partner = jnp.where(first_half, pltpu.roll(zg, LANES - HEAD_DIM // 2, 1),
                                pltpu.roll(zg, HEAD_DIM // 2, 1))
            r = zg * cos + partner * sin
            if scale != 1.0:
                r = r * scale
            outs.append(r.astype(BF16))
        return jnp.concatenate(outs, axis=1)

    qb_ref[...] = rope(mm(_C_QB, _C_KB), 1.0)
    kb_ref[...] = rope(mm(_C_KB, _C_VB), HEAD_DIM ** -0.5)

    kw = mm(_C_KW, _C_END)
    wscale = (N_IDX_HEADS ** -0.5) * (D_IDX ** -0.5)
    kw = kw * jnp.where(lane >= D_IDX, wscale, 1.0)
    kw_ref[...] = kw
    kw16_ref[...] = kw.astype(BF16)


def _project(x2d, g_pre, w_cat, cos_t, sin_t, tm):
    rows = x2d.shape[0]
    d_model = x2d.shape[1]
    n_tab = cos_t.shape[0] // tm
    row_spec = lambda w: pl.BlockSpec((tm, w), lambda i: (i, 0))
    outs = [
        ("qa", D_A, BF16), ("ka", LANES, F32), ("va", LANES, F32), ("k16", LANES, BF16), ("v16", LANES, BF16),
        ("qi", N_IDX_HEADS * D_IDX, BF16), ("ga", D_A, F32), ("gb", D_B, F32),
        ("qb", D_B, BF16), ("kb", D_B, BF16), ("vb", D_B, BF16), ("kw", LANES, F32), ("kw16", LANES, BF16),
    ]
    res = pl.pallas_call(
        _proj_kernel,
        grid=(rows // tm,),
        in_specs=[
            row_spec(d_model),
            pl.BlockSpec((1, d_model), lambda i: (0, 0)),
            pl.BlockSpec(w_cat.shape, lambda i: (0, 0)),
            pl.BlockSpec((tm, LANES), lambda i: (i % n_tab, 0)),
            pl.BlockSpec((tm, LANES), lambda i: (i % n_tab, 0)),
        ],
        out_specs=[row_spec(w) for _, w, _ in outs],
        out_shape=[jax.ShapeDtypeStruct((rows, w), dt) for _, w, dt in outs],
        scratch_shapes=[pltpu.VMEM((tm, d_model), BF16)],
        compiler_params=pltpu.CompilerParams(dimension_semantics=("arbitrary",), vmem_limit_bytes=VMEM_LIMIT),
        name="proj",
    )(x2d, g_pre, w_cat, cos_t, sin_t)
    return {n: r for (n, _, _), r in zip(outs, res)}


def _dsa_prompt_kernel(rb_ref, qi_ref, kwq_ref, kw16_ref, qa_ref, k16_ref, v16_ref, oa_ref,
                       sc, bt, vt, m_scr, l_scr, acc_scr, cut_scr, *, topk, search_blk):
    b = pl.program_id(0)
    j = pl.program_id(1)
    row = lax.broadcasted_iota(I32, (BLK, BLK), 0)
    col = lax.broadcasted_iota(I32, (BLK, BLK), 1)

    @pl.when((b == 0) & (j == 0))
    def _init_bias():
        for d in range(2):
            bkt = _t5_bucket(col - row + d * BLK)
            for h in range(N_HEADS_A):
                bt[d, h] = _bias_from_bucket(bkt, rb_ref, h)
        for h in range(N_HEADS_A):
            bt[2, h] = jnp.full((BLK, BLK), rb_ref[N_BUCKETS - 1, h], F32)

    vblk = v16_ref[pl.ds(pl.multiple_of(j * BLK, BLK), BLK), :]
    vt[j] = vblk.astype(F32).T.astype(BF16)

    qi = qi_ref[...]
    qi_stack = jnp.concatenate([qi[:, h * D_IDX:(h + 1) * D_IDX] for h in range(N_IDX_HEADS)], axis=0)
    wi_t = kwq_ref[...].T

    def score_blk(i):
        kib = kw16_ref[pl.ds(pl.multiple_of(i * BLK, BLK), BLK), :][:, :D_IDX]
        s = _nt_dot(kib, qi_stack)
        acc = jnp.zeros((BLK, BLK), F32)
        for h in range(N_IDX_HEADS):
            acc = acc + wi_t[D_IDX + h:D_IDX + h + 1, :] * jnp.maximum(s[:, h * BLK:(h + 1) * BLK], 0.0)
        return acc

    def past_body(i, c):
        sc[i] = score_blk(i)
        return c

    lax.fori_loop(0, j, past_body, 0)
    sc[j] = jnp.where(row <= col, score_blk(j), -jnp.inf)

    n_chunk = (j + search_blk) // search_blk

    def fill_body(i, c):
        sc[i] = jnp.full((BLK, BLK), -jnp.inf, F32)
        return c

    lax.fori_loop(j + 1, n_chunk * search_blk, fill_body, 0)

    def col_sum(w):
        return jnp.sum(w.reshape(BLK // SUBLANES, SUBLANES, BLK), axis=0)

    def count_ge(cand):
        def body(c, acc):
            for u in range(search_blk):
                acc = acc + col_sum(jnp.where(sc[c * search_blk + u] >= cand, 1.0, 0.0))
            return acc
        acc = lax.fori_loop(0, n_chunk, body, jnp.zeros((SUBLANES, BLK), F32))
        return jnp.sum(acc, axis=0, keepdims=True)

    kf = float(topk)
    c0 = count_ge(jnp.zeros((1, BLK), F32))
    key0 = jnp.where(c0 >= kf, 0, INT_MIN).astype(I32)

    def bit_body(it, key):
        cand = key | jnp.left_shift(jnp.int32(1), 30 - it)
        cnt = count_ge(_key_to_float(cand))
        return jnp.where(cnt >= kf, cand, key)

    key = lax.fori_loop(0, 31, bit_body, key0)
    key = jnp.maximum(key, KEY_NEG_FLT_MAX)
    thr = _key_to_float(key)

    def count2_body(c, accs):
        a_gt, a_ge = accs
        for u in range(search_blk):
            t = sc[c * search_blk + u]
            a_gt = a_gt + col_sum(jnp.where(t > thr, 1.0, 0.0))
            a_ge = a_ge + col_sum(jnp.where(t >= thr, 1.0, 0.0))
        return a_gt, a_ge

    z8 = jnp.zeros((SUBLANES, BLK), F32)
    a_gt, a_ge = lax.fori_loop(0, n_chunk, count2_body, (z8, z8))
    cnt_gt = jnp.sum(a_gt, axis=0, keepdims=True)
    cnt_ge = jnp.sum(a_ge, axis=0, keepdims=True)
    need = kf - cnt_gt
    straddle = jnp.max(jnp.where(cnt_ge > kf, 1.0, 0.0))

    n_keys_max = sc.shape[0] * BLK
    cut_scr[...] = jnp.full(cut_scr.shape, 2 * n_keys_max, I32)

    @pl.when(straddle > 0.0)
    def _tie_cut():
        def count_tie(cand):
            def body(i, acc):
                t = sc[i]
                hit = jnp.where(t == thr, jnp.where(row + i * BLK < cand, 1.0, 0.0), 0.0)
                return acc + col_sum(hit)
            acc = lax.fori_loop(0, j + 1, body, z8)
            return jnp.sum(acc, axis=0, keepdims=True)

        def cut_body(it, cut):
            cand = cut | jnp.left_shift(jnp.int32(1), (n_keys_max.bit_length() - 1) - it)
            return jnp.where(count_tie(cand) <= need, cand, cut)

        cut = lax.fori_loop(0, n_keys_max.bit_length(), cut_body, jnp.zeros((1, BLK), I32))
        cut_scr[...] = jnp.broadcast_to(cut, cut_scr.shape)

    cut = cut_scr[0:1, :]

    def mask_body(i, c):
        t = sc[i]
        tie_keep = jnp.where(row + i * BLK < cut, 0.0, -jnp.inf)
        sc[i] = jnp.where(t > thr, 0.0, jnp.where(t == thr, tie_keep, -jnp.inf))
        return c

    lax.fori_loop(0, j + 1, mask_body, 0)

    qa = qa_ref[...]
    g_per_kv = N_HEADS_A // N_KV_A
    q_stack = [jnp.concatenate([qa[:, (n * g_per_kv + g) * HEAD_DIM:(n * g_per_kv + g + 1) * HEAD_DIM]
                                for g in range(g_per_kv)], axis=0) for n in range(N_KV_A)]
    m_scr[...] = jnp.full(m_scr.shape, NEG_BIG, F32)
    l_scr[...] = jnp.zeros(l_scr.shape, F32)
    acc_scr[...] = jnp.zeros(acc_scr.shape, F32)

    def att_body(i, c):
        kblk = k16_ref[pl.ds(pl.multiple_of(i * BLK, BLK), BLK), :]
        nm = sc[i]
        d = jnp.minimum(j - i, 2)
        vti = vt[i]
        for n in range(N_KV_A):
            lg = _nt_dot(kblk[:, n * HEAD_DIM:(n + 1) * HEAD_DIM], q_stack[n])
            ps, alphas = [], []
            for g in range(g_per_kv):
                h = n * g_per_kv + g
                x = lg[:, g * BLK:(g + 1) * BLK] + bt[d, h] + nm
                m_old = m_scr[h]
                m_new = jnp.maximum(m_old, jnp.max(x, axis=0, keepdims=True))
                alpha = jnp.exp(m_old - m_new)
                p = jnp.exp(x - m_new)
                l_scr[h] = l_scr[h] * alpha + jnp.sum(p, axis=0, keepdims=True)
                m_scr[h] = m_new
                ps.append(p.astype(BF16))
                alphas.append(alpha)
            pv = _dot(vti[n * HEAD_DIM:(n + 1) * HEAD_DIM, :], jnp.concatenate(ps, axis=1))
            acc_scr[n] = acc_scr[n] * jnp.concatenate(alphas, axis=1) + pv
        return c

    lax.fori_loop(0, j + 1, att_body, 0)

    parts = []
    for n in range(N_KV_A):
        acc = acc_scr[n]
        for g in range(g_per_kv):
            h = n * g_per_kv + g
            parts.append(acc[:, g * BLK:(g + 1) * BLK] * (1.0 / l_scr[h]))
    oa_ref[...] = jnp.concatenate(parts, axis=0).T


def _dsa_prompt(p, rel_bias, batch, seq):
    nblk = seq // BLK
    topk = min(TOPK_MAX, seq // 4)
    search_blk = 4 if nblk % 4 == 0 else 1
    qrow = lambda w: pl.BlockSpec((BLK, w), lambda b, j: (b * nblk + j, 0))
    brow = lambda w: pl.BlockSpec((seq, w), lambda b, j: (b, 0))
    return pl.pallas_call(
        functools.partial(_dsa_prompt_kernel, topk=topk, search_blk=search_blk),
        grid=(batch, nblk),
        in_specs=[
            pl.BlockSpec(memory_space=pltpu.SMEM),
            qrow(N_IDX_HEADS * D_IDX), qrow(LANES), brow(LANES), qrow(D_A), brow(LANES), brow(LANES),
        ],
        out_specs=qrow(D_A),
        out_shape=jax.ShapeDtypeStruct((batch * seq, D_A), F32),
        scratch_shapes=[
            pltpu.VMEM((nblk, BLK, BLK), F32),
            pltpu.VMEM((3, N_HEADS_A, BLK, BLK), F32),
            pltpu.VMEM((nblk, BLK, BLK), BF16),
            pltpu.VMEM((N_HEADS_A, 1, BLK), F32),
            pltpu.VMEM((N_HEADS_A, 1, BLK), F32),
            pltpu.VMEM((N_KV_A, HEAD_DIM, (N_HEADS_A // N_KV_A) * BLK), F32),
            pltpu.VMEM((SUBLANES, BLK), I32),
        ],
        compiler_params=pltpu.CompilerParams(dimension_semantics=("arbitrary", "arbitrary"),
                                             vmem_limit_bytes=VMEM_LIMIT),
        name="dsa_prompt",
    )(rel_bias, p["qi"], p["kw"], p["kw16"], p["qa"], p["k16"], p["v16"])


def _split3_dot(x, b16):
    hi = x.astype(BF16)
    r1 = x - hi.astype(F32)
    mid = r1.astype(BF16)
    lo = (r1 - mid.astype(F32)).astype(BF16)
    return _dot(hi, b16) + _dot(mid, b16) + _dot(lo, b16)


def _retention_kernel(q_ref, k_ref, v_ref, s0_ref, dmat_ref, din_ref, dout_ref, decm_ref, bdm_ref, bones_ref,
                      yn_ref, s_out_ref, s_scr):
    c = pl.program_id(1)

    @pl.when(c == 0)
    def _load_state():
        s_scr[...] = s0_ref[0]

    q = q_ref[0]
    k = k_ref[0]
    v = v_ref[0]
    qd = (q.astype(F32) * din_ref[...]).astype(BF16)
    kdt = (k.astype(F32) * dout_ref[...]).T
    lane = lax.broadcasted_iota(I32, (q.shape[0], LANES), 1)
    lo = lane < HEAD_DIM
    zero = jnp.zeros((), BF16)
    bones = bones_ref[...]
    for p in range(N_HEADS_B // 2):
        sl = slice(p * LANES, (p + 1) * LANES)
        qp, kp, vp = q[:, sl], k[:, sl], v[:, sl]
        s_a = _nt_dot(jnp.where(lo, qp, zero), kp) * dmat_ref[2 * p]
        s_b = _nt_dot(jnp.where(lo, zero, qp), kp) * dmat_ref[2 * p + 1]
        intra = (_dot(s_a.astype(BF16), jnp.where(lo, vp, zero))
                 + _dot(s_b.astype(BF16), jnp.where(lo, zero, vp)))
        s_p = s_scr[p]
        o = _dot(qd[:, sl], s_p.astype(BF16)) + intra
        mu = _split3_dot(o, bones)
        dlt = o - mu
        var = _split3_dot(dlt * dlt, bones)
        yn_ref[0, :, sl] = dlt * lax.rsqrt(var + EPS)
        upd = _dot(kdt[sl, :].astype(BF16), vp)
        s_scr[p] = s_p * decm_ref[p] + upd * bdm_ref[...]

    @pl.when(c == pl.num_programs(1) - 1)
    def _store_state():
        s_out_ref[0] = s_scr[...]


def _ret_tables(chunk, n_real):
    h = N_HEADS_B
    log_g = jnp.log1p(-jnp.exp2(-5.0 - jnp.arange(h, dtype=F32)))
    i = jnp.arange(chunk, dtype=F32)
    diff = i[:, None] - i[None, :]
    dmat = jnp.where(diff >= 0, jnp.exp(jnp.maximum(diff, 0.0)[None] * log_g[:, None, None]), 0.0)
    dec_in = jnp.exp((i + 1.0)[:, None] * log_g[None, :])
    dec_out = jnp.exp((n_real - 1.0 - i)[:, None] * log_g[None, :])
    dec_chunk = jnp.exp(n_real * log_g)
    din = jnp.repeat(dec_in, HEAD_DIM, axis=1)
    dout = jnp.repeat(dec_out, HEAD_DIM, axis=1)
    head_of = np.arange(LANES) // HEAD_DIM
    bd = jnp.asarray((head_of[:, None] == head_of[None, :]).astype(np.float32))
    dc = jnp.repeat(dec_chunk, HEAD_DIM).reshape(h // 2, LANES)
    decm = dc[:, :, None] * bd[None]
    bones = (bd / HEAD_DIM).astype(BF16)
    return dmat, din, dout, decm, bd, bones


def _retention(q3, k3, v3, s0, n_real):
    batch, seq, _ = q3.shape
    chunk = min(RET_CHUNK, seq)
    n = seq // chunk
    dmat, din, dout, decm, bd, bones = _ret_tables(chunk, n_real)
    tok = pl.BlockSpec((1, chunk, D_B), lambda b, c: (b, c, 0))
    st = pl.BlockSpec((1, N_HEADS_B // 2, LANES, LANES), lambda b, c: (b, 0, 0, 0))
    full = lambda a: pl.BlockSpec(a.shape, lambda b, c: (0,) * a.ndim)
    return pl.pallas_call(
        _retention_kernel,
        grid=(batch, n),
        in_specs=[tok, tok, tok, st, full(dmat), full(din), full(dout), full(decm), full(bd), full(bones)],
        out_specs=[tok, st],
        out_shape=[jax.ShapeDtypeStruct((batch, seq, D_B), F32),
                   jax.ShapeDtypeStruct((batch, N_HEADS_B // 2, LANES, LANES), F32)],
        scratch_shapes=[pltpu.VMEM((N_HEADS_B // 2, LANES, LANES), F32)],
        compiler_params=pltpu.CompilerParams(dimension_semantics=("arbitrary", "arbitrary"),
                                             vmem_limit_bytes=VMEM_LIMIT),
        name="retention",
    )(q3, k3, v3, s0, dmat, din, dout, decm, bd, bones)


def _state_to_pairs(s):
    b = s.shape[0]
    s = s.reshape(b, N_HEADS_B // 2, 2, HEAD_DIM, HEAD_DIM)
    z = jnp.zeros_like(s[:, :, 0])
    top = jnp.concatenate([s[:, :, 0], z], axis=-1)
    bot = jnp.concatenate([z, s[:, :, 1]], axis=-1)
    return jnp.concatenate([top, bot], axis=-2)


def _pairs_to_state(sp):
    b = sp.shape[0]
    a = sp[:, :, :HEAD_DIM, :HEAD_DIM]
    d = sp[:, :, HEAD_DIM:, HEAD_DIM:]
    return jnp.stack([a, d], axis=2).reshape(b, N_HEADS_B, HEAD_DIM, HEAD_DIM)


def _merge_kernel(x_ref, oa_ref, yn_ref, ga_ref, gb_ref, p_ref, wo_ref, gp_ref, wup_ref, wg_ref, out_ref):
    ga = ga_ref[...]
    gb = gb_ref[...]
    ya = (ga * _sigmoid(ga) * oa_ref[...]).astype(BF16)
    yb = (gb * _sigmoid(gb) * yn_ref[...]).astype(BF16)
    y = _dot(ya, wo_ref[0:D_A, :]) + _dot(yb, wo_ref[D_A:D_A + D_B, :])
    ms = jnp.mean(y * y, axis=-1, keepdims=True)
    x1 = x_ref[...] + y * lax.rsqrt(ms + EPS) * gp_ref[...]
    ple = _dot(p_ref[...].astype(BF16), wup_ref[...])
    gate = _sigmoid(_dot(x1.astype(BF16), wg_ref[...]))
    out_ref[...] = x1 + ple * gate


def _merge(x2d, oa, yn, ga, gb, p2d, wo16, g_post, wup16, wg16, tm):
    rows, d_model = x2d.shape
    row = lambda w: pl.BlockSpec((tm, w), lambda i: (i, 0))
    full = lambda a: pl.BlockSpec(a.shape, lambda i: (0, 0))
    return pl.pallas_call(
        _merge_kernel,
        grid=(rows // tm,),
        in_specs=[row(d_model), row(D_A), row(D_B), row(D_A), row(D_B), row(p2d.shape[1]),
                  full(wo16), full(g_post), full(wup16), full(wg16)],
        out_specs=row(d_model),
        out_shape=jax.ShapeDtypeStruct((rows, d_model), F32),
        compiler_params=pltpu.CompilerParams(dimension_semantics=("arbitrary",), vmem_limit_bytes=VMEM_LIMIT),
        name="merge",
    )(x2d, oa, yn, ga, gb, p2d, wo16, g_post, wup16, wg16)


def _sample_index_kernel(pt_ref, qi_ref, w_ref, *rest, n_pg):
    pages = rest[:n_pg]
    out_ref = rest[n_pg]
    qi = qi_ref[0]
    w = w_ref[0]
    n_q = qi.shape[0] // N_IDX_HEADS
    for m in range(n_pg):
        s = _nt_dot(qi, pages[m][0].astype(BF16))
        r = jnp.maximum(s, 0.0) * w
        out_ref[0, :, m * PAGE_SIZE:(m + 1) * PAGE_SIZE] = jnp.sum(
            r.reshape(n_q, N_IDX_HEADS, PAGE_SIZE), axis=1)


def _sample_index(page_table, qi_qh, w_qh, cache_kidx, n_pg):
    batch, n_pages = page_table.shape
    n_q = qi_qh.shape[1] // N_IDX_HEADS
    page_spec = lambda m: pl.BlockSpec((1, PAGE_SIZE, D_IDX),
                                       lambda b, c, pt, m=m: (pt[b, c * n_pg + m], 0, 0))
    grid_spec = pltpu.PrefetchScalarGridSpec(
        num_scalar_prefetch=1,
        grid=(batch, n_pages // n_pg),
        in_specs=[pl.BlockSpec((1,) + qi_qh.shape[1:], lambda b, c, pt: (b, 0, 0)),
                  pl.BlockSpec((1,) + w_qh.shape[1:], lambda b, c, pt: (b, 0, 0))]
                 + [page_spec(m) for m in range(n_pg)],
        out_specs=pl.BlockSpec((1, n_q, n_pg * PAGE_SIZE), lambda b, c, pt: (b, 0, c)),
    )
    return pl.pallas_call(
        functools.partial(_sample_index_kernel, n_pg=n_pg),
        grid_spec=grid_spec,
        out_shape=jax.ShapeDtypeStruct((batch, n_q, n_pages * PAGE_SIZE), F32),
        compiler_params=pltpu.CompilerParams(dimension_semantics=("arbitrary", "arbitrary"),
                                             vmem_limit_bytes=VMEM_LIMIT),
        name="sample_index",
    )(page_table, qi_qh, w_qh, *([cache_kidx] * n_pg))


def _sample_select_kernel(sp_ref, qi_ref, w_ref, kw16_ref, thr_ref, cut_ref, nmn_ref, snew_scr,
                          *, topk, n_q, lane_chunk):
    rows, past = sp_ref.shape
    r_i = lax.broadcasted_iota(I32, (rows, LANES), 0)
    l_i = lax.broadcasted_iota(I32, (rows, LANES), 1)

    s = _nt_dot(qi_ref[...], kw16_ref[...][:, :D_IDX])
    r = jnp.maximum(s, 0.0) * w_ref[...]
    s_new = jnp.sum(r.reshape(rows, N_IDX_HEADS, LANES), axis=1)
    ok = ((l_i // n_q) == (r_i // n_q)) & ((l_i % n_q) <= (r_i % n_q))
    snew_scr[...] = jnp.where(ok, s_new, -jnp.inf)

    n_chunks = past // lane_chunk
    tiles = lane_chunk // LANES
    kf = float(topk)

    def count(ind):
        acc = ind(snew_scr[...], l_i + past)
        for ch in range(n_chunks):
            x = sp_ref[:, ch * lane_chunk:(ch + 1) * lane_chunk]
            for t in range(tiles):
                idx = l_i + (ch * lane_chunk + t * LANES)
                acc = acc + ind(x[:, t * LANES:(t + 1) * LANES], idx)
        return jnp.broadcast_to(jnp.sum(acc, axis=1, keepdims=True), (rows, LANES))

    c0 = count(lambda x, idx: jnp.where(x >= 0.0, 1.0, 0.0))
    key0 = jnp.where(c0 >= kf, 0, INT_MIN).astype(I32)

    def bit_body(it, key):
        cand = key | jnp.left_shift(jnp.int32(1), 30 - it)
        cf = _key_to_float(cand)
        return jnp.where(count(lambda x, idx: jnp.where(x >= cf, 1.0, 0.0)) >= kf, cand, key)

    key = lax.fori_loop(0, 31, bit_body, key0)
    key = jnp.maximum(key, KEY_NEG_FLT_MAX)
    thr = _key_to_float(key)
    cnt_gt = count(lambda x, idx: jnp.where(x > thr, 1.0, 0.0))
    cnt_ge = count(lambda x, idx: jnp.where(x >= thr, 1.0, 0.0))
    need = kf - cnt_gt
    straddle = jnp.max(jnp.where(cnt_ge > kf, 1.0, 0.0))

    idx_bits = (past + LANES).bit_length()
    thr_ref[...] = thr
    cut_ref[...] = jnp.full((rows, LANES), 1 << idx_bits, I32)

    @pl.when(straddle > 0.0)
    def _tie_cut():
        def cut_body(it, cut):
            cand = cut | jnp.left_shift(jnp.int32(1), (idx_bits - 1) - it)
            cnt = count(lambda x, idx: jnp.where(x == thr, jnp.where(idx < cand, 1.0, 0.0), 0.0))
            return jnp.where(cnt <= need, cand, cut)
        cut_ref[...] = lax.fori_loop(0, idx_bits, cut_body, jnp.zeros((rows, LANES), I32))

    cut = cut_ref[...]
    sn = snew_scr[...]
    tie_keep = jnp.where(l_i + past < cut, 0.0, -jnp.inf)
    nmn_ref[...] = jnp.where(sn > thr, 0.0, jnp.where(sn == thr, tie_keep, -jnp.inf))


def _sample_select(scores_past, qi_rows, w_rows, kw16_s, topk, n_q):
    rows, past = scores_past.shape
    lane_chunk = 2048 if past % 2048 == 0 else LANES
    full = lambda a: pl.BlockSpec(a.shape, lambda i: (0,) * a.ndim)
    o = jax.ShapeDtypeStruct((rows, LANES), F32)
    return pl.pallas_call(
        functools.partial(_sample_select_kernel, topk=topk, n_q=n_q, lane_chunk=lane_chunk),
        grid=(1,),
        in_specs=[full(scores_past), full(qi_rows), full(w_rows), full(kw16_s)],
        out_specs=[pl.BlockSpec((rows, LANES), lambda i: (0, 0))] * 3,
        out_shape=[o, jax.ShapeDtypeStruct((rows, LANES), I32), o],
        scratch_shapes=[pltpu.VMEM((rows, LANES), F32)],
        compiler_params=pltpu.CompilerParams(dimension_semantics=("arbitrary",), vmem_limit_bytes=VMEM_LIMIT),
        name="sample_select",
    )(scores_past, qi_rows, w_rows, kw16_s)


def _sample_attn_kernel(pt_ref, rb_ref, q_ref, sp_ref, thr_ref, cut_ref, nmn_ref, k16n_ref, v16n_ref, *rest,
                        n_pg, n_q, past):
    kpages = rest[:n_pg]
    vpages = rest[n_pg:2 * n_pg]
    o_ref = rest[2 * n_pg]
    m_scr, l_scr, acc_scr = rest[2 * n_pg + 1:]
    b = pl.program_id(0)
    c = pl.program_id(1)
    g_per_kv = N_HEADS_A // N_KV_A
    rows_n = g_per_kv * n_q
    rep = lambda a: jnp.concatenate([a] * g_per_kv, axis=0)

    @pl.when(c == 0)
    def _init():
        m_scr[...] = jnp.full(m_scr.shape, NEG_BIG, F32)
        l_scr[...] = jnp.zeros(l_scr.shape, F32)
        acc_scr[...] = jnp.zeros(acc_scr.shape, F32)

    thr = rep(thr_ref[0])
    cut = rep(cut_ref[0])
    qpos = past + rep(lax.broadcasted_iota(I32, (n_q, LANES), 0))
    lane = lax.broadcasted_iota(I32, (rows_n, LANES), 1)

    def head_bias(dist, n):
        bkt = _t5_bucket(dist)
        return jnp.concatenate([_bias_from_bucket(bkt[g * n_q:(g + 1) * n_q], rb_ref, n * g_per_kv + g)
                                for g in range(g_per_kv)], axis=0)

    def far_bias(n):
        return jnp.concatenate([jnp.full((n_q, LANES), rb_ref[N_BUCKETS - 1, n * g_per_kv + g], F32)
                                for g in range(g_per_kv)], axis=0)

    def softmax_step(n, xs, vs):
        sl = slice(n * rows_n, (n + 1) * rows_n)
        mx = xs[0]
        for x in xs[1:]:
            mx = jnp.maximum(mx, x)
        m_old = m_scr[sl, :]
        m_new = jnp.maximum(m_old, jnp.max(mx, axis=1, keepdims=True))
        alpha = jnp.exp(m_old - m_new)
        lsum = jnp.zeros((rows_n, LANES), F32)
        pv = jnp.zeros((rows_n, HEAD_DIM), F32)
        for x, v in zip(xs, vs):
            p = jnp.exp(x - m_new)
            lsum = lsum + p
            pv = pv + _dot(p.astype(BF16), v)
        l_scr[sl, :] = l_scr[sl, :] * alpha + jnp.sum(lsum, axis=1, keepdims=True)
        acc_scr[sl, :] = acc_scr[sl, :] * alpha[:, :HEAD_DIM] + pv
        m_scr[sl, :] = m_new

    q = q_ref[0]
    nms = []
    for m in range(n_pg):
        sc = rep(sp_ref[0, :, m * PAGE_SIZE:(m + 1) * PAGE_SIZE])
        idx = lane + (c * n_pg + m) * PAGE_SIZE
        tie_keep = jnp.where(idx < cut, 0.0, -jnp.inf)
        nms.append(jnp.where(sc > thr, 0.0, jnp.where(sc == thr, tie_keep, -jnp.inf)))
    for n in range(N_KV_A):
        qn = q[n * rows_n:(n + 1) * rows_n]
        xs, vs = [], []
        for m in range(n_pg):
            kp = kpages[m][0][:, n * HEAD_DIM:(n + 1) * HEAD_DIM].astype(BF16)
            lg = _nt_dot(qn, kp)
            if m == n_pg - 1:
                bias = head_bias(qpos - (lane + (c * n_pg + m) * PAGE_SIZE), n)
            else:
                bias = far_bias(n)
            xs.append(lg + bias + nms[m])
            vs.append(vpages[m][0][:, n * HEAD_DIM:(n + 1) * HEAD_DIM].astype(BF16))
        softmax_step(n, xs, vs)

    @pl.when(c == pl.num_programs(1) - 1)
    def _finish():
        nmn = rep(nmn_ref[0])
        dist = rep(lax.broadcasted_iota(I32, (n_q, LANES), 0)) - (lane % n_q)
        for n in range(N_KV_A):
            qn = q[n * rows_n:(n + 1) * rows_n]
            kn = k16n_ref[...][:, n * HEAD_DIM:(n + 1) * HEAD_DIM]
            x = _nt_dot(qn, kn) + head_bias(dist, n) + nmn
            softmax_step(n, [x], [v16n_ref[...][:, n * HEAD_DIM:(n + 1) * HEAD_DIM]])
        o_ref[0] = acc_scr[...] * (1.0 / l_scr[...][:, :HEAD_DIM])


def _sample_attn(page_table, rel_bias, q_hq, scores_past3, thr3, cut3, nmn3, k16n, v16n, cache_k, cache_v, n_pg):
    batch, n_pages = page_table.shape
    n_q = scores_past3.shape[1]
    past = n_pages * PAGE_SIZE
    rows = q_hq.shape[1]
    page_spec = lambda m: pl.BlockSpec((1, PAGE_SIZE, LANES),
                                       lambda b, c, pt, m=m: (pt[b, c * n_pg + m], 0, 0))
    per_b = lambda a: pl.BlockSpec((1,) + a.shape[1:], lambda b, c, pt: (b,) + (0,) * (a.ndim - 1))
    full = lambda a: pl.BlockSpec(a.shape, lambda b, c, pt: (0,) * a.ndim)
    grid_spec = pltpu.PrefetchScalarGridSpec(
        num_scalar_prefetch=1,
        grid=(batch, n_pages // n_pg),
        in_specs=[pl.BlockSpec(memory_space=pltpu.SMEM), per_b(q_hq),
                  pl.BlockSpec((1, n_q, n_pg * PAGE_SIZE), lambda b, c, pt: (b, 0, c)),
                  per_b(thr3), per_b(cut3), per_b(nmn3), full(k16n), full(v16n)]
                 + [page_spec(m) for m in range(n_pg)] * 2,
        out_specs=pl.BlockSpec((1, rows, HEAD_DIM), lambda b, c, pt: (b, 0, 0)),
        scratch_shapes=[pltpu.VMEM((rows, LANES), F32), pltpu.VMEM((rows, LANES), F32),
                        pltpu.VMEM((rows, HEAD_DIM), F32)],
    )
    return pl.pallas_call(
        functools.partial(_sample_attn_kernel, n_pg=n_pg, n_q=n_q, past=past),
        grid_spec=grid_spec,
        out_shape=jax.ShapeDtypeStruct((batch, rows, HEAD_DIM), F32),
        compiler_params=pltpu.CompilerParams(dimension_semantics=("arbitrary", "arbitrary"),
                                             vmem_limit_bytes=VMEM_LIMIT),
        name="sample_attn",
    )(page_table, rel_bias, q_hq, scores_past3, thr3, cut3, nmn3, k16n, v16n,
      *([cache_k] * n_pg), *([cache_v] * n_pg))


def _rope_tables(pos):
    half = HEAD_DIM // 2
    inv = ROPE_BASE ** (-jnp.arange(half, dtype=F32) / half)
    ang = pos.astype(F32)[:, None] * inv[None, :]
    cos = jnp.cos(ang)
    sin = jnp.sin(ang)
    reps = LANES // HEAD_DIM
    cos_t = jnp.tile(jnp.concatenate([cos, cos], axis=1), (1, reps))
    sin_t = jnp.tile(jnp.concatenate([-sin, sin], axis=1), (1, reps))
    return cos_t, sin_t


def _cat_weight(w_in):
    offs = np.cumsum((0,) + SPLIT_SIZES)
    qa, ka, va, qi, ki, wi, ga, qb, kb, vb, gb = [w_in[:, offs[i]:offs[i + 1]] for i in range(len(SPLIT_SIZES))]
    pad = jnp.zeros((w_in.shape[0], LANES - D_IDX - N_IDX_HEADS), w_in.dtype)
    return jnp.concatenate([qa, ka, va, qi, ga, qb, kb, vb, gb, ki, wi, pad], axis=1).astype(BF16)


def _largest_divisor(n, cap):
    d = cap
    while n % d:
        d //= 2
    return d


def kernel(x_prompt, x_sample, cache_k, cache_v, cache_kidx, state_ret, page_table, p_prompt, p_sample,
           rel_bias, w_in, w_out, g_pre, g_post, w_ple_up, w_ple_gate):
    batch, seq, d_model = x_prompt.shape
    dec_b, dec_t, _ = x_sample.shape
    depth = w_in.shape[0]
    n_pages = page_table.shape[1]
    past = n_pages * PAGE_SIZE
    rows_s = dec_b * dec_t
    assert depth == 1 and rows_s == LANES and seq % BLK == 0

    w_cat = _cat_weight(w_in[0])
    gpre = g_pre[0].reshape(1, d_model)
    gpost = g_post[0].reshape(1, d_model)
    wo16 = w_out[0].astype(BF16)
    wup16 = w_ple_up[0].astype(BF16)
    wg16 = w_ple_gate[0].astype(BF16)

    tm = _largest_divisor(seq, 512)
    cos_p, sin_p = _rope_tables(jnp.arange(seq))
    xp2 = x_prompt.reshape(batch * seq, d_model)
    pp = _project(xp2, gpre, w_cat, cos_p, sin_p, tm)
    oa_p = _dsa_prompt(pp, rel_bias, batch, seq)
    r3 = lambda a: a.reshape(batch, seq, D_B)
    s0_p = jnp.zeros((batch, N_HEADS_B // 2, LANES, LANES), F32)
    yn_p, sp_pairs = _retention(r3(pp["qb"]), r3(pp["kb"]), r3(pp["vb"]), s0_p, float(min(RET_CHUNK, seq)))
    y_prompt = _merge(xp2, oa_p, yn_p.reshape(batch * seq, D_B), pp["ga"], pp["gb"],
                      p_prompt[0].reshape(batch * seq, -1), wo16, gpost, wup16, wg16, tm)

    pos_s = past + jnp.arange(dec_t)
    cos_s, sin_s = _rope_tables(jnp.tile(pos_s, dec_b))
    xs2 = x_sample.reshape(rows_s, d_model)
    ps = _project(xs2, gpre, w_cat, cos_s, sin_s, rows_s)
    topk_s = min(TOPK_MAX, (past + dec_t) // 4)

    qi_rows = ps["qi"].reshape(rows_s * N_IDX_HEADS, D_IDX)
    w_rows = jnp.broadcast_to(ps["kw"][:, D_IDX:D_IDX + N_IDX_HEADS].reshape(rows_s * N_IDX_HEADS, 1),
                              (rows_s * N_IDX_HEADS, LANES))
    n_pg = _largest_divisor(n_pages, 16)
    scores_past = _sample_index(page_table, qi_rows.reshape(dec_b, dec_t * N_IDX_HEADS, D_IDX),
                                w_rows.reshape(dec_b, dec_t * N_IDX_HEADS, LANES), cache_kidx[0], n_pg)
    thr, cut, nmn = _sample_select(scores_past.reshape(rows_s, past), qi_rows, w_rows, ps["kw16"], topk_s, dec_t)

    q_hq = ps["qa"].reshape(dec_b, dec_t, N_HEADS_A, HEAD_DIM).transpose(0, 2, 1, 3).reshape(
        dec_b, N_HEADS_A * dec_t, HEAD_DIM)
    b3 = lambda a: a.reshape(dec_b, dec_t, LANES)
    o_hq = _sample_attn(page_table, rel_bias, q_hq, scores_past, b3(thr), b3(cut), b3(nmn),
                        ps["k16"], ps["v16"], cache_k[0].reshape(-1, PAGE_SIZE, LANES),
                        cache_v[0].reshape(-1, PAGE_SIZE, LANES), n_pg)
    oa_s = o_hq.reshape(dec_b, N_HEADS_A, dec_t, HEAD_DIM).transpose(0, 2, 1, 3).reshape(rows_s, D_A)

    chunk_s = RET_CHUNK
    padt = lambda a: jnp.pad(a.reshape(dec_b, dec_t, D_B), ((0, 0), (0, chunk_s - dec_t), (0, 0)))
    yn_s, ss_pairs = _retention(padt(ps["qb"]), padt(ps["kb"]), padt(ps["vb"]),
                                _state_to_pairs(state_ret[0]), float(math.gcd(dec_t, RET_CHUNK)))
    y_sample = _merge(xs2, oa_s, yn_s[:, :dec_t].reshape(rows_s, D_B), ps["ga"], ps["gb"],
                      p_sample[0].reshape(rows_s, -1), wo16, gpost, wup16, wg16, rows_s)

    def kv(a, b, t):
        return a.reshape(1, b, t, N_KV_A, HEAD_DIM)

    return (
        y_prompt.reshape(batch, seq, d_model),
        y_sample.reshape(dec_b, dec_t, d_model),
        kv(pp["ka"], batch, seq), kv(pp["va"], batch, seq),
        pp["kw"][:, :D_IDX].reshape(1, batch, seq, D_IDX),
        _pairs_to_state(sp_pairs)[None].astype(state_ret.dtype),
        kv(ps["ka"], dec_b, dec_t), kv(ps["va"], dec_b, dec_t),
        ps["kw"][:, :D_IDX].reshape(1, dec_b, dec_t, D_IDX),
        _pairs_to_state(ss_pairs)[None].astype(state_ret.dtype),
    )
```

```python
import functools
import math

import jax
import jax.numpy as jnp
import numpy as np
from jax import lax
from jax.experimental import pallas as pl
from jax.experimental.pallas import tpu as pltpu

F32 = jnp.float32
BF16 = jnp.bfloat16
I32 = jnp.int32

HEAD_DIM = 64
N_HEADS_A = 8
N_KV_A = 2
N_IDX_HEADS = 8
D_IDX = 64
TOPK_MAX = 256
N_HEADS_B = 8
D_A = N_HEADS_A * HEAD_DIM
D_B = N_HEADS_B * HEAD_DIM
N_BUCKETS = 32
MAX_DISTANCE = 128
ROPE_BASE = 10000.0
RET_CHUNK = 128
PAGE_SIZE = 128
EPS = 1e-6
SPLIT_SIZES = (D_A, N_KV_A * HEAD_DIM, N_KV_A * HEAD_DIM, N_IDX_HEADS * D_IDX, D_IDX, N_IDX_HEADS,
               D_A, D_B, D_B, D_B, D_B)

LANES = 128
SUBLANES = 8
BLK = 128
VMEM_LIMIT = 48 * 1024 * 1024

INT_MIN = -(2 ** 31)
KEY_NEG_FLT_MAX = INT_MIN + 0x00800000
NEG_BIG = -1e30

_C_QA, _C_KA, _C_VA, _C_QI, _C_GA, _C_QB, _C_KB, _C_VB, _C_GB, _C_KW, _C_END = (
    0, 512, 640, 768, 1280, 1792, 2304, 2816, 3328, 3840, 3968)


def _nt_dot(a, b):
    return lax.dot_general(a, b, (((1,), (1,)), ((), ())), preferred_element_type=F32)


def _dot(a, b):
    return jnp.dot(a, b, preferred_element_type=F32)


def _sigmoid(x):
    return 1.0 / (1.0 + jnp.exp(-x))


def _key_to_float(k):
    bits = jnp.where(k >= 0, k, k ^ 0x7FFFFFFF)
    return lax.bitcast_convert_type(bits, F32)


def _t5_bucket(n):
    max_exact = N_BUCKETS // 2
    n = jnp.maximum(n, 0)
    nf = jnp.maximum(n, 1).astype(F32)
    large = max_exact + (jnp.log(nf / max_exact) / math.log(MAX_DISTANCE / max_exact)
                         * (N_BUCKETS - max_exact)).astype(I32)
    large = jnp.minimum(large, N_BUCKETS - 1)
    return jnp.where(n < max_exact, n, large)


def _bias_from_bucket(bkt, rb_ref, h):
    out = jnp.zeros(bkt.shape, F32)
    for k in range(N_BUCKETS):
        out = jnp.where(bkt == k, rb_ref[k, h], out)
    return out


def _proj_kernel(x_ref, g_ref, w_ref, cos_ref, sin_ref,
                 qa_ref, ka_ref, va_ref, k16_ref, v16_ref, qi_ref, ga_ref, gb_ref,
                 qb_ref, kb_ref, vb_ref, kw_ref, kw16_ref, h_scr):
    x = x_ref[...]
    ms = jnp.mean(x * x, axis=-1, keepdims=True)
    h_scr[...] = (x * lax.rsqrt(ms + EPS) * g_ref[...]).astype(BF16)

    def mm(lo, hi):
        return _dot(h_scr[...], w_ref[:, lo:hi])

    qa_ref[...] = (mm(_C_QA, _C_KA) * (HEAD_DIM ** -0.5)).astype(BF16)
    ka = mm(_C_KA, _C_VA)
    ka_ref[...] = ka
    k16_ref[...] = ka.astype(BF16)
    va = mm(_C_VA, _C_QI)
    va_ref[...] = va
    v16_ref[...] = va.astype(BF16)
    qi_ref[...] = mm(_C_QI, _C_GA).astype(BF16)
    ga_ref[...] = mm(_C_GA, _C_QB)
    gb_ref[...] = mm(_C_GB, _C_KW)
    vb_ref[...] = mm(_C_VB, _C_GB).astype(BF16)

    cos = cos_ref[...]
    sin = sin_ref[...]
    lane = lax.broadcasted_iota(I32, cos.shape, 1)
    first_half = (lane % HEAD_DIM) < (HEAD_DIM // 2)

    def rope(z, scale):
        outs = []
        for g in range(z.shape[1] // LANES):
            zg = z[:, g * LANES:(g + 1) * LANES]
            partner = jnp.where(first_half, pltpu.roll(zg, LANES - HEAD_DIM // 2, 1),
                                pltpu.roll(zg, HEAD_DIM // 2, 1))
            r = zg * cos + partner * sin
            if scale != 1.0:
                r = r * scale
            outs.append(r.astype(BF16))
        return jnp.concatenate(outs, axis=1)

    qb_ref[...] = rope(mm(_C_QB, _C_KB), 1.0)
    kb_ref[...] = rope(mm(_C_KB, _C_VB), HEAD_DIM ** -0.5)

    kw = mm(_C_KW, _C_END)
    wscale = (N_IDX_HEADS ** -0.5) * (D_IDX ** -0.5)
    kw = kw * jnp.where(lane >= D_IDX, wscale, 1.0)
    kw_ref[...] = kw
    kw16_ref[...] = kw.astype(BF16)


def _project(x2d, g_pre, w_cat, cos_t, sin_t, tm):
    rows = x2d.shape[0]
    d_model = x2d.shape[1]
    n_tab = cos_t.shape[0] // tm
    row_spec = lambda w: pl.BlockSpec((tm, w), lambda i: (i, 0))
    outs = [
        ("qa", D_A, BF16), ("ka", LANES, F32), ("va", LANES, F32), ("k16", LANES, BF16), ("v16", LANES, BF16),
        ("qi", N_IDX_HEADS * D_IDX, BF16), ("ga", D_A, F32), ("gb", D_B, F32),
        ("qb", D_B, BF16), ("kb", D_B, BF16), ("vb", D_B, BF16), ("kw", LANES, F32), ("kw16", LANES, BF16),
    ]
    res = pl.pallas_call(
        _proj_kernel,
        grid=(rows // tm,),
        in_specs=[
            row_spec(d_model),
            pl.BlockSpec((1, d_model), lambda i: (0, 0)),
            pl.BlockSpec(w_cat.shape, lambda i: (0, 0)),
            pl.BlockSpec((tm, LANES), lambda i: (i % n_tab, 0)),
            pl.BlockSpec((tm, LANES), lambda i: (i % n_tab, 0)),
        ],
        out_specs=[row_spec(w) for _, w, _ in outs],
        out_shape=[jax.ShapeDtypeStruct((rows, w), dt) for _, w, dt in outs],
        scratch_shapes=[pltpu.VMEM((tm, d_model), BF16)],
        compiler_params=pltpu.CompilerParams(dimension_semantics=("arbitrary",), vmem_limit_bytes=VMEM_LIMIT),
        name="proj",
    )(x2d, g_pre, w_cat, cos_t, sin_t)
    return {n: r for (n, _, _), r in zip(outs, res)}


def _dsa_prompt_kernel(rb_ref, qi_ref, kwq_ref, kw16_ref, qa_ref, k16_ref, v16_ref, oa_ref,
                       sc, bt, vt, lgs, m8_scr, acc_scr, cut_scr, *, topk, search_blk, score_blk_n):
    b = pl.program_id(0)
    j = pl.program_id(1)
    row = lax.broadcasted_iota(I32, (BLK, BLK), 0)
    col = lax.broadcasted_iota(I32, (BLK, BLK), 1)

    @pl.when((b == 0) & (j == 0))
    def _init_bias():
        for d in range(2):
            bkt = _t5_bucket(col - row + d * BLK)
            for h in range(N_HEADS_A):
                bt[d, h] = _bias_from_bucket(bkt, rb_ref, h) - rb_ref[N_BUCKETS - 1, h]

    vblk = v16_ref[pl.ds(pl.multiple_of(j * BLK, BLK), BLK), :]
    vt[j] = vblk.astype(F32).T.astype(BF16)

    qi = qi_ref[...]
    qi_stack = jnp.concatenate([qi[:, h * D_IDX:(h + 1) * D_IDX] for h in range(N_IDX_HEADS)], axis=0)
    wi_t = kwq_ref[...].T
    qpos = col + j * BLK

    def score_body(c, carry):
        for u in range(score_blk_n):
            i = c * score_blk_n + u
            kib = kw16_ref[pl.ds(pl.multiple_of(i * BLK, BLK), BLK), :][:, :D_IDX]
            s = _nt_dot(kib, qi_stack)
            acc = jnp.zeros((BLK, BLK), F32)
            for h in range(N_IDX_HEADS):
                acc = acc + wi_t[D_IDX + h:D_IDX + h + 1, :] * jnp.maximum(s[:, h * BLK:(h + 1) * BLK], 0.0)
            sc[i] = jnp.where(row + i * BLK <= qpos, acc, -jnp.inf)
        return carry

    n_score = (j + score_blk_n) // score_blk_n
    lax.fori_loop(0, n_score, score_body, 0)

    n_chunk = (j + search_blk) // search_blk

    def fill_body(i, c):
        sc[i] = jnp.full((BLK, BLK), -jnp.inf, F32)
        return c

    lax.fori_loop(n_score * score_blk_n, n_chunk * search_blk, fill_body, 0)

    def col_sum(w):
        return jnp.sum(w.reshape(BLK // SUBLANES, SUBLANES, BLK), axis=0)

    def count_ge(cand):
        def body(c, acc):
            for u in range(search_blk):
                acc = acc + col_sum(jnp.where(sc[c * search_blk + u] >= cand, 1.0, 0.0))
            return acc
        acc = lax.fori_loop(0, n_chunk, body, jnp.zeros((SUBLANES, BLK), F32))
        return jnp.sum(acc, axis=0, keepdims=True)

    kf = float(topk)
    c0 = count_ge(jnp.zeros((1, BLK), F32))
    key0 = jnp.where(c0 >= kf, 0, INT_MIN).astype(I32)

    def bit_body(it, key):
        cand = key | jnp.left_shift(jnp.int32(1), 30 - it)
        cnt = count_ge(_key_to_float(cand))
        return jnp.where(cnt >= kf, cand, key)

    key = lax.fori_loop(0, 31, bit_body, key0)
    key = jnp.maximum(key, KEY_NEG_FLT_MAX)
    thr = _key_to_float(key)

    def count2_body(c, accs):
        a_gt, a_ge = accs
        for u in range(search_blk):
            t = sc[c * search_blk + u]
            a_gt = a_gt + col_sum(jnp.where(t > thr, 1.0, 0.0))
            a_ge = a_ge + col_sum(jnp.where(t >= thr, 1.0, 0.0))
        return a_gt, a_ge

    z8 = jnp.zeros((SUBLANES, BLK), F32)
    a_gt, a_ge = lax.fori_loop(0, n_chunk, count2_body, (z8, z8))
    cnt_gt = jnp.sum(a_gt, axis=0, keepdims=True)
    cnt_ge = jnp.sum(a_ge, axis=0, keepdims=True)
    need = kf - cnt_gt
    straddle = jnp.max(jnp.where(cnt_ge > kf, 1.0, 0.0))

    n_keys_max = sc.shape[0] * BLK
    cut_scr[...] = jnp.full(cut_scr.shape, 2 * n_keys_max, I32)

    @pl.when(straddle > 0.0)
    def _tie_cut():
        def count_tie(cand):
            def body(i, acc):
                t = sc[i]
                hit = jnp.where(t == thr, jnp.where(row + i * BLK < cand, 1.0, 0.0), 0.0)
                return acc + col_sum(hit)
            acc = lax.fori_loop(0, j + 1, body, z8)
            return jnp.sum(acc, axis=0, keepdims=True)

        def cut_body(it, cut):
            cand = cut | jnp.left_shift(jnp.int32(1), (n_keys_max.bit_length() - 1) - it)
            return jnp.where(count_tie(cand) <= need, cand, cut)

        cut = lax.fori_loop(0, n_keys_max.bit_length(), cut_body, jnp.zeros((1, BLK), I32))
        cut_scr[...] = jnp.broadcast_to(cut, cut_scr.shape)

    cut = cut_scr[0:1, :]

    def mask_body(i, c):
        t = sc[i]
        tie_keep = jnp.where(row + i * BLK < cut, 0.0, -jnp.inf)
        sc[i] = jnp.where(t > thr, 0.0, jnp.where(t == thr, tie_keep, -jnp.inf))
        return c

    lax.fori_loop(0, j + 1, mask_body, 0)

    qa = qa_ref[...]
    g_per_kv = N_HEADS_A // N_KV_A
    q_stack = [jnp.concatenate([qa[:, (n * g_per_kv + g) * HEAD_DIM:(n * g_per_kv + g + 1) * HEAD_DIM]
                                for g in range(g_per_kv)], axis=0) for n in range(N_KV_A)]

    def col_max(w):
        return jnp.max(w.reshape(BLK // SUBLANES, SUBLANES, BLK), axis=0)

    def logits_blk(i, near):
        kblk = k16_ref[pl.ds(pl.multiple_of(i * BLK, BLK), BLK), :]
        nm = sc[i]
        mx = []
        for n in range(N_KV_A):
            lg = _nt_dot(kblk[:, n * HEAD_DIM:(n + 1) * HEAD_DIM], q_stack[n])
            for g in range(g_per_kv):
                h = n * g_per_kv + g
                x = lg[:, g * BLK:(g + 1) * BLK] + nm
                if near is not None:
                    x = x + bt[near, h]
                lgs[i, h] = x
                mx.append(col_max(x))
        return mx

    def far_body(i, m8):
        return tuple(jnp.maximum(a, c) for a, c in zip(m8, logits_blk(i, None)))

    neg8 = jnp.full((SUBLANES, BLK), NEG_BIG, F32)
    m8 = lax.fori_loop(0, jnp.maximum(j - 1, 0), far_body, (neg8,) * N_HEADS_A)
    for h in range(N_HEADS_A):
        m8_scr[h] = m8[h]

    @pl.when(j >= 1)
    def _prev_block():
        mx = logits_blk(j - 1, 1)
        for h in range(N_HEADS_A):
            m8_scr[h] = jnp.maximum(m8_scr[h], mx[h])

    mx = logits_blk(j, 0)
    m_row = [jnp.max(jnp.maximum(m8_scr[h], mx[h]), axis=0, keepdims=True) for h in range(N_HEADS_A)]

    acc_scr[...] = jnp.zeros(acc_scr.shape, F32)

    def pv_body(i, l8):
        vti = vt[i]
        l8 = list(l8)
        for n in range(N_KV_A):
            ps = []
            for g in range(g_per_kv):
                h = n * g_per_kv + g
                p = jnp.exp(lgs[i, h] - m_row[h])
                l8[h] = l8[h] + col_sum(p)
                ps.append(p.astype(BF16))
            acc_scr[n] += _dot(vti[n * HEAD_DIM:(n + 1) * HEAD_DIM, :], jnp.concatenate(ps, axis=1))
        return tuple(l8)

    l8 = lax.fori_loop(0, j + 1, pv_body, (z8,) * N_HEADS_A)

    parts = []
    for n in range(N_KV_A):
        acc = acc_scr[n]
        for g in range(g_per_kv):
            h = n * g_per_kv + g
            parts.append(acc[:, g * BLK:(g + 1) * BLK] * (1.0 / jnp.sum(l8[h], axis=0, keepdims=True)))
    oa_ref[...] = jnp.concatenate(parts, axis=0).T


def _dsa_prompt(p, rel_bias, batch, seq):
    nblk = seq // BLK
    topk = min(TOPK_MAX, seq // 4)
    search_blk = 4 if nblk % 4 == 0 else 1
    score_blk_n = 2 if nblk % 2 == 0 else 1
    qrow = lambda w: pl.BlockSpec((BLK, w), lambda b, j: (b * nblk + j, 0))
    brow = lambda w: pl.BlockSpec((seq, w), lambda b, j: (b, 0))
    return pl.pallas_call(
        functools.partial(_dsa_prompt_kernel, topk=topk, search_blk=search_blk, score_blk_n=score_blk_n),
        grid=(batch, nblk),
        in_specs=[
            pl.BlockSpec(memory_space=pltpu.SMEM),
            qrow(N_IDX_HEADS * D_IDX), qrow(LANES), brow(LANES), qrow(D_A), brow(LANES), brow(LANES),
        ],
        out_specs=qrow(D_A),
        out_shape=jax.ShapeDtypeStruct((batch * seq, D_A), F32),
        scratch_shapes=[
            pltpu.VMEM((nblk, BLK, BLK), F32),
            pltpu.VMEM((2, N_HEADS_A, BLK, BLK), F32),
            pltpu.VMEM((nblk, BLK, BLK), BF16),
            pltpu.VMEM((nblk, N_HEADS_A, BLK, BLK), F32),
            pltpu.VMEM((N_HEADS_A, SUBLANES, BLK), F32),
            pltpu.VMEM((N_KV_A, HEAD_DIM, (N_HEADS_A // N_KV_A) * BLK), F32),
            pltpu.VMEM((SUBLANES, BLK), I32),
        ],
        compiler_params=pltpu.CompilerParams(dimension_semantics=("arbitrary", "arbitrary"),
                                             vmem_limit_bytes=VMEM_LIMIT),
        name="dsa_prompt",
    )(rel_bias, p["qi"], p["kw"], p["kw16"], p["qa"], p["k16"], p["v16"])


def _split3_dot(x, b16):
    hi = x.astype(BF16)
    r1 = x - hi.astype(F32)
    mid = r1.astype(BF16)
    lo = (r1 - mid.astype(F32)).astype(BF16)
    return _dot(hi, b16) + _dot(mid, b16) + _dot(lo, b16)


def _retention_kernel(q_ref, k_ref, v_ref, s0_ref, dmat_ref, din_ref, dout_ref, decm_ref, bdm_ref, bones_ref,
                      yn_ref, s_out_ref, s_scr):
    c = pl.program_id(1)

    @pl.when(c == 0)
    def _load_state():
        s_scr[...] = s0_ref[0]

    q = q_ref[0]
    k = k_ref[0]
    v = v_ref[0]
    qd = (q.astype(F32) * din_ref[...]).astype(BF16)
    kdt = (k.astype(F32) * dout_ref[...]).T
    lane = lax.broadcasted_iota(I32, (q.shape[0], LANES), 1)
    lo = lane < HEAD_DIM
    zero = jnp.zeros((), BF16)
    bones = bones_ref[...]
    for p in range(N_HEADS_B // 2):
        sl = slice(p * LANES, (p + 1) * LANES)
        qp, kp, vp = q[:, sl], k[:, sl], v[:, sl]
        s_a = _nt_dot(jnp.where(lo, qp, zero), kp) * dmat_ref[2 * p]
        s_b = _nt_dot(jnp.where(lo, zero, qp), kp) * dmat_ref[2 * p + 1]
        intra = (_dot(s_a.astype(BF16), jnp.where(lo, vp, zero))
                 + _dot(s_b.astype(BF16), jnp.where(lo, zero, vp)))
        s_p = s_scr[p]
        o = _dot(qd[:, sl], s_p.astype(BF16)) + intra
        mu = _split3_dot(o, bones)
        dlt = o - mu
        var = _split3_dot(dlt * dlt, bones)
        yn_ref[0, :, sl] = dlt * lax.rsqrt(var + EPS)
        upd = _dot(kdt[sl, :].astype(BF16), vp)
        s_scr[p] = s_p * decm_ref[p] + upd * bdm_ref[...]

    @pl.when(c == pl.num_programs(1) - 1)
    def _store_state():
        s_out_ref[0] = s_scr[...]


def _ret_tables(chunk, n_real):
    h = N_HEADS_B
    log_g = jnp.log1p(-jnp.exp2(-5.0 - jnp.arange(h, dtype=F32)))
    i = jnp.arange(chunk, dtype=F32)
    diff = i[:, None] - i[None, :]
    dmat = jnp.where(diff >= 0, jnp.exp(jnp.maximum(diff, 0.0)[None] * log_g[:, None, None]), 0.0)
    dec_in = jnp.exp((i + 1.0)[:, None] * log_g[None, :])
    dec_out = jnp.exp((n_real - 1.0 - i)[:, None] * log_g[None, :])
    dec_chunk = jnp.exp(n_real * log_g)
    din = jnp.repeat(dec_in, HEAD_DIM, axis=1)
    dout = jnp.repeat(dec_out, HEAD_DIM, axis=1)
    head_of = np.arange(LANES) // HEAD_DIM
    bd = jnp.asarray((head_of[:, None] == head_of[None, :]).astype(np.float32))
    dc = jnp.repeat(dec_chunk, HEAD_DIM).reshape(h // 2, LANES)
    decm = dc[:, :, None] * bd[None]
    bones = (bd / HEAD_DIM).astype(BF16)
    return dmat, din, dout, decm, bd, bones


def _retention(q3, k3, v3, s0, n_real):
    batch, seq, _ = q3.shape
    chunk = min(RET_CHUNK, seq)
    n = seq // chunk
    dmat, din, dout, decm, bd, bones = _ret_tables(chunk, n_real)
    tok = pl.BlockSpec((1, chunk, D_B), lambda b, c: (b, c, 0))
    st = pl.BlockSpec((1, N_HEADS_B // 2, LANES, LANES), lambda b, c: (b, 0, 0, 0))
    full = lambda a: pl.BlockSpec(a.shape, lambda b, c: (0,) * a.ndim)
    return pl.pallas_call(
        _retention_kernel,
        grid=(batch, n),
        in_specs=[tok, tok, tok, st, full(dmat), full(din), full(dout), full(decm), full(bd), full(bones)],
        out_specs=[tok, st],
        out_shape=[jax.ShapeDtypeStruct((batch, seq, D_B), F32),
                   jax.ShapeDtypeStruct((batch, N_HEADS_B // 2, LANES, LANES), F32)],
        scratch_shapes=[pltpu.VMEM((N_HEADS_B // 2, LANES, LANES), F32)],
        compiler_params=pltpu.CompilerParams(dimension_semantics=("arbitrary", "arbitrary"),
                                             vmem_limit_bytes=VMEM_LIMIT),
        name="retention",
    )(q3, k3, v3, s0, dmat, din, dout, decm, bd, bones)


def _state_to_pairs(s):
    b = s.shape[0]
    s = s.reshape(b, N_HEADS_B // 2, 2, HEAD_DIM, HEAD_DIM)
    z = jnp.zeros_like(s[:, :, 0])
    top = jnp.concatenate([s[:, :, 0], z], axis=-1)
    bot = jnp.concatenate([z, s[:, :, 1]], axis=-1)
    return jnp.concatenate([top, bot], axis=-2)


def _pairs_to_state(sp):
    b = sp.shape[0]
    a = sp[:, :, :HEAD_DIM, :HEAD_DIM]
    d = sp[:, :, HEAD_DIM:, HEAD_DIM:]
    return jnp.stack([a, d], axis=2).reshape(b, N_HEADS_B, HEAD_DIM, HEAD_DIM)


def _merge_kernel(x_ref, oa_ref, yn_ref, ga_ref, gb_ref, p_ref, wo_ref, gp_ref, wup_ref, wg_ref, out_ref):
    ga = ga_ref[...]
    gb = gb_ref[...]
    ya = (ga * _sigmoid(ga) * oa_ref[...]).astype(BF16)
    yb = (gb * _sigmoid(gb) * yn_ref[...]).astype(BF16)
    y = _dot(ya, wo_ref[0:D_A, :]) + _dot(yb, wo_ref[D_A:D_A + D_B, :])
    ms = jnp.mean(y * y, axis=-1, keepdims=True)
    x1 = x_ref[...] + y * lax.rsqrt(ms + EPS) * gp_ref[...]
    ple = _dot(p_ref[...].astype(BF16), wup_ref[...])
    gate = _sigmoid(_dot(x1.astype(BF16), wg_ref[...]))
    out_ref[...] = x1 + ple * gate


def _merge(x2d, oa, yn, ga, gb, p2d, wo16, g_post, wup16, wg16, tm):
    rows, d_model = x2d.shape
    row = lambda w: pl.BlockSpec((tm, w), lambda i: (i, 0))
    full = lambda a: pl.BlockSpec(a.shape, lambda i: (0, 0))
    return pl.pallas_call(
        _merge_kernel,
        grid=(rows // tm,),
        in_specs=[row(d_model), row(D_A), row(D_B), row(D_A), row(D_B), row(p2d.shape[1]),
                  full(wo16), full(g_post), full(wup16), full(wg16)],
        out_specs=row(d_model),
        out_shape=jax.ShapeDtypeStruct((rows, d_model), F32),
        compiler_params=pltpu.CompilerParams(dimension_semantics=("arbitrary",), vmem_limit_bytes=VMEM_LIMIT),
        name="merge",
    )(x2d, oa, yn, ga, gb, p2d, wo16, g_post, wup16, wg16)


def _sample_index_kernel(pt_ref, qi_ref, w_ref, *rest, n_pg):
    pages = rest[:n_pg]
    out_ref = rest[n_pg]
    qi = qi_ref[0]
    w = w_ref[0]
    n_q = qi.shape[0] // N_IDX_HEADS
    for m in range(n_pg):
        s = _dot(qi, pages[m][0].astype(BF16))
        r = jnp.maximum(s, 0.0) * w
        out_ref[0, :, m * PAGE_SIZE:(m + 1) * PAGE_SIZE] = jnp.sum(
            r.reshape(n_q, N_IDX_HEADS, PAGE_SIZE), axis=1)


def _sample_index(page_table, qi_qh, w_qh, cache_kidx, n_pg):
    batch, n_pages = page_table.shape
    n_q = qi_qh.shape[1] // N_IDX_HEADS
    page_spec = lambda m: pl.BlockSpec((1, D_IDX, PAGE_SIZE),
                                       lambda b, c, pt, m=m: (pt[b, c * n_pg + m], 0, 0))
    grid_spec = pltpu.PrefetchScalarGridSpec(
        num_scalar_prefetch=1,
        grid=(batch, n_pages // n_pg),
        in_specs=[pl.BlockSpec((1,) + qi_qh.shape[1:], lambda b, c, pt: (b, 0, 0)),
                  pl.BlockSpec((1,) + w_qh.shape[1:], lambda b, c, pt: (b, 0, 0))]
                 + [page_spec(m) for m in range(n_pg)],
        out_specs=pl.BlockSpec((1, n_q, n_pg * PAGE_SIZE), lambda b, c, pt: (b, 0, c)),
    )
    return pl.pallas_call(
        functools.partial(_sample_index_kernel, n_pg=n_pg),
        grid_spec=grid_spec,
        out_shape=jax.ShapeDtypeStruct((batch, n_q, n_pages * PAGE_SIZE), F32),
        compiler_params=pltpu.CompilerParams(dimension_semantics=("arbitrary", "arbitrary"),
                                             vmem_limit_bytes=VMEM_LIMIT),
        name="sample_index",
    )(page_table, qi_qh, w_qh, *([cache_kidx] * n_pg))


def _sample_select_kernel(sp_ref, qi_ref, w_ref, kw16_ref, thr_ref, cut_ref, nmn_ref, snew_scr,
                          *, topk, n_q, lane_chunk):
    rows, past = sp_ref.shape
    r_i = lax.broadcasted_iota(I32, (rows, LANES), 0)
    l_i = lax.broadcasted_iota(I32, (rows, LANES), 1)

    s = _nt_dot(qi_ref[...], kw16_ref[...][:, :D_IDX])
    r = jnp.maximum(s, 0.0) * w_ref[...]
    s_new = jnp.sum(r.reshape(rows, N_IDX_HEADS, LANES), axis=1)
    ok = ((l_i // n_q) == (r_i // n_q)) & ((l_i % n_q) <= (r_i % n_q))
    snew_scr[...] = jnp.where(ok, s_new, -jnp.inf)

    n_chunks = past // lane_chunk
    tiles = lane_chunk // LANES
    kf = float(topk)

    def count(ind):
        acc = ind(snew_scr[...], l_i + past)
        for ch in range(n_chunks):
            x = sp_ref[:, ch * lane_chunk:(ch + 1) * lane_chunk]
            for t in range(tiles):
                idx = l_i + (ch * lane_chunk + t * LANES)
                acc = acc + ind(x[:, t * LANES:(t + 1) * LANES], idx)
        return jnp.broadcast_to(jnp.sum(acc, axis=1, keepdims=True), (rows, LANES))

    c0 = count(lambda x, idx: jnp.where(x >= 0.0, 1.0, 0.0))
    key0 = jnp.where(c0 >= kf, 0, INT_MIN).astype(I32)

    def bit_body(it, key):
        cand = key | jnp.left_shift(jnp.int32(1), 30 - it)
        cf = _key_to_float(cand)
        return jnp.where(count(lambda x, idx: jnp.where(x >= cf, 1.0, 0.0)) >= kf, cand, key)

    key = lax.fori_loop(0, 31, bit_body, key0)
    key = jnp.maximum(key, KEY_NEG_FLT_MAX)
    thr = _key_to_float(key)
    cnt_gt = count(lambda x, idx: jnp.where(x > thr, 1.0, 0.0))
    cnt_ge = count(lambda x, idx: jnp.where(x >= thr, 1.0, 0.0))
    need = kf - cnt_gt
    straddle = jnp.max(jnp.where(cnt_ge > kf, 1.0, 0.0))

    idx_bits = (past + LANES).bit_length()
    thr_ref[...] = thr
    cut_ref[...] = jnp.full((rows, LANES), 1 << idx_bits, I32)

    @pl.when(straddle > 0.0)
    def _tie_cut():
        def cut_body(it, cut):
            cand = cut | jnp.left_shift(jnp.int32(1), (idx_bits - 1) - it)
            cnt = count(lambda x, idx: jnp.where(x == thr, jnp.where(idx < cand, 1.0, 0.0), 0.0))
            return jnp.where(cnt <= need, cand, cut)
        cut_ref[...] = lax.fori_loop(0, idx_bits, cut_body, jnp.zeros((rows, LANES), I32))

    cut = cut_ref[...]
    sn = snew_scr[...]
    tie_keep = jnp.where(l_i + past < cut, 0.0, -jnp.inf)
    nmn_ref[...] = jnp.where(sn > thr, 0.0, jnp.where(sn == thr, tie_keep, -jnp.inf))


def _sample_select(scores_past, qi_rows, w_rows, kw16_s, topk, n_q):
    rows, past = scores_past.shape
    lane_chunk = 2048 if past % 2048 == 0 else LANES
    full = lambda a: pl.BlockSpec(a.shape, lambda i: (0,) * a.ndim)
    o = jax.ShapeDtypeStruct((rows, LANES), F32)
    return pl.pallas_call(
        functools.partial(_sample_select_kernel, topk=topk, n_q=n_q, lane_chunk=lane_chunk),
        grid=(1,),
        in_specs=[full(scores_past), full(qi_rows), full(w_rows), full(kw16_s)],
        out_specs=[pl.BlockSpec((rows, LANES), lambda i: (0, 0))] * 3,
        out_shape=[o, jax.ShapeDtypeStruct((rows, LANES), I32), o],
        scratch_shapes=[pltpu.VMEM((rows, LANES), F32)],
        compiler_params=pltpu.CompilerParams(dimension_semantics=("arbitrary",), vmem_limit_bytes=VMEM_LIMIT),
        name="sample_select",
    )(scores_past, qi_rows, w_rows, kw16_s)


def _sample_attn_kernel(pt_ref, rb_ref, q_ref, sp_ref, thr_ref, cut_ref, nmn_ref, k16n_ref, v16n_ref, *rest,
                        n_pg, n_q, past):
    kpages = rest[:n_pg]
    vpages = rest[n_pg:2 * n_pg]
    o_ref = rest[2 * n_pg]
    m_scr, l_scr, acc_scr = rest[2 * n_pg + 1:]
    b = pl.program_id(0)
    c = pl.program_id(1)
    g_per_kv = N_HEADS_A // N_KV_A
    rows_n = g_per_kv * n_q
    rep = lambda a: jnp.concatenate([a] * g_per_kv, axis=0)

    @pl.when(c == 0)
    def _init():
        m_scr[...] = jnp.full(m_scr.shape, NEG_BIG, F32)
        l_scr[...] = jnp.zeros(l_scr.shape, F32)
        acc_scr[...] = jnp.zeros(acc_scr.shape, F32)

    thr = rep(thr_ref[0])
    cut = rep(cut_ref[0])
    qpos = past + rep(lax.broadcasted_iota(I32, (n_q, LANES), 0))
    lane = lax.broadcasted_iota(I32, (rows_n, LANES), 1)

    def head_bias(dist, n):
        bkt = _t5_bucket(dist)
        return jnp.concatenate([_bias_from_bucket(bkt[g * n_q:(g + 1) * n_q], rb_ref, n * g_per_kv + g)
                                for g in range(g_per_kv)], axis=0)

    def far_bias(n):
        return jnp.concatenate([jnp.full((n_q, LANES), rb_ref[N_BUCKETS - 1, n * g_per_kv + g], F32)
                                for g in range(g_per_kv)], axis=0)

    def softmax_step(n, xs, vs, v_transposed):
        sl = slice(n * rows_n, (n + 1) * rows_n)
        mx = xs[0]
        for x in xs[1:]:
            mx = jnp.maximum(mx, x)
        m_old = m_scr[sl, :]
        m_new = jnp.maximum(m_old, jnp.max(mx, axis=1, keepdims=True))
        alpha = jnp.exp(m_old - m_new)
        lsum = jnp.zeros((rows_n, LANES), F32)
        pv = jnp.zeros((rows_n, HEAD_DIM), F32)
        for x, v in zip(xs, vs):
            p = jnp.exp(x - m_new)
            lsum = lsum + p
            pv = pv + (_nt_dot(p.astype(BF16), v) if v_transposed else _dot(p.astype(BF16), v))
        l_scr[sl, :] = l_scr[sl, :] * alpha + jnp.sum(lsum, axis=1, keepdims=True)
        acc_scr[sl, :] = acc_scr[sl, :] * alpha[:, :HEAD_DIM] + pv
        m_scr[sl, :] = m_new

    q = q_ref[0]
    nms = []
    for m in range(n_pg):
        sc = rep(sp_ref[0, :, m * PAGE_SIZE:(m + 1) * PAGE_SIZE])
        idx = lane + (c * n_pg + m) * PAGE_SIZE
        tie_keep = jnp.where(idx < cut, 0.0, -jnp.inf)
        nms.append(jnp.where(sc > thr, 0.0, jnp.where(sc == thr, tie_keep, -jnp.inf)))
    for n in range(N_KV_A):
        qn = q[n * rows_n:(n + 1) * rows_n]
        xs, vs = [], []
        for m in range(n_pg):
            kp = kpages[m][0][n * HEAD_DIM:(n + 1) * HEAD_DIM, :].astype(BF16)
            lg = _dot(qn, kp)
            if m == n_pg - 1:
                bias = head_bias(qpos - (lane + (c * n_pg + m) * PAGE_SIZE), n)
            else:
                bias = far_bias(n)
            xs.append(lg + bias + nms[m])
            vs.append(vpages[m][0][n * HEAD_DIM:(n + 1) * HEAD_DIM, :].astype(BF16))
        softmax_step(n, xs, vs, True)

    @pl.when(c == pl.num_programs(1) - 1)
    def _finish():
        nmn = rep(nmn_ref[0])
        dist = rep(lax.broadcasted_iota(I32, (n_q, LANES), 0)) - (lane % n_q)
        for n in range(N_KV_A):
            qn = q[n * rows_n:(n + 1) * rows_n]
            kn = k16n_ref[...][:, n * HEAD_DIM:(n + 1) * HEAD_DIM]
            x = _nt_dot(qn, kn) + head_bias(dist, n) + nmn
            softmax_step(n, [x], [v16n_ref[...][:, n * HEAD_DIM:(n + 1) * HEAD_DIM]], False)
        o_ref[0] = acc_scr[...] * (1.0 / l_scr[...][:, :HEAD_DIM])


def _sample_attn(page_table, rel_bias, q_hq, scores_past3, thr3, cut3, nmn3, k16n, v16n, cache_k, cache_v, n_pg):
    batch, n_pages = page_table.shape
    n_q = scores_past3.shape[1]
    past = n_pages * PAGE_SIZE
    rows = q_hq.shape[1]
    page_spec = lambda m: pl.BlockSpec((1, PAGE_SIZE, LANES),
                                       lambda b, c, pt, m=m: (pt[b, c * n_pg + m], 0, 0))
    per_b = lambda a: pl.BlockSpec((1,) + a.shape[1:], lambda b, c, pt: (b,) + (0,) * (a.ndim - 1))
    full = lambda a: pl.BlockSpec(a.shape, lambda b, c, pt: (0,) * a.ndim)
    grid_spec = pltpu.PrefetchScalarGridSpec(
        num_scalar_prefetch=1,
        grid=(batch, n_pages // n_pg),
        in_specs=[pl.BlockSpec(memory_space=pltpu.SMEM), per_b(q_hq),
                  pl.BlockSpec((1, n_q, n_pg * PAGE_SIZE), lambda b, c, pt: (b, 0, c)),
                  per_b(thr3), per_b(cut3), per_b(nmn3), full(k16n), full(v16n)]
                 + [page_spec(m) for m in range(n_pg)] * 2,
        out_specs=pl.BlockSpec((1, rows, HEAD_DIM), lambda b, c, pt: (b, 0, 0)),
        scratch_shapes=[pltpu.VMEM((rows, LANES), F32), pltpu.VMEM((rows, LANES), F32),
                        pltpu.VMEM((rows, HEAD_DIM), F32)],
    )
    return pl.pallas_call(
        functools.partial(_sample_attn_kernel, n_pg=n_pg, n_q=n_q, past=past),
        grid_spec=grid_spec,
        out_shape=jax.ShapeDtypeStruct((batch, rows, HEAD_DIM), F32),
        compiler_params=pltpu.CompilerParams(dimension_semantics=("arbitrary", "arbitrary"),
                                             vmem_limit_bytes=VMEM_LIMIT),
        name="sample_attn",
    )(page_table, rel_bias, q_hq, scores_past3, thr3, cut3, nmn3, k16n, v16n,
      *([cache_k] * n_pg), *([cache_v] * n_pg))


def _rope_tables(pos):
    half = HEAD_DIM // 2
    inv = ROPE_BASE ** (-jnp.arange(half, dtype=F32) / half)
    ang = pos.astype(F32)[:, None] * inv[None, :]
    cos = jnp.cos(ang)
    sin = jnp.sin(ang)
    reps = LANES // HEAD_DIM
    cos_t = jnp.tile(jnp.concatenate([cos, cos], axis=1), (1, reps))
    sin_t = jnp.tile(jnp.concatenate([-sin, sin], axis=1), (1, reps))
    return cos_t, sin_t


def _cat_weight(w_in):
    offs = np.cumsum((0,) + SPLIT_SIZES)
    qa, ka, va, qi, ki, wi, ga, qb, kb, vb, gb = [w_in[:, offs[i]:offs[i + 1]] for i in range(len(SPLIT_SIZES))]
    pad = jnp.zeros((w_in.shape[0], LANES - D_IDX - N_IDX_HEADS), w_in.dtype)
    return jnp.concatenate([qa, ka, va, qi, ga, qb, kb, vb, gb, ki, wi, pad], axis=1).astype(BF16)


def _pages_t(cache):
    pool, page = cache.shape[:2]
    return jnp.transpose(cache, (0, 2, 3, 1)).reshape(pool, -1, page)


def _largest_divisor(n, cap):
    d = cap
    while n % d:
        d //= 2
    return d


def kernel(x_prompt, x_sample, cache_k, cache_v, cache_kidx, state_ret, page_table, p_prompt, p_sample,
           rel_bias, w_in, w_out, g_pre, g_post, w_ple_up, w_ple_gate):
    batch, seq, d_model = x_prompt.shape
    dec_b, dec_t, _ = x_sample.shape
    depth = w_in.shape[0]
    n_pages = page_table.shape[1]
    past = n_pages * PAGE_SIZE
    rows_s = dec_b * dec_t
    assert depth == 1 and rows_s == LANES and seq % BLK == 0

    w_cat = _cat_weight(w_in[0])
    gpre = g_pre[0].reshape(1, d_model)
    gpost = g_post[0].reshape(1, d_model)
    wo16 = w_out[0].astype(BF16)
    wup16 = w_ple_up[0].astype(BF16)
    wg16 = w_ple_gate[0].astype(BF16)

    tm = _largest_divisor(seq, 512)
    cos_p, sin_p = _rope_tables(jnp.arange(seq))
    xp2 = x_prompt.reshape(batch * seq, d_model)
    pp = _project(xp2, gpre, w_cat, cos_p, sin_p, tm)
    oa_p = _dsa_prompt(pp, rel_bias, batch, seq)
    r3 = lambda a: a.reshape(batch, seq, D_B)
    s0_p = jnp.zeros((batch, N_HEADS_B // 2, LANES, LANES), F32)
    yn_p, sp_pairs = _retention(r3(pp["qb"]), r3(pp["kb"]), r3(pp["vb"]), s0_p, float(min(RET_CHUNK, seq)))
    y_prompt = _merge(xp2, oa_p, yn_p.reshape(batch * seq, D_B), pp["ga"], pp["gb"],
                      p_prompt[0].reshape(batch * seq, -1), wo16, gpost, wup16, wg16, tm)

    pos_s = past + jnp.arange(dec_t)
    cos_s, sin_s = _rope_tables(jnp.tile(pos_s, dec_b))
    xs2 = x_sample.reshape(rows_s, d_model)
    ps = _project(xs2, gpre, w_cat, cos_s, sin_s, rows_s)
    topk_s = min(TOPK_MAX, (past + dec_t) // 4)

    qi_rows = ps["qi"].reshape(rows_s * N_IDX_HEADS, D_IDX)
    w_rows = jnp.broadcast_to(ps["kw"][:, D_IDX:D_IDX + N_IDX_HEADS].reshape(rows_s * N_IDX_HEADS, 1),
                              (rows_s * N_IDX_HEADS, LANES))
    n_pg = _largest_divisor(n_pages, 16)
    scores_past = _sample_index(page_table, qi_rows.reshape(dec_b, dec_t * N_IDX_HEADS, D_IDX),
                                w_rows.reshape(dec_b, dec_t * N_IDX_HEADS, LANES),
                                jnp.transpose(cache_kidx[0], (0, 2, 1)), n_pg)
    thr, cut, nmn = _sample_select(scores_past.reshape(rows_s, past), qi_rows, w_rows, ps["kw16"], topk_s, dec_t)

    q_hq = ps["qa"].reshape(dec_b, dec_t, N_HEADS_A, HEAD_DIM).transpose(0, 2, 1, 3).reshape(
        dec_b, N_HEADS_A * dec_t, HEAD_DIM)
    b3 = lambda a: a.reshape(dec_b, dec_t, LANES)
    o_hq = _sample_attn(page_table, rel_bias, q_hq, scores_past, b3(thr), b3(cut), b3(nmn),
                        ps["k16"], ps["v16"], _pages_t(cache_k[0]), _pages_t(cache_v[0]), n_pg)
    oa_s = o_hq.reshape(dec_b, N_HEADS_A, dec_t, HEAD_DIM).transpose(0, 2, 1, 3).reshape(rows_s, D_A)

    chunk_s = RET_CHUNK
    padt = lambda a: jnp.pad(a.reshape(dec_b, dec_t, D_B), ((0, 0), (0, chunk_s - dec_t), (0, 0)))
    yn_s, ss_pairs = _retention(padt(ps["qb"]), padt(ps["kb"]), padt(ps["vb"]),
                                _state_to_pairs(state_ret[0]), float(math.gcd(dec_t, RET_CHUNK)))
    y_sample = _merge(xs2, oa_s, yn_s[:, :dec_t].reshape(rows_s, D_B), ps["ga"], ps["gb"],
                      p_sample[0].reshape(rows_s, -1), wo16, gpost, wup16, wg16, rows_s)

    def kv(a, b, t):
        return a.reshape(1, b, t, N_KV_A, HEAD_DIM)

    return (
        y_prompt.reshape(batch, seq, d_model),
        y_sample.reshape(dec_b, dec_t, d_model),
        kv(pp["ka"], batch, seq), kv(pp["va"], batch, seq),
        pp["kw"][:, :D_IDX].reshape(1, batch, seq, D_IDX),
        _pairs_to_state(sp_pairs)[None].astype(state_ret.dtype),
        kv(ps["ka"], dec_b, dec_t), kv(ps["va"], dec_b, dec_t),
        ps["kw"][:, :D_IDX].reshape(1, dec_b, dec_t, D_IDX),
        _pairs_to_state(ss_pairs)[None].astype(state_ret.dtype),
    )
```

```python
import functools
import math

import jax
import jax.numpy as jnp
import numpy as np
from jax import lax
from jax.experimental import pallas as pl
from jax.experimental.pallas import tpu as pltpu

F32 = jnp.float32
BF16 = jnp.bfloat16
I32 = jnp.int32

HEAD_DIM = 64
N_HEADS_A = 8
N_KV_A = 2
N_IDX_HEADS = 8
D_IDX = 64
TOPK_MAX = 256
N_HEADS_B = 8
D_A = N_HEADS_A * HEAD_DIM
D_B = N_HEADS_B * HEAD_DIM
N_BUCKETS = 32
MAX_DISTANCE = 128
ROPE_BASE = 10000.0
RET_CHUNK = 128
PAGE_SIZE = 128
EPS = 1e-6
SPLIT_SIZES = (D_A, N_KV_A * HEAD_DIM, N_KV_A * HEAD_DIM, N_IDX_HEADS * D_IDX, D_IDX, N_IDX_HEADS,
               D_A, D_B, D_B, D_B, D_B)

LANES = 128
SUBLANES = 8
BLK = 128
VMEM_LIMIT = 48 * 1024 * 1024
VMEM_LIMIT_PAGED = 56 * 1024 * 1024

INT_MIN = -(2 ** 31)
KEY_NEG_FLT_MAX = INT_MIN + 0x00800000
NEG_BIG = -1e30

_C_QA, _C_KA, _C_VA, _C_QI, _C_GA, _C_QB, _C_KB, _C_VB, _C_GB, _C_KW, _C_END = (
    0, 512, 640, 768, 1280, 1792, 2304, 2816, 3328, 3840, 3968)


def _nt_dot(a, b):
    return lax.dot_general(a, b, (((1,), (1,)), ((), ())), preferred_element_type=F32)


def _dot(a, b):
    return jnp.dot(a, b, preferred_element_type=F32)


def _sigmoid(x):
    return 1.0 / (1.0 + jnp.exp(-x))


def _key_to_float(k):
    bits = jnp.where(k >= 0, k, k ^ 0x7FFFFFFF)
    return lax.bitcast_convert_type(bits, F32)


def _t5_bucket(n):
    max_exact = N_BUCKETS // 2
    n = jnp.maximum(n, 0)
    nf = jnp.maximum(n, 1).astype(F32)
    large = max_exact + jnp.floor(jnp.log(nf / max_exact) / math.log(MAX_DISTANCE / max_exact)
                                  * (N_BUCKETS - max_exact)).astype(I32)
    large = jnp.minimum(large, N_BUCKETS - 1)
    return jnp.where(n < max_exact, n, large)


def _bias_from_bucket(bkt, rb_ref, h):
    out = jnp.zeros(bkt.shape, F32)
    for k in range(N_BUCKETS):
        out = jnp.where(bkt == k, rb_ref[k, h], out)
    return out


def _proj_kernel(x_ref, g_ref, w_ref, cos_ref, sin_ref,
                 qa_ref, ka_ref, va_ref, k16_ref, v16_ref, qi_ref, ga_ref, gb_ref,
                 qb_ref, kb_ref, vb_ref, kw_ref, kw16_ref, h_scr):
    x = x_ref[...]
    ms = jnp.mean(x * x, axis=-1, keepdims=True)
    h_scr[...] = (x * lax.rsqrt(ms + EPS) * g_ref[...]).astype(BF16)

    def mm(lo, hi):
        return _dot(h_scr[...], w_ref[:, lo:hi])

    qa_ref[...] = (mm(_C_QA, _C_KA) * (HEAD_DIM ** -0.5)).astype(BF16)
    ka = mm(_C_KA, _C_VA)
    ka_ref[...] = ka
    k16_ref[...] = ka.astype(BF16)
    va = mm(_C_VA, _C_QI)
    va_ref[...] = va
    v16_ref[...] = va.astype(BF16)
    qi_ref[...] = mm(_C_QI, _C_GA).astype(BF16)
    ga_ref[...] = mm(_C_GA, _C_QB)
    gb_ref[...] = mm(_C_GB, _C_KW)
    vb_ref[...] = mm(_C_VB, _C_GB).astype(BF16)

    cos = cos_ref[...]
    sin = sin_ref[...]
    lane = lax.broadcasted_iota(I32, cos.shape, 1)
    first_half = (lane % HEAD_DIM) < (HEAD_DIM // 2)

    def rope(z, scale):
        outs = []
        for g in range(z.shape[1] // LANES):
            zg = z[:, g * LANES:(g + 1) * LANES]
            partner = jnp.where(first_half, pltpu.roll(zg, LANES - HEAD_DIM // 2, 1),
                                pltpu.roll(zg, HEAD_DIM // 2, 1))
            r = zg * cos + partner * sin
            if scale != 1.0:
                r = r * scale
            outs.append(r.astype(BF16))
        return jnp.concatenate(outs, axis=1)

    qb_ref[...] = rope(mm(_C_QB, _C_KB), 1.0)
    kb_ref[...] = rope(mm(_C_KB, _C_VB), HEAD_DIM ** -0.5)

    kw = mm(_C_KW, _C_END)
    wscale = (N_IDX_HEADS ** -0.5) * (D_IDX ** -0.5)
    kw = kw * jnp.where(lane >= D_IDX, wscale, 1.0)
    kw_ref[...] = kw
    kw16_ref[...] = kw.astype(BF16)


def _project(x2d, g_pre, w_cat, cos_t, sin_t, tm):
    rows = x2d.shape[0]
    d_model = x2d.shape[1]
    n_tab = cos_t.shape[0] // tm
    row_spec = lambda w: pl.BlockSpec((tm, w), lambda i: (i, 0))
    outs = [
        ("qa", D_A, BF16), ("ka", LANES, F32), ("va", LANES, F32), ("k16", LANES, BF16), ("v16", LANES, BF16),
        ("qi", N_IDX_HEADS * D_IDX, BF16), ("ga", D_A, F32), ("gb", D_B, F32),
        ("qb", D_B, BF16), ("kb", D_B, BF16), ("vb", D_B, BF16), ("kw", LANES, F32), ("kw16", LANES, BF16),
    ]
    res = pl.pallas_call(
        _proj_kernel,
        grid=(rows // tm,),
        in_specs=[
            row_spec(d_model),
            pl.BlockSpec((1, d_model), lambda i: (0, 0)),
            pl.BlockSpec(w_cat.shape, lambda i: (0, 0)),
            pl.BlockSpec((tm, LANES), lambda i: (i % n_tab, 0)),
            pl.BlockSpec((tm, LANES), lambda i: (i % n_tab, 0)),
        ],
        out_specs=[row_spec(w) for _, w, _ in outs],
        out_shape=[jax.ShapeDtypeStruct((rows, w), dt) for _, w, dt in outs],
        scratch_shapes=[pltpu.VMEM((tm, d_model), BF16)],
        compiler_params=pltpu.CompilerParams(dimension_semantics=("arbitrary",), vmem_limit_bytes=VMEM_LIMIT),
        name="proj",
    )(x2d, g_pre, w_cat, cos_t, sin_t)
    return {n: r for (n, _, _), r in zip(outs, res)}


def _dsa_prompt_kernel(rb_ref, qi_ref, kwq_ref, kw16_ref, qa_ref, k16_ref, v16_ref, oa_ref,
                       sc, bt, vt, lgs, acc_scr, cut_scr, *, topk, search_blk, score_blk_n, att_blk_n):
    b = pl.program_id(0)
    j = pl.program_id(1)
    row = lax.broadcasted_iota(I32, (BLK, BLK), 0)
    col = lax.broadcasted_iota(I32, (BLK, BLK), 1)

    @pl.when((b == 0) & (j == 0))
    def _init_bias():
        for d in range(2):
            bkt = _t5_bucket(col - row + d * BLK)
            for h in range(N_HEADS_A):
                bt[d, h] = _bias_from_bucket(bkt, rb_ref, h) - rb_ref[N_BUCKETS - 1, h]
        bt[2] = jnp.zeros(bt.shape[1:], F32)

    @pl.when(j == 0)
    def _clear_values():
        vt[...] = jnp.zeros(vt.shape, BF16)

    vblk = v16_ref[pl.ds(pl.multiple_of(j * BLK, BLK), BLK), :]
    vt[j] = vblk.astype(F32).T.astype(BF16)

    qi = qi_ref[...]
    qi_stack = jnp.concatenate([qi[:, h * D_IDX:(h + 1) * D_IDX] for h in range(N_IDX_HEADS)], axis=0)
    wi_t = kwq_ref[...].T
    qpos = col + j * BLK

    def score_body(c, carry):
        for u in range(score_blk_n):
            i = c * score_blk_n + u
            kib = kw16_ref[pl.ds(pl.multiple_of(i * BLK, BLK), BLK), :][:, :D_IDX]
            s = _nt_dot(kib, qi_stack)
            acc = jnp.zeros((BLK, BLK), F32)
            for h in range(N_IDX_HEADS):
                acc = acc + wi_t[D_IDX + h:D_IDX + h + 1, :] * jnp.maximum(s[:, h * BLK:(h + 1) * BLK], 0.0)
            sc[i] = jnp.where(row + i * BLK <= qpos, acc, -jnp.inf)
        return carry

    n_score = (j + score_blk_n) // score_blk_n
    lax.fori_loop(0, n_score, score_body, 0)

    n_chunk = (j + search_blk) // search_blk

    def fill_body(i, c):
        sc[i] = jnp.full((BLK, BLK), -jnp.inf, F32)
        return c

    lax.fori_loop(n_score * score_blk_n, n_chunk * search_blk, fill_body, 0)

    def col_sum(w):
        return jnp.sum(w.reshape(BLK // SUBLANES, SUBLANES, BLK), axis=0)

    def count_ge(cand):
        def body(c, acc):
            for u in range(search_blk):
                acc = acc + col_sum(jnp.where(sc[c * search_blk + u] >= cand, 1.0, 0.0))
            return acc
        acc = lax.fori_loop(0, n_chunk, body, jnp.zeros((SUBLANES, BLK), F32))
        return jnp.sum(acc, axis=0, keepdims=True)

    kf = float(topk)
    c0 = count_ge(jnp.zeros((1, BLK), F32))
    key0 = jnp.where(c0 >= kf, 0, INT_MIN).astype(I32)

    def bit_body(it, key):
        cand = key | jnp.left_shift(jnp.int32(1), 30 - it)
        cnt = count_ge(_key_to_float(cand))
        return jnp.where(cnt >= kf, cand, key)

    key = lax.fori_loop(0, 31, bit_body, key0)
    key = jnp.maximum(key, KEY_NEG_FLT_MAX)
    thr = _key_to_float(key)

    def count2_body(c, accs):
        a_gt, a_ge = accs
        for u in range(search_blk):
            t = sc[c * search_blk + u]
            a_gt = a_gt + col_sum(jnp.where(t > thr, 1.0, 0.0))
            a_ge = a_ge + col_sum(jnp.where(t >= thr, 1.0, 0.0))
        return a_gt, a_ge

    z8 = jnp.zeros((SUBLANES, BLK), F32)
    a_gt, a_ge = lax.fori_loop(0, n_chunk, count2_body, (z8, z8))
    cnt_gt = jnp.sum(a_gt, axis=0, keepdims=True)
    cnt_ge = jnp.sum(a_ge, axis=0, keepdims=True)
    need = kf - cnt_gt
    straddle = jnp.max(jnp.where(cnt_ge > kf, 1.0, 0.0))

    n_keys_max = sc.shape[0] * BLK
    cut_scr[...] = jnp.full(cut_scr.shape, 2 * n_keys_max, I32)

    @pl.when(straddle > 0.0)
    def _tie_cut():
        def count_tie(cand):
            def body(i, acc):
                t = sc[i]
                hit = jnp.where(t == thr, jnp.where(row + i * BLK < cand, 1.0, 0.0), 0.0)
                return acc + col_sum(hit)
            acc = lax.fori_loop(0, j + 1, body, z8)
            return jnp.sum(acc, axis=0, keepdims=True)

        def cut_body(it, cut):
            cand = cut | jnp.left_shift(jnp.int32(1), (n_keys_max.bit_length() - 1) - it)
            return jnp.where(count_tie(cand) <= need, cand, cut)

        cut = lax.fori_loop(0, n_keys_max.bit_length(), cut_body, jnp.zeros((1, BLK), I32))
        cut_scr[...] = jnp.broadcast_to(cut, cut_scr.shape)

    cut = cut_scr[0:1, :]

    def mask_body(i, c):
        t = sc[i]
        tie_keep = jnp.where(row + i * BLK < cut, 0.0, -jnp.inf)
        sc[i] = jnp.where(t > thr, 0.0, jnp.where(t == thr, tie_keep, -jnp.inf))
        return c

    lax.fori_loop(0, j + 1, mask_body, 0)

    qa = qa_ref[...]
    g_per_kv = N_HEADS_A // N_KV_A
    q_stack = [jnp.concatenate([qa[:, (n * g_per_kv + g) * HEAD_DIM:(n * g_per_kv + g + 1) * HEAD_DIM]
                                for g in range(g_per_kv)], axis=0) for n in range(N_KV_A)]

    def col_max(w):
        return jnp.max(w.reshape(BLK // SUBLANES, SUBLANES, BLK), axis=0)

    n_att = (j + att_blk_n) // att_blk_n

    def logits_body(c, m8):
        m8 = list(m8)
        for u in range(att_blk_n):
            i = c * att_blk_n + u
            kblk = k16_ref[pl.ds(pl.multiple_of(i * BLK, BLK), BLK), :]
            nm = sc[i]
            near = jnp.clip(j - i, 0, 2)
            for n in range(N_KV_A):
                lg = _nt_dot(kblk[:, n * HEAD_DIM:(n + 1) * HEAD_DIM], q_stack[n])
                for g in range(g_per_kv):
                    h = n * g_per_kv + g
                    x = lg[:, g * BLK:(g + 1) * BLK] + nm + bt[near, h]
                    lgs[i, h] = x
                    m8[h] = jnp.maximum(m8[h], col_max(x))
        return tuple(m8)

    neg8 = jnp.full((SUBLANES, BLK), NEG_BIG, F32)
    m8 = lax.fori_loop(0, n_att, logits_body, (neg8,) * N_HEADS_A)
    m_row = [jnp.max(m8[h], axis=0, keepdims=True) for h in range(N_HEADS_A)]

    acc_scr[...] = jnp.zeros(acc_scr.shape, F32)

    def pv_body(c, l8):
        l8 = list(l8)
        for u in range(att_blk_n):
            i = c * att_blk_n + u
            vti = vt[i]
            for n in range(N_KV_A):
                ps = []
                for g in range(g_per_kv):
                    h = n * g_per_kv + g
                    p = jnp.exp(lgs[i, h] - m_row[h])
                    l8[h] = l8[h] + col_sum(p)
                    ps.append(p.astype(BF16))
                acc_scr[n] += _dot(vti[n * HEAD_DIM:(n + 1) * HEAD_DIM, :], jnp.concatenate(ps, axis=1))
        return tuple(l8)

    l8 = lax.fori_loop(0, n_att, pv_body, (z8,) * N_HEADS_A)

    parts = []
    for n in range(N_KV_A):
        acc = acc_scr[n]
        for g in range(g_per_kv):
            h = n * g_per_kv + g
            parts.append(acc[:, g * BLK:(g + 1) * BLK] * (1.0 / jnp.sum(l8[h], axis=0, keepdims=True)))
    oa_ref[...] = jnp.concatenate(parts, axis=0).T


def _dsa_prompt(p, rel_bias, batch, seq):
    nblk = seq // BLK
    topk = min(TOPK_MAX, seq // 4)
    search_blk = 4 if nblk % 4 == 0 else 1
    score_blk_n = search_blk
    att_blk_n = 2 if nblk % 2 == 0 else 1
    qrow = lambda w: pl.BlockSpec((BLK, w), lambda b, j: (b * nblk + j, 0))
    brow = lambda w: pl.BlockSpec((seq, w), lambda b, j: (b, 0))
    return pl.pallas_call(
        functools.partial(_dsa_prompt_kernel, topk=topk, search_blk=search_blk, score_blk_n=score_blk_n,
                          att_blk_n=att_blk_n),
        grid=(batch, nblk),
        in_specs=[
            pl.BlockSpec(memory_space=pltpu.SMEM),
            qrow(N_IDX_HEADS * D_IDX), qrow(LANES), brow(LANES), qrow(D_A), brow(LANES), brow(LANES),
        ],
        out_specs=qrow(D_A),
        out_shape=jax.ShapeDtypeStruct((batch * seq, D_A), F32),
        scratch_shapes=[
            pltpu.VMEM((nblk, BLK, BLK), F32),
            pltpu.VMEM((3, N_HEADS_A, BLK, BLK), F32),
            pltpu.VMEM((nblk, BLK, BLK), BF16),
            pltpu.VMEM((nblk, N_HEADS_A, BLK, BLK), F32),
            pltpu.VMEM((N_KV_A, HEAD_DIM, (N_HEADS_A // N_KV_A) * BLK), F32),
            pltpu.VMEM((SUBLANES, BLK), I32),
        ],
        compiler_params=pltpu.CompilerParams(dimension_semantics=("arbitrary", "arbitrary"),
                                             vmem_limit_bytes=VMEM_LIMIT),
        name="dsa_prompt",
    )(rel_bias, p["qi"], p["kw"], p["kw16"], p["qa"], p["k16"], p["v16"])


def _split3_dot(x, b16):
    hi = x.astype(BF16)
    r1 = x - hi.astype(F32)
    mid = r1.astype(BF16)
    lo = (r1 - mid.astype(F32)).astype(BF16)
    return _dot(hi, b16) + _dot(mid, b16) + _dot(lo, b16)


def _retention_kernel(q_ref, k_ref, v_ref, s0_ref, dmat_ref, din_ref, dout_ref, decm_ref, bdm_ref, bones_ref,
                      yn_ref, s_out_ref, s_scr):
    c = pl.program_id(1)

    @pl.when(c == 0)
    def _load_state():
        s_scr[...] = s0_ref[0]

    q = q_ref[0]
    k = k_ref[0]
    v = v_ref[0]
    qd = (q.astype(F32) * din_ref[...]).astype(BF16)
    kdt = (k.astype(F32) * dout_ref[...]).T
    lane = lax.broadcasted_iota(I32, (q.shape[0], LANES), 1)
    lo = lane < HEAD_DIM
    zero = jnp.zeros((), BF16)
    bones = bones_ref[...]
    for p in range(N_HEADS_B // 2):
        sl = slice(p * LANES, (p + 1) * LANES)
        qp, kp, vp = q[:, sl], k[:, sl], v[:, sl]
        s_a = _nt_dot(jnp.where(lo, qp, zero), kp) * dmat_ref[2 * p]
        s_b = _nt_dot(jnp.where(lo, zero, qp), kp) * dmat_ref[2 * p + 1]
        intra = (_dot(s_a.astype(BF16), jnp.where(lo, vp, zero))
                 + _dot(s_b.astype(BF16), jnp.where(lo, zero, vp)))
        s_p = s_scr[p]
        o = _dot(qd[:, sl], s_p.astype(BF16)) + intra
        mu = _split3_dot(o, bones)
        dlt = o - mu
        var = _split3_dot(dlt * dlt, bones)
        yn_ref[0, :, sl] = dlt * lax.rsqrt(var + EPS)
        upd = _dot(kdt[sl, :].astype(BF16), vp)
        s_scr[p] = s_p * decm_ref[p] + upd * bdm_ref[...]

    s_out_ref[0] = s_scr[...]


def _ret_tables(chunk, n_real):
    h = N_HEADS_B
    log_g = jnp.log1p(-jnp.exp2(-5.0 - jnp.arange(h, dtype=F32)))
    i = jnp.arange(chunk, dtype=F32)
    diff = i[:, None] - i[None, :]
    dmat = jnp.where(diff >= 0, jnp.exp(jnp.maximum(diff, 0.0)[None] * log_g[:, None, None]), 0.0)
    dec_in = jnp.exp((i + 1.0)[:, None] * log_g[None, :])
    dec_out = jnp.exp((n_real - 1.0 - i)[:, None] * log_g[None, :])
    dec_chunk = jnp.exp(n_real * log_g)
    din = jnp.repeat(dec_in, HEAD_DIM, axis=1)
    dout = jnp.repeat(dec_out, HEAD_DIM, axis=1)
    head_of = np.arange(LANES) // HEAD_DIM
    bd = jnp.asarray((head_of[:, None] == head_of[None, :]).astype(np.float32))
    dc = jnp.repeat(dec_chunk, HEAD_DIM).reshape(h // 2, LANES)
    decm = dc[:, :, None] * bd[None]
    bones = (bd / HEAD_DIM).astype(BF16)
    return dmat, din, dout, decm, bd, bones


def _retention(q3, k3, v3, s0, n_real):
    batch, seq, _ = q3.shape
    chunk = min(RET_CHUNK, seq)
    n = seq // chunk
    dmat, din, dout, decm, bd, bones = _ret_tables(chunk, n_real)
    tok = pl.BlockSpec((1, chunk, D_B), lambda b, c: (b, c, 0))
    st = pl.BlockSpec((1, N_HEADS_B // 2, LANES, LANES), lambda b, c: (b, 0, 0, 0))
    full = lambda a: pl.BlockSpec(a.shape, lambda b, c: (0,) * a.ndim)
    return pl.pallas_call(
        _retention_kernel,
        grid=(batch, n),
        in_specs=[tok, tok, tok, st, full(dmat), full(din), full(dout), full(decm), full(bd), full(bones)],
        out_specs=[tok, st],
        out_shape=[jax.ShapeDtypeStruct((batch, seq, D_B), F32),
                   jax.ShapeDtypeStruct((batch, N_HEADS_B // 2, LANES, LANES), F32)],
        scratch_shapes=[pltpu.VMEM((N_HEADS_B // 2, LANES, LANES), F32)],
        compiler_params=pltpu.CompilerParams(dimension_semantics=("arbitrary", "arbitrary"),
                                             vmem_limit_bytes=VMEM_LIMIT),
        name="retention",
    )(q3, k3, v3, s0, dmat, din, dout, decm, bd, bones)


def _state_to_pairs(s):
    b = s.shape[0]
    s = s.reshape(b, N_HEADS_B // 2, 2, HEAD_DIM, HEAD_DIM)
    z = jnp.zeros_like(s[:, :, 0])
    top = jnp.concatenate([s[:, :, 0], z], axis=-1)
    bot = jnp.concatenate([z, s[:, :, 1]], axis=-1)
    return jnp.concatenate([top, bot], axis=-2)


def _pairs_to_state(sp):
    b = sp.shape[0]
    a = sp[:, :, :HEAD_DIM, :HEAD_DIM]
    d = sp[:, :, HEAD_DIM:, HEAD_DIM:]
    return jnp.stack([a, d], axis=2).reshape(b, N_HEADS_B, HEAD_DIM, HEAD_DIM)


def _merge_kernel(x_ref, oa_ref, yn_ref, ga_ref, gb_ref, p_ref, wo_ref, gp_ref, wup_ref, wg_ref, out_ref):
    ga = ga_ref[...]
    gb = gb_ref[...]
    ya = (ga * _sigmoid(ga) * oa_ref[...]).astype(BF16)
    yb = (gb * _sigmoid(gb) * yn_ref[...]).astype(BF16)
    y = _dot(ya, wo_ref[0:D_A, :]) + _dot(yb, wo_ref[D_A:D_A + D_B, :])
    ms = jnp.mean(y * y, axis=-1, keepdims=True)
    x1 = x_ref[...] + y * lax.rsqrt(ms + EPS) * gp_ref[...]
    ple = _dot(p_ref[...].astype(BF16), wup_ref[...])
    gate = _sigmoid(_dot(x1.astype(BF16), wg_ref[...]))
    out_ref[...] = x1 + ple * gate


def _merge(x2d, oa, yn, ga, gb, p2d, wo16, g_post, wup16, wg16, tm):
    rows, d_model = x2d.shape
    row = lambda w: pl.BlockSpec((tm, w), lambda i: (i, 0))
    full = lambda a: pl.BlockSpec(a.shape, lambda i: (0, 0))
    return pl.pallas_call(
        _merge_kernel,
        grid=(rows // tm,),
        in_specs=[row(d_model), row(D_A), row(D_B), row(D_A), row(D_B), row(p2d.shape[1]),
                  full(wo16), full(g_post), full(wup16), full(wg16)],
        out_specs=row(d_model),
        out_shape=jax.ShapeDtypeStruct((rows, d_model), F32),
        compiler_params=pltpu.CompilerParams(dimension_semantics=("arbitrary",), vmem_limit_bytes=VMEM_LIMIT),
        name="merge",
    )(x2d, oa, yn, ga, gb, p2d, wo16, g_post, wup16, wg16)


def _page_copy(pages_hbm, buf, sem, pt_ref, b, p, slot):
    return pltpu.make_async_copy(pages_hbm.at[pt_ref[b, p]], buf.at[slot, p], sem.at[slot])


def _start_batch_pages(streams, pt_ref, b, slot, n_pages):
    def body(p, c):
        for pages_hbm, buf, sem in streams:
            _page_copy(pages_hbm, buf, sem, pt_ref, b, p, slot).start()
        return c
    lax.fori_loop(0, n_pages, body, 0)


def _wait_batch_pages(streams, pt_ref, b, slot, n_pages):
    def body(p, c):
        for pages_hbm, buf, sem in streams:
            _page_copy(pages_hbm, buf, sem, pt_ref, b, p, slot).wait()
        return c
    lax.fori_loop(0, n_pages, body, 0)


def _sample_index_kernel(pt_ref, qi_ref, w_ref, pages_hbm, out_ref, buf, sem, *, n_pages, unroll):
    b = pl.program_id(0)
    slot = b % 2
    streams = [(pages_hbm, buf, sem)]

    @pl.when(b == 0)
    def _first():
        _start_batch_pages(streams, pt_ref, 0, 0, n_pages)

    _wait_batch_pages(streams, pt_ref, b, slot, n_pages)

    qi = qi_ref[0]
    w = w_ref[0]
    n_q = qi.shape[0] // N_IDX_HEADS

    def run(prefetch):
        def body(c, carry):
            for u in range(unroll):
                p = c * unroll + u
                if prefetch:
                    _page_copy(pages_hbm, buf, sem, pt_ref, b + 1, p, 1 - slot).start()
                s = _dot(qi, buf[slot, p].astype(BF16))
                r = jnp.maximum(s, 0.0) * w
                out_ref[0, c, :, u * PAGE_SIZE:(u + 1) * PAGE_SIZE] = jnp.sum(
                    r.reshape(n_q, N_IDX_HEADS, PAGE_SIZE), axis=1)
            return carry
        lax.fori_loop(0, n_pages // unroll, body, 0)

    has_next = b + 1 < pl.num_programs(0)
    pl.when(has_next)(lambda: run(True))
    pl.when(jnp.logical_not(has_next))(lambda: run(False))


def _sample_index(page_table, qi_qh, w_qh, kidx_t, unroll):
    batch, n_pages = page_table.shape
    n_q = qi_qh.shape[1] // N_IDX_HEADS
    groups = n_pages // unroll
    grid_spec = pltpu.PrefetchScalarGridSpec(
        num_scalar_prefetch=1,
        grid=(batch,),
        in_specs=[pl.BlockSpec((1,) + qi_qh.shape[1:], lambda b, pt: (b, 0, 0)),
                  pl.BlockSpec((1,) + w_qh.shape[1:], lambda b, pt: (b, 0, 0)),
                  pl.BlockSpec(memory_space=pl.ANY)],
        out_specs=pl.BlockSpec((1, groups, n_q, unroll * PAGE_SIZE), lambda b, pt: (b, 0, 0, 0)),
        scratch_shapes=[pltpu.VMEM((2, n_pages) + kidx_t.shape[1:], kidx_t.dtype),
                        pltpu.SemaphoreType.DMA((2,))],
    )
    return pl.pallas_call(
        functools.partial(_sample_index_kernel, n_pages=n_pages, unroll=unroll),
        grid_spec=grid_spec,
        out_shape=jax.ShapeDtypeStruct((batch, groups, n_q, unroll * PAGE_SIZE), F32),
        compiler_params=pltpu.CompilerParams(dimension_semantics=("arbitrary",), vmem_limit_bytes=VMEM_LIMIT),
        name="sample_index",
    )(page_table, qi_qh, w_qh, kidx_t)


def _sample_select_kernel(sp_ref, qi_ref, w_ref, kw16_ref, thr_ref, cut_ref, nmn_ref, snew_scr,
                          *, topk, n_q, lane_chunk):
    rows, past = sp_ref.shape
    r_i = lax.broadcasted_iota(I32, (rows, LANES), 0)
    l_i = lax.broadcasted_iota(I32, (rows, LANES), 1)

    s = _nt_dot(qi_ref[...], kw16_ref[...][:, :D_IDX])
    r = jnp.maximum(s, 0.0) * w_ref[...]
    s_new = jnp.sum(r.reshape(rows, N_IDX_HEADS, LANES), axis=1)
    ok = ((l_i // n_q) == (r_i // n_q)) & ((l_i % n_q) <= (r_i % n_q))
    snew_scr[...] = jnp.where(ok, s_new, -jnp.inf)

    n_chunks = past // lane_chunk
    tiles = lane_chunk // LANES
    kf = float(topk)

    def count(ind):
        acc = ind(snew_scr[...], l_i + past)
        for ch in range(n_chunks):
            x = sp_ref[:, ch * lane_chunk:(ch + 1) * lane_chunk]
            for t in range(tiles):
                idx = l_i + (ch * lane_chunk + t * LANES)
                acc = acc + ind(x[:, t * LANES:(t + 1) * LANES], idx)
        return jnp.broadcast_to(jnp.sum(acc, axis=1, keepdims=True), (rows, LANES))

    c0 = count(lambda x, idx: jnp.where(x >= 0.0, 1.0, 0.0))
    key0 = jnp.where(c0 >= kf, 0, INT_MIN).astype(I32)

    def bit_body(it, key):
        cand = key | jnp.left_shift(jnp.int32(1), 30 - it)
        cf = _key_to_float(cand)
        return jnp.where(count(lambda x, idx: jnp.where(x >= cf, 1.0, 0.0)) >= kf, cand, key)

    key = lax.fori_loop(0, 31, bit_body, key0)
    key = jnp.maximum(key, KEY_NEG_FLT_MAX)
    thr = _key_to_float(key)
    cnt_gt = count(lambda x, idx: jnp.where(x > thr, 1.0, 0.0))
    cnt_ge = count(lambda x, idx: jnp.where(x >= thr, 1.0, 0.0))
    need = kf - cnt_gt
    straddle = jnp.max(jnp.where(cnt_ge > kf, 1.0, 0.0))

    idx_bits = (past + LANES).bit_length()
    thr_ref[...] = thr
    cut_ref[...] = jnp.full((rows, LANES), 1 << idx_bits, I32)

    @pl.when(straddle > 0.0)
    def _tie_cut():
        def cut_body(it, cut):
            cand = cut | jnp.left_shift(jnp.int32(1), (idx_bits - 1) - it)
            cnt = count(lambda x, idx: jnp.where(x == thr, jnp.where(idx < cand, 1.0, 0.0), 0.0))
            return jnp.where(cnt <= need, cand, cut)
        cut_ref[...] = lax.fori_loop(0, idx_bits, cut_body, jnp.zeros((rows, LANES), I32))

    cut = cut_ref[...]
    sn = snew_scr[...]
    tie_keep = jnp.where(l_i + past < cut, 0.0, -jnp.inf)
    nmn_ref[...] = jnp.where(sn > thr, 0.0, jnp.where(sn == thr, tie_keep, -jnp.inf))


def _sample_select(scores_past, qi_rows, w_rows, kw16_s, topk, n_q):
    rows, past = scores_past.shape
    lane_chunk = 2048 if past % 2048 == 0 else LANES
    full = lambda a: pl.BlockSpec(a.shape, lambda i: (0,) * a.ndim)
    o = jax.ShapeDtypeStruct((rows, LANES), F32)
    return pl.pallas_call(
        functools.partial(_sample_select_kernel, topk=topk, n_q=n_q, lane_chunk=lane_chunk),
        grid=(1,),
        in_specs=[full(scores_past), full(qi_rows), full(w_rows), full(kw16_s)],
        out_specs=[pl.BlockSpec((rows, LANES), lambda i: (0, 0))] * 3,
        out_shape=[o, jax.ShapeDtypeStruct((rows, LANES), I32), o],
        scratch_shapes=[pltpu.VMEM((rows, LANES), F32)],
        compiler_params=pltpu.CompilerParams(dimension_semantics=("arbitrary",), vmem_limit_bytes=VMEM_LIMIT),
        name="sample_select",
    )(scores_past, qi_rows, w_rows, kw16_s)


def _sample_attn_kernel(pt_ref, rbc_ref, q_ref, sp_ref, thr_ref, cut_ref, nmn_ref, k16n_ref, v16n_ref,
                        k_hbm, v_hbm, o_ref, kbuf, vbuf, ksem, vsem, lgs, bias_scr, mx_scr,
                        *, n_pages, n_q, unroll):
    b = pl.program_id(0)
    slot = b % 2
    rows = n_q * N_HEADS_A
    groups = n_pages // unroll
    streams = [(k_hbm, kbuf, ksem), (v_hbm, vbuf, vsem)]
    lane = lax.broadcasted_iota(I32, (rows, LANES), 1)
    rq = lax.broadcasted_iota(I32, (rows, LANES), 0) // N_HEADS_A
    expand = lambda a: jnp.concatenate(
        [jnp.broadcast_to(a[t:t + 1], (N_HEADS_A, LANES)) for t in range(n_q)], axis=0)

    @pl.when(b == 0)
    def _first():
        _start_batch_pages(streams, pt_ref, 0, 0, n_pages)
        far = jnp.concatenate([rbc_ref[N_BUCKETS - 1]] * n_q, axis=0)
        for t, dist in enumerate((PAGE_SIZE + rq - lane, rq - lane % n_q)):
            bkt = _t5_bucket(dist)
            tile = jnp.zeros((rows, LANES), F32)
            for k in range(N_BUCKETS - 1):
                tile = jnp.where(bkt == k, jnp.concatenate([rbc_ref[k]] * n_q, axis=0) - far, tile)
            bias_scr[t] = tile

    _wait_batch_pages(streams, pt_ref, b, slot, n_pages)

    thr = expand(thr_ref[0])
    cut = expand(cut_ref[0])
    q = q_ref[0]

    def logits_pass(prefetch):
        def body(c, mx):
            for u in range(unroll):
                p = c * unroll + u
                if prefetch:
                    for pages_hbm, buf, sem in streams:
                        _page_copy(pages_hbm, buf, sem, pt_ref, b + 1, p, 1 - slot).start()
                lg = _dot(q, kbuf[slot, p].astype(BF16))
                sc = expand(sp_ref[0, c, :, u * PAGE_SIZE:(u + 1) * PAGE_SIZE])
                tie_keep = jnp.where(lane + p * PAGE_SIZE < cut, 0.0, -jnp.inf)
                x = lg + jnp.where(sc > thr, 0.0, jnp.where(sc == thr, tie_keep, -jnp.inf))
                if u == unroll - 1:
                    x = x + bias_scr[0] * jnp.where(c == groups - 1, 1.0, 0.0)
                lgs[p] = x
                mx = jnp.maximum(mx, x)
            return mx
        mx_scr[...] = lax.fori_loop(0, groups, body, jnp.full((rows, LANES), NEG_BIG, F32))

    has_next = b + 1 < pl.num_programs(0)
    pl.when(has_next)(lambda: logits_pass(True))
    pl.when(jnp.logical_not(has_next))(lambda: logits_pass(False))

    xn = _nt_dot(q, k16n_ref[...]) + bias_scr[1] + expand(nmn_ref[0])
    m = jnp.max(jnp.maximum(mx_scr[...], xn), axis=1, keepdims=True)

    def pv_body(c, carry):
        lsum, acc = carry
        for u in range(unroll):
            p = c * unroll + u
            pr = jnp.exp(lgs[p] - m)
            lsum = lsum + pr
            acc = acc + _nt_dot(pr.astype(BF16), vbuf[slot, p].astype(BF16))
        return lsum, acc

    pn = jnp.exp(xn - m)
    lsum, acc = lax.fori_loop(0, groups, pv_body, (pn, _dot(pn.astype(BF16), v16n_ref[...])))
    out = acc * (1.0 / jnp.sum(lsum, axis=1, keepdims=True))
    head = lax.broadcasted_iota(I32, (rows, HEAD_DIM), 0) % N_HEADS_A
    o_ref[0] = jnp.where(head < N_HEADS_A // N_KV_A, out[:, :HEAD_DIM], out[:, HEAD_DIM:])


def _sample_attn(page_table, rb_col, q_bd, scores4, thr3, cut3, nmn3, k16n, v16n, k_pages_t, v_pages_t, unroll):
    batch, n_pages = page_table.shape
    n_q = scores4.shape[2]
    rows = q_bd.shape[1]
    per_b = lambda a: pl.BlockSpec((1,) + a.shape[1:], lambda b, pt: (b,) + (0,) * (a.ndim - 1))
    full = lambda a: pl.BlockSpec(a.shape, lambda b, pt: (0,) * a.ndim)
    page_buf = pltpu.VMEM((2, n_pages) + k_pages_t.shape[1:], k_pages_t.dtype)
    grid_spec = pltpu.PrefetchScalarGridSpec(
        num_scalar_prefetch=1,
        grid=(batch,),
        in_specs=[full(rb_col), per_b(q_bd), per_b(scores4), per_b(thr3), per_b(cut3), per_b(nmn3),
                  full(k16n), full(v16n), pl.BlockSpec(memory_space=pl.ANY), pl.BlockSpec(memory_space=pl.ANY)],
        out_specs=pl.BlockSpec((1, rows, HEAD_DIM), lambda b, pt: (b, 0, 0)),
        scratch_shapes=[page_buf, page_buf, pltpu.SemaphoreType.DMA((2,)), pltpu.SemaphoreType.DMA((2,)),
                        pltpu.VMEM((n_pages, rows, LANES), F32), pltpu.VMEM((2, rows, LANES), F32),
                        pltpu.VMEM((rows, LANES), F32)],
    )
    return pl.pallas_call(
        functools.partial(_sample_attn_kernel, n_pages=n_pages, n_q=n_q, unroll=unroll),
        grid_spec=grid_spec,
        out_shape=jax.ShapeDtypeStruct((batch, rows, HEAD_DIM), F32),
        compiler_params=pltpu.CompilerParams(dimension_semantics=("arbitrary",),
                                             vmem_limit_bytes=VMEM_LIMIT_PAGED),
        name="sample_attn",
    )(page_table, rb_col, q_bd, scores4, thr3, cut3, nmn3, k16n, v16n, k_pages_t, v_pages_t)


def _rope_tables(pos):
    half = HEAD_DIM // 2
    inv = ROPE_BASE ** (-jnp.arange(half, dtype=F32) / half)
    ang = pos.astype(F32)[:, None] * inv[None, :]
    cos = jnp.cos(ang)
    sin = jnp.sin(ang)
    reps = LANES // HEAD_DIM
    cos_t = jnp.tile(jnp.concatenate([cos, cos], axis=1), (1, reps))
    sin_t = jnp.tile(jnp.concatenate([-sin, sin], axis=1), (1, reps))
    return cos_t, sin_t


def _cat_weight(w_in):
    offs = np.cumsum((0,) + SPLIT_SIZES)
    qa, ka, va, qi, ki, wi, ga, qb, kb, vb, gb = [w_in[:, offs[i]:offs[i + 1]] for i in range(len(SPLIT_SIZES))]
    pad = jnp.zeros((w_in.shape[0], LANES - D_IDX - N_IDX_HEADS), w_in.dtype)
    return jnp.concatenate([qa, ka, va, qi, ga, qb, kb, vb, gb, ki, wi, pad], axis=1).astype(BF16)


def _pages_t(cache):
    pool, page = cache.shape[:2]
    return jnp.transpose(cache, (0, 2, 3, 1)).reshape(pool, -1, page)


def _largest_divisor(n, cap):
    d = cap
    while n % d:
        d //= 2
    return d


def kernel(x_prompt, x_sample, cache_k, cache_v, cache_kidx, state_ret, page_table, p_prompt, p_sample,
           rel_bias, w_in, w_out, g_pre, g_post, w_ple_up, w_ple_gate):
    batch, seq, d_model = x_prompt.shape
    dec_b, dec_t, _ = x_sample.shape
    depth = w_in.shape[0]
    n_pages = page_table.shape[1]
    past = n_pages * PAGE_SIZE
    rows_s = dec_b * dec_t
    assert depth == 1 and rows_s == LANES and seq % BLK == 0

    w_cat = _cat_weight(w_in[0])
    gpre = g_pre[0].reshape(1, d_model)
    gpost = g_post[0].reshape(1, d_model)
    wo16 = w_out[0].astype(BF16)
    wup16 = w_ple_up[0].astype(BF16)
    wg16 = w_ple_gate[0].astype(BF16)

    tm = _largest_divisor(seq, 512)
    cos_p, sin_p = _rope_tables(jnp.arange(seq))
    xp2 = x_prompt.reshape(batch * seq, d_model)
    pp = _project(xp2, gpre, w_cat, cos_p, sin_p, tm)
    oa_p = _dsa_prompt(pp, rel_bias, batch, seq)
    r3 = lambda a: a.reshape(batch, seq, D_B)
    s0_p = jnp.zeros((batch, N_HEADS_B // 2, LANES, LANES), F32)
    yn_p, sp_pairs = _retention(r3(pp["qb"]), r3(pp["kb"]), r3(pp["vb"]), s0_p, float(min(RET_CHUNK, seq)))
    y_prompt = _merge(xp2, oa_p, yn_p.reshape(batch * seq, D_B), pp["ga"], pp["gb"],
                      p_prompt[0].reshape(batch * seq, -1), wo16, gpost, wup16, wg16, tm)

    pos_s = past + jnp.arange(dec_t)
    cos_s, sin_s = _rope_tables(jnp.tile(pos_s, dec_b))
    xs2 = x_sample.reshape(rows_s, d_model)
    ps = _project(xs2, gpre, w_cat, cos_s, sin_s, rows_s)
    topk_s = min(TOPK_MAX, (past + dec_t) // 4)

    qi_rows = ps["qi"].reshape(rows_s * N_IDX_HEADS, D_IDX)
    w_rows = jnp.broadcast_to(ps["kw"][:, D_IDX:D_IDX + N_IDX_HEADS].reshape(rows_s * N_IDX_HEADS, 1),
                              (rows_s * N_IDX_HEADS, LANES))
    unroll = _largest_divisor(n_pages, 8)
    scores4 = _sample_index(page_table, qi_rows.reshape(dec_b, dec_t * N_IDX_HEADS, D_IDX),
                            w_rows.reshape(dec_b, dec_t * N_IDX_HEADS, LANES),
                            jnp.transpose(cache_kidx[0], (0, 2, 1)), unroll)
    scores_past = scores4.transpose(0, 2, 1, 3).reshape(rows_s, past)
    thr, cut, nmn = _sample_select(scores_past, qi_rows, w_rows, ps["kw16"], topk_s, dec_t)

    kv_of_head = np.arange(N_HEADS_A) // (N_HEADS_A // N_KV_A)
    place = jnp.asarray((kv_of_head[:, None] == np.arange(N_KV_A)[None, :]).astype(np.float32)).astype(BF16)
    q_bd = (ps["qa"].reshape(dec_b, dec_t, N_HEADS_A, 1, HEAD_DIM) * place[None, None, :, :, None]).reshape(
        dec_b, dec_t * N_HEADS_A, N_KV_A * HEAD_DIM)
    rb_col = jnp.broadcast_to(rel_bias[:, :, None], rel_bias.shape + (LANES,))
    b3 = lambda a: a.reshape(dec_b, dec_t, LANES)
    o_qh = _sample_attn(page_table, rb_col, q_bd, scores4, b3(thr), b3(cut), b3(nmn),
                        ps["k16"], ps["v16"], _pages_t(cache_k[0]), _pages_t(cache_v[0]), unroll)
    oa_s = o_qh.reshape(rows_s, D_A)

    chunk_s = RET_CHUNK
    padt = lambda a: jnp.pad(a.reshape(dec_b, dec_t, D_B), ((0, 0), (0, chunk_s - dec_t), (0, 0)))
    yn_s, ss_pairs = _retention(padt(ps["qb"]), padt(ps["kb"]), padt(ps["vb"]),
                                _state_to_pairs(state_ret[0]), float(math.gcd(dec_t, RET_CHUNK)))
    y_sample = _merge(xs2, oa_s, yn_s[:, :dec_t].reshape(rows_s, D_B), ps["ga"], ps["gb"],
                      p_sample[0].reshape(rows_s, -1), wo16, gpost, wup16, wg16, rows_s)

    def kv(a, b, t):
        return a.reshape(1, b, t, N_KV_A, HEAD_DIM)

    return (
        y_prompt.reshape(batch, seq, d_model),
        y_sample.reshape(dec_b, dec_t, d_model),
        kv(pp["ka"], batch, seq), kv(pp["va"], batch, seq),
        pp["kw"][:, :D_IDX].reshape(1, batch, seq, D_IDX),
        _pairs_to_state(sp_pairs)[None].astype(state_ret.dtype),
        kv(ps["ka"], dec_b, dec_t), kv(ps["va"], dec_b, dec_t),
        ps["kw"][:, :D_IDX].reshape(1, dec_b, dec_t, D_IDX),
        _pairs_to_state(ss_pairs)[None].astype(state_ret.dtype),
    )
```

```python
import functools
import math

import jax
import jax.numpy as jnp
import numpy as np
from jax import lax
from jax.experimental import pallas as pl
from jax.experimental.pallas import tpu as pltpu

F32 = jnp.float32
BF16 = jnp.bfloat16
I32 = jnp.int32

HEAD_DIM = 64
N_HEADS_A = 8
N_KV_A = 2
N_IDX_HEADS = 8
D_IDX = 64
TOPK_MAX = 256
N_HEADS_B = 8
D_A = N_HEADS_A * HEAD_DIM
D_B = N_HEADS_B * HEAD_DIM
N_BUCKETS = 32
MAX_DISTANCE = 128
ROPE_BASE = 10000.0
RET_CHUNK = 128
PAGE_SIZE = 128
EPS = 1e-6
SPLIT_SIZES = (D_A, N_KV_A * HEAD_DIM, N_KV_A * HEAD_DIM, N_IDX_HEADS * D_IDX, D_IDX, N_IDX_HEADS,
               D_A, D_B, D_B, D_B, D_B)

LANES = 128
SUBLANES = 8
BLK = 128
VMEM_LIMIT = 48 * 1024 * 1024
VMEM_LIMIT_PAGED = 56 * 1024 * 1024

INT_MIN = -(2 ** 31)
KEY_NEG_FLT_MAX = INT_MIN + 0x00800000
NEG_BIG = -1e30

_C_QA, _C_KA, _C_VA, _C_QI, _C_GA, _C_QB, _C_KB, _C_VB, _C_GB, _C_KW, _C_END = (
    0, 512, 640, 768, 1280, 1792, 2304, 2816, 3328, 3840, 3968)


def _nt_dot(a, b):
    return lax.dot_general(a, b, (((1,), (1,)), ((), ())), preferred_element_type=F32)


def _dot(a, b):
    return jnp.dot(a, b, preferred_element_type=F32)


def _sigmoid(x):
    return 1.0 / (1.0 + jnp.exp(-x))


def _key_to_float(k):
    bits = jnp.where(k >= 0, k, k ^ 0x7FFFFFFF)
    return lax.bitcast_convert_type(bits, F32)


def _t5_bucket(n):
    max_exact = N_BUCKETS // 2
    n = jnp.maximum(n, 0)
    nf = jnp.maximum(n, 1).astype(F32)
    large = max_exact + jnp.floor(jnp.log(nf / max_exact) / math.log(MAX_DISTANCE / max_exact)
                                  * (N_BUCKETS - max_exact)).astype(I32)
    large = jnp.minimum(large, N_BUCKETS - 1)
    return jnp.where(n < max_exact, n, large)


def _bias_from_bucket(bkt, rb_ref, h):
    out = jnp.zeros(bkt.shape, F32)
    for k in range(N_BUCKETS):
        out = jnp.where(bkt == k, rb_ref[k, h], out)
    return out


def _proj_kernel(x_ref, g_ref, w_ref, cos_ref, sin_ref,
                 qa_ref, ka_ref, va_ref, k16_ref, v16_ref, qi_ref, ga_ref, gb_ref,
                 qb_ref, kb_ref, vb_ref, kw_ref, kw16_ref, h_scr):
    x = x_ref[...]
    ms = jnp.mean(x * x, axis=-1, keepdims=True)
    h_scr[...] = (x * lax.rsqrt(ms + EPS) * g_ref[...]).astype(BF16)

    def mm(lo, hi):
        return _dot(h_scr[...], w_ref[:, lo:hi])

    qa_ref[...] = (mm(_C_QA, _C_KA) * (HEAD_DIM ** -0.5)).astype(BF16)
    ka = mm(_C_KA, _C_VA)
    ka_ref[...] = ka
    k16_ref[...] = ka.astype(BF16)
    va = mm(_C_VA, _C_QI)
    va_ref[...] = va
    v16_ref[...] = va.astype(BF16)
    qi_ref[...] = mm(_C_QI, _C_GA).astype(BF16)
    ga_ref[...] = mm(_C_GA, _C_QB)
    gb_ref[...] = mm(_C_GB, _C_KW)
    vb_ref[...] = mm(_C_VB, _C_GB).astype(BF16)

    cos = cos_ref[...]
    sin = sin_ref[...]
    lane = lax.broadcasted_iota(I32, cos.shape, 1)
    first_half = (lane % HEAD_DIM) < (HEAD_DIM // 2)

    def rope(z, scale):
        outs = []
        for g in range(z.shape[1] // LANES):
            zg = z[:, g * LANES:(g + 1) * LANES]
            partner = jnp.where(first_half, pltpu.roll(zg, LANES - HEAD_DIM // 2, 1),
                                pltpu.roll(zg, HEAD_DIM // 2, 1))
            r = zg * cos + partner * sin
            if scale != 1.0:
                r = r * scale
            outs.append(r.astype(BF16))
        return jnp.concatenate(outs, axis=1)

    qb_ref[...] = rope(mm(_C_QB, _C_KB), 1.0)
    kb_ref[...] = rope(mm(_C_KB, _C_VB), HEAD_DIM ** -0.5)

    kw = mm(_C_KW, _C_END)
    wscale = (N_IDX_HEADS ** -0.5) * (D_IDX ** -0.5)
    kw = kw * jnp.where(lane >= D_IDX, wscale, 1.0)
    kw_ref[...] = kw
    kw16_ref[...] = kw.astype(BF16)


def _project(x2d, g_pre, w_cat, cos_t, sin_t, tm):
    rows = x2d.shape[0]
    d_model = x2d.shape[1]
    n_tab = cos_t.shape[0] // tm
    row_spec = lambda w: pl.BlockSpec((tm, w), lambda i: (i, 0))
    outs = [
        ("qa", D_A, BF16), ("ka", LANES, F32), ("va", LANES, F32), ("k16", LANES, BF16), ("v16", LANES, BF16),
        ("qi", N_IDX_HEADS * D_IDX, BF16), ("ga", D_A, F32), ("gb", D_B, F32),
        ("qb", D_B, BF16), ("kb", D_B, BF16), ("vb", D_B, BF16), ("kw", LANES, F32), ("kw16", LANES, BF16),
    ]
    res = pl.pallas_call(
        _proj_kernel,
        grid=(rows // tm,),
        in_specs=[
            row_spec(d_model),
            pl.BlockSpec((1, d_model), lambda i: (0, 0)),
            pl.BlockSpec(w_cat.shape, lambda i: (0, 0)),
            pl.BlockSpec((tm, LANES), lambda i: (i % n_tab, 0)),
            pl.BlockSpec((tm, LANES), lambda i: (i % n_tab, 0)),
        ],
        out_specs=[row_spec(w) for _, w, _ in outs],
        out_shape=[jax.ShapeDtypeStruct((rows, w), dt) for _, w, dt in outs],
        scratch_shapes=[pltpu.VMEM((tm, d_model), BF16)],
        compiler_params=pltpu.CompilerParams(dimension_semantics=("arbitrary",), vmem_limit_bytes=VMEM_LIMIT),
        name="proj",
    )(x2d, g_pre, w_cat, cos_t, sin_t)
    return {n: r for (n, _, _), r in zip(outs, res)}


def _dsa_prompt_kernel(rb_ref, qi_ref, kwq_ref, kw16_ref, qa_ref, k16_ref, v16_ref, oa_ref,
                       sc, bt, vt, lgs, acc_scr, cut_scr, *, topk, search_blk, score_blk_n, att_blk_n):
    b = pl.program_id(0)
    j = pl.program_id(1)
    row = lax.broadcasted_iota(I32, (BLK, BLK), 0)
    col = lax.broadcasted_iota(I32, (BLK, BLK), 1)

    @pl.when((b == 0) & (j == 0))
    def _init_bias():
        for d in range(2):
            bkt = _t5_bucket(col - row + d * BLK)
            for h in range(N_HEADS_A):
                bt[d, h] = _bias_from_bucket(bkt, rb_ref, h) - rb_ref[N_BUCKETS - 1, h]
        bt[2] = jnp.zeros(bt.shape[1:], F32)

    @pl.when(j == 0)
    def _clear_values():
        vt[...] = jnp.zeros(vt.shape, BF16)

    vblk = v16_ref[pl.ds(pl.multiple_of(j * BLK, BLK), BLK), :]
    vt[j] = vblk.astype(F32).T.astype(BF16)

    qi = qi_ref[...]
    qi_stack = jnp.concatenate([qi[:, h * D_IDX:(h + 1) * D_IDX] for h in range(N_IDX_HEADS)], axis=0)
    wi_t = kwq_ref[...].T
    qpos = col + j * BLK

    def score_body(c, carry):
        for u in range(score_blk_n):
            i = c * score_blk_n + u
            kib = kw16_ref[pl.ds(pl.multiple_of(i * BLK, BLK), BLK), :][:, :D_IDX]
            s = _nt_dot(kib, qi_stack)
            acc = jnp.zeros((BLK, BLK), F32)
            for h in range(N_IDX_HEADS):
                acc = acc + wi_t[D_IDX + h:D_IDX + h + 1, :] * jnp.maximum(s[:, h * BLK:(h + 1) * BLK], 0.0)
            sc[i] = jnp.where(row + i * BLK <= qpos, acc, -jnp.inf)
        return carry

    n_score = (j + score_blk_n) // score_blk_n
    lax.fori_loop(0, n_score, score_body, 0)

    n_chunk = (j + search_blk) // search_blk

    def fill_body(i, c):
        sc[i] = jnp.full((BLK, BLK), -jnp.inf, F32)
        return c

    lax.fori_loop(n_score * score_blk_n, n_chunk * search_blk, fill_body, 0)

    def col_sum(w):
        return jnp.sum(w.reshape(BLK // SUBLANES, SUBLANES, BLK), axis=0)

    def count_ge(cand):
        def body(c, acc):
            for u in range(search_blk):
                acc = acc + col_sum(jnp.where(sc[c * search_blk + u] >= cand, 1.0, 0.0))
            return acc
        acc = lax.fori_loop(0, n_chunk, body, jnp.zeros((SUBLANES, BLK), F32))
        return jnp.sum(acc, axis=0, keepdims=True)

    kf = float(topk)
    c0 = count_ge(jnp.zeros((1, BLK), F32))
    key0 = jnp.where(c0 >= kf, 0, INT_MIN).astype(I32)

    def bit_body(it, key):
        cand = key | jnp.left_shift(jnp.int32(1), 30 - it)
        cnt = count_ge(_key_to_float(cand))
        return jnp.where(cnt >= kf, cand, key)

    key = lax.fori_loop(0, 31, bit_body, key0)
    key = jnp.maximum(key, KEY_NEG_FLT_MAX)
    thr = _key_to_float(key)

    def count2_body(c, accs):
        a_gt, a_ge = accs
        for u in range(search_blk):
            t = sc[c * search_blk + u]
            a_gt = a_gt + col_sum(jnp.where(t > thr, 1.0, 0.0))
            a_ge = a_ge + col_sum(jnp.where(t >= thr, 1.0, 0.0))
        return a_gt, a_ge

    z8 = jnp.zeros((SUBLANES, BLK), F32)
    a_gt, a_ge = lax.fori_loop(0, n_chunk, count2_body, (z8, z8))
    cnt_gt = jnp.sum(a_gt, axis=0, keepdims=True)
    cnt_ge = jnp.sum(a_ge, axis=0, keepdims=True)
    need = kf - cnt_gt
    straddle = jnp.max(jnp.where(cnt_ge > kf, 1.0, 0.0))

    n_keys_max = sc.shape[0] * BLK
    cut_scr[...] = jnp.full(cut_scr.shape, 2 * n_keys_max, I32)

    @pl.when(straddle > 0.0)
    def _tie_cut():
        def count_tie(cand):
            def body(i, acc):
                t = sc[i]
                hit = jnp.where(t == thr, jnp.where(row + i * BLK < cand, 1.0, 0.0), 0.0)
                return acc + col_sum(hit)
            acc = lax.fori_loop(0, j + 1, body, z8)
            return jnp.sum(acc, axis=0, keepdims=True)

        def cut_body(it, cut):
            cand = cut | jnp.left_shift(jnp.int32(1), (n_keys_max.bit_length() - 1) - it)
            return jnp.where(count_tie(cand) <= need, cand, cut)

        cut = lax.fori_loop(0, n_keys_max.bit_length(), cut_body, jnp.zeros((1, BLK), I32))
        cut_scr[...] = jnp.broadcast_to(cut, cut_scr.shape)

    cut = cut_scr[0:1, :]

    def mask_body(i, c):
        t = sc[i]
        tie_keep = jnp.where(row + i * BLK < cut, 0.0, -jnp.inf)
        sc[i] = jnp.where(t > thr, 0.0, jnp.where(t == thr, tie_keep, -jnp.inf))
        return c

    lax.fori_loop(0, j + 1, mask_body, 0)

    qa = qa_ref[...]
    g_per_kv = N_HEADS_A // N_KV_A
    q_stack = [jnp.concatenate([qa[:, (n * g_per_kv + g) * HEAD_DIM:(n * g_per_kv + g + 1) * HEAD_DIM]
                                for g in range(g_per_kv)], axis=0) for n in range(N_KV_A)]

    def col_max(w):
        return jnp.max(w.reshape(BLK // SUBLANES, SUBLANES, BLK), axis=0)

    n_att = (j + att_blk_n) // att_blk_n

    def logits_body(c, m8):
        m8 = list(m8)
        for u in range(att_blk_n):
            i = c * att_blk_n + u
            kblk = k16_ref[pl.ds(pl.multiple_of(i * BLK, BLK), BLK), :]
            nm = sc[i]
            near = jnp.clip(j - i, 0, 2)
            for n in range(N_KV_A):
                lg = _nt_dot(kblk[:, n * HEAD_DIM:(n + 1) * HEAD_DIM], q_stack[n])
                for g in range(g_per_kv):
                    h = n * g_per_kv + g
                    x = lg[:, g * BLK:(g + 1) * BLK] + nm + bt[near, h]
                    lgs[i, h] = x
                    m8[h] = jnp.maximum(m8[h], col_max(x))
        return tuple(m8)

    neg8 = jnp.full((SUBLANES, BLK), NEG_BIG, F32)
    m8 = lax.fori_loop(0, n_att, logits_body, (neg8,) * N_HEADS_A)
    m_row = [jnp.max(m8[h], axis=0, keepdims=True) for h in range(N_HEADS_A)]

    acc_scr[...] = jnp.zeros(acc_scr.shape, F32)

    def pv_body(c, l8):
        l8 = list(l8)
        for u in range(att_blk_n):
            i = c * att_blk_n + u
            vti = vt[i]
            for n in range(N_KV_A):
                ps = []
                for g in range(g_per_kv):
                    h = n * g_per_kv + g
                    p = jnp.exp(lgs[i, h] - m_row[h])
                    l8[h] = l8[h] + col_sum(p)
                    ps.append(p.astype(BF16))
                acc_scr[n] += _dot(vti[n * HEAD_DIM:(n + 1) * HEAD_DIM, :], jnp.concatenate(ps, axis=1))
        return tuple(l8)

    l8 = lax.fori_loop(0, n_att, pv_body, (z8,) * N_HEADS_A)

    parts = []
    for n in range(N_KV_A):
        acc = acc_scr[n]
        for g in range(g_per_kv):
            h = n * g_per_kv + g
            parts.append(acc[:, g * BLK:(g + 1) * BLK] * (1.0 / jnp.sum(l8[h], axis=0, keepdims=True)))
    oa_ref[...] = jnp.concatenate(parts, axis=0).T


def _dsa_prompt(p, rel_bias, batch, seq):
    nblk = seq // BLK
    topk = min(TOPK_MAX, seq // 4)
    search_blk = 4 if nblk % 4 == 0 else 1
    score_blk_n = search_blk
    att_blk_n = search_blk
    qrow = lambda w: pl.BlockSpec((BLK, w), lambda b, j: (b * nblk + j, 0))
    brow = lambda w: pl.BlockSpec((seq, w), lambda b, j: (b, 0))
    return pl.pallas_call(
        functools.partial(_dsa_prompt_kernel, topk=topk, search_blk=search_blk, score_blk_n=score_blk_n,
                          att_blk_n=att_blk_n),
        grid=(batch, nblk),
        in_specs=[
            pl.BlockSpec(memory_space=pltpu.SMEM),
            qrow(N_IDX_HEADS * D_IDX), qrow(LANES), brow(LANES), qrow(D_A), brow(LANES), brow(LANES),
        ],
        out_specs=qrow(D_A),
        out_shape=jax.ShapeDtypeStruct((batch * seq, D_A), F32),
        scratch_shapes=[
            pltpu.VMEM((nblk, BLK, BLK), F32),
            pltpu.VMEM((3, N_HEADS_A, BLK, BLK), F32),
            pltpu.VMEM((nblk, BLK, BLK), BF16),
            pltpu.VMEM((nblk, N_HEADS_A, BLK, BLK), F32),
            pltpu.VMEM((N_KV_A, HEAD_DIM, (N_HEADS_A // N_KV_A) * BLK), F32),
            pltpu.VMEM((SUBLANES, BLK), I32),
        ],
        compiler_params=pltpu.CompilerParams(dimension_semantics=("arbitrary", "arbitrary"),
                                             vmem_limit_bytes=VMEM_LIMIT),
        name="dsa_prompt",
    )(rel_bias, p["qi"], p["kw"], p["kw16"], p["qa"], p["k16"], p["v16"])


def _retention_kernel(q_ref, k_ref, v_ref, s0_ref, dmat_ref, din_ref, dout_ref, decm_ref, bdm_ref,
                      yn_ref, s_out_ref, s_scr, *, n_sub):
    c = pl.program_id(1)
    chunk = din_ref.shape[0]

    @pl.when(c == 0)
    def _load_state():
        s_scr[...] = s0_ref[0]

    lo = lax.broadcasted_iota(I32, (chunk, LANES), 1) < HEAD_DIM
    zero = jnp.zeros((), BF16)
    for p in range(N_HEADS_B // 2):
        sl = slice(p * LANES, (p + 1) * LANES)
        st = s_scr[p]
        for u in range(n_sub):
            rows = slice(u * chunk, (u + 1) * chunk)
            q, k, v = q_ref[0, rows, sl], k_ref[0, rows, sl], v_ref[0, rows, sl]
            qd = (q.astype(F32) * din_ref[:, sl]).astype(BF16)
            kd = (k.astype(F32) * dout_ref[:, sl]).astype(BF16)
            vt = v.astype(F32).T.astype(BF16)
            s_a = (_nt_dot(jnp.where(lo, k, zero), q) * dmat_ref[2 * p]).astype(BF16)
            s_b = (_nt_dot(jnp.where(lo, zero, k), q) * dmat_ref[2 * p + 1]).astype(BF16)
            intra = jnp.concatenate([_dot(vt[:HEAD_DIM], s_a), _dot(vt[HEAD_DIM:], s_b)], axis=0)
            o = (_nt_dot(st.astype(BF16), qd) + intra).reshape(2, HEAD_DIM, chunk)
            dlt = o - jnp.mean(o, axis=1, keepdims=True)
            var = jnp.mean(dlt * dlt, axis=1, keepdims=True)
            yn_ref[0, rows, sl] = (dlt * lax.rsqrt(var + EPS)).reshape(LANES, chunk).T
            st = st * decm_ref[p] + _dot(vt, kd) * bdm_ref[...]
        s_scr[p] = st

    s_out_ref[0] = s_scr[...]


def _ret_tables(chunk, n_real):
    h = N_HEADS_B
    log_g = jnp.log1p(-jnp.exp2(-5.0 - jnp.arange(h, dtype=F32)))
    i = jnp.arange(chunk, dtype=F32)
    diff = i[:, None] - i[None, :]
    dmat = jnp.where(diff >= 0, jnp.exp(jnp.maximum(diff, 0.0)[None] * log_g[:, None, None]), 0.0)
    dmat_t = jnp.swapaxes(dmat, 1, 2)
    dec_in = jnp.exp((i + 1.0)[:, None] * log_g[None, :])
    dec_out = jnp.exp((n_real - 1.0 - i)[:, None] * log_g[None, :])
    dec_chunk = jnp.exp(n_real * log_g)
    din = jnp.repeat(dec_in, HEAD_DIM, axis=1)
    dout = jnp.repeat(dec_out, HEAD_DIM, axis=1)
    head_of = np.arange(LANES) // HEAD_DIM
    bd = jnp.asarray((head_of[:, None] == head_of[None, :]).astype(np.float32))
    dc = jnp.repeat(dec_chunk, HEAD_DIM).reshape(h // 2, LANES)
    decm = dc[:, :, None] * bd[None]
    return dmat_t, din, dout, decm, bd


def _retention(q3, k3, v3, s0t, n_real):
    batch, seq, _ = q3.shape
    chunk = min(RET_CHUNK, seq)
    n_sub = _largest_divisor(seq // chunk, 4)
    dmat_t, din, dout, decm, bd = _ret_tables(chunk, n_real)
    tok = pl.BlockSpec((1, n_sub * chunk, D_B), lambda b, c: (b, c, 0))
    st = pl.BlockSpec((1, N_HEADS_B // 2, LANES, LANES), lambda b, c: (b, 0, 0, 0))
    full = lambda a: pl.BlockSpec(a.shape, lambda b, c: (0,) * a.ndim)
    return pl.pallas_call(
        functools.partial(_retention_kernel, n_sub=n_sub),
        grid=(batch, seq // (n_sub * chunk)),
        in_specs=[tok, tok, tok, st, full(dmat_t), full(din), full(dout), full(decm), full(bd)],
        out_specs=[tok, st],
        out_shape=[jax.ShapeDtypeStruct((batch, seq, D_B), F32),
                   jax.ShapeDtypeStruct((batch, N_HEADS_B // 2, LANES, LANES), F32)],
        scratch_shapes=[pltpu.VMEM((N_HEADS_B // 2, LANES, LANES), F32)],
        compiler_params=pltpu.CompilerParams(dimension_semantics=("arbitrary", "arbitrary"),
                                             vmem_limit_bytes=VMEM_LIMIT),
        name="retention",
    )(q3, k3, v3, s0t, dmat_t, din, dout, decm, bd)


def _state_to_pairs(s):
    b = s.shape[0]
    s = jnp.swapaxes(s, -1, -2).reshape(b, N_HEADS_B // 2, 2, HEAD_DIM, HEAD_DIM)
    z = jnp.zeros_like(s[:, :, 0])
    top = jnp.concatenate([s[:, :, 0], z], axis=-1)
    bot = jnp.concatenate([z, s[:, :, 1]], axis=-1)
    return jnp.concatenate([top, bot], axis=-2)


def _pairs_to_state(sp):
    b = sp.shape[0]
    a = sp[:, :, :HEAD_DIM, :HEAD_DIM]
    d = sp[:, :, HEAD_DIM:, HEAD_DIM:]
    return jnp.swapaxes(jnp.stack([a, d], axis=2).reshape(b, N_HEADS_B, HEAD_DIM, HEAD_DIM), -1, -2)


def _merge_kernel(x_ref, oa_ref, yn_ref, ga_ref, gb_ref, p_ref, wo_ref, gp_ref, wup_ref, wg_ref, out_ref):
    ga = ga_ref[...]
    gb = gb_ref[...]
    ya = (ga * _sigmoid(ga) * oa_ref[...]).astype(BF16)
    yb = (gb * _sigmoid(gb) * yn_ref[...]).astype(BF16)
    y = _dot(ya, wo_ref[0:D_A, :]) + _dot(yb, wo_ref[D_A:D_A + D_B, :])
    ms = jnp.mean(y * y, axis=-1, keepdims=True)
    x1 = x_ref[...] + y * lax.rsqrt(ms + EPS) * gp_ref[...]
    ple = _dot(p_ref[...].astype(BF16), wup_ref[...])
    gate = _sigmoid(_dot(x1.astype(BF16), wg_ref[...]))
    out_ref[...] = x1 + ple * gate


def _merge(x2d, oa, yn, ga, gb, p2d, wo16, g_post, wup16, wg16, tm):
    rows, d_model = x2d.shape
    row = lambda w: pl.BlockSpec((tm, w), lambda i: (i, 0))
    full = lambda a: pl.BlockSpec(a.shape, lambda i: (0, 0))
    return pl.pallas_call(
        _merge_kernel,
        grid=(rows // tm,),
        in_specs=[row(d_model), row(D_A), row(D_B), row(D_A), row(D_B), row(p2d.shape[1]),
                  full(wo16), full(g_post), full(wup16), full(wg16)],
        out_specs=row(d_model),
        out_shape=jax.ShapeDtypeStruct((rows, d_model), F32),
        compiler_params=pltpu.CompilerParams(dimension_semantics=("arbitrary",), vmem_limit_bytes=VMEM_LIMIT),
        name="merge",
    )(x2d, oa, yn, ga, gb, p2d, wo16, g_post, wup16, wg16)


def _page_copy(pages_hbm, buf, sem, pt_ref, b, p, slot):
    return pltpu.make_async_copy(pages_hbm.at[pt_ref[b, p]], buf.at[slot, p], sem.at[slot])


def _start_batch_pages(streams, pt_ref, b, slot, n_pages):
    def body(p, c):
        for pages_hbm, buf, sem in streams:
            _page_copy(pages_hbm, buf, sem, pt_ref, b, p, slot).start()
        return c
    lax.fori_loop(0, n_pages, body, 0)


def _wait_batch_pages(streams, pt_ref, b, slot, n_pages):
    def body(p, c):
        for pages_hbm, buf, sem in streams:
            _page_copy(pages_hbm, buf, sem, pt_ref, b, p, slot).wait()
        return c
    lax.fori_loop(0, n_pages, body, 0)


def _sample_index_kernel(pt_ref, qi_ref, w_ref, pages_hbm, out_ref, buf, sem, *, n_pages, unroll):
    b = pl.program_id(0)
    slot = b % 2
    streams = [(pages_hbm, buf, sem)]

    @pl.when(b == 0)
    def _first():
        _start_batch_pages(streams, pt_ref, 0, 0, n_pages)

    _wait_batch_pages(streams, pt_ref, b, slot, n_pages)

    qi = qi_ref[0]
    w = w_ref[0]
    n_q = qi.shape[0] // N_IDX_HEADS

    def run(prefetch):
        def body(c, carry):
            for u in range(unroll):
                p = c * unroll + u
                if prefetch:
                    _page_copy(pages_hbm, buf, sem, pt_ref, b + 1, p, 1 - slot).start()
                s = _dot(qi, buf[slot, p].astype(BF16))
                r = jnp.maximum(s, 0.0) * w
                out_ref[0, c, :, u * PAGE_SIZE:(u + 1) * PAGE_SIZE] = jnp.sum(
                    r.reshape(n_q, N_IDX_HEADS, PAGE_SIZE), axis=1)
            return carry
        lax.fori_loop(0, n_pages // unroll, body, 0)

    has_next = b + 1 < pl.num_programs(0)
    pl.when(has_next)(lambda: run(True))
    pl.when(jnp.logical_not(has_next))(lambda: run(False))


def _sample_index(page_table, qi_qh, w_qh, kidx_t, unroll):
    batch, n_pages = page_table.shape
    n_q = qi_qh.shape[1] // N_IDX_HEADS
    groups = n_pages // unroll
    grid_spec = pltpu.PrefetchScalarGridSpec(
        num_scalar_prefetch=1,
        grid=(batch,),
        in_specs=[pl.BlockSpec((1,) + qi_qh.shape[1:], lambda b, pt: (b, 0, 0)),
                  pl.BlockSpec((1,) + w_qh.shape[1:], lambda b, pt: (b, 0, 0)),
                  pl.BlockSpec(memory_space=pl.ANY)],
        out_specs=pl.BlockSpec((1, groups, n_q, unroll * PAGE_SIZE), lambda b, pt: (b, 0, 0, 0)),
        scratch_shapes=[pltpu.VMEM((2, n_pages) + kidx_t.shape[1:], kidx_t.dtype),
                        pltpu.SemaphoreType.DMA((2,))],
    )
    return pl.pallas_call(
        functools.partial(_sample_index_kernel, n_pages=n_pages, unroll=unroll),
        grid_spec=grid_spec,
        out_shape=jax.ShapeDtypeStruct((batch, groups, n_q, unroll * PAGE_SIZE), F32),
        compiler_params=pltpu.CompilerParams(dimension_semantics=("arbitrary",), vmem_limit_bytes=VMEM_LIMIT),
        name="sample_index",
    )(page_table, qi_qh, w_qh, kidx_t)


def _sample_select_kernel(sp_ref, qi_ref, w_ref, kw16_ref, thr_ref, cut_ref, nmn_ref, snew_scr,
                          *, topk, n_q, lane_chunk):
    rows, past = sp_ref.shape
    r_i = lax.broadcasted_iota(I32, (rows, LANES), 0)
    l_i = lax.broadcasted_iota(I32, (rows, LANES), 1)

    s = _nt_dot(qi_ref[...], kw16_ref[...][:, :D_IDX])
    r = jnp.maximum(s, 0.0) * w_ref[...]
    s_new = jnp.sum(r.reshape(rows, N_IDX_HEADS, LANES), axis=1)
    ok = ((l_i // n_q) == (r_i // n_q)) & ((l_i % n_q) <= (r_i % n_q))
    snew_scr[...] = jnp.where(ok, s_new, -jnp.inf)

    n_chunks = past // lane_chunk
    tiles = lane_chunk // LANES
    kf = float(topk)

    def count(ind):
        acc = ind(snew_scr[...], l_i + past)
        for ch in range(n_chunks):
            x = sp_ref[:, ch * lane_chunk:(ch + 1) * lane_chunk]
            for t in range(tiles):
                idx = l_i + (ch * lane_chunk + t * LANES)
                acc = acc + ind(x[:, t * LANES:(t + 1) * LANES], idx)
        return jnp.broadcast_to(jnp.sum(acc, axis=1, keepdims=True), (rows, LANES))

    c0 = count(lambda x, idx: jnp.where(x >= 0.0, 1.0, 0.0))
    key0 = jnp.where(c0 >= kf, 0, INT_MIN).astype(I32)

    def bit_body(it, key):
        cand = key | jnp.left_shift(jnp.int32(1), 30 - it)
        cf = _key_to_float(cand)
        return jnp.where(count(lambda x, idx: jnp.where(x >= cf, 1.0, 0.0)) >= kf, cand, key)

    key = lax.fori_loop(0, 31, bit_body, key0)
    key = jnp.maximum(key, KEY_NEG_FLT_MAX)
    thr = _key_to_float(key)
    cnt_gt = count(lambda x, idx: jnp.where(x > thr, 1.0, 0.0))
    cnt_ge = count(lambda x, idx: jnp.where(x >= thr, 1.0, 0.0))
    need = kf - cnt_gt
    straddle = jnp.max(jnp.where(cnt_ge > kf, 1.0, 0.0))

    idx_bits = (past + LANES).bit_length()
    thr_ref[...] = thr
    cut_ref[...] = jnp.full((rows, LANES), 1 << idx_bits, I32)

    @pl.when(straddle > 0.0)
    def _tie_cut():
        def cut_body(it, cut):
            cand = cut | jnp.left_shift(jnp.int32(1), (idx_bits - 1) - it)
            cnt = count(lambda x, idx: jnp.where(x == thr, jnp.where(idx < cand, 1.0, 0.0), 0.0))
            return jnp.where(cnt <= need, cand, cut)
        cut_ref[...] = lax.fori_loop(0, idx_bits, cut_body, jnp.zeros((rows, LANES), I32))

    cut = cut_ref[...]
    sn = snew_scr[...]
    tie_keep = jnp.where(l_i + past < cut, 0.0, -jnp.inf)
    nmn_ref[...] = jnp.where(sn > thr, 0.0, jnp.where(sn == thr, tie_keep, -jnp.inf))


def _sample_select(scores_past, qi_rows, w_rows, kw16_s, topk, n_q):
    rows, past = scores_past.shape
    lane_chunk = 2048 if past % 2048 == 0 else LANES
    full = lambda a: pl.BlockSpec(a.shape, lambda i: (0,) * a.ndim)
    o = jax.ShapeDtypeStruct((rows, LANES), F32)
    return pl.pallas_call(
        functools.partial(_sample_select_kernel, topk=topk, n_q=n_q, lane_chunk=lane_chunk),
        grid=(1,),
        in_specs=[full(scores_past), full(qi_rows), full(w_rows), full(kw16_s)],
        out_specs=[pl.BlockSpec((rows, LANES), lambda i: (0, 0))] * 3,
        out_shape=[o, jax.ShapeDtypeStruct((rows, LANES), I32), o],
        scratch_shapes=[pltpu.VMEM((rows, LANES), F32)],
        compiler_params=pltpu.CompilerParams(dimension_semantics=("arbitrary",), vmem_limit_bytes=VMEM_LIMIT),
        name="sample_select",
    )(scores_past, qi_rows, w_rows, kw16_s)


def _sample_attn_kernel(pt_ref, rbc_ref, q_ref, sp_ref, thr_ref, cut_ref, nmn_ref, k16n_ref, v16n_ref,
                        k_hbm, v_hbm, o_ref, kbuf, vbuf, ksem, vsem, lgs, bias_scr, mx_scr,
                        *, n_pages, n_q, unroll):
    b = pl.program_id(0)
    slot = b % 2
    rows = n_q * N_HEADS_A
    groups = n_pages // unroll
    streams = [(k_hbm, kbuf, ksem), (v_hbm, vbuf, vsem)]
    lane = lax.broadcasted_iota(I32, (rows, LANES), 1)
    rq = lax.broadcasted_iota(I32, (rows, LANES), 0) // N_HEADS_A
    expand = lambda a: jnp.concatenate(
        [jnp.broadcast_to(a[t:t + 1], (N_HEADS_A, LANES)) for t in range(n_q)], axis=0)

    @pl.when(b == 0)
    def _first():
        _start_batch_pages(streams, pt_ref, 0, 0, n_pages)
        far = jnp.concatenate([rbc_ref[N_BUCKETS - 1]] * n_q, axis=0)
        for t, dist in enumerate((PAGE_SIZE + rq - lane, rq - lane % n_q)):
            bkt = _t5_bucket(dist)
            tile = jnp.zeros((rows, LANES), F32)
            for k in range(N_BUCKETS - 1):
                tile = jnp.where(bkt == k, jnp.concatenate([rbc_ref[k]] * n_q, axis=0) - far, tile)
            bias_scr[t] = tile

    _wait_batch_pages(streams, pt_ref, b, slot, n_pages)

    thr = expand(thr_ref[0])
    cut = expand(cut_ref[0])
    q = q_ref[0]

    def logits_pass(prefetch):
        def body(c, mx):
            for u in range(unroll):
                p = c * unroll + u
                if prefetch:
                    for pages_hbm, buf, sem in streams:
                        _page_copy(pages_hbm, buf, sem, pt_ref, b + 1, p, 1 - slot).start()
                lg = _dot(q, kbuf[slot, p].astype(BF16))
                sc = expand(sp_ref[0, c, :, u * PAGE_SIZE:(u + 1) * PAGE_SIZE])
                tie_keep = jnp.where(lane + p * PAGE_SIZE < cut, 0.0, -jnp.inf)
                x = lg + jnp.where(sc > thr, 0.0, jnp.where(sc == thr, tie_keep, -jnp.inf))
                if u == unroll - 1:
                    x = x + bias_scr[0] * jnp.where(c == groups - 1, 1.0, 0.0)
                lgs[p] = x
                mx = jnp.maximum(mx, x)
            return mx
        mx_scr[...] = lax.fori_loop(0, groups, body, jnp.full((rows, LANES), NEG_BIG, F32))

    has_next = b + 1 < pl.num_programs(0)
    pl.when(has_next)(lambda: logits_pass(True))
    pl.when(jnp.logical_not(has_next))(lambda: logits_pass(False))

    xn = _nt_dot(q, k16n_ref[...]) + bias_scr[1] + expand(nmn_ref[0])
    m = jnp.max(jnp.maximum(mx_scr[...], xn), axis=1, keepdims=True)

    def pv_body(c, carry):
        lsum, acc = carry
        for u in range(unroll):
            p = c * unroll + u
            pr = jnp.exp(lgs[p] - m)
            lsum = lsum + pr
            acc = acc + _nt_dot(pr.astype(BF16), vbuf[slot, p].astype(BF16))
        return lsum, acc

    pn = jnp.exp(xn - m)
    lsum, acc = lax.fori_loop(0, groups, pv_body, (pn, _dot(pn.astype(BF16), v16n_ref[...])))
    out = acc * (1.0 / jnp.sum(lsum, axis=1, keepdims=True))
    head = lax.broadcasted_iota(I32, (rows, HEAD_DIM), 0) % N_HEADS_A
    o_ref[0] = jnp.where(head < N_HEADS_A // N_KV_A, out[:, :HEAD_DIM], out[:, HEAD_DIM:])


def _sample_attn(page_table, rb_col, q_bd, scores4, thr3, cut3, nmn3, k16n, v16n, k_pages_t, v_pages_t, unroll):
    batch, n_pages = page_table.shape
    n_q = scores4.shape[2]
    rows = q_bd.shape[1]
    per_b = lambda a: pl.BlockSpec((1,) + a.shape[1:], lambda b, pt: (b,) + (0,) * (a.ndim - 1))
    full = lambda a: pl.BlockSpec(a.shape, lambda b, pt: (0,) * a.ndim)
    page_buf = pltpu.VMEM((2, n_pages) + k_pages_t.shape[1:], k_pages_t.dtype)
    grid_spec = pltpu.PrefetchScalarGridSpec(
        num_scalar_prefetch=1,
        grid=(batch,),
        in_specs=[full(rb_col), per_b(q_bd), per_b(scores4), per_b(thr3), per_b(cut3), per_b(nmn3),
                  full(k16n), full(v16n), pl.BlockSpec(memory_space=pl.ANY), pl.BlockSpec(memory_space=pl.ANY)],
        out_specs=pl.BlockSpec((1, rows, HEAD_DIM), lambda b, pt: (b, 0, 0)),
        scratch_shapes=[page_buf, page_buf, pltpu.SemaphoreType.DMA((2,)), pltpu.SemaphoreType.DMA((2,)),
                        pltpu.VMEM((n_pages, rows, LANES), F32), pltpu.VMEM((2, rows, LANES), F32),
                        pltpu.VMEM((rows, LANES), F32)],
    )
    return pl.pallas_call(
        functools.partial(_sample_attn_kernel, n_pages=n_pages, n_q=n_q, unroll=unroll),
        grid_spec=grid_spec,
        out_shape=jax.ShapeDtypeStruct((batch, rows, HEAD_DIM), F32),
        compiler_params=pltpu.CompilerParams(dimension_semantics=("arbitrary",),
                                             vmem_limit_bytes=VMEM_LIMIT_PAGED),
        name="sample_attn",
    )(page_table, rb_col, q_bd, scores4, thr3, cut3, nmn3, k16n, v16n, k_pages_t, v_pages_t)


def _rope_tables(pos):
    half = HEAD_DIM // 2
    inv = ROPE_BASE ** (-jnp.arange(half, dtype=F32) / half)
    ang = pos.astype(F32)[:, None] * inv[None, :]
    cos = jnp.cos(ang)
    sin = jnp.sin(ang)
    reps = LANES // HEAD_DIM
    cos_t = jnp.tile(jnp.concatenate([cos, cos], axis=1), (1, reps))
    sin_t = jnp.tile(jnp.concatenate([-sin, sin], axis=1), (1, reps))
    return cos_t, sin_t


def _cat_weight(w_in):
    offs = np.cumsum((0,) + SPLIT_SIZES)
    qa, ka, va, qi, ki, wi, ga, qb, kb, vb, gb = [w_in[:, offs[i]:offs[i + 1]] for i in range(len(SPLIT_SIZES))]
    pad = jnp.zeros((w_in.shape[0], LANES - D_IDX - N_IDX_HEADS), w_in.dtype)
    return jnp.concatenate([qa, ka, va, qi, ga, qb, kb, vb, gb, ki, wi, pad], axis=1).astype(BF16)


def _pages_t(cache):
    pool, page = cache.shape[:2]
    return jnp.transpose(cache, (0, 2, 3, 1)).reshape(pool, -1, page)


def _largest_divisor(n, cap):
    d = cap
    while n % d:
        d //= 2
    return d


def kernel(x_prompt, x_sample, cache_k, cache_v, cache_kidx, state_ret, page_table, p_prompt, p_sample,
           rel_bias, w_in, w_out, g_pre, g_post, w_ple_up, w_ple_gate):
    batch, seq, d_model = x_prompt.shape
    dec_b, dec_t, _ = x_sample.shape
    depth = w_in.shape[0]
    n_pages = page_table.shape[1]
    past = n_pages * PAGE_SIZE
    rows_s = dec_b * dec_t
    assert depth == 1 and rows_s == LANES and seq % BLK == 0

    w_cat = _cat_weight(w_in[0])
    gpre = g_pre[0].reshape(1, d_model)
    gpost = g_post[0].reshape(1, d_model)
    wo16 = w_out[0].astype(BF16)
    wup16 = w_ple_up[0].astype(BF16)
    wg16 = w_ple_gate[0].astype(BF16)

    tm = _largest_divisor(seq, 512)
    cos_p, sin_p = _rope_tables(jnp.arange(seq))
    xp2 = x_prompt.reshape(batch * seq, d_model)
    pp = _project(xp2, gpre, w_cat, cos_p, sin_p, tm)
    oa_p = _dsa_prompt(pp, rel_bias, batch, seq)
    r3 = lambda a: a.reshape(batch, seq, D_B)
    s0_p = jnp.zeros((batch, N_HEADS_B // 2, LANES, LANES), F32)
    yn_p, sp_pairs = _retention(r3(pp["qb"]), r3(pp["kb"]), r3(pp["vb"]), s0_p, float(min(RET_CHUNK, seq)))
    y_prompt = _merge(xp2, oa_p, yn_p.reshape(batch * seq, D_B), pp["ga"], pp["gb"],
                      p_prompt[0].reshape(batch * seq, -1), wo16, gpost, wup16, wg16, tm)

    pos_s = past + jnp.arange(dec_t)
    cos_s, sin_s = _rope_tables(jnp.tile(pos_s, dec_b))
    xs2 = x_sample.reshape(rows_s, d_model)
    ps = _project(xs2, gpre, w_cat, cos_s, sin_s, rows_s)
    topk_s = min(TOPK_MAX, (past + dec_t) // 4)

    qi_rows = ps["qi"].reshape(rows_s * N_IDX_HEADS, D_IDX)
    w_rows = jnp.broadcast_to(ps["kw"][:, D_IDX:D_IDX + N_IDX_HEADS].reshape(rows_s * N_IDX_HEADS, 1),
                              (rows_s * N_IDX_HEADS, LANES))
    unroll = _largest_divisor(n_pages, 8)
    scores4 = _sample_index(page_table, qi_rows.reshape(dec_b, dec_t * N_IDX_HEADS, D_IDX),
                            w_rows.reshape(dec_b, dec_t * N_IDX_HEADS, LANES),
                            jnp.transpose(cache_kidx[0], (0, 2, 1)), unroll)
    scores_past = scores4.transpose(0, 2, 1, 3).reshape(rows_s, past)
    thr, cut, nmn = _sample_select(scores_past, qi_rows, w_rows, ps["kw16"], topk_s, dec_t)

    kv_of_head = np.arange(N_HEADS_A) // (N_HEADS_A // N_KV_A)
    place = jnp.asarray((kv_of_head[:, None] == np.arange(N_KV_A)[None, :]).astype(np.float32)).astype(BF16)
    q_bd = (ps["qa"].reshape(dec_b, dec_t, N_HEADS_A, 1, HEAD_DIM) * place[None, None, :, :, None]).reshape(
        dec_b, dec_t * N_HEADS_A, N_KV_A * HEAD_DIM)
    rb_col = jnp.broadcast_to(rel_bias[:, :, None], rel_bias.shape + (LANES,))
    b3 = lambda a: a.reshape(dec_b, dec_t, LANES)
    o_qh = _sample_attn(page_table, rb_col, q_bd, scores4, b3(thr), b3(cut), b3(nmn),
                        ps["k16"], ps["v16"], _pages_t(cache_k[0]), _pages_t(cache_v[0]), unroll)
    oa_s = o_qh.reshape(rows_s, D_A)

    chunk_s = RET_CHUNK
    padt = lambda a: jnp.pad(a.reshape(dec_b, dec_t, D_B), ((0, 0), (0, chunk_s - dec_t), (0, 0)))
    yn_s, ss_pairs = _retention(padt(ps["qb"]), padt(ps["kb"]), padt(ps["vb"]),
                                _state_to_pairs(state_ret[0]), float(math.gcd(dec_t, RET_CHUNK)))
    y_sample = _merge(xs2, oa_s, yn_s[:, :dec_t].reshape(rows_s, D_B), ps["ga"], ps["gb"],
                      p_sample[0].reshape(rows_s, -1), wo16, gpost, wup16, wg16, rows_s)

    def kv(a, b, t):
        return a.reshape(1, b, t, N_KV_A, HEAD_DIM)

    return (
        y_prompt.reshape(batch, seq, d_model),
        y_sample.reshape(dec_b, dec_t, d_model),
        kv(pp["ka"], batch, seq), kv(pp["va"], batch, seq),
        pp["kw"][:, :D_IDX].reshape(1, batch, seq, D_IDX),
        _pairs_to_state(sp_pairs)[None].astype(state_ret.dtype),
        kv(ps["ka"], dec_b, dec_t), kv(ps["va"], dec_b, dec_t),
        ps["kw"][:, :D_IDX].reshape(1, dec_b, dec_t, D_IDX),
        _pairs_to_state(ss_pairs)[None].astype(state_ret.dtype),
    )
```

```python
import functools
import math

import jax
import jax.numpy as jnp
import numpy as np
from jax import lax
from jax.experimental import pallas as pl
from jax.experimental.pallas import tpu as pltpu

F32 = jnp.float32
BF16 = jnp.bfloat16
I32 = jnp.int32

HEAD_DIM = 64
N_HEADS_A = 8
N_KV_A = 2
N_IDX_HEADS = 8
D_IDX = 64
TOPK_MAX = 256
N_HEADS_B = 8
D_A = N_HEADS_A * HEAD_DIM
D_B = N_HEADS_B * HEAD_DIM
N_BUCKETS = 32
MAX_DISTANCE = 128
ROPE_BASE = 10000.0
RET_CHUNK = 128
PAGE_SIZE = 128
EPS = 1e-6
SPLIT_SIZES = (D_A, N_KV_A * HEAD_DIM, N_KV_A * HEAD_DIM, N_IDX_HEADS * D_IDX, D_IDX, N_IDX_HEADS,
               D_A, D_B, D_B, D_B, D_B)

LANES = 128
SUBLANES = 8
BLK = 128
VMEM_LIMIT = 48 * 1024 * 1024
VMEM_LIMIT_PAGED = 56 * 1024 * 1024

INT_MIN = -(2 ** 31)
KEY_NEG_FLT_MAX = INT_MIN + 0x00800000
NEG_BIG = -1e30

_C_QA, _C_KA, _C_VA, _C_QI, _C_GA, _C_QB, _C_KB, _C_VB, _C_GB, _C_KW, _C_END = (
    0, 512, 640, 768, 1280, 1792, 2304, 2816, 3328, 3840, 3968)


def _nt_dot(a, b):
    return lax.dot_general(a, b, (((1,), (1,)), ((), ())), preferred_element_type=F32)


def _dot(a, b):
    return jnp.dot(a, b, preferred_element_type=F32)


def _sigmoid(x):
    return 1.0 / (1.0 + jnp.exp(-x))


def _key_to_float(k):
    bits = jnp.where(k >= 0, k, k ^ 0x7FFFFFFF)
    return lax.bitcast_convert_type(bits, F32)


def _t5_bucket(n):
    max_exact = N_BUCKETS // 2
    n = jnp.maximum(n, 0)
    nf = jnp.maximum(n, 1).astype(F32)
    large = max_exact + jnp.floor(jnp.log(nf / max_exact) / math.log(MAX_DISTANCE / max_exact)
                                  * (N_BUCKETS - max_exact)).astype(I32)
    large = jnp.minimum(large, N_BUCKETS - 1)
    return jnp.where(n < max_exact, n, large)


def _bias_from_bucket(bkt, rb_ref, h):
    out = jnp.zeros(bkt.shape, F32)
    for k in range(N_BUCKETS):
        out = jnp.where(bkt == k, rb_ref[k, h], out)
    return out


def _proj_kernel(x_ref, g_ref, w_ref, cos_ref, sin_ref,
                 qa_ref, kat_ref, vat_ref, k16_ref, v16_ref, qi_ref, ga_ref, gb_ref,
                 qb_ref, kb_ref, vb_ref, kw_ref, kw16_ref, kit_ref, h_scr):
    x = x_ref[...]
    ms = jnp.mean(x * x, axis=-1, keepdims=True)
    h_scr[...] = (x * lax.rsqrt(ms + EPS) * g_ref[...]).astype(BF16)

    def mm(lo, hi):
        return _dot(h_scr[...], w_ref[:, lo:hi])

    qa_ref[...] = (mm(_C_QA, _C_KA) * (HEAD_DIM ** -0.5)).astype(BF16)
    ka = mm(_C_KA, _C_VA)
    kat_ref[0] = ka.T
    k16_ref[...] = ka.astype(BF16)
    va = mm(_C_VA, _C_QI)
    vat_ref[0] = va.T
    v16_ref[...] = va.astype(BF16)
    qi_ref[...] = mm(_C_QI, _C_GA).astype(BF16)
    ga_ref[...] = mm(_C_GA, _C_QB)
    gb_ref[...] = mm(_C_GB, _C_KW)
    vb_ref[...] = mm(_C_VB, _C_GB).astype(BF16)

    cos = cos_ref[...]
    sin = sin_ref[...]
    lane = lax.broadcasted_iota(I32, cos.shape, 1)
    first_half = (lane % HEAD_DIM) < (HEAD_DIM // 2)

    def rope(z, scale):
        outs = []
        for g in range(z.shape[1] // LANES):
            zg = z[:, g * LANES:(g + 1) * LANES]
            partner = jnp.where(first_half, pltpu.roll(zg, LANES - HEAD_DIM // 2, 1),
                                pltpu.roll(zg, HEAD_DIM // 2, 1))
            r = zg * cos + partner * sin
            if scale != 1.0:
                r = r * scale
            outs.append(r.astype(BF16))
        return jnp.concatenate(outs, axis=1)

    qb_ref[...] = rope(mm(_C_QB, _C_KB), 1.0)
    kb_ref[...] = rope(mm(_C_KB, _C_VB), HEAD_DIM ** -0.5)

    kw = mm(_C_KW, _C_END)
    wscale = (N_IDX_HEADS ** -0.5) * (D_IDX ** -0.5)
    kw = kw * jnp.where(lane >= D_IDX, wscale, 1.0)
    kw_ref[...] = kw
    kw16_ref[...] = kw.astype(BF16)
    kit_ref[0] = kw.T[:D_IDX]


def _project(x2d, g_pre, w_cat, cos_t, sin_t, tm, seq):
    rows = x2d.shape[0]
    d_model = x2d.shape[1]
    n_tab = cos_t.shape[0] // tm
    tiles = seq // tm
    row_spec = lambda w: pl.BlockSpec((tm, w), lambda i: (i, 0))
    t_spec = lambda w: pl.BlockSpec((1, w, tm), lambda i: (i // tiles, 0, i % tiles))
    outs = [
        ("qa", D_A, BF16), ("ka_t", LANES, F32), ("va_t", LANES, F32), ("k16", LANES, BF16), ("v16", LANES, BF16),
        ("qi", N_IDX_HEADS * D_IDX, BF16), ("ga", D_A, F32), ("gb", D_B, F32),
        ("qb", D_B, BF16), ("kb", D_B, BF16), ("vb", D_B, BF16), ("kw", LANES, F32), ("kw16", LANES, BF16),
        ("ki_t", D_IDX, F32),
    ]
    transposed = lambda n: n.endswith("_t")
    res = pl.pallas_call(
        _proj_kernel,
        grid=(rows // tm,),
        in_specs=[
            row_spec(d_model),
            pl.BlockSpec((1, d_model), lambda i: (0, 0)),
            pl.BlockSpec(w_cat.shape, lambda i: (0, 0)),
            pl.BlockSpec((tm, LANES), lambda i: (i % n_tab, 0)),
            pl.BlockSpec((tm, LANES), lambda i: (i % n_tab, 0)),
        ],
        out_specs=[t_spec(w) if transposed(n) else row_spec(w) for n, w, _ in outs],
        out_shape=[jax.ShapeDtypeStruct((rows // seq, w, seq) if transposed(n) else (rows, w), dt)
                   for n, w, dt in outs],
        scratch_shapes=[pltpu.VMEM((tm, d_model), BF16)],
        compiler_params=pltpu.CompilerParams(dimension_semantics=("arbitrary",), vmem_limit_bytes=VMEM_LIMIT),
        name="proj",
    )(x2d, g_pre, w_cat, cos_t, sin_t)
    return {n: r for (n, _, _), r in zip(outs, res)}


def _dsa_prompt_kernel(rb_ref, qi_ref, kwq_ref, kw16_ref, qa_ref, k16_ref, v16_ref, oa_ref,
                       sc, bt, vt, lgs, acc_scr, cut_scr, *, topk, search_blk, score_blk_n, att_blk_n):
    b = pl.program_id(0)
    j = pl.program_id(1)
    row = lax.broadcasted_iota(I32, (BLK, BLK), 0)
    col = lax.broadcasted_iota(I32, (BLK, BLK), 1)

    @pl.when((b == 0) & (j == 0))
    def _init_bias():
        for d in range(2):
            bkt = _t5_bucket(col - row + d * BLK)
            for h in range(N_HEADS_A):
                bt[d, h] = _bias_from_bucket(bkt, rb_ref, h) - rb_ref[N_BUCKETS - 1, h]
        bt[2] = jnp.zeros(bt.shape[1:], F32)

    @pl.when(j == 0)
    def _clear_values():
        vt[...] = jnp.zeros(vt.shape, BF16)

    vblk = v16_ref[pl.ds(pl.multiple_of(j * BLK, BLK), BLK), :]
    vt[j] = vblk.astype(F32).T.astype(BF16)

    qi = qi_ref[...]
    qi_stack = jnp.concatenate([qi[:, h * D_IDX:(h + 1) * D_IDX] for h in range(N_IDX_HEADS)], axis=0)
    wi_t = kwq_ref[...].T
    qpos = col + j * BLK

    def score_body(c, carry):
        for u in range(score_blk_n):
            i = c * score_blk_n + u
            kib = kw16_ref[pl.ds(pl.multiple_of(i * BLK, BLK), BLK), :][:, :D_IDX]
            s = _nt_dot(kib, qi_stack)
            acc = jnp.zeros((BLK, BLK), F32)
            for h in range(N_IDX_HEADS):
                acc = acc + wi_t[D_IDX + h:D_IDX + h + 1, :] * jnp.maximum(s[:, h * BLK:(h + 1) * BLK], 0.0)
            sc[i] = jnp.where(row + i * BLK <= qpos, acc, -jnp.inf)
        return carry

    n_score = (j + score_blk_n) // score_blk_n
    lax.fori_loop(0, n_score, score_body, 0)

    n_chunk = (j + search_blk) // search_blk

    def fill_body(i, c):
        sc[i] = jnp.full((BLK, BLK), -jnp.inf, F32)
        return c

    lax.fori_loop(n_score * score_blk_n, n_chunk * search_blk, fill_body, 0)

    def col_sum(w):
        return jnp.sum(w.reshape(BLK // SUBLANES, SUBLANES, BLK), axis=0)

    def count_ge(cand):
        def body(c, acc):
            for u in range(search_blk):
                acc = acc + col_sum(jnp.where(sc[c * search_blk + u] >= cand, 1.0, 0.0))
            return acc
        acc = lax.fori_loop(0, n_chunk, body, jnp.zeros((SUBLANES, BLK), F32))
        return jnp.sum(acc, axis=0, keepdims=True)

    kf = float(topk)
    c0 = count_ge(jnp.zeros((1, BLK), F32))
    key0 = jnp.where(c0 >= kf, 0, INT_MIN).astype(I32)

    def bit_body(it, key):
        cand = key | jnp.left_shift(jnp.int32(1), 30 - it)
        cnt = count_ge(_key_to_float(cand))
        return jnp.where(cnt >= kf, cand, key)

    n_bits = jnp.where((j + 1) * BLK <= topk, 0, 31)
    key = lax.fori_loop(0, n_bits, bit_body, key0)
    key = jnp.maximum(key, KEY_NEG_FLT_MAX)
    thr = _key_to_float(key)

    def count2_body(c, accs):
        a_gt, a_ge = accs
        for u in range(search_blk):
            t = sc[c * search_blk + u]
            a_gt = a_gt + col_sum(jnp.where(t > thr, 1.0, 0.0))
            a_ge = a_ge + col_sum(jnp.where(t >= thr, 1.0, 0.0))
        return a_gt, a_ge

    z8 = jnp.zeros((SUBLANES, BLK), F32)
    a_gt, a_ge = lax.fori_loop(0, n_chunk, count2_body, (z8, z8))
    cnt_gt = jnp.sum(a_gt, axis=0, keepdims=True)
    cnt_ge = jnp.sum(a_ge, axis=0, keepdims=True)
    need = kf - cnt_gt
    straddle = jnp.max(jnp.where(cnt_ge > kf, 1.0, 0.0))

    n_keys_max = sc.shape[0] * BLK
    cut_scr[...] = jnp.full(cut_scr.shape, 2 * n_keys_max, I32)

    @pl.when(straddle > 0.0)
    def _tie_cut():
        def count_tie(cand):
            def body(i, acc):
                t = sc[i]
                hit = jnp.where(t == thr, jnp.where(row + i * BLK < cand, 1.0, 0.0), 0.0)
                return acc + col_sum(hit)
            acc = lax.fori_loop(0, j + 1, body, z8)
            return jnp.sum(acc, axis=0, keepdims=True)

        def cut_body(it, cut):
            cand = cut | jnp.left_shift(jnp.int32(1), (n_keys_max.bit_length() - 1) - it)
            return jnp.where(count_tie(cand) <= need, cand, cut)

        cut = lax.fori_loop(0, n_keys_max.bit_length(), cut_body, jnp.zeros((1, BLK), I32))
        cut_scr[...] = jnp.broadcast_to(cut, cut_scr.shape)

    cut = cut_scr[0:1, :]

    def mask_body(i, c):
        t = sc[i]
        tie_keep = jnp.where(row + i * BLK < cut, 0.0, -jnp.inf)
        sc[i] = jnp.where(t > thr, 0.0, jnp.where(t == thr, tie_keep, -jnp.inf))
        return c

    lax.fori_loop(0, j + 1, mask_body, 0)

    qa = qa_ref[...]
    g_per_kv = N_HEADS_A // N_KV_A
    q_stack = [jnp.concatenate([qa[:, (n * g_per_kv + g) * HEAD_DIM:(n * g_per_kv + g + 1) * HEAD_DIM]
                                for g in range(g_per_kv)], axis=0) for n in range(N_KV_A)]

    def col_max(w):
        return jnp.max(w.reshape(BLK // SUBLANES, SUBLANES, BLK), axis=0)

    n_att = (j + att_blk_n) // att_blk_n

    def logits_body(c, m8):
        m8 = list(m8)
        for u in range(att_blk_n):
            i = c * att_blk_n + u
            kblk = k16_ref[pl.ds(pl.multiple_of(i * BLK, BLK), BLK), :]
            nm = sc[i]
            near = jnp.clip(j - i, 0, 2)
            for n in range(N_KV_A):
                lg = _nt_dot(kblk[:, n * HEAD_DIM:(n + 1) * HEAD_DIM], q_stack[n])
                for g in range(g_per_kv):
                    h = n * g_per_kv + g
                    x = lg[:, g * BLK:(g + 1) * BLK] + nm + bt[near, h]
                    lgs[i, h] = x
                    m8[h] = jnp.maximum(m8[h], col_max(x))
        return tuple(m8)

    neg8 = jnp.full((SUBLANES, BLK), NEG_BIG, F32)
    m8 = lax.fori_loop(0, n_att, logits_body, (neg8,) * N_HEADS_A)
    m_row = [jnp.max(m8[h], axis=0, keepdims=True) for h in range(N_HEADS_A)]

    acc_scr[...] = jnp.zeros(acc_scr.shape, F32)

    def pv_body(c, l8):
        l8 = list(l8)
        for u in range(att_blk_n):
            i = c * att_blk_n + u
            vti = vt[i]
            for n in range(N_KV_A):
                ps = []
                for g in range(g_per_kv):
                    h = n * g_per_kv + g
                    p = jnp.exp(lgs[i, h] - m_row[h])
                    l8[h] = l8[h] + col_sum(p)
                    ps.append(p.astype(BF16))
                acc_scr[n] += _dot(vti[n * HEAD_DIM:(n + 1) * HEAD_DIM, :], jnp.concatenate(ps, axis=1))
        return tuple(l8)

    l8 = lax.fori_loop(0, n_att, pv_body, (z8,) * N_HEADS_A)

    parts = []
    for n in range(N_KV_A):
        acc = acc_scr[n]
        for g in range(g_per_kv):
            h = n * g_per_kv + g
            parts.append(acc[:, g * BLK:(g + 1) * BLK] * (1.0 / jnp.sum(l8[h], axis=0, keepdims=True)))
    oa_ref[...] = jnp.concatenate(parts, axis=0).T


def _dsa_prompt(p, rel_bias, batch, seq):
    nblk = seq // BLK
    topk = min(TOPK_MAX, seq // 4)
    search_blk = 4 if nblk % 4 == 0 else 1
    score_blk_n = search_blk
    att_blk_n = search_blk
    qrow = lambda w: pl.BlockSpec((BLK, w), lambda b, j: (b * nblk + j, 0))
    brow = lambda w: pl.BlockSpec((seq, w), lambda b, j: (b, 0))
    return pl.pallas_call(
        functools.partial(_dsa_prompt_kernel, topk=topk, search_blk=search_blk, score_blk_n=score_blk_n,
                          att_blk_n=att_blk_n),
        grid=(batch, nblk),
        in_specs=[
            pl.BlockSpec(memory_space=pltpu.SMEM),
            qrow(N_IDX_HEADS * D_IDX), qrow(LANES), brow(LANES), qrow(D_A), brow(LANES), brow(LANES),
        ],
        out_specs=qrow(D_A),
        out_shape=jax.ShapeDtypeStruct((batch * seq, D_A), F32),
        scratch_shapes=[
            pltpu.VMEM((nblk, BLK, BLK), F32),
            pltpu.VMEM((3, N_HEADS_A, BLK, BLK), F32),
            pltpu.VMEM((nblk, BLK, BLK), BF16),
            pltpu.VMEM((nblk, N_HEADS_A, BLK, BLK), F32),
            pltpu.VMEM((N_KV_A, HEAD_DIM, (N_HEADS_A // N_KV_A) * BLK), F32),
            pltpu.VMEM((SUBLANES, BLK), I32),
        ],
        compiler_params=pltpu.CompilerParams(dimension_semantics=("arbitrary", "arbitrary"),
                                             vmem_limit_bytes=VMEM_LIMIT),
        name="dsa_prompt",
    )(rel_bias, p["qi"], p["kw"], p["kw16"], p["qa"], p["k16"], p["v16"])


def _retention_kernel(q_ref, k_ref, v_ref, s0_ref, dmat_ref, din_ref, dout_ref, decm_ref, bdm_ref,
                      yn_ref, s_out_ref, s_scr, *, n_sub):
    c = pl.program_id(1)
    chunk = din_ref.shape[0]

    @pl.when(c == 0)
    def _load_state():
        s_scr[...] = s0_ref[0]

    lo = lax.broadcasted_iota(I32, (chunk, LANES), 1) < HEAD_DIM
    zero = jnp.zeros((), BF16)
    for p in range(N_HEADS_B // 2):
        sl = slice(p * LANES, (p + 1) * LANES)
        st = s_scr[p]
        for u in range(n_sub):
            rows = slice(u * chunk, (u + 1) * chunk)
            q, k, v = q_ref[0, rows, sl], k_ref[0, rows, sl], v_ref[0, rows, sl]
            qd = (q.astype(F32) * din_ref[:, sl]).astype(BF16)
            kd = (k.astype(F32) * dout_ref[:, sl]).astype(BF16)
            vt = v.astype(F32).T.astype(BF16)
            s_a = (_nt_dot(jnp.where(lo, k, zero), q) * dmat_ref[2 * p]).astype(BF16)
            s_b = (_nt_dot(jnp.where(lo, zero, k), q) * dmat_ref[2 * p + 1]).astype(BF16)
            intra = jnp.concatenate([_dot(vt[:HEAD_DIM], s_a), _dot(vt[HEAD_DIM:], s_b)], axis=0)
            o = (_nt_dot(st.astype(BF16), qd) + intra).reshape(2, HEAD_DIM, chunk)
            dlt = o - jnp.mean(o, axis=1, keepdims=True)
            var = jnp.mean(dlt * dlt, axis=1, keepdims=True)
            yn_ref[0, rows, sl] = (dlt * lax.rsqrt(var + EPS)).reshape(LANES, chunk).T
            st = st * decm_ref[p] + _dot(vt, kd) * bdm_ref[...]
        s_scr[p] = st

    s_out_ref[0] = s_scr[...]


def _ret_tables(chunk, n_real):
    h = N_HEADS_B
    log_g = jnp.log1p(-jnp.exp2(-5.0 - jnp.arange(h, dtype=F32)))
    i = jnp.arange(chunk, dtype=F32)
    diff = i[:, None] - i[None, :]
    dmat = jnp.where(diff >= 0, jnp.exp(jnp.maximum(diff, 0.0)[None] * log_g[:, None, None]), 0.0)
    dmat_t = jnp.swapaxes(dmat, 1, 2)
    dec_in = jnp.exp((i + 1.0)[:, None] * log_g[None, :])
    dec_out = jnp.exp((n_real - 1.0 - i)[:, None] * log_g[None, :])
    dec_chunk = jnp.exp(n_real * log_g)
    din = jnp.repeat(dec_in, HEAD_DIM, axis=1)
    dout = jnp.repeat(dec_out, HEAD_DIM, axis=1)
    head_of = np.arange(LANES) // HEAD_DIM
    bd = jnp.asarray((head_of[:, None] == head_of[None, :]).astype(np.float32))
    dc = jnp.repeat(dec_chunk, HEAD_DIM).reshape(h // 2, LANES)
    decm = dc[:, :, None] * bd[None]
    return dmat_t, din, dout, decm, bd


def _retention(q3, k3, v3, s0t, n_real):
    batch, seq, _ = q3.shape
    chunk = min(RET_CHUNK, seq)
    n_sub = _largest_divisor(seq // chunk, 4)
    dmat_t, din, dout, decm, bd = _ret_tables(chunk, n_real)
    tok = pl.BlockSpec((1, n_sub * chunk, D_B), lambda b, c: (b, c, 0))
    st = pl.BlockSpec((1, N_HEADS_B // 2, LANES, LANES), lambda b, c: (b, 0, 0, 0))
    full = lambda a: pl.BlockSpec(a.shape, lambda b, c: (0,) * a.ndim)
    return pl.pallas_call(
        functools.partial(_retention_kernel, n_sub=n_sub),
        grid=(batch, seq // (n_sub * chunk)),
        in_specs=[tok, tok, tok, st, full(dmat_t), full(din), full(dout), full(decm), full(bd)],
        out_specs=[tok, st],
        out_shape=[jax.ShapeDtypeStruct((batch, seq, D_B), F32),
                   jax.ShapeDtypeStruct((batch, N_HEADS_B // 2, LANES, LANES), F32)],
        scratch_shapes=[pltpu.VMEM((N_HEADS_B // 2, LANES, LANES), F32)],
        compiler_params=pltpu.CompilerParams(dimension_semantics=("arbitrary", "arbitrary"),
                                             vmem_limit_bytes=VMEM_LIMIT),
        name="retention",
    )(q3, k3, v3, s0t, dmat_t, din, dout, decm, bd)


def _state_to_pairs(s):
    b = s.shape[0]
    s = jnp.swapaxes(s, -1, -2).reshape(b, N_HEADS_B // 2, 2, HEAD_DIM, HEAD_DIM)
    z = jnp.zeros_like(s[:, :, 0])
    top = jnp.concatenate([s[:, :, 0], z], axis=-1)
    bot = jnp.concatenate([z, s[:, :, 1]], axis=-1)
    return jnp.concatenate([top, bot], axis=-2)


def _pairs_to_state(sp):
    b = sp.shape[0]
    a = sp[:, :, :HEAD_DIM, :HEAD_DIM]
    d = sp[:, :, HEAD_DIM:, HEAD_DIM:]
    return jnp.swapaxes(jnp.stack([a, d], axis=2).reshape(b, N_HEADS_B, HEAD_DIM, HEAD_DIM), -1, -2)


def _merge_kernel(x_ref, oa_ref, yn_ref, ga_ref, gb_ref, p_ref, wo_ref, gp_ref, wup_ref, wg_ref, out_ref):
    ga = ga_ref[...]
    gb = gb_ref[...]
    ya = (ga * _sigmoid(ga) * oa_ref[...]).astype(BF16)
    yb = (gb * _sigmoid(gb) * yn_ref[...]).astype(BF16)
    y = _dot(ya, wo_ref[0:D_A, :]) + _dot(yb, wo_ref[D_A:D_A + D_B, :])
    ms = jnp.mean(y * y, axis=-1, keepdims=True)
    x1 = x_ref[...] + y * lax.rsqrt(ms + EPS) * gp_ref[...]
    ple = _dot(p_ref[...].astype(BF16), wup_ref[...])
    gate = _sigmoid(_dot(x1.astype(BF16), wg_ref[...]))
    out_ref[...] = x1 + ple * gate


def _merge(x2d, oa, yn, ga, gb, p2d, wo16, g_post, wup16, wg16, tm):
    rows, d_model = x2d.shape
    row = lambda w: pl.BlockSpec((tm, w), lambda i: (i, 0))
    full = lambda a: pl.BlockSpec(a.shape, lambda i: (0, 0))
    return pl.pallas_call(
        _merge_kernel,
        grid=(rows // tm,),
        in_specs=[row(d_model), row(D_A), row(D_B), row(D_A), row(D_B), row(p2d.shape[1]),
                  full(wo16), full(g_post), full(wup16), full(wg16)],
        out_specs=row(d_model),
        out_shape=jax.ShapeDtypeStruct((rows, d_model), F32),
        compiler_params=pltpu.CompilerParams(dimension_semantics=("arbitrary",), vmem_limit_bytes=VMEM_LIMIT),
        name="merge",
    )(x2d, oa, yn, ga, gb, p2d, wo16, g_post, wup16, wg16)


def _page_copy(pages_hbm, buf, sem, pt_ref, b, p, slot):
    return pltpu.make_async_copy(pages_hbm.at[pt_ref[b, p]], buf.at[slot, p], sem.at[slot])


def _start_batch_pages(streams, pt_ref, b, slot, n_pages):
    def body(p, c):
        for pages_hbm, buf, sem in streams:
            _page_copy(pages_hbm, buf, sem, pt_ref, b, p, slot).start()
        return c
    lax.fori_loop(0, n_pages, body, 0)


def _wait_batch_pages(streams, pt_ref, b, slot, n_pages):
    for pages_hbm, buf, sem in streams:
        for p in range(n_pages):
            _page_copy(pages_hbm, buf, sem, pt_ref, b, p, slot).wait()


def _sample_index_kernel(pt_ref, qi_ref, w_ref, pages_hbm, out_ref, buf, sem, *, n_pages, unroll):
    b = pl.program_id(0)
    slot = b % 2
    streams = [(pages_hbm, buf, sem)]

    @pl.when(b == 0)
    def _first():
        _start_batch_pages(streams, pt_ref, 0, 0, n_pages)

    _wait_batch_pages(streams, pt_ref, b, slot, n_pages)

    qi = qi_ref[0]
    w = w_ref[0]
    n_q = qi.shape[0] // N_IDX_HEADS

    def run(prefetch):
        def body(c, carry):
            for u in range(unroll):
                p = c * unroll + u
                if prefetch:
                    _page_copy(pages_hbm, buf, sem, pt_ref, b + 1, p, 1 - slot).start()
                s = _dot(qi, buf[slot, p].astype(BF16))
                r = jnp.maximum(s, 0.0) * w
                out_ref[0, c, :, u * PAGE_SIZE:(u + 1) * PAGE_SIZE] = jnp.sum(
                    r.reshape(n_q, N_IDX_HEADS, PAGE_SIZE), axis=1)
            return carry
        lax.fori_loop(0, n_pages // unroll, body, 0)

    has_next = b + 1 < pl.num_programs(0)
    pl.when(has_next)(lambda: run(True))
    pl.when(jnp.logical_not(has_next))(lambda: run(False))


def _sample_index(page_table, qi_qh, w_qh, kidx_t, unroll):
    batch, n_pages = page_table.shape
    n_q = qi_qh.shape[1] // N_IDX_HEADS
    groups = n_pages // unroll
    grid_spec = pltpu.PrefetchScalarGridSpec(
        num_scalar_prefetch=1,
        grid=(batch,),
        in_specs=[pl.BlockSpec((1,) + qi_qh.shape[1:], lambda b, pt: (b, 0, 0)),
                  pl.BlockSpec((1,) + w_qh.shape[1:], lambda b, pt: (b, 0, 0)),
                  pl.BlockSpec(memory_space=pl.ANY)],
        out_specs=pl.BlockSpec((1, groups, n_q, unroll * PAGE_SIZE), lambda b, pt: (b, 0, 0, 0)),
        scratch_shapes=[pltpu.VMEM((2, n_pages) + kidx_t.shape[1:], kidx_t.dtype),
                        pltpu.SemaphoreType.DMA((2,))],
    )
    return pl.pallas_call(
        functools.partial(_sample_index_kernel, n_pages=n_pages, unroll=unroll),
        grid_spec=grid_spec,
        out_shape=jax.ShapeDtypeStruct((batch, groups, n_q, unroll * PAGE_SIZE), F32),
        compiler_params=pltpu.CompilerParams(dimension_semantics=("arbitrary",), vmem_limit_bytes=VMEM_LIMIT),
        name="sample_index",
    )(page_table, qi_qh, w_qh, kidx_t)


def _sample_select_kernel(sp_ref, qi_ref, w_ref, kw16_ref, thr_ref, cut_ref, nmn_ref, snew_scr,
                          *, topk, n_q, lane_chunk):
    rows, past = sp_ref.shape
    r_i = lax.broadcasted_iota(I32, (rows, LANES), 0)
    l_i = lax.broadcasted_iota(I32, (rows, LANES), 1)

    s = _nt_dot(qi_ref[...], kw16_ref[...][:, :D_IDX])
    r = jnp.maximum(s, 0.0) * w_ref[...]
    s_new = jnp.sum(r.reshape(rows, N_IDX_HEADS, LANES), axis=1)
    ok = ((l_i // n_q) == (r_i // n_q)) & ((l_i % n_q) <= (r_i % n_q))
    snew_scr[...] = jnp.where(ok, s_new, -jnp.inf)

    n_chunks = past // lane_chunk
    tiles = lane_chunk // LANES
    kf = float(topk)

    def count(ind):
        acc = ind(snew_scr[...], l_i + past)
        for ch in range(n_chunks):
            x = sp_ref[:, ch * lane_chunk:(ch + 1) * lane_chunk]
            for t in range(tiles):
                idx = l_i + (ch * lane_chunk + t * LANES)
                acc = acc + ind(x[:, t * LANES:(t + 1) * LANES], idx)
        return jnp.broadcast_to(jnp.sum(acc, axis=1, keepdims=True), (rows, LANES))

    c0 = count(lambda x, idx: jnp.where(x >= 0.0, 1.0, 0.0))
    key0 = jnp.where(c0 >= kf, 0, INT_MIN).astype(I32)

    def bit_body(it, key):
        cand = key | jnp.left_shift(jnp.int32(1), 30 - it)
        cf = _key_to_float(cand)
        return jnp.where(count(lambda x, idx: jnp.where(x >= cf, 1.0, 0.0)) >= kf, cand, key)

    key = lax.fori_loop(0, 31, bit_body, key0)
    key = jnp.maximum(key, KEY_NEG_FLT_MAX)
    thr = _key_to_float(key)
    cnt_gt = count(lambda x, idx: jnp.where(x > thr, 1.0, 0.0))
    cnt_ge = count(lambda x, idx: jnp.where(x >= thr, 1.0, 0.0))
    need = kf - cnt_gt
    straddle = jnp.max(jnp.where(cnt_ge > kf, 1.0, 0.0))

    idx_bits = (past + LANES).bit_length()
    thr_ref[...] = thr
    cut_ref[...] = jnp.full((rows, LANES), 1 << idx_bits, I32)

    @pl.when(straddle > 0.0)
    def _tie_cut():
        def cut_body(it, cut):
            cand = cut | jnp.left_shift(jnp.int32(1), (idx_bits - 1) - it)
            cnt = count(lambda x, idx: jnp.where(x == thr, jnp.where(idx < cand, 1.0, 0.0), 0.0))
            return jnp.where(cnt <= need, cand, cut)
        cut_ref[...] = lax.fori_loop(0, idx_bits, cut_body, jnp.zeros((rows, LANES), I32))

    cut = cut_ref[...]
    sn = snew_scr[...]
    tie_keep = jnp.where(l_i + past < cut, 0.0, -jnp.inf)
    nmn_ref[...] = jnp.where(sn > thr, 0.0, jnp.where(sn == thr, tie_keep, -jnp.inf))


def _sample_select(scores_past, qi_rows, w_rows, kw16_s, topk, n_q):
    rows, past = scores_past.shape
    lane_chunk = 2048 if past % 2048 == 0 else LANES
    full = lambda a: pl.BlockSpec(a.shape, lambda i: (0,) * a.ndim)
    o = jax.ShapeDtypeStruct((rows, LANES), F32)
    return pl.pallas_call(
        functools.partial(_sample_select_kernel, topk=topk, n_q=n_q, lane_chunk=lane_chunk),
        grid=(1,),
        in_specs=[full(scores_past), full(qi_rows), full(w_rows), full(kw16_s)],
        out_specs=[pl.BlockSpec((rows, LANES), lambda i: (0, 0))] * 3,
        out_shape=[o, jax.ShapeDtypeStruct((rows, LANES), I32), o],
        scratch_shapes=[pltpu.VMEM((rows, LANES), F32)],
        compiler_params=pltpu.CompilerParams(dimension_semantics=("arbitrary",), vmem_limit_bytes=VMEM_LIMIT),
        name="sample_select",
    )(scores_past, qi_rows, w_rows, kw16_s)


def _sample_attn_kernel(pt_ref, rbc_ref, q_ref, sp_ref, thr_ref, cut_ref, nmn_ref, k16n_ref, v16n_ref,
                        k_hbm, v_hbm, o_ref, kbuf, vbuf, ksem, vsem, lgs, bias_scr, mx_scr,
                        *, n_pages, n_q, unroll):
    b = pl.program_id(0)
    slot = b % 2
    rows = n_q * N_HEADS_A
    groups = n_pages // unroll
    streams = [(k_hbm, kbuf, ksem), (v_hbm, vbuf, vsem)]
    lane = lax.broadcasted_iota(I32, (rows, LANES), 1)
    rq = lax.broadcasted_iota(I32, (rows, LANES), 0) // N_HEADS_A
    expand = lambda a: jnp.concatenate(
        [jnp.broadcast_to(a[t:t + 1], (N_HEADS_A, LANES)) for t in range(n_q)], axis=0)

    @pl.when(b == 0)
    def _first():
        _start_batch_pages(streams, pt_ref, 0, 0, n_pages)
        far = jnp.concatenate([rbc_ref[N_BUCKETS - 1]] * n_q, axis=0)
        for t, dist in enumerate((PAGE_SIZE + rq - lane, rq - lane % n_q)):
            bkt = _t5_bucket(dist)
            tile = jnp.zeros((rows, LANES), F32)
            for k in range(N_BUCKETS - 1):
                tile = jnp.where(bkt == k, jnp.concatenate([rbc_ref[k]] * n_q, axis=0) - far, tile)
            bias_scr[t] = tile

    _wait_batch_pages(streams, pt_ref, b, slot, n_pages)

    thr = expand(thr_ref[0])
    cut = expand(cut_ref[0])
    q = q_ref[0]

    def logits_pass(prefetch):
        def body(c, mx):
            for u in range(unroll):
                p = c * unroll + u
                if prefetch:
                    for pages_hbm, buf, sem in streams:
                        _page_copy(pages_hbm, buf, sem, pt_ref, b + 1, p, 1 - slot).start()
                lg = _dot(q, kbuf[slot, p].astype(BF16))
                sc = expand(sp_ref[0, c, :, u * PAGE_SIZE:(u + 1) * PAGE_SIZE])
                tie_keep = jnp.where(lane + p * PAGE_SIZE < cut, 0.0, -jnp.inf)
                x = lg + jnp.where(sc > thr, 0.0, jnp.where(sc == thr, tie_keep, -jnp.inf))
                if u == unroll - 1:
                    x = x + bias_scr[0] * jnp.where(c == groups - 1, 1.0, 0.0)
                lgs[p] = x
                mx = jnp.maximum(mx, x)
            return mx
        mx_scr[...] = lax.fori_loop(0, groups, body, jnp.full((rows, LANES), NEG_BIG, F32))

    has_next = b + 1 < pl.num_programs(0)
    pl.when(has_next)(lambda: logits_pass(True))
    pl.when(jnp.logical_not(has_next))(lambda: logits_pass(False))

    xn = _nt_dot(q, k16n_ref[...]) + bias_scr[1] + expand(nmn_ref[0])
    m = jnp.max(jnp.maximum(mx_scr[...], xn), axis=1, keepdims=True)

    def pv_body(c, carry):
        lsum, acc = carry
        for u in range(unroll):
            p = c * unroll + u
            pr = jnp.exp(lgs[p] - m)
            lsum = lsum + pr
            acc = acc + _nt_dot(pr.astype(BF16), vbuf[slot, p].astype(BF16))
        return lsum, acc

    pn = jnp.exp(xn - m)
    lsum, acc = lax.fori_loop(0, groups, pv_body, (pn, _dot(pn.astype(BF16), v16n_ref[...])))
    out = acc * (1.0 / jnp.sum(lsum, axis=1, keepdims=True))
    head = lax.broadcasted_iota(I32, (rows, HEAD_DIM), 0) % N_HEADS_A
    o_ref[0] = jnp.where(head < N_HEADS_A // N_KV_A, out[:, :HEAD_DIM], out[:, HEAD_DIM:])


def _sample_attn(page_table, rb_col, q_bd, scores4, thr3, cut3, nmn3, k16n, v16n, k_pages_t, v_pages_t, unroll):
    batch, n_pages = page_table.shape
    n_q = scores4.shape[2]
    rows = q_bd.shape[1]
    per_b = lambda a: pl.BlockSpec((1,) + a.shape[1:], lambda b, pt: (b,) + (0,) * (a.ndim - 1))
    full = lambda a: pl.BlockSpec(a.shape, lambda b, pt: (0,) * a.ndim)
    page_buf = pltpu.VMEM((2, n_pages) + k_pages_t.shape[1:], k_pages_t.dtype)
    grid_spec = pltpu.PrefetchScalarGridSpec(
        num_scalar_prefetch=1,
        grid=(batch,),
        in_specs=[full(rb_col), per_b(q_bd), per_b(scores4), per_b(thr3), per_b(cut3), per_b(nmn3),
                  full(k16n), full(v16n), pl.BlockSpec(memory_space=pl.ANY), pl.BlockSpec(memory_space=pl.ANY)],
        out_specs=pl.BlockSpec((1, rows, HEAD_DIM), lambda b, pt: (b, 0, 0)),
        scratch_shapes=[page_buf, page_buf, pltpu.SemaphoreType.DMA((2,)), pltpu.SemaphoreType.DMA((2,)),
                        pltpu.VMEM((n_pages, rows, LANES), F32), pltpu.VMEM((2, rows, LANES), F32),
                        pltpu.VMEM((rows, LANES), F32)],
    )
    return pl.pallas_call(
        functools.partial(_sample_attn_kernel, n_pages=n_pages, n_q=n_q, unroll=unroll),
        grid_spec=grid_spec,
        out_shape=jax.ShapeDtypeStruct((batch, rows, HEAD_DIM), F32),
        compiler_params=pltpu.CompilerParams(dimension_semantics=("arbitrary",),
                                             vmem_limit_bytes=VMEM_LIMIT_PAGED),
        name="sample_attn",
    )(page_table, rb_col, q_bd, scores4, thr3, cut3, nmn3, k16n, v16n, k_pages_t, v_pages_t)


def _rope_tables(pos):
    half = HEAD_DIM // 2
    inv = ROPE_BASE ** (-jnp.arange(half, dtype=F32) / half)
    ang = pos.astype(F32)[:, None] * inv[None, :]
    cos = jnp.cos(ang)
    sin = jnp.sin(ang)
    reps = LANES // HEAD_DIM
    cos_t = jnp.tile(jnp.concatenate([cos, cos], axis=1), (1, reps))
    sin_t = jnp.tile(jnp.concatenate([-sin, sin], axis=1), (1, reps))
    return cos_t, sin_t


def _cat_weight(w_in):
    offs = np.cumsum((0,) + SPLIT_SIZES)
    qa, ka, va, qi, ki, wi, ga, qb, kb, vb, gb = [w_in[:, offs[i]:offs[i + 1]] for i in range(len(SPLIT_SIZES))]
    pad = jnp.zeros((w_in.shape[0], LANES - D_IDX - N_IDX_HEADS), w_in.dtype)
    return jnp.concatenate([qa, ka, va, qi, ga, qb, kb, vb, gb, ki, wi, pad], axis=1).astype(BF16)


def _pages_t(cache):
    pool, page = cache.shape[:2]
    return jnp.transpose(cache, (0, 2, 3, 1)).reshape(pool, -1, page)


def _largest_divisor(n, cap):
    d = cap
    while n % d:
        d //= 2
    return d


def kernel(x_prompt, x_sample, cache_k, cache_v, cache_kidx, state_ret, page_table, p_prompt, p_sample,
           rel_bias, w_in, w_out, g_pre, g_post, w_ple_up, w_ple_gate):
    batch, seq, d_model = x_prompt.shape
    dec_b, dec_t, _ = x_sample.shape
    depth = w_in.shape[0]
    n_pages = page_table.shape[1]
    past = n_pages * PAGE_SIZE
    rows_s = dec_b * dec_t
    assert depth == 1 and rows_s == LANES and seq % BLK == 0

    w_cat = _cat_weight(w_in[0])
    gpre = g_pre[0].reshape(1, d_model)
    gpost = g_post[0].reshape(1, d_model)
    wo16 = w_out[0].astype(BF16)
    wup16 = w_ple_up[0].astype(BF16)
    wg16 = w_ple_gate[0].astype(BF16)

    tm = _largest_divisor(seq, 512)
    cos_p, sin_p = _rope_tables(jnp.arange(seq))
    xp2 = x_prompt.reshape(batch * seq, d_model)
    pp = _project(xp2, gpre, w_cat, cos_p, sin_p, tm, seq)
    oa_p = _dsa_prompt(pp, rel_bias, batch, seq)
    r3 = lambda a: a.reshape(batch, seq, D_B)
    s0_p = jnp.zeros((batch, N_HEADS_B // 2, LANES, LANES), F32)
    yn_p, sp_pairs = _retention(r3(pp["qb"]), r3(pp["kb"]), r3(pp["vb"]), s0_p, float(min(RET_CHUNK, seq)))
    y_prompt = _merge(xp2, oa_p, yn_p.reshape(batch * seq, D_B), pp["ga"], pp["gb"],
                      p_prompt[0].reshape(batch * seq, -1), wo16, gpost, wup16, wg16, tm)

    pos_s = past + jnp.arange(dec_t)
    cos_s, sin_s = _rope_tables(jnp.tile(pos_s, dec_b))
    xs2 = x_sample.reshape(rows_s, d_model)
    ps = _project(xs2, gpre, w_cat, cos_s, sin_s, rows_s, rows_s)
    topk_s = min(TOPK_MAX, (past + dec_t) // 4)

    qi_rows = ps["qi"].reshape(rows_s * N_IDX_HEADS, D_IDX)
    w_rows = jnp.broadcast_to(ps["kw"][:, D_IDX:D_IDX + N_IDX_HEADS].reshape(rows_s * N_IDX_HEADS, 1),
                              (rows_s * N_IDX_HEADS, LANES))
    unroll = _largest_divisor(n_pages, 32)
    scores4 = _sample_index(page_table, qi_rows.reshape(dec_b, dec_t * N_IDX_HEADS, D_IDX),
                            w_rows.reshape(dec_b, dec_t * N_IDX_HEADS, LANES),
                            jnp.transpose(cache_kidx[0], (0, 2, 1)), unroll)
    scores_past = scores4.transpose(0, 2, 1, 3).reshape(rows_s, past)
    thr, cut, nmn = _sample_select(scores_past, qi_rows, w_rows, ps["kw16"], topk_s, dec_t)

    kv_of_head = np.arange(N_HEADS_A) // (N_HEADS_A // N_KV_A)
    place = jnp.asarray((kv_of_head[:, None] == np.arange(N_KV_A)[None, :]).astype(np.float32)).astype(BF16)
    q_bd = (ps["qa"].reshape(dec_b, dec_t, N_HEADS_A, 1, HEAD_DIM) * place[None, None, :, :, None]).reshape(
        dec_b, dec_t * N_HEADS_A, N_KV_A * HEAD_DIM)
    rb_col = jnp.broadcast_to(rel_bias[:, :, None], rel_bias.shape + (LANES,))
    b3 = lambda a: a.reshape(dec_b, dec_t, LANES)
    o_qh = _sample_attn(page_table, rb_col, q_bd, scores4, b3(thr), b3(cut), b3(nmn),
                        ps["k16"], ps["v16"], _pages_t(cache_k[0]), _pages_t(cache_v[0]), unroll)
    oa_s = o_qh.reshape(rows_s, D_A)

    chunk_s = RET_CHUNK
    padt = lambda a: jnp.pad(a.reshape(dec_b, dec_t, D_B), ((0, 0), (0, chunk_s - dec_t), (0, 0)))
    yn_s, ss_pairs = _retention(padt(ps["qb"]), padt(ps["kb"]), padt(ps["vb"]),
                                _state_to_pairs(state_ret[0]), float(math.gcd(dec_t, RET_CHUNK)))
    y_sample = _merge(xs2, oa_s, yn_s[:, :dec_t].reshape(rows_s, D_B), ps["ga"], ps["gb"],
                      p_sample[0].reshape(rows_s, -1), wo16, gpost, wup16, wg16, rows_s)

    def kv(a_t, b, t):
        a = a_t.reshape(a_t.shape[0], N_KV_A, HEAD_DIM, -1).transpose(0, 3, 1, 2)
        return a.reshape(1, b, t, N_KV_A, HEAD_DIM)

    def ki(a_t, b, t):
        return a_t.transpose(0, 2, 1).reshape(1, b, t, D_IDX)

    return (
        y_prompt.reshape(batch, seq, d_model),
        y_sample.reshape(dec_b, dec_t, d_model),
        kv(pp["ka_t"], batch, seq), kv(pp["va_t"], batch, seq),
        ki(pp["ki_t"], batch, seq),
        _pairs_to_state(sp_pairs)[None].astype(state_ret.dtype),
        kv(ps["ka_t"], dec_b, dec_t), kv(ps["va_t"], dec_b, dec_t),
        ki(ps["ki_t"], dec_b, dec_t),
        _pairs_to_state(ss_pairs)[None].astype(state_ret.dtype),
    )
```

```python
import functools
import math

import jax
import jax.numpy as jnp
import numpy as np
from jax import lax
from jax.experimental import pallas as pl
from jax.experimental.pallas import tpu as pltpu

F32 = jnp.float32
BF16 = jnp.bfloat16
I32 = jnp.int32

HEAD_DIM = 64
N_HEADS_A = 8
N_KV_A = 2
N_IDX_HEADS = 8
D_IDX = 64
TOPK_MAX = 256
N_HEADS_B = 8
D_A = N_HEADS_A * HEAD_DIM
D_B = N_HEADS_B * HEAD_DIM
N_BUCKETS = 32
MAX_DISTANCE = 128
ROPE_BASE = 10000.0
RET_CHUNK = 128
PAGE_SIZE = 128
EPS = 1e-6
SPLIT_SIZES = (D_A, N_KV_A * HEAD_DIM, N_KV_A * HEAD_DIM, N_IDX_HEADS * D_IDX, D_IDX, N_IDX_HEADS,
               D_A, D_B, D_B, D_B, D_B)

LANES = 128
SUBLANES = 8
BLK = 128
VMEM_LIMIT = 48 * 1024 * 1024
VMEM_LIMIT_PAGED = 56 * 1024 * 1024

INT_MIN = -(2 ** 31)
KEY_NEG_FLT_MAX = INT_MIN + 0x00800000
NEG_BIG = -1e30

_C_QA, _C_KA, _C_VA, _C_QI, _C_GA, _C_QB, _C_KB, _C_VB, _C_GB, _C_KW, _C_END = (
    0, 512, 640, 768, 1280, 1792, 2304, 2816, 3328, 3840, 3968)


def _nt_dot(a, b):
    return lax.dot_general(a, b, (((1,), (1,)), ((), ())), preferred_element_type=F32)


def _dot(a, b):
    return jnp.dot(a, b, preferred_element_type=F32)


def _sigmoid(x):
    return 1.0 / (1.0 + jnp.exp(-x))


def _key_to_float(k):
    bits = jnp.where(k >= 0, k, k ^ 0x7FFFFFFF)
    return lax.bitcast_convert_type(bits, F32)


def _t5_bucket(n):
    max_exact = N_BUCKETS // 2
    n = jnp.maximum(n, 0)
    nf = jnp.maximum(n, 1).astype(F32)
    large = max_exact + jnp.floor(jnp.log(nf / max_exact) / math.log(MAX_DISTANCE / max_exact)
                                  * (N_BUCKETS - max_exact)).astype(I32)
    large = jnp.minimum(large, N_BUCKETS - 1)
    return jnp.where(n < max_exact, n, large)


def _bias_from_bucket(bkt, rb_ref, h):
    out = jnp.zeros(bkt.shape, F32)
    for k in range(N_BUCKETS):
        out = jnp.where(bkt == k, rb_ref[k, h], out)
    return out


def _proj_kernel(x_ref, g_ref, w_ref, cos_ref, sin_ref,
                 qa_ref, kat_ref, vat_ref, k16_ref, v16_ref, qi_ref, ga_ref, gb_ref,
                 qb_ref, kb_ref, vb_ref, kw_ref, kw16_ref, kit_ref, h_scr):
    x = x_ref[...]
    ms = jnp.mean(x * x, axis=-1, keepdims=True)
    h_scr[...] = (x * lax.rsqrt(ms + EPS) * g_ref[...]).astype(BF16)

    def mm(lo, hi):
        return _dot(h_scr[...], w_ref[:, lo:hi])

    qa_ref[...] = (mm(_C_QA, _C_KA) * (HEAD_DIM ** -0.5)).astype(BF16)
    ka = mm(_C_KA, _C_VA)
    kat_ref[0] = ka.T
    k16_ref[...] = ka.astype(BF16)
    va = mm(_C_VA, _C_QI)
    vat_ref[0] = va.T
    v16_ref[...] = va.astype(BF16)
    qi_ref[...] = mm(_C_QI, _C_GA).astype(BF16)
    ga_ref[...] = mm(_C_GA, _C_QB)
    gb_ref[...] = mm(_C_GB, _C_KW)
    vb_ref[...] = mm(_C_VB, _C_GB).astype(BF16)

    cos = cos_ref[...]
    sin = sin_ref[...]
    lane = lax.broadcasted_iota(I32, cos.shape, 1)
    first_half = (lane % HEAD_DIM) < (HEAD_DIM // 2)

    def rope(z, scale):
        outs = []
        for g in range(z.shape[1] // LANES):
            zg = z[:, g * LANES:(g + 1) * LANES]
            partner = jnp.where(first_half, pltpu.roll(zg, LANES - HEAD_DIM // 2, 1),
                                pltpu.roll(zg, HEAD_DIM // 2, 1))
            r = zg * cos + partner * sin
            if scale != 1.0:
                r = r * scale
            outs.append(r.astype(BF16))
        return jnp.concatenate(outs, axis=1)

    qb_ref[...] = rope(mm(_C_QB, _C_KB), 1.0)
    kb_ref[...] = rope(mm(_C_KB, _C_VB), HEAD_DIM ** -0.5)

    kw = mm(_C_KW, _C_END)
    wscale = (N_IDX_HEADS ** -0.5) * (D_IDX ** -0.5)
    kw = kw * jnp.where(lane >= D_IDX, wscale, 1.0)
    kw_ref[...] = kw
    kw16_ref[...] = kw.astype(BF16)
    kit_ref[0] = kw.T[:D_IDX]


def _project(x2d, g_pre, w_cat, cos_t, sin_t, tm, seq):
    rows = x2d.shape[0]
    d_model = x2d.shape[1]
    n_tab = cos_t.shape[0] // tm
    tiles = seq // tm
    row_spec = lambda w: pl.BlockSpec((tm, w), lambda i: (i, 0))
    t_spec = lambda w: pl.BlockSpec((1, w, tm), lambda i: (i // tiles, 0, i % tiles))
    outs = [
        ("qa", D_A, BF16), ("ka_t", LANES, F32), ("va_t", LANES, F32), ("k16", LANES, BF16), ("v16", LANES, BF16),
        ("qi", N_IDX_HEADS * D_IDX, BF16), ("ga", D_A, F32), ("gb", D_B, F32),
        ("qb", D_B, BF16), ("kb", D_B, BF16), ("vb", D_B, BF16), ("kw", LANES, F32), ("kw16", LANES, BF16),
        ("ki_t", D_IDX, F32),
    ]
    transposed = lambda n: n.endswith("_t")
    res = pl.pallas_call(
        _proj_kernel,
        grid=(rows // tm,),
        in_specs=[
            row_spec(d_model),
            pl.BlockSpec((1, d_model), lambda i: (0, 0)),
            pl.BlockSpec(w_cat.shape, lambda i: (0, 0)),
            pl.BlockSpec((tm, LANES), lambda i: (i % n_tab, 0)),
            pl.BlockSpec((tm, LANES), lambda i: (i % n_tab, 0)),
        ],
        out_specs=[t_spec(w) if transposed(n) else row_spec(w) for n, w, _ in outs],
        out_shape=[jax.ShapeDtypeStruct((rows // seq, w, seq) if transposed(n) else (rows, w), dt)
                   for n, w, dt in outs],
        scratch_shapes=[pltpu.VMEM((tm, d_model), BF16)],
        compiler_params=pltpu.CompilerParams(dimension_semantics=("arbitrary",), vmem_limit_bytes=VMEM_LIMIT),
        name="proj",
    )(x2d, g_pre, w_cat, cos_t, sin_t)
    return {n: r for (n, _, _), r in zip(outs, res)}


def _dsa_prompt_kernel(rb_ref, qi_ref, kwq_ref, kw16_ref, qa_ref, k16_ref, v16_ref, oa_ref,
                       sc, bt, vt, lgs, acc_scr, cut_scr, key_scr, *, topk, search_blk, score_blk_n, att_blk_n):
    b = pl.program_id(0)
    j = pl.program_id(1)
    row = lax.broadcasted_iota(I32, (BLK, BLK), 0)
    col = lax.broadcasted_iota(I32, (BLK, BLK), 1)

    @pl.when((b == 0) & (j == 0))
    def _init_bias():
        for d in range(2):
            bkt = _t5_bucket(col - row + d * BLK)
            for h in range(N_HEADS_A):
                bt[d, h] = _bias_from_bucket(bkt, rb_ref, h) - rb_ref[N_BUCKETS - 1, h]
        bt[2] = jnp.zeros(bt.shape[1:], F32)

    @pl.when(j == 0)
    def _clear_values():
        vt[...] = jnp.zeros(vt.shape, BF16)

    vblk = v16_ref[pl.ds(pl.multiple_of(j * BLK, BLK), BLK), :]
    vt[j] = vblk.astype(F32).T.astype(BF16)

    qi = qi_ref[...]
    qi_stack = jnp.concatenate([qi[:, h * D_IDX:(h + 1) * D_IDX] for h in range(N_IDX_HEADS)], axis=0)
    wi_t = kwq_ref[...].T
    qpos = col + j * BLK

    def score_body(c, carry):
        for u in range(score_blk_n):
            i = c * score_blk_n + u
            kib = kw16_ref[pl.ds(pl.multiple_of(i * BLK, BLK), BLK), :][:, :D_IDX]
            s = _nt_dot(kib, qi_stack)
            acc = jnp.zeros((BLK, BLK), F32)
            for h in range(N_IDX_HEADS):
                acc = acc + wi_t[D_IDX + h:D_IDX + h + 1, :] * jnp.maximum(s[:, h * BLK:(h + 1) * BLK], 0.0)
            sc[i] = jnp.where(row + i * BLK <= qpos, acc, -jnp.inf)
        return carry

    n_score = (j + score_blk_n) // score_blk_n
    lax.fori_loop(0, n_score, score_body, 0)

    n_chunk = (j + search_blk) // search_blk

    def fill_body(i, c):
        sc[i] = jnp.full((BLK, BLK), -jnp.inf, F32)
        return c

    lax.fori_loop(n_score * score_blk_n, n_chunk * search_blk, fill_body, 0)

    def col_sum(w):
        return jnp.sum(w.reshape(BLK // SUBLANES, SUBLANES, BLK), axis=0)

    qa = qa_ref[...]
    g_per_kv = N_HEADS_A // N_KV_A
    q_stack = [jnp.concatenate([qa[:, (n * g_per_kv + g) * HEAD_DIM:(n * g_per_kv + g + 1) * HEAD_DIM]
                                for g in range(g_per_kv)], axis=0) for n in range(N_KV_A)]

    kf = float(topk)
    n_units = search_blk * N_KV_A

    def search(n_groups):
        def count_ge(cand, unit=None):
            acc = jnp.zeros((SUBLANES, BLK), F32)
            for c in range(n_groups):
                for u in range(search_blk):
                    acc = acc + col_sum(jnp.where(sc[c * search_blk + u] >= cand, 1.0, 0.0))
                if unit is not None:
                    i = c * search_blk + unit[0]
                    n = unit[1]
                    lg = _nt_dot(k16_ref[i * BLK:(i + 1) * BLK, n * HEAD_DIM:(n + 1) * HEAD_DIM], q_stack[n])
                    for g in range(g_per_kv):
                        lgs[i, n * g_per_kv + g] = lg[:, g * BLK:(g + 1) * BLK]
            return jnp.sum(acc, axis=0, keepdims=True)

        def bit_step(bit, key, unit=None):
            cand = key | jnp.left_shift(jnp.int32(1), bit)
            cnt = count_ge(_key_to_float(cand), unit)
            return jnp.where(cnt >= kf, cand, key)

        c0 = count_ge(jnp.zeros((1, BLK), F32))
        key = jnp.where(c0 >= kf, 0, INT_MIN).astype(I32)
        for it in range(n_units):
            key = bit_step(30 - it, key, (it // N_KV_A, it % N_KV_A))
        n_rest = jnp.where((j + 1) * BLK <= topk, 0, 31 - n_units)
        key = lax.fori_loop(0, n_rest, lambda it, k: bit_step(30 - n_units - it, k), key)
        key_scr[...] = jnp.broadcast_to(key, key_scr.shape)

    for n_groups in range(1, sc.shape[0] // search_blk + 1):
        pl.when(n_chunk == n_groups)(functools.partial(search, n_groups))

    key = jnp.maximum(key_scr[0:1, :], KEY_NEG_FLT_MAX)
    thr = _key_to_float(key)

    def count2_body(c, accs):
        a_gt, a_ge = accs
        for u in range(search_blk):
            t = sc[c * search_blk + u]
            a_gt = a_gt + col_sum(jnp.where(t > thr, 1.0, 0.0))
            a_ge = a_ge + col_sum(jnp.where(t >= thr, 1.0, 0.0))
        return a_gt, a_ge

    z8 = jnp.zeros((SUBLANES, BLK), F32)
    a_gt, a_ge = lax.fori_loop(0, n_chunk, count2_body, (z8, z8))
    cnt_gt = jnp.sum(a_gt, axis=0, keepdims=True)
    cnt_ge = jnp.sum(a_ge, axis=0, keepdims=True)
    need = kf - cnt_gt
    straddle = jnp.max(jnp.where(cnt_ge > kf, 1.0, 0.0))

    n_keys_max = sc.shape[0] * BLK
    cut_scr[...] = jnp.full(cut_scr.shape, 2 * n_keys_max, I32)

    @pl.when(straddle > 0.0)
    def _tie_cut():
        def count_tie(cand):
            def body(i, acc):
                t = sc[i]
                hit = jnp.where(t == thr, jnp.where(row + i * BLK < cand, 1.0, 0.0), 0.0)
                return acc + col_sum(hit)
            acc = lax.fori_loop(0, j + 1, body, z8)
            return jnp.sum(acc, axis=0, keepdims=True)

        def cut_body(it, cut):
            cand = cut | jnp.left_shift(jnp.int32(1), (n_keys_max.bit_length() - 1) - it)
            return jnp.where(count_tie(cand) <= need, cand, cut)

        cut = lax.fori_loop(0, n_keys_max.bit_length(), cut_body, jnp.zeros((1, BLK), I32))
        cut_scr[...] = jnp.broadcast_to(cut, cut_scr.shape)

    cut = cut_scr[0:1, :]

    def col_max(w):
        return jnp.max(w.reshape(BLK // SUBLANES, SUBLANES, BLK), axis=0)

    def mask_body(c, m8, biased):
        m8 = list(m8)
        for u in range(search_blk):
            i = c * search_blk + u
            t = sc[i]
            tie_keep = jnp.where(row + i * BLK < cut, 0.0, -jnp.inf)
            nm = jnp.where(t > thr, 0.0, jnp.where(t == thr, tie_keep, -jnp.inf))
            near = jnp.clip(j - i, 0, 2)
            for h in range(N_HEADS_A):
                x = lgs[i, h] + nm
                if biased:
                    x = x + bt[near, h]
                lgs[i, h] = x
                m8[h] = jnp.maximum(m8[h], col_max(x))
        return tuple(m8)

    n_far = jnp.maximum(n_chunk - 2, 0)
    neg8 = jnp.full((SUBLANES, BLK), NEG_BIG, F32)
    m8 = lax.fori_loop(0, n_far, functools.partial(mask_body, biased=False), (neg8,) * N_HEADS_A)
    m8 = lax.fori_loop(n_far, n_chunk, functools.partial(mask_body, biased=True), m8)
    m_row = [jnp.max(m8[h], axis=0, keepdims=True) for h in range(N_HEADS_A)]

    acc_scr[...] = jnp.zeros(acc_scr.shape, F32)

    def pv_body(c, l8):
        l8 = list(l8)
        for u in range(att_blk_n):
            i = c * att_blk_n + u
            vti = vt[i]
            for n in range(N_KV_A):
                ps = []
                for g in range(g_per_kv):
                    h = n * g_per_kv + g
                    p = jnp.exp(lgs[i, h] - m_row[h])
                    l8[h] = l8[h] + col_sum(p)
                    ps.append(p.astype(BF16))
                acc_scr[n] += _dot(vti[n * HEAD_DIM:(n + 1) * HEAD_DIM, :], jnp.concatenate(ps, axis=1))
        return tuple(l8)

    l8 = lax.fori_loop(0, (j + att_blk_n) // att_blk_n, pv_body, (z8,) * N_HEADS_A)

    parts = []
    for n in range(N_KV_A):
        acc = acc_scr[n]
        for g in range(g_per_kv):
            h = n * g_per_kv + g
            parts.append(acc[:, g * BLK:(g + 1) * BLK] * (1.0 / jnp.sum(l8[h], axis=0, keepdims=True)))
    oa_ref[...] = jnp.concatenate(parts, axis=0).T


def _dsa_prompt(p, rel_bias, batch, seq):
    nblk = seq // BLK
    topk = min(TOPK_MAX, seq // 4)
    search_blk = 4 if nblk % 4 == 0 else 1
    score_blk_n = search_blk
    att_blk_n = search_blk
    qrow = lambda w: pl.BlockSpec((BLK, w), lambda b, j: (b * nblk + j, 0))
    brow = lambda w: pl.BlockSpec((seq, w), lambda b, j: (b, 0))
    return pl.pallas_call(
        functools.partial(_dsa_prompt_kernel, topk=topk, search_blk=search_blk, score_blk_n=score_blk_n,
                          att_blk_n=att_blk_n),
        grid=(batch, nblk),
        in_specs=[
            pl.BlockSpec(memory_space=pltpu.SMEM),
            qrow(N_IDX_HEADS * D_IDX), qrow(LANES), brow(LANES), qrow(D_A), brow(LANES), brow(LANES),
        ],
        out_specs=qrow(D_A),
        out_shape=jax.ShapeDtypeStruct((batch * seq, D_A), F32),
        scratch_shapes=[
            pltpu.VMEM((nblk, BLK, BLK), F32),
            pltpu.VMEM((3, N_HEADS_A, BLK, BLK), F32),
            pltpu.VMEM((nblk, BLK, BLK), BF16),
            pltpu.VMEM((nblk, N_HEADS_A, BLK, BLK), F32),
            pltpu.VMEM((N_KV_A, HEAD_DIM, (N_HEADS_A // N_KV_A) * BLK), F32),
            pltpu.VMEM((SUBLANES, BLK), I32),
            pltpu.VMEM((SUBLANES, BLK), I32),
        ],
        compiler_params=pltpu.CompilerParams(dimension_semantics=("arbitrary", "arbitrary"),
                                             vmem_limit_bytes=VMEM_LIMIT),
        name="dsa_prompt",
    )(rel_bias, p["qi"], p["kw"], p["kw16"], p["qa"], p["k16"], p["v16"])


def _retention_kernel(q_ref, k_ref, v_ref, s0_ref, dmat_ref, din_ref, dout_ref, decm_ref, bdm_ref,
                      yn_ref, s_out_ref, s_scr, *, n_sub):
    c = pl.program_id(1)
    chunk = din_ref.shape[0]

    @pl.when(c == 0)
    def _load_state():
        s_scr[...] = s0_ref[0]

    lo = lax.broadcasted_iota(I32, (chunk, LANES), 1) < HEAD_DIM
    zero = jnp.zeros((), BF16)
    for p in range(N_HEADS_B // 2):
        sl = slice(p * LANES, (p + 1) * LANES)
        st = s_scr[p]
        for u in range(n_sub):
            rows = slice(u * chunk, (u + 1) * chunk)
            q, k, v = q_ref[0, rows, sl], k_ref[0, rows, sl], v_ref[0, rows, sl]
            qd = (q.astype(F32) * din_ref[:, sl]).astype(BF16)
            kd = (k.astype(F32) * dout_ref[:, sl]).astype(BF16)
            vt = v.astype(F32).T.astype(BF16)
            s_a = (_nt_dot(jnp.where(lo, k, zero), q) * dmat_ref[2 * p]).astype(BF16)
            s_b = (_nt_dot(jnp.where(lo, zero, k), q) * dmat_ref[2 * p + 1]).astype(BF16)
            intra = jnp.concatenate([_dot(vt[:HEAD_DIM], s_a), _dot(vt[HEAD_DIM:], s_b)], axis=0)
            o = (_nt_dot(st.astype(BF16), qd) + intra).reshape(2, HEAD_DIM, chunk)
            dlt = o - jnp.mean(o, axis=1, keepdims=True)
            var = jnp.mean(dlt * dlt, axis=1, keepdims=True)
            yn_ref[0, rows, sl] = (dlt * lax.rsqrt(var + EPS)).reshape(LANES, chunk).T
            st = st * decm_ref[p] + _dot(vt, kd) * bdm_ref[...]
        s_scr[p] = st

    s_out_ref[0] = s_scr[...]


def _ret_tables(chunk, n_real):
    h = N_HEADS_B
    log_g = jnp.log1p(-jnp.exp2(-5.0 - jnp.arange(h, dtype=F32)))
    i = jnp.arange(chunk, dtype=F32)
    diff = i[:, None] - i[None, :]
    dmat = jnp.where(diff >= 0, jnp.exp(jnp.maximum(diff, 0.0)[None] * log_g[:, None, None]), 0.0)
    dmat_t = jnp.swapaxes(dmat, 1, 2)
    dec_in = jnp.exp((i + 1.0)[:, None] * log_g[None, :])
    dec_out = jnp.exp((n_real - 1.0 - i)[:, None] * log_g[None, :])
    dec_chunk = jnp.exp(n_real * log_g)
    din = jnp.repeat(dec_in, HEAD_DIM, axis=1)
    dout = jnp.repeat(dec_out, HEAD_DIM, axis=1)
    head_of = np.arange(LANES) // HEAD_DIM
    bd = jnp.asarray((head_of[:, None] == head_of[None, :]).astype(np.float32))
    dc = jnp.repeat(dec_chunk, HEAD_DIM).reshape(h // 2, LANES)
    decm = dc[:, :, None] * bd[None]
    return dmat_t, din, dout, decm, bd


def _retention(q3, k3, v3, s0t, n_real):
    batch, seq, _ = q3.shape
    chunk = min(RET_CHUNK, seq)
    n_sub = _largest_divisor(seq // chunk, 4)
    dmat_t, din, dout, decm, bd = _ret_tables(chunk, n_real)
    tok = pl.BlockSpec((1, n_sub * chunk, D_B), lambda b, c: (b, c, 0))
    st = pl.BlockSpec((1, N_HEADS_B // 2, LANES, LANES), lambda b, c: (b, 0, 0, 0))
    full = lambda a: pl.BlockSpec(a.shape, lambda b, c: (0,) * a.ndim)
    return pl.pallas_call(
        functools.partial(_retention_kernel, n_sub=n_sub),
        grid=(batch, seq // (n_sub * chunk)),
        in_specs=[tok, tok, tok, st, full(dmat_t), full(din), full(dout), full(decm), full(bd)],
        out_specs=[tok, st],
        out_shape=[jax.ShapeDtypeStruct((batch, seq, D_B), F32),
                   jax.ShapeDtypeStruct((batch, N_HEADS_B // 2, LANES, LANES), F32)],
        scratch_shapes=[pltpu.VMEM((N_HEADS_B // 2, LANES, LANES), F32)],
        compiler_params=pltpu.CompilerParams(dimension_semantics=("arbitrary", "arbitrary"),
                                             vmem_limit_bytes=VMEM_LIMIT),
        name="retention",
    )(q3, k3, v3, s0t, dmat_t, din, dout, decm, bd)


def _state_to_pairs(s):
    b = s.shape[0]
    s = jnp.swapaxes(s, -1, -2).reshape(b, N_HEADS_B // 2, 2, HEAD_DIM, HEAD_DIM)
    z = jnp.zeros_like(s[:, :, 0])
    top = jnp.concatenate([s[:, :, 0], z], axis=-1)
    bot = jnp.concatenate([z, s[:, :, 1]], axis=-1)
    return jnp.concatenate([top, bot], axis=-2)


def _pairs_to_state(sp):
    b = sp.shape[0]
    a = sp[:, :, :HEAD_DIM, :HEAD_DIM]
    d = sp[:, :, HEAD_DIM:, HEAD_DIM:]
    return jnp.swapaxes(jnp.stack([a, d], axis=2).reshape(b, N_HEADS_B, HEAD_DIM, HEAD_DIM), -1, -2)


def _merge_kernel(x_ref, oa_ref, yn_ref, ga_ref, gb_ref, p_ref, wo_ref, gp_ref, wup_ref, wg_ref, out_ref):
    ga = ga_ref[...]
    gb = gb_ref[...]
    ya = (ga * _sigmoid(ga) * oa_ref[...]).astype(BF16)
    yb = (gb * _sigmoid(gb) * yn_ref[...]).astype(BF16)
    y = _dot(ya, wo_ref[0:D_A, :]) + _dot(yb, wo_ref[D_A:D_A + D_B, :])
    ms = jnp.mean(y * y, axis=-1, keepdims=True)
    x1 = x_ref[...] + y * lax.rsqrt(ms + EPS) * gp_ref[...]
    ple = _dot(p_ref[...].astype(BF16), wup_ref[...])
    gate = _sigmoid(_dot(x1.astype(BF16), wg_ref[...]))
    out_ref[...] = x1 + ple * gate


def _merge(x2d, oa, yn, ga, gb, p2d, wo16, g_post, wup16, wg16, tm):
    rows, d_model = x2d.shape
    row = lambda w: pl.BlockSpec((tm, w), lambda i: (i, 0))
    full = lambda a: pl.BlockSpec(a.shape, lambda i: (0, 0))
    return pl.pallas_call(
        _merge_kernel,
        grid=(rows // tm,),
        in_specs=[row(d_model), row(D_A), row(D_B), row(D_A), row(D_B), row(p2d.shape[1]),
                  full(wo16), full(g_post), full(wup16), full(wg16)],
        out_specs=row(d_model),
        out_shape=jax.ShapeDtypeStruct((rows, d_model), F32),
        compiler_params=pltpu.CompilerParams(dimension_semantics=("arbitrary",), vmem_limit_bytes=VMEM_LIMIT),
        name="merge",
    )(x2d, oa, yn, ga, gb, p2d, wo16, g_post, wup16, wg16)


def _page_copy(pages_hbm, buf, sem, pt_ref, b, p, slot):
    return pltpu.make_async_copy(pages_hbm.at[pt_ref[b, p]], buf.at[slot, p], sem.at[slot])


def _start_batch_pages(streams, pt_ref, b, slot, n_pages):
    def body(p, c):
        for pages_hbm, buf, sem in streams:
            _page_copy(pages_hbm, buf, sem, pt_ref, b, p, slot).start()
        return c
    lax.fori_loop(0, n_pages, body, 0)


def _wait_batch_pages(streams, pt_ref, b, slot, n_pages):
    for pages_hbm, buf, sem in streams:
        for p in range(n_pages):
            _page_copy(pages_hbm, buf, sem, pt_ref, b, p, slot).wait()


def _sample_index_kernel(pt_ref, qi_ref, w_ref, pages_hbm, out_ref, buf, sem, *, n_pages, unroll):
    b = pl.program_id(0)
    slot = b % 2
    streams = [(pages_hbm, buf, sem)]

    @pl.when(b == 0)
    def _first():
        _start_batch_pages(streams, pt_ref, 0, 0, n_pages)

    _wait_batch_pages(streams, pt_ref, b, slot, n_pages)

    qi = qi_ref[0]
    w = w_ref[0]
    n_q = qi.shape[0] // N_IDX_HEADS

    def run(prefetch):
        def body(c, carry):
            for u in range(unroll):
                p = c * unroll + u
                if prefetch:
                    _page_copy(pages_hbm, buf, sem, pt_ref, b + 1, p, 1 - slot).start()
                s = _dot(qi, buf[slot, p].astype(BF16))
                r = jnp.maximum(s, 0.0) * w
                out_ref[0, c, :, u * PAGE_SIZE:(u + 1) * PAGE_SIZE] = jnp.sum(
                    r.reshape(n_q, N_IDX_HEADS, PAGE_SIZE), axis=1)
            return carry
        lax.fori_loop(0, n_pages // unroll, body, 0)

    has_next = b + 1 < pl.num_programs(0)
    pl.when(has_next)(lambda: run(True))
    pl.when(jnp.logical_not(has_next))(lambda: run(False))


def _sample_index(page_table, qi_qh, w_qh, kidx_t, unroll):
    batch, n_pages = page_table.shape
    n_q = qi_qh.shape[1] // N_IDX_HEADS
    groups = n_pages // unroll
    grid_spec = pltpu.PrefetchScalarGridSpec(
        num_scalar_prefetch=1,
        grid=(batch,),
        in_specs=[pl.BlockSpec((1,) + qi_qh.shape[1:], lambda b, pt: (b, 0, 0)),
                  pl.BlockSpec((1,) + w_qh.shape[1:], lambda b, pt: (b, 0, 0)),
                  pl.BlockSpec(memory_space=pl.ANY)],
        out_specs=pl.BlockSpec((1, groups, n_q, unroll * PAGE_SIZE), lambda b, pt: (b, 0, 0, 0)),
        scratch_shapes=[pltpu.VMEM((2, n_pages) + kidx_t.shape[1:], kidx_t.dtype),
                        pltpu.SemaphoreType.DMA((2,))],
    )
    return pl.pallas_call(
        functools.partial(_sample_index_kernel, n_pages=n_pages, unroll=unroll),
        grid_spec=grid_spec,
        out_shape=jax.ShapeDtypeStruct((batch, groups, n_q, unroll * PAGE_SIZE), F32),
        compiler_params=pltpu.CompilerParams(dimension_semantics=("arbitrary",), vmem_limit_bytes=VMEM_LIMIT),
        name="sample_index",
    )(page_table, qi_qh, w_qh, kidx_t)


def _sample_select_kernel(sp_ref, qi_ref, w_ref, kw16_ref, thr_ref, cut_ref, nmn_ref, snew_scr,
                          *, topk, n_q, lane_chunk):
    rows, past = sp_ref.shape
    r_i = lax.broadcasted_iota(I32, (rows, LANES), 0)
    l_i = lax.broadcasted_iota(I32, (rows, LANES), 1)

    s = _nt_dot(qi_ref[...], kw16_ref[...][:, :D_IDX])
    r = jnp.maximum(s, 0.0) * w_ref[...]
    s_new = jnp.sum(r.reshape(rows, N_IDX_HEADS, LANES), axis=1)
    ok = ((l_i // n_q) == (r_i // n_q)) & ((l_i % n_q) <= (r_i % n_q))
    snew_scr[...] = jnp.where(ok, s_new, -jnp.inf)

    n_chunks = past // lane_chunk
    tiles = lane_chunk // LANES
    kf = float(topk)

    def count(ind):
        acc = ind(snew_scr[...], l_i + past)
        for ch in range(n_chunks):
            x = sp_ref[:, ch * lane_chunk:(ch + 1) * lane_chunk]
            for t in range(tiles):
                idx = l_i + (ch * lane_chunk + t * LANES)
                acc = acc + ind(x[:, t * LANES:(t + 1) * LANES], idx)
        return jnp.broadcast_to(jnp.sum(acc, axis=1, keepdims=True), (rows, LANES))

    c0 = count(lambda x, idx: jnp.where(x >= 0.0, 1.0, 0.0))
    key0 = jnp.where(c0 >= kf, 0, INT_MIN).astype(I32)

    def bit_body(it, key):
        cand = key | jnp.left_shift(jnp.int32(1), 30 - it)
        cf = _key_to_float(cand)
        return jnp.where(count(lambda x, idx: jnp.where(x >= cf, 1.0, 0.0)) >= kf, cand, key)

    key = lax.fori_loop(0, 31, bit_body, key0)
    key = jnp.maximum(key, KEY_NEG_FLT_MAX)
    thr = _key_to_float(key)
    cnt_gt = count(lambda x, idx: jnp.where(x > thr, 1.0, 0.0))
    cnt_ge = count(lambda x, idx: jnp.where(x >= thr, 1.0, 0.0))
    need = kf - cnt_gt
    straddle = jnp.max(jnp.where(cnt_ge > kf, 1.0, 0.0))

    idx_bits = (past + LANES).bit_length()
    thr_ref[...] = thr
    cut_ref[...] = jnp.full((rows, LANES), 1 << idx_bits, I32)

    @pl.when(straddle > 0.0)
    def _tie_cut():
        def cut_body(it, cut):
            cand = cut | jnp.left_shift(jnp.int32(1), (idx_bits - 1) - it)
            cnt = count(lambda x, idx: jnp.where(x == thr, jnp.where(idx < cand, 1.0, 0.0), 0.0))
            return jnp.where(cnt <= need, cand, cut)
        cut_ref[...] = lax.fori_loop(0, idx_bits, cut_body, jnp.zeros((rows, LANES), I32))

    cut = cut_ref[...]
    sn = snew_scr[...]
    tie_keep = jnp.where(l_i + past < cut, 0.0, -jnp.inf)
    nmn_ref[...] = jnp.where(sn > thr, 0.0, jnp.where(sn == thr, tie_keep, -jnp.inf))


def _sample_select(scores_past, qi_rows, w_rows, kw16_s, topk, n_q):
    rows, past = scores_past.shape
    lane_chunk = 2048 if past % 2048 == 0 else LANES
    full = lambda a: pl.BlockSpec(a.shape, lambda i: (0,) * a.ndim)
    o = jax.ShapeDtypeStruct((rows, LANES), F32)
    return pl.pallas_call(
        functools.partial(_sample_select_kernel, topk=topk, n_q=n_q, lane_chunk=lane_chunk),
        grid=(1,),
        in_specs=[full(scores_past), full(qi_rows), full(w_rows), full(kw16_s)],
        out_specs=[pl.BlockSpec((rows, LANES), lambda i: (0, 0))] * 3,
        out_shape=[o, jax.ShapeDtypeStruct((rows, LANES), I32), o],
        scratch_shapes=[pltpu.VMEM((rows, LANES), F32)],
        compiler_params=pltpu.CompilerParams(dimension_semantics=("arbitrary",), vmem_limit_bytes=VMEM_LIMIT),
        name="sample_select",
    )(scores_past, qi_rows, w_rows, kw16_s)


def _sample_attn_kernel(pt_ref, rbc_ref, q_ref, sp_ref, thr_ref, cut_ref, nmn_ref, k16n_ref, v16n_ref,
                        k_hbm, v_hbm, o_ref, kbuf, vbuf, ksem, vsem, lgs, bias_scr, mx_scr,
                        *, n_pages, n_q, unroll):
    b = pl.program_id(0)
    slot = b % 2
    rows = n_q * N_HEADS_A
    groups = n_pages // unroll
    streams = [(k_hbm, kbuf, ksem), (v_hbm, vbuf, vsem)]
    lane = lax.broadcasted_iota(I32, (rows, LANES), 1)
    rq = lax.broadcasted_iota(I32, (rows, LANES), 0) // N_HEADS_A
    expand = lambda a: jnp.concatenate(
        [jnp.broadcast_to(a[t:t + 1], (N_HEADS_A, LANES)) for t in range(n_q)], axis=0)

    @pl.when(b == 0)
    def _first():
        _start_batch_pages(streams, pt_ref, 0, 0, n_pages)
        far = jnp.concatenate([rbc_ref[N_BUCKETS - 1]] * n_q, axis=0)
        for t, dist in enumerate((PAGE_SIZE + rq - lane, rq - lane % n_q)):
            bkt = _t5_bucket(dist)
            tile = jnp.zeros((rows, LANES), F32)
            for k in range(N_BUCKETS - 1):
                tile = jnp.where(bkt == k, jnp.concatenate([rbc_ref[k]] * n_q, axis=0) - far, tile)
            bias_scr[t] = tile

    _wait_batch_pages(streams, pt_ref, b, slot, n_pages)

    thr = expand(thr_ref[0])
    cut = expand(cut_ref[0])
    q = q_ref[0]

    def logits_pass(prefetch):
        def body(c, mx):
            for u in range(unroll):
                p = c * unroll + u
                if prefetch:
                    for pages_hbm, buf, sem in streams:
                        _page_copy(pages_hbm, buf, sem, pt_ref, b + 1, p, 1 - slot).start()
                lg = _dot(q, kbuf[slot, p].astype(BF16))
                sc = expand(sp_ref[0, c, :, u * PAGE_SIZE:(u + 1) * PAGE_SIZE])
                tie_keep = jnp.where(lane + p * PAGE_SIZE < cut, 0.0, -jnp.inf)
                x = lg + jnp.where(sc > thr, 0.0, jnp.where(sc == thr, tie_keep, -jnp.inf))
                if u == unroll - 1:
                    x = x + bias_scr[0] * jnp.where(c == groups - 1, 1.0, 0.0)
                lgs[p] = x
                mx = jnp.maximum(mx, x)
            return mx
        mx_scr[...] = lax.fori_loop(0, groups, body, jnp.full((rows, LANES), NEG_BIG, F32))

    has_next = b + 1 < pl.num_programs(0)
    pl.when(has_next)(lambda: logits_pass(True))
    pl.when(jnp.logical_not(has_next))(lambda: logits_pass(False))

    xn = _nt_dot(q, k16n_ref[...]) + bias_scr[1] + expand(nmn_ref[0])
    m = jnp.max(jnp.maximum(mx_scr[...], xn), axis=1, keepdims=True)

    def pv_body(c, carry):
        lsum, acc = carry
        for u in range(unroll):
            p = c * unroll + u
            pr = jnp.exp(lgs[p] - m)
            lsum = lsum + pr
            acc = acc + _nt_dot(pr.astype(BF16), vbuf[slot, p].astype(BF16))
        return lsum, acc

    pn = jnp.exp(xn - m)
    lsum, acc = lax.fori_loop(0, groups, pv_body, (pn, _dot(pn.astype(BF16), v16n_ref[...])))
    out = acc * (1.0 / jnp.sum(lsum, axis=1, keepdims=True))
    head = lax.broadcasted_iota(I32, (rows, HEAD_DIM), 0) % N_HEADS_A
    o_ref[0] = jnp.where(head < N_HEADS_A // N_KV_A, out[:, :HEAD_DIM], out[:, HEAD_DIM:])


def _sample_attn(page_table, rb_col, q_bd, scores4, thr3, cut3, nmn3, k16n, v16n, k_pages_t, v_pages_t, unroll):
    batch, n_pages = page_table.shape
    n_q = scores4.shape[2]
    rows = q_bd.shape[1]
    per_b = lambda a: pl.BlockSpec((1,) + a.shape[1:], lambda b, pt: (b,) + (0,) * (a.ndim - 1))
    full = lambda a: pl.BlockSpec(a.shape, lambda b, pt: (0,) * a.ndim)
    page_buf = pltpu.VMEM((2, n_pages) + k_pages_t.shape[1:], k_pages_t.dtype)
    grid_spec = pltpu.PrefetchScalarGridSpec(
        num_scalar_prefetch=1,
        grid=(batch,),
        in_specs=[full(rb_col), per_b(q_bd), per_b(scores4), per_b(thr3), per_b(cut3), per_b(nmn3),
                  full(k16n), full(v16n), pl.BlockSpec(memory_space=pl.ANY), pl.BlockSpec(memory_space=pl.ANY)],
        out_specs=pl.BlockSpec((1, rows, HEAD_DIM), lambda b, pt: (b, 0, 0)),
        scratch_shapes=[page_buf, page_buf, pltpu.SemaphoreType.DMA((2,)), pltpu.SemaphoreType.DMA((2,)),
                        pltpu.VMEM((n_pages, rows, LANES), F32), pltpu.VMEM((2, rows, LANES), F32),
                        pltpu.VMEM((rows, LANES), F32)],
    )
    return pl.pallas_call(
        functools.partial(_sample_attn_kernel, n_pages=n_pages, n_q=n_q, unroll=unroll),
        grid_spec=grid_spec,
        out_shape=jax.ShapeDtypeStruct((batch, rows, HEAD_DIM), F32),
        compiler_params=pltpu.CompilerParams(dimension_semantics=("arbitrary",),
                                             vmem_limit_bytes=VMEM_LIMIT_PAGED),
        name="sample_attn",
    )(page_table, rb_col, q_bd, scores4, thr3, cut3, nmn3, k16n, v16n, k_pages_t, v_pages_t)


def _rope_tables(pos):
    half = HEAD_DIM // 2
    inv = ROPE_BASE ** (-jnp.arange(half, dtype=F32) / half)
    ang = pos.astype(F32)[:, None] * inv[None, :]
    cos = jnp.cos(ang)
    sin = jnp.sin(ang)
    reps = LANES // HEAD_DIM
    cos_t = jnp.tile(jnp.concatenate([cos, cos], axis=1), (1, reps))
    sin_t = jnp.tile(jnp.concatenate([-sin, sin], axis=1), (1, reps))
    return cos_t, sin_t


def _cat_weight(w_in):
    offs = np.cumsum((0,) + SPLIT_SIZES)
    qa, ka, va, qi, ki, wi, ga, qb, kb, vb, gb = [w_in[:, offs[i]:offs[i + 1]] for i in range(len(SPLIT_SIZES))]
    pad = jnp.zeros((w_in.shape[0], LANES - D_IDX - N_IDX_HEADS), w_in.dtype)
    return jnp.concatenate([qa, ka, va, qi, ga, qb, kb, vb, gb, ki, wi, pad], axis=1).astype(BF16)


def _pages_t(cache):
    pool, page = cache.shape[:2]
    return jnp.transpose(cache, (0, 2, 3, 1)).reshape(pool, -1, page)


def _largest_divisor(n, cap):
    d = cap
    while n % d:
        d //= 2
    return d


def kernel(x_prompt, x_sample, cache_k, cache_v, cache_kidx, state_ret, page_table, p_prompt, p_sample,
           rel_bias, w_in, w_out, g_pre, g_post, w_ple_up, w_ple_gate):
    batch, seq, d_model = x_prompt.shape
    dec_b, dec_t, _ = x_sample.shape
    depth = w_in.shape[0]
    n_pages = page_table.shape[1]
    past = n_pages * PAGE_SIZE
    rows_s = dec_b * dec_t
    assert depth == 1 and rows_s == LANES and seq % BLK == 0

    w_cat = _cat_weight(w_in[0])
    gpre = g_pre[0].reshape(1, d_model)
    gpost = g_post[0].reshape(1, d_model)
    wo16 = w_out[0].astype(BF16)
    wup16 = w_ple_up[0].astype(BF16)
    wg16 = w_ple_gate[0].astype(BF16)

    tm = _largest_divisor(seq, 512)
    cos_p, sin_p = _rope_tables(jnp.arange(seq))
    xp2 = x_prompt.reshape(batch * seq, d_model)
    pp = _project(xp2, gpre, w_cat, cos_p, sin_p, tm, seq)
    oa_p = _dsa_prompt(pp, rel_bias, batch, seq)
    r3 = lambda a: a.reshape(batch, seq, D_B)
    s0_p = jnp.zeros((batch, N_HEADS_B // 2, LANES, LANES), F32)
    yn_p, sp_pairs = _retention(r3(pp["qb"]), r3(pp["kb"]), r3(pp["vb"]), s0_p, float(min(RET_CHUNK, seq)))
    y_prompt = _merge(xp2, oa_p, yn_p.reshape(batch * seq, D_B), pp["ga"], pp["gb"],
                      p_prompt[0].reshape(batch * seq, -1), wo16, gpost, wup16, wg16, tm)

    pos_s = past + jnp.arange(dec_t)
    cos_s, sin_s = _rope_tables(jnp.tile(pos_s, dec_b))
    xs2 = x_sample.reshape(rows_s, d_model)
    ps = _project(xs2, gpre, w_cat, cos_s, sin_s, rows_s, rows_s)
    topk_s = min(TOPK_MAX, (past + dec_t) // 4)

    qi_rows = ps["qi"].reshape(rows_s * N_IDX_HEADS, D_IDX)
    w_rows = jnp.broadcast_to(ps["kw"][:, D_IDX:D_IDX + N_IDX_HEADS].reshape(rows_s * N_IDX_HEADS, 1),
                              (rows_s * N_IDX_HEADS, LANES))
    unroll = _largest_divisor(n_pages, 32)
    scores4 = _sample_index(page_table, qi_rows.reshape(dec_b, dec_t * N_IDX_HEADS, D_IDX),
                            w_rows.reshape(dec_b, dec_t * N_IDX_HEADS, LANES),
                            jnp.transpose(cache_kidx[0], (0, 2, 1)), unroll)
    scores_past = scores4.transpose(0, 2, 1, 3).reshape(rows_s, past)
    thr, cut, nmn = _sample_select(scores_past, qi_rows, w_rows, ps["kw16"], topk_s, dec_t)

    kv_of_head = np.arange(N_HEADS_A) // (N_HEADS_A // N_KV_A)
    place = jnp.asarray((kv_of_head[:, None] == np.arange(N_KV_A)[None, :]).astype(np.float32)).astype(BF16)
    q_bd = (ps["qa"].reshape(dec_b, dec_t, N_HEADS_A, 1, HEAD_DIM) * place[None, None, :, :, None]).reshape(
        dec_b, dec_t * N_HEADS_A, N_KV_A * HEAD_DIM)
    rb_col = jnp.broadcast_to(rel_bias[:, :, None], rel_bias.shape + (LANES,))
    b3 = lambda a: a.reshape(dec_b, dec_t, LANES)
    o_qh = _sample_attn(page_table, rb_col, q_bd, scores4, b3(thr), b3(cut), b3(nmn),
                        ps["k16"], ps["v16"], _pages_t(cache_k[0]), _pages_t(cache_v[0]), unroll)
    oa_s = o_qh.reshape(rows_s, D_A)

    chunk_s = RET_CHUNK
    padt = lambda a: jnp.pad(a.reshape(dec_b, dec_t, D_B), ((0, 0), (0, chunk_s - dec_t), (0, 0)))
    yn_s, ss_pairs = _retention(padt(ps["qb"]), padt(ps["kb"]), padt(ps["vb"]),
                                _state_to_pairs(state_ret[0]), float(math.gcd(dec_t, RET_CHUNK)))
    y_sample = _merge(xs2, oa_s, yn_s[:, :dec_t].reshape(rows_s, D_B), ps["ga"], ps["gb"],
                      p_sample[0].reshape(rows_s, -1), wo16, gpost, wup16, wg16, rows_s)

    def kv(a_t, b, t):
        a = a_t.reshape(a_t.shape[0], N_KV_A, HEAD_DIM, -1).transpose(0, 3, 1, 2)
        return a.reshape(1, b, t, N_KV_A, HEAD_DIM)

    def ki(a_t, b, t):
        return a_t.transpose(0, 2, 1).reshape(1, b, t, D_IDX)

    return (
        y_prompt.reshape(batch, seq, d_model),
        y_sample.reshape(dec_b, dec_t, d_model),
        kv(pp["ka_t"], batch, seq), kv(pp["va_t"], batch, seq),
        ki(pp["ki_t"], batch, seq),
        _pairs_to_state(sp_pairs)[None].astype(state_ret.dtype),
        kv(ps["ka_t"], dec_b, dec_t), kv(ps["va_t"], dec_b, dec_t),
        ki(ps["ki_t"], dec_b, dec_t),
        _pairs_to_state(ss_pairs)[None].astype(state_ret.dtype),
    )
```

```python
import functools
import math

import jax
import jax.numpy as jnp
import numpy as np
from jax import lax
from jax.experimental import pallas as pl
from jax.experimental.pallas import tpu as pltpu

F32 = jnp.float32
BF16 = jnp.bfloat16
I32 = jnp.int32

HEAD_DIM = 64
N_HEADS_A = 8
N_KV_A = 2
N_IDX_HEADS = 8
D_IDX = 64
TOPK_MAX = 256
N_HEADS_B = 8
D_A = N_HEADS_A * HEAD_DIM
D_B = N_HEADS_B * HEAD_DIM
N_BUCKETS = 32
MAX_DISTANCE = 128
ROPE_BASE = 10000.0
RET_CHUNK = 128
PAGE_SIZE = 128
EPS = 1e-6
SPLIT_SIZES = (D_A, N_KV_A * HEAD_DIM, N_KV_A * HEAD_DIM, N_IDX_HEADS * D_IDX, D_IDX, N_IDX_HEADS,
               D_A, D_B, D_B, D_B, D_B)

LANES = 128
SUBLANES = 8
BLK = 128
VMEM_LIMIT = 48 * 1024 * 1024
VMEM_LIMIT_PAGED = 56 * 1024 * 1024

INT_MIN = -(2 ** 31)
KEY_NEG_FLT_MAX = INT_MIN + 0x00800000
NEG_BIG = -1e30
LOG2E = math.log2(math.e)
ONES_ROWS = 16

_C_QA, _C_KA, _C_VA, _C_QI, _C_GA, _C_QB, _C_KB, _C_VB, _C_GB, _C_KW, _C_END = (
    0, 512, 640, 768, 1280, 1792, 2304, 2816, 3328, 3840, 3968)


def _nt_dot(a, b):
    return lax.dot_general(a, b, (((1,), (1,)), ((), ())), preferred_element_type=F32)


def _dot(a, b):
    return jnp.dot(a, b, preferred_element_type=F32)


def _sigmoid(x):
    return 1.0 / (1.0 + jnp.exp(-x))


def _key_to_float(k):
    bits = jnp.where(k >= 0, k, k ^ 0x7FFFFFFF)
    return lax.bitcast_convert_type(bits, F32)


def _t5_bucket(n):
    max_exact = N_BUCKETS // 2
    n = jnp.maximum(n, 0)
    nf = jnp.maximum(n, 1).astype(F32)
    large = max_exact + jnp.floor(jnp.log(nf / max_exact) / math.log(MAX_DISTANCE / max_exact)
                                  * (N_BUCKETS - max_exact)).astype(I32)
    large = jnp.minimum(large, N_BUCKETS - 1)
    return jnp.where(n < max_exact, n, large)


def _bias_from_bucket(bkt, rb_ref, h):
    out = jnp.zeros(bkt.shape, F32)
    for k in range(N_BUCKETS):
        out = jnp.where(bkt == k, rb_ref[k, h], out)
    return out


def _proj_kernel(x_ref, g_ref, w_ref, cos_ref, sin_ref,
                 qa_ref, kat_ref, vat_ref, k16_ref, v16_ref, qi_ref, ga_ref, gb_ref,
                 qb_ref, kb_ref, vb_ref, kw_ref, kw16_ref, kit_ref, h_scr):
    x = x_ref[...]
    ms = jnp.mean(x * x, axis=-1, keepdims=True)
    h_scr[...] = (x * lax.rsqrt(ms + EPS) * g_ref[...]).astype(BF16)

    def mm(lo, hi):
        return _dot(h_scr[...], w_ref[:, lo:hi])

    qa_ref[...] = (mm(_C_QA, _C_KA) * (HEAD_DIM ** -0.5 * LOG2E)).astype(BF16)
    ka = mm(_C_KA, _C_VA)
    kat_ref[0] = ka.T
    k16_ref[...] = ka.astype(BF16)
    va = mm(_C_VA, _C_QI)
    vat_ref[0] = va.T
    v16_ref[...] = va.astype(BF16)
    qi_ref[...] = mm(_C_QI, _C_GA).astype(BF16)
    ga_ref[...] = mm(_C_GA, _C_QB)
    gb_ref[...] = mm(_C_GB, _C_KW)
    vb_ref[...] = mm(_C_VB, _C_GB).astype(BF16)

    cos = cos_ref[...]
    sin = sin_ref[...]
    lane = lax.broadcasted_iota(I32, cos.shape, 1)
    first_half = (lane % HEAD_DIM) < (HEAD_DIM // 2)

    def rope(z, scale):
        outs = []
        for g in range(z.shape[1] // LANES):
            zg = z[:, g * LANES:(g + 1) * LANES]
            partner = jnp.where(first_half, pltpu.roll(zg, LANES - HEAD_DIM // 2, 1),
                                pltpu.roll(zg, HEAD_DIM // 2, 1))
            r = zg * cos + partner * sin
            if scale != 1.0:
                r = r * scale
            outs.append(r.astype(BF16))
        return jnp.concatenate(outs, axis=1)

    qb_ref[...] = rope(mm(_C_QB, _C_KB), 1.0)
    kb_ref[...] = rope(mm(_C_KB, _C_VB), HEAD_DIM ** -0.5)

    kw = mm(_C_KW, _C_END)
    wscale = (N_IDX_HEADS ** -0.5) * (D_IDX ** -0.5)
    kw = kw * jnp.where(lane >= D_IDX, wscale, 1.0)
    kw_ref[...] = kw
    kw16_ref[...] = kw.astype(BF16)
    kit_ref[0] = kw.T[:D_IDX]


def _project(x2d, g_pre, w_cat, cos_t, sin_t, tm, seq):
    rows = x2d.shape[0]
    d_model = x2d.shape[1]
    n_tab = cos_t.shape[0] // tm
    tiles = seq // tm
    row_spec = lambda w: pl.BlockSpec((tm, w), lambda i: (i, 0))
    t_spec = lambda w: pl.BlockSpec((1, w, tm), lambda i: (i // tiles, 0, i % tiles))
    outs = [
        ("qa", D_A, BF16), ("ka_t", LANES, F32), ("va_t", LANES, F32), ("k16", LANES, BF16), ("v16", LANES, BF16),
        ("qi", N_IDX_HEADS * D_IDX, BF16), ("ga", D_A, F32), ("gb", D_B, F32),
        ("qb", D_B, BF16), ("kb", D_B, BF16), ("vb", D_B, BF16), ("kw", LANES, F32), ("kw16", LANES, BF16),
        ("ki_t", D_IDX, F32),
    ]
    transposed = lambda n: n.endswith("_t")
    res = pl.pallas_call(
        _proj_kernel,
        grid=(rows // tm,),
        in_specs=[
            row_spec(d_model),
            pl.BlockSpec((1, d_model), lambda i: (0, 0)),
            pl.BlockSpec(w_cat.shape, lambda i: (0, 0)),
            pl.BlockSpec((tm, LANES), lambda i: (i % n_tab, 0)),
            pl.BlockSpec((tm, LANES), lambda i: (i % n_tab, 0)),
        ],
        out_specs=[t_spec(w) if transposed(n) else row_spec(w) for n, w, _ in outs],
        out_shape=[jax.ShapeDtypeStruct((rows // seq, w, seq) if transposed(n) else (rows, w), dt)
                   for n, w, dt in outs],
        scratch_shapes=[pltpu.VMEM((tm, d_model), BF16)],
        compiler_params=pltpu.CompilerParams(dimension_semantics=("arbitrary",), vmem_limit_bytes=VMEM_LIMIT),
        name="proj",
    )(x2d, g_pre, w_cat, cos_t, sin_t)
    return {n: r for (n, _, _), r in zip(outs, res)}


def _dsa_prompt_kernel(rb_ref, qi_ref, kwq_ref, qin_ref, kwqn_ref, kw16_ref, qa_ref, k16_ref, v16_ref, oa_ref,
                       sc2, bt, vt, lgs, acc_scr, cut_scr, key_scr, *, topk, search_blk):
    b = pl.program_id(0)
    j = pl.program_id(1)
    nblk = pl.num_programs(1)
    sc = sc2.at[j % 2]
    sc_next = sc2.at[1 - j % 2]
    row = lax.broadcasted_iota(I32, (BLK, BLK), 0)
    col = lax.broadcasted_iota(I32, (BLK, BLK), 1)

    @pl.when((b == 0) & (j == 0))
    def _init_bias():
        for d in range(2):
            bkt = _t5_bucket(col - row + d * BLK)
            for h in range(N_HEADS_A):
                bt[d, h] = (_bias_from_bucket(bkt, rb_ref, h) - rb_ref[N_BUCKETS - 1, h]) * LOG2E
        bt[2] = jnp.zeros(bt.shape[1:], F32)

    @pl.when(j == 0)
    def _clear_values():
        vt[...] = jnp.zeros(vt.shape, BF16)

    vblk = v16_ref[pl.ds(pl.multiple_of(j * BLK, BLK), BLK), :]
    vtj = vblk.astype(F32).T.astype(BF16)
    ones = jnp.ones((ONES_ROWS, BLK), BF16)
    vt[j] = jnp.concatenate([piece for n in range(N_KV_A)
                             for piece in (vtj[n * HEAD_DIM:(n + 1) * HEAD_DIM], ones)], axis=0)

    def indexer_queries(qi_blk_ref, kw_blk_ref):
        qi = qi_blk_ref[...]
        qi_stack = jnp.concatenate([qi[:, h * D_IDX:(h + 1) * D_IDX] for h in range(N_IDX_HEADS)], axis=0)
        return qi_stack, kw_blk_ref[...].T

    def score_group(c, dst, queries, first_q):
        qi_stack, wi_t = queries
        for u in range(search_blk):
            i = c * search_blk + u
            kib = kw16_ref[pl.ds(pl.multiple_of(i * BLK, BLK), BLK), :][:, :D_IDX]
            s = _nt_dot(kib, qi_stack)
            acc = jnp.zeros((BLK, BLK), F32)
            for h in range(N_IDX_HEADS):
                acc = acc + wi_t[D_IDX + h:D_IDX + h + 1, :] * jnp.maximum(s[:, h * BLK:(h + 1) * BLK], 0.0)
            dst[i] = jnp.where(row + i * BLK <= col + first_q, acc, -jnp.inf)

    @pl.when(j == 0)
    def _own_scores():
        score_group(0, sc, indexer_queries(qi_ref, kwq_ref), 0)

    n_chunk = (j + search_blk) // search_blk

    def col_sum(w):
        return jnp.sum(w.reshape(BLK // SUBLANES, SUBLANES, BLK), axis=0)

    qa = qa_ref[...]
    g_per_kv = N_HEADS_A // N_KV_A
    q_stack = [jnp.concatenate([qa[:, (n * g_per_kv + g) * HEAD_DIM:(n * g_per_kv + g + 1) * HEAD_DIM]
                                for g in range(g_per_kv)], axis=0) for n in range(N_KV_A)]

    kf = float(topk)
    n_units = search_blk * N_KV_A

    def search(n_groups):
        def count_ge(cand, unit=None):
            acc = jnp.zeros((SUBLANES, BLK), F32)
            for c in range(n_groups):
                for u in range(search_blk):
                    acc = acc + col_sum(jnp.where(sc[c * search_blk + u] >= cand, 1.0, 0.0))
                if unit is not None:
                    i = c * search_blk + unit[0]
                    n = unit[1]
                    lg = _nt_dot(k16_ref[i * BLK:(i + 1) * BLK, n * HEAD_DIM:(n + 1) * HEAD_DIM], q_stack[n])
                    for g in range(g_per_kv):
                        lgs[i, n * g_per_kv + g] = lg[:, g * BLK:(g + 1) * BLK]
            return jnp.sum(acc, axis=0, keepdims=True)

        def bit_step(bit, key, unit=None):
            cand = key | jnp.left_shift(jnp.int32(1), bit)
            cnt = count_ge(_key_to_float(cand), unit)
            return jnp.where(cnt >= kf, cand, key)

        c0 = count_ge(jnp.zeros((1, BLK), F32))
        key = jnp.where(c0 >= kf, 0, INT_MIN).astype(I32)
        for it in range(n_units):
            key = bit_step(30 - it, key, (it // N_KV_A, it % N_KV_A))
        n_rest = jnp.where((j + 1) * BLK <= topk, 0, 31 - n_units)
        key = lax.fori_loop(0, n_rest, lambda it, k: bit_step(30 - n_units - it, k), key)
        key_scr[...] = jnp.broadcast_to(key, key_scr.shape)

    for n_groups in range(1, sc.shape[0] // search_blk + 1):
        pl.when(n_chunk == n_groups)(functools.partial(search, n_groups))

    key = jnp.maximum(key_scr[0:1, :], KEY_NEG_FLT_MAX)
    thr = _key_to_float(key)

    def count2_body(c, accs):
        a_gt, a_ge = accs
        for u in range(search_blk):
            t = sc[c * search_blk + u]
            a_gt = a_gt + col_sum(jnp.where(t > thr, 1.0, 0.0))
            a_ge = a_ge + col_sum(jnp.where(t >= thr, 1.0, 0.0))
        return a_gt, a_ge

    z8 = jnp.zeros((SUBLANES, BLK), F32)
    a_gt, a_ge = lax.fori_loop(0, n_chunk, count2_body, (z8, z8))
    cnt_gt = jnp.sum(a_gt, axis=0, keepdims=True)
    cnt_ge = jnp.sum(a_ge, axis=0, keepdims=True)
    need = kf - cnt_gt
    straddle = jnp.max(jnp.where(cnt_ge > kf, 1.0, 0.0))

    n_keys_max = sc.shape[0] * BLK
    cut_scr[...] = jnp.full(cut_scr.shape, 2 * n_keys_max, I32)

    @pl.when(straddle > 0.0)
    def _tie_cut():
        def count_tie(cand):
            def body(i, acc):
                t = sc[i]
                hit = jnp.where(t == thr, jnp.where(row + i * BLK < cand, 1.0, 0.0), 0.0)
                return acc + col_sum(hit)
            acc = lax.fori_loop(0, j + 1, body, z8)
            return jnp.sum(acc, axis=0, keepdims=True)

        def cut_body(it, cut):
            cand = cut | jnp.left_shift(jnp.int32(1), (n_keys_max.bit_length() - 1) - it)
            return jnp.where(count_tie(cand) <= need, cand, cut)

        cut = lax.fori_loop(0, n_keys_max.bit_length(), cut_body, jnp.zeros((1, BLK), I32))
        cut_scr[...] = jnp.broadcast_to(cut, cut_scr.shape)

    cut = cut_scr[0:1, :]

    def col_max(w):
        return jnp.max(w.reshape(BLK // SUBLANES, SUBLANES, BLK), axis=0)

    def mask_body(c, m8, biased):
        m8 = list(m8)
        for u in range(search_blk):
            i = c * search_blk + u
            t = sc[i]
            tie_keep = jnp.where(row + i * BLK < cut, 0.0, -jnp.inf)
            nm = jnp.where(t > thr, 0.0, jnp.where(t == thr, tie_keep, -jnp.inf))
            near = jnp.clip(j - i, 0, 2)
            for h in range(N_HEADS_A):
                x = lgs[i, h] + nm
                if biased:
                    x = x + bt[near, h]
                lgs[i, h] = x
                m8[h] = jnp.maximum(m8[h], col_max(x))
        return tuple(m8)

    n_far = jnp.maximum(n_chunk - 2, 0)
    neg8 = jnp.full((SUBLANES, BLK), NEG_BIG, F32)
    m8 = lax.fori_loop(0, n_far, functools.partial(mask_body, biased=False), (neg8,) * N_HEADS_A)
    m8 = lax.fori_loop(n_far, n_chunk, functools.partial(mask_body, biased=True), m8)
    m_row = [jnp.max(m8[h], axis=0, keepdims=True) for h in range(N_HEADS_A)]

    acc_scr[...] = jnp.zeros(acc_scr.shape, F32)

    vrows = HEAD_DIM + ONES_ROWS

    next_queries = indexer_queries(qin_ref, kwqn_ref)
    next_first_q = (j + 1) * BLK

    def pv_body(c, carry):
        for u in range(search_blk):
            i = c * search_blk + u
            vti = vt[i]
            for n in range(N_KV_A):
                ps = [jnp.exp2(lgs[i, n * g_per_kv + g] - m_row[n * g_per_kv + g]).astype(BF16)
                      for g in range(g_per_kv)]
                acc_scr[n] += _dot(vti[n * vrows:(n + 1) * vrows, :], jnp.concatenate(ps, axis=1))
        score_group(c, sc_next, next_queries, next_first_q)
        return carry

    lax.fori_loop(0, n_chunk, pv_body, 0)

    @pl.when(((j + 1) % search_blk == 0) & (j + 1 < nblk))
    def _next_reaches_one_more_group():
        score_group(n_chunk, sc_next, next_queries, next_first_q)

    parts = []
    for n in range(N_KV_A):
        acc = acc_scr[n]
        out = acc[:HEAD_DIM] * (1.0 / acc[HEAD_DIM:HEAD_DIM + 1])
        parts += [out[:, g * BLK:(g + 1) * BLK] for g in range(g_per_kv)]
    oa_ref[...] = jnp.concatenate(parts, axis=0).T


def _dsa_prompt(p, rel_bias, batch, seq):
    nblk = seq // BLK
    topk = min(TOPK_MAX, seq // 4)
    search_blk = 4 if nblk % 4 == 0 else 1
    qrow = lambda w: pl.BlockSpec((BLK, w), lambda b, j: (b * nblk + j, 0))
    qnext = lambda w: pl.BlockSpec((BLK, w), lambda b, j: (b * nblk + jnp.minimum(j + 1, nblk - 1), 0))
    brow = lambda w: pl.BlockSpec((seq, w), lambda b, j: (b, 0))
    return pl.pallas_call(
        functools.partial(_dsa_prompt_kernel, topk=topk, search_blk=search_blk),
        grid=(batch, nblk),
        in_specs=[
            pl.BlockSpec(memory_space=pltpu.SMEM),
            qrow(N_IDX_HEADS * D_IDX), qrow(LANES), qnext(N_IDX_HEADS * D_IDX), qnext(LANES),
            brow(LANES), qrow(D_A), brow(LANES), brow(LANES),
        ],
        out_specs=qrow(D_A),
        out_shape=jax.ShapeDtypeStruct((batch * seq, D_A), F32),
        scratch_shapes=[
            pltpu.VMEM((2, nblk, BLK, BLK), F32),
            pltpu.VMEM((3, N_HEADS_A, BLK, BLK), F32),
            pltpu.VMEM((nblk, N_KV_A * (HEAD_DIM + ONES_ROWS), BLK), BF16),
            pltpu.VMEM((nblk, N_HEADS_A, BLK, BLK), F32),
            pltpu.VMEM((N_KV_A, HEAD_DIM + ONES_ROWS, (N_HEADS_A // N_KV_A) * BLK), F32),
            pltpu.VMEM((SUBLANES, BLK), I32),
            pltpu.VMEM((SUBLANES, BLK), I32),
        ],
        compiler_params=pltpu.CompilerParams(dimension_semantics=("arbitrary", "arbitrary"),
                                             vmem_limit_bytes=VMEM_LIMIT),
        name="dsa_prompt",
    )(rel_bias, p["qi"], p["kw"], p["qi"], p["kw"], p["kw16"], p["qa"], p["k16"], p["v16"])


def _retention_kernel(q_ref, k_ref, v_ref, s0_ref, dmat_ref, din_ref, dout_ref, decm_ref, bdm_ref,
                      yn_ref, s_out_ref, s_scr, *, n_sub):
    c = pl.program_id(1)
    chunk = din_ref.shape[0]

    @pl.when(c == 0)
    def _load_state():
        s_scr[...] = s0_ref[0]

    lo = lax.broadcasted_iota(I32, (chunk, LANES), 1) < HEAD_DIM
    zero = jnp.zeros((), BF16)
    for p in range(N_HEADS_B // 2):
        sl = slice(p * LANES, (p + 1) * LANES)
        st = s_scr[p]
        for u in range(n_sub):
            rows = slice(u * chunk, (u + 1) * chunk)
            q, k, v = q_ref[0, rows, sl], k_ref[0, rows, sl], v_ref[0, rows, sl]
            qd = (q.astype(F32) * din_ref[:, sl]).astype(BF16)
            kd = (k.astype(F32) * dout_ref[:, sl]).astype(BF16)
            vt = v.astype(F32).T.astype(BF16)
            s_a = (_nt_dot(jnp.where(lo, k, zero), q) * dmat_ref[2 * p]).astype(BF16)
            s_b = (_nt_dot(jnp.where(lo, zero, k), q) * dmat_ref[2 * p + 1]).astype(BF16)
            intra = jnp.concatenate([_dot(vt[:HEAD_DIM], s_a), _dot(vt[HEAD_DIM:], s_b)], axis=0)
            o = (_nt_dot(st.astype(BF16), qd) + intra).reshape(2, HEAD_DIM, chunk)
            dlt = o - jnp.mean(o, axis=1, keepdims=True)
            var = jnp.mean(dlt * dlt, axis=1, keepdims=True)
            yn_ref[0, rows, sl] = (dlt * lax.rsqrt(var + EPS)).reshape(LANES, chunk).T
            st = st * decm_ref[p] + _dot(vt, kd) * bdm_ref[...]
        s_scr[p] = st

    s_out_ref[0] = s_scr[...]


def _ret_tables(chunk, n_real):
    h = N_HEADS_B
    log_g = jnp.log1p(-jnp.exp2(-5.0 - jnp.arange(h, dtype=F32)))
    i = jnp.arange(chunk, dtype=F32)
    diff = i[:, None] - i[None, :]
    dmat = jnp.where(diff >= 0, jnp.exp(jnp.maximum(diff, 0.0)[None] * log_g[:, None, None]), 0.0)
    dmat_t = jnp.swapaxes(dmat, 1, 2)
    dec_in = jnp.exp((i + 1.0)[:, None] * log_g[None, :])
    dec_out = jnp.exp((n_real - 1.0 - i)[:, None] * log_g[None, :])
    dec_chunk = jnp.exp(n_real * log_g)
    din = jnp.repeat(dec_in, HEAD_DIM, axis=1)
    dout = jnp.repeat(dec_out, HEAD_DIM, axis=1)
    head_of = np.arange(LANES) // HEAD_DIM
    bd = jnp.asarray((head_of[:, None] == head_of[None, :]).astype(np.float32))
    dc = jnp.repeat(dec_chunk, HEAD_DIM).reshape(h // 2, LANES)
    decm = dc[:, :, None] * bd[None]
    return dmat_t, din, dout, decm, bd


def _retention(q3, k3, v3, s0t, n_real):
    batch, seq, _ = q3.shape
    chunk = min(RET_CHUNK, seq)
    n_sub = _largest_divisor(seq // chunk, 4)
    dmat_t, din, dout, decm, bd = _ret_tables(chunk, n_real)
    tok = pl.BlockSpec((1, n_sub * chunk, D_B), lambda b, c: (b, c, 0))
    st = pl.BlockSpec((1, N_HEADS_B // 2, LANES, LANES), lambda b, c: (b, 0, 0, 0))
    full = lambda a: pl.BlockSpec(a.shape, lambda b, c: (0,) * a.ndim)
    return pl.pallas_call(
        functools.partial(_retention_kernel, n_sub=n_sub),
        grid=(batch, seq // (n_sub * chunk)),
        in_specs=[tok, tok, tok, st, full(dmat_t), full(din), full(dout), full(decm), full(bd)],
        out_specs=[tok, st],
        out_shape=[jax.ShapeDtypeStruct((batch, seq, D_B), F32),
                   jax.ShapeDtypeStruct((batch, N_HEADS_B // 2, LANES, LANES), F32)],
        scratch_shapes=[pltpu.VMEM((N_HEADS_B // 2, LANES, LANES), F32)],
        compiler_params=pltpu.CompilerParams(dimension_semantics=("arbitrary", "arbitrary"),
                                             vmem_limit_bytes=VMEM_LIMIT),
        name="retention",
    )(q3, k3, v3, s0t, dmat_t, din, dout, decm, bd)


def _state_to_pairs(s):
    b = s.shape[0]
    s = jnp.swapaxes(s, -1, -2).reshape(b, N_HEADS_B // 2, 2, HEAD_DIM, HEAD_DIM)
    z = jnp.zeros_like(s[:, :, 0])
    top = jnp.concatenate([s[:, :, 0], z], axis=-1)
    bot = jnp.concatenate([z, s[:, :, 1]], axis=-1)
    return jnp.concatenate([top, bot], axis=-2)


def _pairs_to_state(sp):
    b = sp.shape[0]
    a = sp[:, :, :HEAD_DIM, :HEAD_DIM]
    d = sp[:, :, HEAD_DIM:, HEAD_DIM:]
    return jnp.swapaxes(jnp.stack([a, d], axis=2).reshape(b, N_HEADS_B, HEAD_DIM, HEAD_DIM), -1, -2)


def _merge_kernel(x_ref, oa_ref, yn_ref, ga_ref, gb_ref, p_ref, wo_ref, gp_ref, wup_ref, wg_ref, out_ref):
    ga = ga_ref[...]
    gb = gb_ref[...]
    ya = (ga * _sigmoid(ga) * oa_ref[...]).astype(BF16)
    yb = (gb * _sigmoid(gb) * yn_ref[...]).astype(BF16)
    y = _dot(ya, wo_ref[0:D_A, :]) + _dot(yb, wo_ref[D_A:D_A + D_B, :])
    ms = jnp.mean(y * y, axis=-1, keepdims=True)
    x1 = x_ref[...] + y * lax.rsqrt(ms + EPS) * gp_ref[...]
    ple = _dot(p_ref[...].astype(BF16), wup_ref[...])
    gate = _sigmoid(_dot(x1.astype(BF16), wg_ref[...]))
    out_ref[...] = x1 + ple * gate


def _merge(x2d, oa, yn, ga, gb, p2d, wo16, g_post, wup16, wg16, tm):
    rows, d_model = x2d.shape
    row = lambda w: pl.BlockSpec((tm, w), lambda i: (i, 0))
    full = lambda a: pl.BlockSpec(a.shape, lambda i: (0, 0))
    return pl.pallas_call(
        _merge_kernel,
        grid=(rows // tm,),
        in_specs=[row(d_model), row(D_A), row(D_B), row(D_A), row(D_B), row(p2d.shape[1]),
                  full(wo16), full(g_post), full(wup16), full(wg16)],
        out_specs=row(d_model),
        out_shape=jax.ShapeDtypeStruct((rows, d_model), F32),
        compiler_params=pltpu.CompilerParams(dimension_semantics=("arbitrary",), vmem_limit_bytes=VMEM_LIMIT),
        name="merge",
    )(x2d, oa, yn, ga, gb, p2d, wo16, g_post, wup16, wg16)


def _page_copy(pages_hbm, buf, sem, pt_ref, b, p, slot):
    return pltpu.make_async_copy(pages_hbm.at[pt_ref[b, p]], buf.at[slot, p], sem.at[slot])


def _start_batch_pages(streams, pt_ref, b, slot, n_pages):
    def body(p, c):
        for pages_hbm, buf, sem in streams:
            _page_copy(pages_hbm, buf, sem, pt_ref, b, p, slot).start()
        return c
    lax.fori_loop(0, n_pages, body, 0)


def _wait_batch_pages(streams, pt_ref, b, slot, n_pages):
    for pages_hbm, buf, sem in streams:
        for p in range(n_pages):
            _page_copy(pages_hbm, buf, sem, pt_ref, b, p, slot).wait()


def _sample_index_kernel(pt_ref, qi_ref, w_ref, pages_hbm, out_ref, buf, sem, *, n_pages, unroll):
    b = pl.program_id(0)
    slot = b % 2
    streams = [(pages_hbm, buf, sem)]

    @pl.when(b == 0)
    def _first():
        _start_batch_pages(streams, pt_ref, 0, 0, n_pages)

    _wait_batch_pages(streams, pt_ref, b, slot, n_pages)

    qi = qi_ref[0]
    w = w_ref[0]
    n_q = qi.shape[0] // N_IDX_HEADS

    def run(prefetch):
        def body(c, carry):
            for u in range(unroll):
                p = c * unroll + u
                if prefetch:
                    _page_copy(pages_hbm, buf, sem, pt_ref, b + 1, p, 1 - slot).start()
                s = _dot(qi, buf[slot, p].astype(BF16))
                r = jnp.maximum(s, 0.0) * w
                out_ref[0, c, :, u * PAGE_SIZE:(u + 1) * PAGE_SIZE] = jnp.sum(
                    r.reshape(n_q, N_IDX_HEADS, PAGE_SIZE), axis=1)
            return carry
        lax.fori_loop(0, n_pages // unroll, body, 0)

    has_next = b + 1 < pl.num_programs(0)
    pl.when(has_next)(lambda: run(True))
    pl.when(jnp.logical_not(has_next))(lambda: run(False))


def _sample_index(page_table, qi_qh, w_qh, kidx_t, unroll):
    batch, n_pages = page_table.shape
    n_q = qi_qh.shape[1] // N_IDX_HEADS
    groups = n_pages // unroll
    grid_spec = pltpu.PrefetchScalarGridSpec(
        num_scalar_prefetch=1,
        grid=(batch,),
        in_specs=[pl.BlockSpec((1,) + qi_qh.shape[1:], lambda b, pt: (b, 0, 0)),
                  pl.BlockSpec((1,) + w_qh.shape[1:], lambda b, pt: (b, 0, 0)),
                  pl.BlockSpec(memory_space=pl.ANY)],
        out_specs=pl.BlockSpec((1, groups, n_q, unroll * PAGE_SIZE), lambda b, pt: (b, 0, 0, 0)),
        scratch_shapes=[pltpu.VMEM((2, n_pages) + kidx_t.shape[1:], kidx_t.dtype),
                        pltpu.SemaphoreType.DMA((2,))],
    )
    return pl.pallas_call(
        functools.partial(_sample_index_kernel, n_pages=n_pages, unroll=unroll),
        grid_spec=grid_spec,
        out_shape=jax.ShapeDtypeStruct((batch, groups, n_q, unroll * PAGE_SIZE), F32),
        compiler_params=pltpu.CompilerParams(dimension_semantics=("arbitrary",), vmem_limit_bytes=VMEM_LIMIT),
        name="sample_index",
    )(page_table, qi_qh, w_qh, kidx_t)


def _sample_select_kernel(sp_ref, qi_ref, w_ref, kw16_ref, thr_ref, cut_ref, nmn_ref, snew_scr,
                          *, topk, n_q, lane_chunk):
    rows, past = sp_ref.shape
    r_i = lax.broadcasted_iota(I32, (rows, LANES), 0)
    l_i = lax.broadcasted_iota(I32, (rows, LANES), 1)

    s = _nt_dot(qi_ref[...], kw16_ref[...][:, :D_IDX])
    r = jnp.maximum(s, 0.0) * w_ref[...]
    s_new = jnp.sum(r.reshape(rows, N_IDX_HEADS, LANES), axis=1)
    ok = ((l_i // n_q) == (r_i // n_q)) & ((l_i % n_q) <= (r_i % n_q))
    snew_scr[...] = jnp.where(ok, s_new, -jnp.inf)

    n_chunks = past // lane_chunk
    tiles = lane_chunk // LANES
    kf = float(topk)

    def count(ind):
        acc = ind(snew_scr[...], l_i + past)
        for ch in range(n_chunks):
            x = sp_ref[:, ch * lane_chunk:(ch + 1) * lane_chunk]
            for t in range(tiles):
                idx = l_i + (ch * lane_chunk + t * LANES)
                acc = acc + ind(x[:, t * LANES:(t + 1) * LANES], idx)
        return jnp.broadcast_to(jnp.sum(acc, axis=1, keepdims=True), (rows, LANES))

    c0 = count(lambda x, idx: jnp.where(x >= 0.0, 1.0, 0.0))
    key0 = jnp.where(c0 >= kf, 0, INT_MIN).astype(I32)

    def bit_body(it, key):
        cand = key | jnp.left_shift(jnp.int32(1), 30 - it)
        cf = _key_to_float(cand)
        return jnp.where(count(lambda x, idx: jnp.where(x >= cf, 1.0, 0.0)) >= kf, cand, key)

    key = lax.fori_loop(0, 31, bit_body, key0)
    key = jnp.maximum(key, KEY_NEG_FLT_MAX)
    thr = _key_to_float(key)
    cnt_gt = count(lambda x, idx: jnp.where(x > thr, 1.0, 0.0))
    cnt_ge = count(lambda x, idx: jnp.where(x >= thr, 1.0, 0.0))
    need = kf - cnt_gt
    straddle = jnp.max(jnp.where(cnt_ge > kf, 1.0, 0.0))

    idx_bits = (past + LANES).bit_length()
    thr_ref[...] = thr
    cut_ref[...] = jnp.full((rows, LANES), 1 << idx_bits, I32)

    @pl.when(straddle > 0.0)
    def _tie_cut():
        def cut_body(it, cut):
            cand = cut | jnp.left_shift(jnp.int32(1), (idx_bits - 1) - it)
            cnt = count(lambda x, idx: jnp.where(x == thr, jnp.where(idx < cand, 1.0, 0.0), 0.0))
            return jnp.where(cnt <= need, cand, cut)
        cut_ref[...] = lax.fori_loop(0, idx_bits, cut_body, jnp.zeros((rows, LANES), I32))

    cut = cut_ref[...]
    sn = snew_scr[...]
    tie_keep = jnp.where(l_i + past < cut, 0.0, -jnp.inf)
    nmn_ref[...] = jnp.where(sn > thr, 0.0, jnp.where(sn == thr, tie_keep, -jnp.inf))


def _sample_select(scores_past, qi_rows, w_rows, kw16_s, topk, n_q):
    rows, past = scores_past.shape
    lane_chunk = 2048 if past % 2048 == 0 else LANES
    full = lambda a: pl.BlockSpec(a.shape, lambda i: (0,) * a.ndim)
    o = jax.ShapeDtypeStruct((rows, LANES), F32)
    return pl.pallas_call(
        functools.partial(_sample_select_kernel, topk=topk, n_q=n_q, lane_chunk=lane_chunk),
        grid=(1,),
        in_specs=[full(scores_past), full(qi_rows), full(w_rows), full(kw16_s)],
        out_specs=[pl.BlockSpec((rows, LANES), lambda i: (0, 0))] * 3,
        out_shape=[o, jax.ShapeDtypeStruct((rows, LANES), I32), o],
        scratch_shapes=[pltpu.VMEM((rows, LANES), F32)],
        compiler_params=pltpu.CompilerParams(dimension_semantics=("arbitrary",), vmem_limit_bytes=VMEM_LIMIT),
        name="sample_select",
    )(scores_past, qi_rows, w_rows, kw16_s)


def _sample_attn_kernel(pt_ref, rbc_ref, q_ref, sp_ref, thr_ref, cut_ref, nmn_ref, k16n_ref, v16n_ref,
                        k_hbm, v_hbm, o_ref, kbuf, vbuf, ksem, vsem, lgs, bias_scr, mx_scr,
                        *, n_pages, n_q, unroll):
    b = pl.program_id(0)
    slot = b % 2
    rows = n_q * N_HEADS_A
    groups = n_pages // unroll
    streams = [(k_hbm, kbuf, ksem), (v_hbm, vbuf, vsem)]
    lane = lax.broadcasted_iota(I32, (rows, LANES), 1)
    rq = lax.broadcasted_iota(I32, (rows, LANES), 0) // N_HEADS_A
    expand = lambda a: jnp.concatenate(
        [jnp.broadcast_to(a[t:t + 1], (N_HEADS_A, LANES)) for t in range(n_q)], axis=0)

    @pl.when(b == 0)
    def _first():
        _start_batch_pages(streams, pt_ref, 0, 0, n_pages)
        far = jnp.concatenate([rbc_ref[N_BUCKETS - 1]] * n_q, axis=0)
        for t, dist in enumerate((PAGE_SIZE + rq - lane, rq - lane % n_q)):
            bkt = _t5_bucket(dist)
            tile = jnp.zeros((rows, LANES), F32)
            for k in range(N_BUCKETS - 1):
                tile = jnp.where(bkt == k, jnp.concatenate([rbc_ref[k]] * n_q, axis=0) - far, tile)
            bias_scr[t] = tile * LOG2E

    _wait_batch_pages(streams, pt_ref, b, slot, n_pages)

    thr = expand(thr_ref[0])
    cut = expand(cut_ref[0])
    q = q_ref[0]

    def logits_pass(prefetch):
        def body(c, mx):
            for u in range(unroll):
                p = c * unroll + u
                if prefetch:
                    for pages_hbm, buf, sem in streams:
                        _page_copy(pages_hbm, buf, sem, pt_ref, b + 1, p, 1 - slot).start()
                lg = _dot(q, kbuf[slot, p].astype(BF16))
                sc = expand(sp_ref[0, c, :, u * PAGE_SIZE:(u + 1) * PAGE_SIZE])
                tie_keep = jnp.where(lane + p * PAGE_SIZE < cut, 0.0, -jnp.inf)
                x = lg + jnp.where(sc > thr, 0.0, jnp.where(sc == thr, tie_keep, -jnp.inf))
                if u == unroll - 1:
                    x = x + bias_scr[0] * jnp.where(c == groups - 1, 1.0, 0.0)
                lgs[p] = x
                mx = jnp.maximum(mx, x)
            return mx
        mx_scr[...] = lax.fori_loop(0, groups, body, jnp.full((rows, LANES), NEG_BIG, F32))

    has_next = b + 1 < pl.num_programs(0)
    pl.when(has_next)(lambda: logits_pass(True))
    pl.when(jnp.logical_not(has_next))(lambda: logits_pass(False))

    xn = _nt_dot(q, k16n_ref[...]) + bias_scr[1] + expand(nmn_ref[0])
    m = jnp.max(jnp.maximum(mx_scr[...], xn), axis=1, keepdims=True)

    def pv_body(c, carry):
        lsum, acc = carry
        for u in range(unroll):
            p = c * unroll + u
            pr = jnp.exp2(lgs[p] - m)
            lsum = lsum + pr
            acc = acc + _nt_dot(pr.astype(BF16), vbuf[slot, p].astype(BF16))
        return lsum, acc

    pn = jnp.exp2(xn - m)
    lsum, acc = lax.fori_loop(0, groups, pv_body, (pn, _dot(pn.astype(BF16), v16n_ref[...])))
    out = acc * (1.0 / jnp.sum(lsum, axis=1, keepdims=True))
    head = lax.broadcasted_iota(I32, (rows, HEAD_DIM), 0) % N_HEADS_A
    o_ref[0] = jnp.where(head < N_HEADS_A // N_KV_A, out[:, :HEAD_DIM], out[:, HEAD_DIM:])


def _sample_attn(page_table, rb_col, q_bd, scores4, thr3, cut3, nmn3, k16n, v16n, k_pages_t, v_pages_t, unroll):
    batch, n_pages = page_table.shape
    n_q = scores4.shape[2]
    rows = q_bd.shape[1]
    per_b = lambda a: pl.BlockSpec((1,) + a.shape[1:], lambda b, pt: (b,) + (0,) * (a.ndim - 1))
    full = lambda a: pl.BlockSpec(a.shape, lambda b, pt: (0,) * a.ndim)
    page_buf = pltpu.VMEM((2, n_pages) + k_pages_t.shape[1:], k_pages_t.dtype)
    grid_spec = pltpu.PrefetchScalarGridSpec(
        num_scalar_prefetch=1,
        grid=(batch,),
        in_specs=[full(rb_col), per_b(q_bd), per_b(scores4), per_b(thr3), per_b(cut3), per_b(nmn3),
                  full(k16n), full(v16n), pl.BlockSpec(memory_space=pl.ANY), pl.BlockSpec(memory_space=pl.ANY)],
        out_specs=pl.BlockSpec((1, rows, HEAD_DIM), lambda b, pt: (b, 0, 0)),
        scratch_shapes=[page_buf, page_buf, pltpu.SemaphoreType.DMA((2,)), pltpu.SemaphoreType.DMA((2,)),
                        pltpu.VMEM((n_pages, rows, LANES), F32), pltpu.VMEM((2, rows, LANES), F32),
                        pltpu.VMEM((rows, LANES), F32)],
    )
    return pl.pallas_call(
        functools.partial(_sample_attn_kernel, n_pages=n_pages, n_q=n_q, unroll=unroll),
        grid_spec=grid_spec,
        out_shape=jax.ShapeDtypeStruct((batch, rows, HEAD_DIM), F32),
        compiler_params=pltpu.CompilerParams(dimension_semantics=("arbitrary",),
                                             vmem_limit_bytes=VMEM_LIMIT_PAGED),
        name="sample_attn",
    )(page_table, rb_col, q_bd, scores4, thr3, cut3, nmn3, k16n, v16n, k_pages_t, v_pages_t)


def _rope_tables(pos):
    half = HEAD_DIM // 2
    inv = ROPE_BASE ** (-jnp.arange(half, dtype=F32) / half)
    ang = pos.astype(F32)[:, None] * inv[None, :]
    cos = jnp.cos(ang)
    sin = jnp.sin(ang)
    reps = LANES // HEAD_DIM
    cos_t = jnp.tile(jnp.concatenate([cos, cos], axis=1), (1, reps))
    sin_t = jnp.tile(jnp.concatenate([-sin, sin], axis=1), (1, reps))
    return cos_t, sin_t


def _cat_weight(w_in):
    offs = np.cumsum((0,) + SPLIT_SIZES)
    qa, ka, va, qi, ki, wi, ga, qb, kb, vb, gb = [w_in[:, offs[i]:offs[i + 1]] for i in range(len(SPLIT_SIZES))]
    pad = jnp.zeros((w_in.shape[0], LANES - D_IDX - N_IDX_HEADS), w_in.dtype)
    return jnp.concatenate([qa, ka, va, qi, ga, qb, kb, vb, gb, ki, wi, pad], axis=1).astype(BF16)


def _pages_t(cache):
    pool, page = cache.shape[:2]
    return jnp.transpose(cache, (0, 2, 3, 1)).reshape(pool, -1, page)


def _largest_divisor(n, cap):
    d = cap
    while n % d:
        d //= 2
    return d


def kernel(x_prompt, x_sample, cache_k, cache_v, cache_kidx, state_ret, page_table, p_prompt, p_sample,
           rel_bias, w_in, w_out, g_pre, g_post, w_ple_up, w_ple_gate):
    batch, seq, d_model = x_prompt.shape
    dec_b, dec_t, _ = x_sample.shape
    depth = w_in.shape[0]
    n_pages = page_table.shape[1]
    past = n_pages * PAGE_SIZE
    rows_s = dec_b * dec_t
    assert depth == 1 and rows_s == LANES and seq % BLK == 0

    w_cat = _cat_weight(w_in[0])
    gpre = g_pre[0].reshape(1, d_model)
    gpost = g_post[0].reshape(1, d_model)
    wo16 = w_out[0].astype(BF16)
    wup16 = w_ple_up[0].astype(BF16)
    wg16 = w_ple_gate[0].astype(BF16)

    tm = _largest_divisor(seq, 512)
    cos_p, sin_p = _rope_tables(jnp.arange(seq))
    xp2 = x_prompt.reshape(batch * seq, d_model)
    pp = _project(xp2, gpre, w_cat, cos_p, sin_p, tm, seq)
    oa_p = _dsa_prompt(pp, rel_bias, batch, seq)
    r3 = lambda a: a.reshape(batch, seq, D_B)
    s0_p = jnp.zeros((batch, N_HEADS_B // 2, LANES, LANES), F32)
    yn_p, sp_pairs = _retention(r3(pp["qb"]), r3(pp["kb"]), r3(pp["vb"]), s0_p, float(min(RET_CHUNK, seq)))
    y_prompt = _merge(xp2, oa_p, yn_p.reshape(batch * seq, D_B), pp["ga"], pp["gb"],
                      p_prompt[0].reshape(batch * seq, -1), wo16, gpost, wup16, wg16, tm)

    pos_s = past + jnp.arange(dec_t)
    cos_s, sin_s = _rope_tables(jnp.tile(pos_s, dec_b))
    xs2 = x_sample.reshape(rows_s, d_model)
    ps = _project(xs2, gpre, w_cat, cos_s, sin_s, rows_s, rows_s)
    topk_s = min(TOPK_MAX, (past + dec_t) // 4)

    qi_rows = ps["qi"].reshape(rows_s * N_IDX_HEADS, D_IDX)
    w_rows = jnp.broadcast_to(ps["kw"][:, D_IDX:D_IDX + N_IDX_HEADS].reshape(rows_s * N_IDX_HEADS, 1),
                              (rows_s * N_IDX_HEADS, LANES))
    unroll = _largest_divisor(n_pages, 32)
    scores4 = _sample_index(page_table, qi_rows.reshape(dec_b, dec_t * N_IDX_HEADS, D_IDX),
                            w_rows.reshape(dec_b, dec_t * N_IDX_HEADS, LANES),
                            jnp.transpose(cache_kidx[0], (0, 2, 1)), unroll)
    scores_past = scores4.transpose(0, 2, 1, 3).reshape(rows_s, past)
    thr, cut, nmn = _sample_select(scores_past, qi_rows, w_rows, ps["kw16"], topk_s, dec_t)

    kv_of_head = np.arange(N_HEADS_A) // (N_HEADS_A // N_KV_A)
    place = jnp.asarray((kv_of_head[:, None] == np.arange(N_KV_A)[None, :]).astype(np.float32)).astype(BF16)
    q_bd = (ps["qa"].reshape(dec_b, dec_t, N_HEADS_A, 1, HEAD_DIM) * place[None, None, :, :, None]).reshape(
        dec_b, dec_t * N_HEADS_A, N_KV_A * HEAD_DIM)
    rb_col = jnp.broadcast_to(rel_bias[:, :, None], rel_bias.shape + (LANES,))
    b3 = lambda a: a.reshape(dec_b, dec_t, LANES)
    o_qh = _sample_attn(page_table, rb_col, q_bd, scores4, b3(thr), b3(cut), b3(nmn),
                        ps["k16"], ps["v16"], _pages_t(cache_k[0]), _pages_t(cache_v[0]), unroll)
    oa_s = o_qh.reshape(rows_s, D_A)

    chunk_s = RET_CHUNK
    padt = lambda a: jnp.pad(a.reshape(dec_b, dec_t, D_B), ((0, 0), (0, chunk_s - dec_t), (0, 0)))
    yn_s, ss_pairs = _retention(padt(ps["qb"]), padt(ps["kb"]), padt(ps["vb"]),
                                _state_to_pairs(state_ret[0]), float(math.gcd(dec_t, RET_CHUNK)))
    y_sample = _merge(xs2, oa_s, yn_s[:, :dec_t].reshape(rows_s, D_B), ps["ga"], ps["gb"],
                      p_sample[0].reshape(rows_s, -1), wo16, gpost, wup16, wg16, rows_s)

    def kv(a_t, b, t):
        a = a_t.reshape(a_t.shape[0], N_KV_A, HEAD_DIM, -1).transpose(0, 3, 1, 2)
        return a.reshape(1, b, t, N_KV_A, HEAD_DIM)

    def ki(a_t, b, t):
        return a_t.transpose(0, 2, 1).reshape(1, b, t, D_IDX)

    return (
        y_prompt.reshape(batch, seq, d_model),
        y_sample.reshape(dec_b, dec_t, d_model),
        kv(pp["ka_t"], batch, seq), kv(pp["va_t"], batch, seq),
        ki(pp["ki_t"], batch, seq),
        _pairs_to_state(sp_pairs)[None].astype(state_ret.dtype),
        kv(ps["ka_t"], dec_b, dec_t), kv(ps["va_t"], dec_b, dec_t),
        ki(ps["ki_t"], dec_b, dec_t),
        _pairs_to_state(ss_pairs)[None].astype(state_ret.dtype),
    )
```

```python
import functools
import math

import jax
import jax.numpy as jnp
import numpy as np
from jax import lax
from jax.experimental import pallas as pl
from jax.experimental.pallas import tpu as pltpu

F32 = jnp.float32
BF16 = jnp.bfloat16
I32 = jnp.int32

HEAD_DIM = 64
N_HEADS_A = 8
N_KV_A = 2
N_IDX_HEADS = 8
D_IDX = 64
TOPK_MAX = 256
N_HEADS_B = 8
D_A = N_HEADS_A * HEAD_DIM
D_B = N_HEADS_B * HEAD_DIM
N_BUCKETS = 32
MAX_DISTANCE = 128
ROPE_BASE = 10000.0
RET_CHUNK = 128
PAGE_SIZE = 128
EPS = 1e-6
SPLIT_SIZES = (D_A, N_KV_A * HEAD_DIM, N_KV_A * HEAD_DIM, N_IDX_HEADS * D_IDX, D_IDX, N_IDX_HEADS,
               D_A, D_B, D_B, D_B, D_B)

LANES = 128
SUBLANES = 8
BLK = 128
VMEM_LIMIT = 48 * 1024 * 1024
VMEM_LIMIT_PAGED = 56 * 1024 * 1024

INT_MIN = -(2 ** 31)
KEY_NEG_FLT_MAX = INT_MIN + 0x00800000
NEG_BIG = -1e30
LOG2E = math.log2(math.e)
ONES_ROWS = 16

_C_QA, _C_KA, _C_VA, _C_QI, _C_KW, _C_GA, _C_QB, _C_KB, _C_VB, _C_GB, _C_END = (
    0, 512, 640, 768, 1280, 1408, 1920, 2432, 2944, 3456, 3968)


def _nt_dot(a, b):
    return lax.dot_general(a, b, (((1,), (1,)), ((), ())), preferred_element_type=F32)


def _dot(a, b):
    return jnp.dot(a, b, preferred_element_type=F32)


def _sigmoid(x):
    return 1.0 / (1.0 + jnp.exp(-x))


def _key_to_float(k):
    bits = jnp.where(k >= 0, k, k ^ 0x7FFFFFFF)
    return lax.bitcast_convert_type(bits, F32)


def _t5_bucket(n):
    max_exact = N_BUCKETS // 2
    n = jnp.maximum(n, 0)
    nf = jnp.maximum(n, 1).astype(F32)
    large = max_exact + jnp.floor(jnp.log(nf / max_exact) / math.log(MAX_DISTANCE / max_exact)
                                  * (N_BUCKETS - max_exact)).astype(I32)
    large = jnp.minimum(large, N_BUCKETS - 1)
    return jnp.where(n < max_exact, n, large)


def _bias_from_bucket(bkt, rb_ref, h):
    out = jnp.zeros(bkt.shape, F32)
    for k in range(N_BUCKETS):
        out = jnp.where(bkt == k, rb_ref[k, h], out)
    return out


def _proj_kernel(x_ref, g_ref, w_ref, cos_ref, sin_ref,
                 qa_ref, kat_ref, vat_ref, k16_ref, v16_ref, qi_ref, ga_ref, gb_ref,
                 qb_ref, kb_ref, vb_ref, kw_ref, kw16_ref, kit_ref, h_scr):
    x = x_ref[...]
    ms = jnp.mean(x * x, axis=-1, keepdims=True)
    h_scr[...] = (x * lax.rsqrt(ms + EPS) * g_ref[...]).astype(BF16)

    def mm(lo, hi):
        return _dot(h_scr[...], w_ref[:, lo:hi])

    qa_ref[...] = (mm(_C_QA, _C_KA) * (HEAD_DIM ** -0.5 * LOG2E)).astype(BF16)
    ka = mm(_C_KA, _C_VA)
    kat_ref[0] = ka.T
    k16_ref[...] = ka.astype(BF16)
    va = mm(_C_VA, _C_QI)
    vat_ref[0] = va.T
    v16_ref[...] = va.astype(BF16)
    qi_ref[...] = mm(_C_QI, _C_KW).astype(BF16)
    ga_ref[...] = mm(_C_GA, _C_QB)
    gb_ref[...] = mm(_C_GB, _C_END)
    vb_ref[...] = mm(_C_VB, _C_GB).astype(BF16)

    cos = cos_ref[...]
    sin = sin_ref[...]
    lane = lax.broadcasted_iota(I32, cos.shape, 1)
    first_half = (lane % HEAD_DIM) < (HEAD_DIM // 2)

    def rope(z, scale):
        outs = []
        for g in range(z.shape[1] // LANES):
            zg = z[:, g * LANES:(g + 1) * LANES]
            partner = jnp.where(first_half, pltpu.roll(zg, LANES - HEAD_DIM // 2, 1),
                                pltpu.roll(zg, HEAD_DIM // 2, 1))
            r = zg * cos + partner * sin
            if scale != 1.0:
                r = r * scale
            outs.append(r.astype(BF16))
        return jnp.concatenate(outs, axis=1)

    qb_ref[...] = rope(mm(_C_QB, _C_KB), 1.0)
    kb_ref[...] = rope(mm(_C_KB, _C_VB), HEAD_DIM ** -0.5)

    kw = mm(_C_KW, _C_GA)
    wscale = (N_IDX_HEADS ** -0.5) * (D_IDX ** -0.5)
    kw = kw * jnp.where(lane >= D_IDX, wscale, 1.0)
    kw_ref[...] = kw
    kw16_ref[...] = kw.astype(BF16)
    kit_ref[0] = kw.T[:D_IDX]


def _project(x2d, g_pre, w_cat, cos_t, sin_t, tm, seq):
    rows = x2d.shape[0]
    d_model = x2d.shape[1]
    n_tab = cos_t.shape[0] // tm
    tiles = seq // tm
    row_spec = lambda w: pl.BlockSpec((tm, w), lambda i: (i, 0))
    t_spec = lambda w: pl.BlockSpec((1, w, tm), lambda i: (i // tiles, 0, i % tiles))
    outs = [
        ("qa", D_A, BF16), ("ka_t", LANES, F32), ("va_t", LANES, F32), ("k16", LANES, BF16), ("v16", LANES, BF16),
        ("qi", N_IDX_HEADS * D_IDX, BF16), ("ga", D_A, F32), ("gb", D_B, F32),
        ("qb", D_B, BF16), ("kb", D_B, BF16), ("vb", D_B, BF16), ("kw", LANES, F32), ("kw16", LANES, BF16),
        ("ki_t", D_IDX, F32),
    ]
    transposed = lambda n: n.endswith("_t")
    res = pl.pallas_call(
        _proj_kernel,
        grid=(rows // tm,),
        in_specs=[
            row_spec(d_model),
            pl.BlockSpec((1, d_model), lambda i: (0, 0)),
            pl.BlockSpec(w_cat.shape, lambda i: (0, 0)),
            pl.BlockSpec((tm, LANES), lambda i: (i % n_tab, 0)),
            pl.BlockSpec((tm, LANES), lambda i: (i % n_tab, 0)),
        ],
        out_specs=[t_spec(w) if transposed(n) else row_spec(w) for n, w, _ in outs],
        out_shape=[jax.ShapeDtypeStruct((rows // seq, w, seq) if transposed(n) else (rows, w), dt)
                   for n, w, dt in outs],
        scratch_shapes=[pltpu.VMEM((tm, d_model), BF16)],
        compiler_params=pltpu.CompilerParams(dimension_semantics=("arbitrary",), vmem_limit_bytes=VMEM_LIMIT),
        name="proj",
    )(x2d, g_pre, w_cat, cos_t, sin_t)
    return {n: r for (n, _, _), r in zip(outs, res)}


def _dsa_prompt_kernel(rb_ref, qi_ref, kwq_ref, qin_ref, kwqn_ref, kw16_ref, qa_ref, k16_ref, v16_ref, oa_ref,
                       sc2, bt, vt, lgs, acc_scr, cut_scr, key_scr, *, topk, search_blk):
    b = pl.program_id(0)
    j = pl.program_id(1)
    nblk = pl.num_programs(1)
    sc = sc2.at[j % 2]
    sc_next = sc2.at[1 - j % 2]
    row = lax.broadcasted_iota(I32, (BLK, BLK), 0)
    col = lax.broadcasted_iota(I32, (BLK, BLK), 1)

    @pl.when((b == 0) & (j == 0))
    def _init_bias():
        for d in range(2):
            bkt = _t5_bucket(col - row + d * BLK)
            for h in range(N_HEADS_A):
                bt[d, h] = (_bias_from_bucket(bkt, rb_ref, h) - rb_ref[N_BUCKETS - 1, h]) * LOG2E
        bt[2] = jnp.zeros(bt.shape[1:], F32)

    @pl.when(j == 0)
    def _clear_values():
        vt[...] = jnp.zeros(vt.shape, BF16)

    vblk = v16_ref[pl.ds(pl.multiple_of(j * BLK, BLK), BLK), :]
    vtj = vblk.astype(F32).T.astype(BF16)
    ones = jnp.ones((ONES_ROWS, BLK), BF16)
    vt[j] = jnp.concatenate([piece for n in range(N_KV_A)
                             for piece in (vtj[n * HEAD_DIM:(n + 1) * HEAD_DIM], ones)], axis=0)

    def indexer_queries(qi_blk_ref, kw_blk_ref):
        qi = qi_blk_ref[...]
        qi_stack = jnp.concatenate([qi[:, h * D_IDX:(h + 1) * D_IDX] for h in range(N_IDX_HEADS)], axis=0)
        return qi_stack, kw_blk_ref[...].T

    def score_group(c, dst, queries, first_q):
        qi_stack, wi_t = queries
        for u in range(search_blk):
            i = c * search_blk + u
            kib = kw16_ref[pl.ds(pl.multiple_of(i * BLK, BLK), BLK), :][:, :D_IDX]
            s = _nt_dot(kib, qi_stack)
            acc = jnp.zeros((BLK, BLK), F32)
            for h in range(N_IDX_HEADS):
                acc = acc + wi_t[D_IDX + h:D_IDX + h + 1, :] * jnp.maximum(s[:, h * BLK:(h + 1) * BLK], 0.0)
            dst[i] = jnp.where(row + i * BLK <= col + first_q, acc, -jnp.inf)

    @pl.when(j == 0)
    def _own_scores():
        score_group(0, sc, indexer_queries(qi_ref, kwq_ref), 0)

    n_chunk = (j + search_blk) // search_blk

    def col_sum(w):
        return jnp.sum(w.reshape(BLK // SUBLANES, SUBLANES, BLK), axis=0)

    qa = qa_ref[...]
    g_per_kv = N_HEADS_A // N_KV_A
    q_stack = [jnp.concatenate([qa[:, (n * g_per_kv + g) * HEAD_DIM:(n * g_per_kv + g + 1) * HEAD_DIM]
                                for g in range(g_per_kv)], axis=0) for n in range(N_KV_A)]

    kf = float(topk)
    n_units = search_blk * N_KV_A

    def search(n_groups):
        def count_ge(cand, unit=None):
            acc = jnp.zeros((SUBLANES, BLK), F32)
            for c in range(n_groups):
                for u in range(search_blk):
                    acc = acc + col_sum(jnp.where(sc[c * search_blk + u] >= cand, 1.0, 0.0))
                if unit is not None:
                    i = c * search_blk + unit[0]
                    n = unit[1]
                    lg = _nt_dot(k16_ref[i * BLK:(i + 1) * BLK, n * HEAD_DIM:(n + 1) * HEAD_DIM], q_stack[n])
                    for g in range(g_per_kv):
                        lgs[i, n * g_per_kv + g] = lg[:, g * BLK:(g + 1) * BLK]
            return jnp.sum(acc, axis=0, keepdims=True)

        def bit_step(bit, key, unit=None):
            cand = key | jnp.left_shift(jnp.int32(1), bit)
            cnt = count_ge(_key_to_float(cand), unit)
            return jnp.where(cnt >= kf, cand, key)

        c0 = count_ge(jnp.zeros((1, BLK), F32))
        key = jnp.where(c0 >= kf, 0, INT_MIN).astype(I32)
        for it in range(n_units):
            key = bit_step(30 - it, key, (it // N_KV_A, it % N_KV_A))
        n_rest = jnp.where((j + 1) * BLK <= topk, 0, 31 - n_units)
        key = lax.fori_loop(0, n_rest, lambda it, k: bit_step(30 - n_units - it, k), key)
        key_scr[...] = jnp.broadcast_to(key, key_scr.shape)

    for n_groups in range(1, sc.shape[0] // search_blk + 1):
        pl.when(n_chunk == n_groups)(functools.partial(search, n_groups))

    key = jnp.maximum(key_scr[0:1, :], KEY_NEG_FLT_MAX)
    thr = _key_to_float(key)

    def count2_body(c, accs):
        a_gt, a_ge = accs
        for u in range(search_blk):
            t = sc[c * search_blk + u]
            a_gt = a_gt + col_sum(jnp.where(t > thr, 1.0, 0.0))
            a_ge = a_ge + col_sum(jnp.where(t >= thr, 1.0, 0.0))
        return a_gt, a_ge

    z8 = jnp.zeros((SUBLANES, BLK), F32)
    a_gt, a_ge = lax.fori_loop(0, n_chunk, count2_body, (z8, z8))
    cnt_gt = jnp.sum(a_gt, axis=0, keepdims=True)
    cnt_ge = jnp.sum(a_ge, axis=0, keepdims=True)
    need = kf - cnt_gt
    straddle = jnp.max(jnp.where(cnt_ge > kf, 1.0, 0.0))

    n_keys_max = sc.shape[0] * BLK
    cut_scr[...] = jnp.full(cut_scr.shape, 2 * n_keys_max, I32)

    @pl.when(straddle > 0.0)
    def _tie_cut():
        def count_tie(cand):
            def body(i, acc):
                t = sc[i]
                hit = jnp.where(t == thr, jnp.where(row + i * BLK < cand, 1.0, 0.0), 0.0)
                return acc + col_sum(hit)
            acc = lax.fori_loop(0, j + 1, body, z8)
            return jnp.sum(acc, axis=0, keepdims=True)

        def cut_body(it, cut):
            cand = cut | jnp.left_shift(jnp.int32(1), (n_keys_max.bit_length() - 1) - it)
            return jnp.where(count_tie(cand) <= need, cand, cut)

        cut = lax.fori_loop(0, n_keys_max.bit_length(), cut_body, jnp.zeros((1, BLK), I32))
        cut_scr[...] = jnp.broadcast_to(cut, cut_scr.shape)

    cut = cut_scr[0:1, :]

    def col_max(w):
        return jnp.max(w.reshape(BLK // SUBLANES, SUBLANES, BLK), axis=0)

    def mask_body(c, m8, biased):
        m8 = list(m8)
        for u in range(search_blk):
            i = c * search_blk + u
            t = sc[i]
            tie_keep = jnp.where(row + i * BLK < cut, 0.0, -jnp.inf)
            nm = jnp.where(t > thr, 0.0, jnp.where(t == thr, tie_keep, -jnp.inf))
            near = jnp.clip(j - i, 0, 2)
            for h in range(N_HEADS_A):
                x = lgs[i, h] + nm
                if biased:
                    x = x + bt[near, h]
                lgs[i, h] = x
                m8[h] = jnp.maximum(m8[h], col_max(x))
        return tuple(m8)

    n_far = jnp.maximum(n_chunk - 2, 0)
    neg8 = jnp.full((SUBLANES, BLK), NEG_BIG, F32)
    m8 = lax.fori_loop(0, n_far, functools.partial(mask_body, biased=False), (neg8,) * N_HEADS_A)
    m8 = lax.fori_loop(n_far, n_chunk, functools.partial(mask_body, biased=True), m8)
    m_row = [jnp.max(m8[h], axis=0, keepdims=True) for h in range(N_HEADS_A)]

    acc_scr[...] = jnp.zeros(acc_scr.shape, F32)

    vrows = HEAD_DIM + ONES_ROWS

    next_queries = indexer_queries(qin_ref, kwqn_ref)
    next_first_q = (j + 1) * BLK

    def pv_body(c, carry):
        for u in range(search_blk):
            i = c * search_blk + u
            vti = vt[i]
            for n in range(N_KV_A):
                ps = [jnp.exp2(lgs[i, n * g_per_kv + g] - m_row[n * g_per_kv + g]).astype(BF16)
                      for g in range(g_per_kv)]
                acc_scr[n] += _dot(vti[n * vrows:(n + 1) * vrows, :], jnp.concatenate(ps, axis=1))
        score_group(c, sc_next, next_queries, next_first_q)
        return carry

    lax.fori_loop(0, n_chunk, pv_body, 0)

    @pl.when(((j + 1) % search_blk == 0) & (j + 1 < nblk))
    def _next_reaches_one_more_group():
        score_group(n_chunk, sc_next, next_queries, next_first_q)

    parts = []
    for n in range(N_KV_A):
        acc = acc_scr[n]
        out = acc[:HEAD_DIM] * (1.0 / acc[HEAD_DIM:HEAD_DIM + 1])
        parts += [out[:, g * BLK:(g + 1) * BLK] for g in range(g_per_kv)]
    oa_ref[...] = jnp.concatenate(parts, axis=0).T


def _dsa_prompt(p, rel_bias, batch, seq):
    nblk = seq // BLK
    topk = min(TOPK_MAX, seq // 4)
    search_blk = 4 if nblk % 4 == 0 else 1
    qrow = lambda w: pl.BlockSpec((BLK, w), lambda b, j: (b * nblk + j, 0))
    qnext = lambda w: pl.BlockSpec((BLK, w), lambda b, j: (b * nblk + jnp.minimum(j + 1, nblk - 1), 0))
    brow = lambda w: pl.BlockSpec((seq, w), lambda b, j: (b, 0))
    return pl.pallas_call(
        functools.partial(_dsa_prompt_kernel, topk=topk, search_blk=search_blk),
        grid=(batch, nblk),
        in_specs=[
            pl.BlockSpec(memory_space=pltpu.SMEM),
            qrow(N_IDX_HEADS * D_IDX), qrow(LANES), qnext(N_IDX_HEADS * D_IDX), qnext(LANES),
            brow(LANES), qrow(D_A), brow(LANES), brow(LANES),
        ],
        out_specs=qrow(D_A),
        out_shape=jax.ShapeDtypeStruct((batch * seq, D_A), F32),
        scratch_shapes=[
            pltpu.VMEM((2, nblk, BLK, BLK), F32),
            pltpu.VMEM((3, N_HEADS_A, BLK, BLK), F32),
            pltpu.VMEM((nblk, N_KV_A * (HEAD_DIM + ONES_ROWS), BLK), BF16),
            pltpu.VMEM((nblk, N_HEADS_A, BLK, BLK), F32),
            pltpu.VMEM((N_KV_A, HEAD_DIM + ONES_ROWS, (N_HEADS_A // N_KV_A) * BLK), F32),
            pltpu.VMEM((SUBLANES, BLK), I32),
            pltpu.VMEM((SUBLANES, BLK), I32),
        ],
        compiler_params=pltpu.CompilerParams(dimension_semantics=("arbitrary", "arbitrary"),
                                             vmem_limit_bytes=VMEM_LIMIT),
        name="dsa_prompt",
    )(rel_bias, p["qi"], p["kw"], p["qi"], p["kw"], p["kw16"], p["qa"], p["k16"], p["v16"])


def _retention_kernel(q_ref, k_ref, v_ref, s0_ref, dmat_ref, din_ref, dout_ref, decm_ref, bdm_ref,
                      yn_ref, s_out_ref, s_scr, *, n_sub):
    c = pl.program_id(1)
    chunk = din_ref.shape[0]

    @pl.when(c == 0)
    def _load_state():
        z = jnp.zeros((HEAD_DIM, HEAD_DIM), F32)
        for p in range(N_HEADS_B // 2):
            pair = jnp.concatenate([jnp.concatenate([s0_ref[0, 2 * p], z], axis=1),
                                    jnp.concatenate([z, s0_ref[0, 2 * p + 1]], axis=1)], axis=0)
            s_scr[p] = pair.T

    lo = lax.broadcasted_iota(I32, (chunk, LANES), 1) < HEAD_DIM
    zero = jnp.zeros((), BF16)
    for p in range(N_HEADS_B // 2):
        sl = slice(p * LANES, (p + 1) * LANES)
        st = s_scr[p]
        for u in range(n_sub):
            rows = slice(u * chunk, (u + 1) * chunk)
            q, k, v = q_ref[0, rows, sl], k_ref[0, rows, sl], v_ref[0, rows, sl]
            qd = (q.astype(F32) * din_ref[:, sl]).astype(BF16)
            kd = (k.astype(F32) * dout_ref[:, sl]).astype(BF16)
            vt = v.astype(F32).T.astype(BF16)
            s_a = (_nt_dot(jnp.where(lo, k, zero), q) * dmat_ref[2 * p]).astype(BF16)
            s_b = (_nt_dot(jnp.where(lo, zero, k), q) * dmat_ref[2 * p + 1]).astype(BF16)
            intra = jnp.concatenate([_dot(vt[:HEAD_DIM], s_a), _dot(vt[HEAD_DIM:], s_b)], axis=0)
            o = (_nt_dot(st.astype(BF16), qd) + intra).reshape(2, HEAD_DIM, chunk)
            dlt = o - jnp.mean(o, axis=1, keepdims=True)
            var = jnp.mean(dlt * dlt, axis=1, keepdims=True)
            yn_ref[0, rows, sl] = (dlt * lax.rsqrt(var + EPS)).reshape(LANES, chunk).T
            st = st * decm_ref[p] + _dot(vt, kd) * bdm_ref[...]
        s_scr[p] = st
        pair = st.T
        s_out_ref[0, 2 * p] = pair[:HEAD_DIM, :HEAD_DIM]
        s_out_ref[0, 2 * p + 1] = pair[HEAD_DIM:, HEAD_DIM:]


def _ret_tables(chunk, n_real):
    h = N_HEADS_B
    log_g = jnp.log1p(-jnp.exp2(-5.0 - jnp.arange(h, dtype=F32)))
    i = jnp.arange(chunk, dtype=F32)
    diff = i[:, None] - i[None, :]
    dmat = jnp.where(diff >= 0, jnp.exp(jnp.maximum(diff, 0.0)[None] * log_g[:, None, None]), 0.0)
    dmat_t = jnp.swapaxes(dmat, 1, 2)
    dec_in = jnp.exp((i + 1.0)[:, None] * log_g[None, :])
    dec_out = jnp.exp((n_real - 1.0 - i)[:, None] * log_g[None, :])
    dec_chunk = jnp.exp(n_real * log_g)
    din = jnp.repeat(dec_in, HEAD_DIM, axis=1)
    dout = jnp.repeat(dec_out, HEAD_DIM, axis=1)
    head_of = np.arange(LANES) // HEAD_DIM
    bd = jnp.asarray((head_of[:, None] == head_of[None, :]).astype(np.float32))
    dc = jnp.repeat(dec_chunk, HEAD_DIM).reshape(h // 2, LANES)
    decm = dc[:, :, None] * bd[None]
    return dmat_t, din, dout, decm, bd


def _retention(q3, k3, v3, s0, n_real):
    batch, seq, _ = q3.shape
    chunk = min(RET_CHUNK, seq)
    n_sub = _largest_divisor(seq // chunk, 4)
    dmat_t, din, dout, decm, bd = _ret_tables(chunk, n_real)
    tok = pl.BlockSpec((1, n_sub * chunk, D_B), lambda b, c: (b, c, 0))
    st = pl.BlockSpec((1,) + s0.shape[1:], lambda b, c: (b, 0, 0, 0))
    full = lambda a: pl.BlockSpec(a.shape, lambda b, c: (0,) * a.ndim)
    return pl.pallas_call(
        functools.partial(_retention_kernel, n_sub=n_sub),
        grid=(batch, seq // (n_sub * chunk)),
        in_specs=[tok, tok, tok, st, full(dmat_t), full(din), full(dout), full(decm), full(bd)],
        out_specs=[tok, st],
        out_shape=[jax.ShapeDtypeStruct((batch, seq, D_B), F32),
                   jax.ShapeDtypeStruct(s0.shape, F32)],
        scratch_shapes=[pltpu.VMEM((N_HEADS_B // 2, LANES, LANES), F32)],
        compiler_params=pltpu.CompilerParams(dimension_semantics=("arbitrary", "arbitrary"),
                                             vmem_limit_bytes=VMEM_LIMIT),
        name="retention",
    )(q3, k3, v3, s0, dmat_t, din, dout, decm, bd)


def _merge_kernel(x_ref, oa_ref, yn_ref, ga_ref, gb_ref, p_ref, wo_ref, gp_ref, wup_ref, wg_ref, out_ref):
    ga = ga_ref[...]
    gb = gb_ref[...]
    ya = (ga * _sigmoid(ga) * oa_ref[...]).astype(BF16)
    yb = (gb * _sigmoid(gb) * yn_ref[...]).astype(BF16)
    y = _dot(ya, wo_ref[0:D_A, :]) + _dot(yb, wo_ref[D_A:D_A + D_B, :])
    ms = jnp.mean(y * y, axis=-1, keepdims=True)
    x1 = x_ref[...] + y * lax.rsqrt(ms + EPS) * gp_ref[...]
    ple = _dot(p_ref[...].astype(BF16), wup_ref[...])
    gate = _sigmoid(_dot(x1.astype(BF16), wg_ref[...]))
    out_ref[...] = x1 + ple * gate


def _merge(x2d, oa, yn, ga, gb, p2d, wo16, g_post, wup16, wg16, tm):
    rows, d_model = x2d.shape
    row = lambda w: pl.BlockSpec((tm, w), lambda i: (i, 0))
    full = lambda a: pl.BlockSpec(a.shape, lambda i: (0, 0))
    return pl.pallas_call(
        _merge_kernel,
        grid=(rows // tm,),
        in_specs=[row(d_model), row(D_A), row(D_B), row(D_A), row(D_B), row(p2d.shape[1]),
                  full(wo16), full(g_post), full(wup16), full(wg16)],
        out_specs=row(d_model),
        out_shape=jax.ShapeDtypeStruct((rows, d_model), F32),
        compiler_params=pltpu.CompilerParams(dimension_semantics=("arbitrary",), vmem_limit_bytes=VMEM_LIMIT),
        name="merge",
    )(x2d, oa, yn, ga, gb, p2d, wo16, g_post, wup16, wg16)


def _page_copy(pages_hbm, buf, sem, pt_ref, b, p, slot):
    return pltpu.make_async_copy(pages_hbm.at[pt_ref[b, p]], buf.at[slot, p], sem.at[slot])


def _start_batch_pages(streams, pt_ref, b, slot, n_pages):
    def body(p, c):
        for pages_hbm, buf, sem in streams:
            _page_copy(pages_hbm, buf, sem, pt_ref, b, p, slot).start()
        return c
    lax.fori_loop(0, n_pages, body, 0)


def _wait_batch_pages(streams, pt_ref, b, slot, n_pages):
    for pages_hbm, buf, sem in streams:
        for p in range(n_pages):
            _page_copy(pages_hbm, buf, sem, pt_ref, b, p, slot).wait()


def _sample_index_kernel(pt_ref, qi_ref, w_ref, pages_hbm, out_ref, buf, sem, *, n_pages, unroll):
    b = pl.program_id(0)
    slot = b % 2
    streams = [(pages_hbm, buf, sem)]

    @pl.when(b == 0)
    def _first():
        _start_batch_pages(streams, pt_ref, 0, 0, n_pages)

    _wait_batch_pages(streams, pt_ref, b, slot, n_pages)

    qi = qi_ref[0]
    w = w_ref[0]
    n_q = qi.shape[0] // N_IDX_HEADS

    def run(prefetch):
        def body(c, carry):
            for u in range(unroll):
                p = c * unroll + u
                if prefetch:
                    _page_copy(pages_hbm, buf, sem, pt_ref, b + 1, p, 1 - slot).start()
                s = _dot(qi, buf[slot, p].astype(BF16))
                r = jnp.maximum(s, 0.0) * w
                out_ref[0, c, :, u * PAGE_SIZE:(u + 1) * PAGE_SIZE] = jnp.sum(
                    r.reshape(n_q, N_IDX_HEADS, PAGE_SIZE), axis=1)
            return carry
        lax.fori_loop(0, n_pages // unroll, body, 0)

    has_next = b + 1 < pl.num_programs(0)
    pl.when(has_next)(lambda: run(True))
    pl.when(jnp.logical_not(has_next))(lambda: run(False))


def _sample_index(page_table, qi_qh, w_qh, kidx_t, unroll):
    batch, n_pages = page_table.shape
    n_q = qi_qh.shape[1] // N_IDX_HEADS
    groups = n_pages // unroll
    grid_spec = pltpu.PrefetchScalarGridSpec(
        num_scalar_prefetch=1,
        grid=(batch,),
        in_specs=[pl.BlockSpec((1,) + qi_qh.shape[1:], lambda b, pt: (b, 0, 0)),
                  pl.BlockSpec((1,) + w_qh.shape[1:], lambda b, pt: (b, 0, 0)),
                  pl.BlockSpec(memory_space=pl.ANY)],
        out_specs=pl.BlockSpec((1, groups, n_q, unroll * PAGE_SIZE), lambda b, pt: (b, 0, 0, 0)),
        scratch_shapes=[pltpu.VMEM((2, n_pages) + kidx_t.shape[1:], kidx_t.dtype),
                        pltpu.SemaphoreType.DMA((2,))],
    )
    return pl.pallas_call(
        functools.partial(_sample_index_kernel, n_pages=n_pages, unroll=unroll),
        grid_spec=grid_spec,
        out_shape=jax.ShapeDtypeStruct((batch, groups, n_q, unroll * PAGE_SIZE), F32),
        compiler_params=pltpu.CompilerParams(dimension_semantics=("arbitrary",), vmem_limit_bytes=VMEM_LIMIT),
        name="sample_index",
    )(page_table, qi_qh, w_qh, kidx_t)


def _sample_select_kernel(sp_ref, qi_ref, w_ref, kw16_ref, thr_ref, cut_ref, nmn_ref, snew_scr,
                          *, topk, n_q, lane_chunk):
    rows, past = sp_ref.shape
    r_i = lax.broadcasted_iota(I32, (rows, LANES), 0)
    l_i = lax.broadcasted_iota(I32, (rows, LANES), 1)

    s = _nt_dot(qi_ref[...], kw16_ref[...][:, :D_IDX])
    r = jnp.maximum(s, 0.0) * w_ref[...]
    s_new = jnp.sum(r.reshape(rows, N_IDX_HEADS, LANES), axis=1)
    ok = ((l_i // n_q) == (r_i // n_q)) & ((l_i % n_q) <= (r_i % n_q))
    snew_scr[...] = jnp.where(ok, s_new, -jnp.inf)

    n_chunks = past // lane_chunk
    tiles = lane_chunk // LANES
    kf = float(topk)

    def count(ind):
        acc = ind(snew_scr[...], l_i + past)
        for ch in range(n_chunks):
            x = sp_ref[:, ch * lane_chunk:(ch + 1) * lane_chunk]
            for t in range(tiles):
                idx = l_i + (ch * lane_chunk + t * LANES)
                acc = acc + ind(x[:, t * LANES:(t + 1) * LANES], idx)
        return jnp.broadcast_to(jnp.sum(acc, axis=1, keepdims=True), (rows, LANES))

    c0 = count(lambda x, idx: jnp.where(x >= 0.0, 1.0, 0.0))
    key0 = jnp.where(c0 >= kf, 0, INT_MIN).astype(I32)

    def bit_body(it, key):
        cand = key | jnp.left_shift(jnp.int32(1), 30 - it)
        cf = _key_to_float(cand)
        return jnp.where(count(lambda x, idx: jnp.where(x >= cf, 1.0, 0.0)) >= kf, cand, key)

    key = lax.fori_loop(0, 31, bit_body, key0)
    key = jnp.maximum(key, KEY_NEG_FLT_MAX)
    thr = _key_to_float(key)
    cnt_gt = count(lambda x, idx: jnp.where(x > thr, 1.0, 0.0))
    cnt_ge = count(lambda x, idx: jnp.where(x >= thr, 1.0, 0.0))
    need = kf - cnt_gt
    straddle = jnp.max(jnp.where(cnt_ge > kf, 1.0, 0.0))

    idx_bits = (past + LANES).bit_length()
    thr_ref[...] = thr
    cut_ref[...] = jnp.full((rows, LANES), 1 << idx_bits, I32)

    @pl.when(straddle > 0.0)
    def _tie_cut():
        def cut_body(it, cut):
            cand = cut | jnp.left_shift(jnp.int32(1), (idx_bits - 1) - it)
            cnt = count(lambda x, idx: jnp.where(x == thr, jnp.where(idx < cand, 1.0, 0.0), 0.0))
            return jnp.where(cnt <= need, cand, cut)
        cut_ref[...] = lax.fori_loop(0, idx_bits, cut_body, jnp.zeros((rows, LANES), I32))

    cut = cut_ref[...]
    sn = snew_scr[...]
    tie_keep = jnp.where(l_i + past < cut, 0.0, -jnp.inf)
    nmn_ref[...] = jnp.where(sn > thr, 0.0, jnp.where(sn == thr, tie_keep, -jnp.inf))


def _sample_select(scores_past, qi_rows, w_rows, kw16_s, topk, n_q):
    rows, past = scores_past.shape
    lane_chunk = 2048 if past % 2048 == 0 else LANES
    full = lambda a: pl.BlockSpec(a.shape, lambda i: (0,) * a.ndim)
    o = jax.ShapeDtypeStruct((rows, LANES), F32)
    return pl.pallas_call(
        functools.partial(_sample_select_kernel, topk=topk, n_q=n_q, lane_chunk=lane_chunk),
        grid=(1,),
        in_specs=[full(scores_past), full(qi_rows), full(w_rows), full(kw16_s)],
        out_specs=[pl.BlockSpec((rows, LANES), lambda i: (0, 0))] * 3,
        out_shape=[o, jax.ShapeDtypeStruct((rows, LANES), I32), o],
        scratch_shapes=[pltpu.VMEM((rows, LANES), F32)],
        compiler_params=pltpu.CompilerParams(dimension_semantics=("arbitrary",), vmem_limit_bytes=VMEM_LIMIT),
        name="sample_select",
    )(scores_past, qi_rows, w_rows, kw16_s)


def _sample_attn_kernel(pt_ref, rbc_ref, q_ref, sp_ref, thr_ref, cut_ref, nmn_ref, k16n_ref, v16n_ref,
                        k_hbm, v_hbm, o_ref, kbuf, vbuf, ksem, vsem, lgs, bias_scr, mx_scr,
                        *, n_pages, n_q, unroll):
    b = pl.program_id(0)
    slot = b % 2
    rows = n_q * N_HEADS_A
    groups = n_pages // unroll
    streams = [(k_hbm, kbuf, ksem), (v_hbm, vbuf, vsem)]
    lane = lax.broadcasted_iota(I32, (rows, LANES), 1)
    rq = lax.broadcasted_iota(I32, (rows, LANES), 0) // N_HEADS_A
    expand = lambda a: jnp.concatenate(
        [jnp.broadcast_to(a[t:t + 1], (N_HEADS_A, LANES)) for t in range(n_q)], axis=0)

    @pl.when(b == 0)
    def _first():
        _start_batch_pages(streams, pt_ref, 0, 0, n_pages)
        far = jnp.concatenate([rbc_ref[N_BUCKETS - 1]] * n_q, axis=0)
        for t, dist in enumerate((PAGE_SIZE + rq - lane, rq - lane % n_q)):
            bkt = _t5_bucket(dist)
            tile = jnp.zeros((rows, LANES), F32)
            for k in range(N_BUCKETS - 1):
                tile = jnp.where(bkt == k, jnp.concatenate([rbc_ref[k]] * n_q, axis=0) - far, tile)
            bias_scr[t] = tile * LOG2E

    _wait_batch_pages(streams, pt_ref, b, slot, n_pages)

    thr = expand(thr_ref[0])
    cut = expand(cut_ref[0])
    q = q_ref[0]

    def logits_pass(prefetch):
        def body(c, mx):
            for u in range(unroll):
                p = c * unroll + u
                if prefetch:
                    for pages_hbm, buf, sem in streams:
                        _page_copy(pages_hbm, buf, sem, pt_ref, b + 1, p, 1 - slot).start()
                lg = _dot(q, kbuf[slot, p].astype(BF16))
                sc = expand(sp_ref[0, c, :, u * PAGE_SIZE:(u + 1) * PAGE_SIZE])
                tie_keep = jnp.where(lane + p * PAGE_SIZE < cut, 0.0, -jnp.inf)
                x = lg + jnp.where(sc > thr, 0.0, jnp.where(sc == thr, tie_keep, -jnp.inf))
                if u == unroll - 1:
                    x = x + bias_scr[0] * jnp.where(c == groups - 1, 1.0, 0.0)
                lgs[p] = x
                mx = jnp.maximum(mx, x)
            return mx
        mx_scr[...] = lax.fori_loop(0, groups, body, jnp.full((rows, LANES), NEG_BIG, F32))

    has_next = b + 1 < pl.num_programs(0)
    pl.when(has_next)(lambda: logits_pass(True))
    pl.when(jnp.logical_not(has_next))(lambda: logits_pass(False))

    xn = _nt_dot(q, k16n_ref[...]) + bias_scr[1] + expand(nmn_ref[0])
    m = jnp.max(jnp.maximum(mx_scr[...], xn), axis=1, keepdims=True)

    def pv_body(c, carry):
        lsum, acc = carry
        for u in range(unroll):
            p = c * unroll + u
            pr = jnp.exp2(lgs[p] - m)
            lsum = lsum + pr
            acc = acc + _nt_dot(pr.astype(BF16), vbuf[slot, p].astype(BF16))
        return lsum, acc

    pn = jnp.exp2(xn - m)
    lsum, acc = lax.fori_loop(0, groups, pv_body, (pn, _dot(pn.astype(BF16), v16n_ref[...])))
    out = acc * (1.0 / jnp.sum(lsum, axis=1, keepdims=True))
    head = lax.broadcasted_iota(I32, (rows, HEAD_DIM), 0) % N_HEADS_A
    o_ref[0] = jnp.where(head < N_HEADS_A // N_KV_A, out[:, :HEAD_DIM], out[:, HEAD_DIM:])


def _sample_attn(page_table, rb_col, q_bd, scores4, thr3, cut3, nmn3, k16n, v16n, k_pages_t, v_pages_t, unroll):
    batch, n_pages = page_table.shape
    n_q = scores4.shape[2]
    rows = q_bd.shape[1]
    per_b = lambda a: pl.BlockSpec((1,) + a.shape[1:], lambda b, pt: (b,) + (0,) * (a.ndim - 1))
    full = lambda a: pl.BlockSpec(a.shape, lambda b, pt: (0,) * a.ndim)
    page_buf = pltpu.VMEM((2, n_pages) + k_pages_t.shape[1:], k_pages_t.dtype)
    grid_spec = pltpu.PrefetchScalarGridSpec(
        num_scalar_prefetch=1,
        grid=(batch,),
        in_specs=[full(rb_col), per_b(q_bd), per_b(scores4), per_b(thr3), per_b(cut3), per_b(nmn3),
                  full(k16n), full(v16n), pl.BlockSpec(memory_space=pl.ANY), pl.BlockSpec(memory_space=pl.ANY)],
        out_specs=pl.BlockSpec((1, rows, HEAD_DIM), lambda b, pt: (b, 0, 0)),
        scratch_shapes=[page_buf, page_buf, pltpu.SemaphoreType.DMA((2,)), pltpu.SemaphoreType.DMA((2,)),
                        pltpu.VMEM((n_pages, rows, LANES), F32), pltpu.VMEM((2, rows, LANES), F32),
                        pltpu.VMEM((rows, LANES), F32)],
    )
    return pl.pallas_call(
        functools.partial(_sample_attn_kernel, n_pages=n_pages, n_q=n_q, unroll=unroll),
        grid_spec=grid_spec,
        out_shape=jax.ShapeDtypeStruct((batch, rows, HEAD_DIM), F32),
        compiler_params=pltpu.CompilerParams(dimension_semantics=("arbitrary",),
                                             vmem_limit_bytes=VMEM_LIMIT_PAGED),
        name="sample_attn",
    )(page_table, rb_col, q_bd, scores4, thr3, cut3, nmn3, k16n, v16n, k_pages_t, v_pages_t)


def _rope_tables(pos):
    half = HEAD_DIM // 2
    inv = ROPE_BASE ** (-jnp.arange(half, dtype=F32) / half)
    ang = pos.astype(F32)[:, None] * inv[None, :]
    cos = jnp.cos(ang)
    sin = jnp.sin(ang)
    reps = LANES // HEAD_DIM
    cos_t = jnp.tile(jnp.concatenate([cos, cos], axis=1), (1, reps))
    sin_t = jnp.tile(jnp.concatenate([-sin, sin], axis=1), (1, reps))
    return cos_t, sin_t


def _cat_weight(w_in):
    split = int(np.sum(SPLIT_SIZES[:6]))
    pad = jnp.zeros((w_in.shape[0], LANES - D_IDX - N_IDX_HEADS), BF16)
    return jnp.concatenate([w_in[:, :split].astype(BF16), pad, w_in[:, split:].astype(BF16)], axis=1)


def _pages_t(cache):
    pool, page = cache.shape[:2]
    return jnp.transpose(cache, (0, 2, 3, 1)).reshape(pool, -1, page)


def _largest_divisor(n, cap):
    d = cap
    while n % d:
        d //= 2
    return d


def kernel(x_prompt, x_sample, cache_k, cache_v, cache_kidx, state_ret, page_table, p_prompt, p_sample,
           rel_bias, w_in, w_out, g_pre, g_post, w_ple_up, w_ple_gate):
    batch, seq, d_model = x_prompt.shape
    dec_b, dec_t, _ = x_sample.shape
    depth = w_in.shape[0]
    n_pages = page_table.shape[1]
    past = n_pages * PAGE_SIZE
    rows_s = dec_b * dec_t
    assert depth == 1 and rows_s == LANES and seq % BLK == 0

    w_cat = _cat_weight(w_in[0])
    gpre = g_pre[0].reshape(1, d_model)
    gpost = g_post[0].reshape(1, d_model)
    wo16 = w_out[0].astype(BF16)
    wup16 = w_ple_up[0].astype(BF16)
    wg16 = w_ple_gate[0].astype(BF16)

    tm = _largest_divisor(seq, 512)
    cos_p, sin_p = _rope_tables(jnp.arange(seq))
    xp2 = x_prompt.reshape(batch * seq, d_model)
    pp = _project(xp2, gpre, w_cat, cos_p, sin_p, tm, seq)
    oa_p = _dsa_prompt(pp, rel_bias, batch, seq)
    r3 = lambda a: a.reshape(batch, seq, D_B)
    s0_p = jnp.zeros((batch, N_HEADS_B, HEAD_DIM, HEAD_DIM), F32)
    yn_p, state_p = _retention(r3(pp["qb"]), r3(pp["kb"]), r3(pp["vb"]), s0_p, float(min(RET_CHUNK, seq)))
    y_prompt = _merge(xp2, oa_p, yn_p.reshape(batch * seq, D_B), pp["ga"], pp["gb"],
                      p_prompt[0].reshape(batch * seq, -1), wo16, gpost, wup16, wg16, tm)

    pos_s = past + jnp.arange(dec_t)
    cos_s, sin_s = _rope_tables(jnp.tile(pos_s, dec_b))
    xs2 = x_sample.reshape(rows_s, d_model)
    ps = _project(xs2, gpre, w_cat, cos_s, sin_s, rows_s, rows_s)
    topk_s = min(TOPK_MAX, (past + dec_t) // 4)

    qi_rows = ps["qi"].reshape(rows_s * N_IDX_HEADS, D_IDX)
    w_rows = jnp.broadcast_to(ps["kw"][:, D_IDX:D_IDX + N_IDX_HEADS].reshape(rows_s * N_IDX_HEADS, 1),
                              (rows_s * N_IDX_HEADS, LANES))
    unroll = _largest_divisor(n_pages, 32)
    scores4 = _sample_index(page_table, qi_rows.reshape(dec_b, dec_t * N_IDX_HEADS, D_IDX),
                            w_rows.reshape(dec_b, dec_t * N_IDX_HEADS, LANES),
                            jnp.transpose(cache_kidx[0], (0, 2, 1)), unroll)
    scores_past = scores4.transpose(0, 2, 1, 3).reshape(rows_s, past)
    thr, cut, nmn = _sample_select(scores_past, qi_rows, w_rows, ps["kw16"], topk_s, dec_t)

    kv_of_head = np.arange(N_HEADS_A) // (N_HEADS_A // N_KV_A)
    place = jnp.asarray((kv_of_head[:, None] == np.arange(N_KV_A)[None, :]).astype(np.float32)).astype(BF16)
    q_bd = (ps["qa"].reshape(dec_b, dec_t, N_HEADS_A, 1, HEAD_DIM) * place[None, None, :, :, None]).reshape(
        dec_b, dec_t * N_HEADS_A, N_KV_A * HEAD_DIM)
    rb_col = jnp.broadcast_to(rel_bias[:, :, None], rel_bias.shape + (LANES,))
    b3 = lambda a: a.reshape(dec_b, dec_t, LANES)
    o_qh = _sample_attn(page_table, rb_col, q_bd, scores4, b3(thr), b3(cut), b3(nmn),
                        ps["k16"], ps["v16"], _pages_t(cache_k[0]), _pages_t(cache_v[0]), unroll)
    oa_s = o_qh.reshape(rows_s, D_A)

    chunk_s = RET_CHUNK
    padt = lambda a: jnp.pad(a.reshape(dec_b, dec_t, D_B), ((0, 0), (0, chunk_s - dec_t), (0, 0)))
    yn_s, state_s = _retention(padt(ps["qb"]), padt(ps["kb"]), padt(ps["vb"]),
                               state_ret[0].astype(F32), float(math.gcd(dec_t, RET_CHUNK)))
    y_sample = _merge(xs2, oa_s, yn_s[:, :dec_t].reshape(rows_s, D_B), ps["ga"], ps["gb"],
                      p_sample[0].reshape(rows_s, -1), wo16, gpost, wup16, wg16, rows_s)

    def kv(a_t, b, t):
        a = a_t.reshape(a_t.shape[0], N_KV_A, HEAD_DIM, -1).transpose(0, 3, 1, 2)
        return a.reshape(1, b, t, N_KV_A, HEAD_DIM)

    def ki(a_t, b, t):
        return a_t.transpose(0, 2, 1).reshape(1, b, t, D_IDX)

    return (
        y_prompt.reshape(batch, seq, d_model),
        y_sample.reshape(dec_b, dec_t, d_model),
        kv(pp["ka_t"], batch, seq), kv(pp["va_t"], batch, seq),
        ki(pp["ki_t"], batch, seq),
        state_p[None].astype(state_ret.dtype),
        kv(ps["ka_t"], dec_b, dec_t), kv(ps["va_t"], dec_b, dec_t),
        ki(ps["ki_t"], dec_b, dec_t),
        state_s[None].astype(state_ret.dtype),
    )
```

```python
import functools
import math

import jax
import jax.numpy as jnp
import numpy as np
from jax import lax
from jax.experimental import pallas as pl
from jax.experimental.pallas import tpu as pltpu

F32 = jnp.float32
BF16 = jnp.bfloat16
I32 = jnp.int32

HEAD_DIM = 64
N_HEADS_A = 8
N_KV_A = 2
N_IDX_HEADS = 8
D_IDX = 64
TOPK_MAX = 256
N_HEADS_B = 8
D_A = N_HEADS_A * HEAD_DIM
D_B = N_HEADS_B * HEAD_DIM
N_BUCKETS = 32
MAX_DISTANCE = 128
ROPE_BASE = 10000.0
RET_CHUNK = 128
PAGE_SIZE = 128
EPS = 1e-6
SPLIT_SIZES = (D_A, N_KV_A * HEAD_DIM, N_KV_A * HEAD_DIM, N_IDX_HEADS * D_IDX, D_IDX, N_IDX_HEADS,
               D_A, D_B, D_B, D_B, D_B)

LANES = 128
SUBLANES = 8
BLK = 128
VMEM_LIMIT = 48 * 1024 * 1024
VMEM_LIMIT_PAGED = 56 * 1024 * 1024

INT_MIN = -(2 ** 31)
KEY_NEG_FLT_MAX = INT_MIN + 0x00800000
NEG_BIG = -1e30
LOG2E = math.log2(math.e)
ONES_ROWS = 16

_C_QA, _C_KA, _C_VA, _C_QI, _C_KW, _C_GA, _C_QB, _C_KB, _C_VB, _C_GB, _C_END = (
    0, 512, 640, 768, 1280, 1408, 1920, 2432, 2944, 3456, 3968)


def _nt_dot(a, b):
    return lax.dot_general(a, b, (((1,), (1,)), ((), ())), preferred_element_type=F32)


def _dot(a, b):
    return jnp.dot(a, b, preferred_element_type=F32)


def _sigmoid(x):
    return 1.0 / (1.0 + jnp.exp(-x))


def _key_to_float(k):
    bits = jnp.where(k >= 0, k, k ^ 0x7FFFFFFF)
    return lax.bitcast_convert_type(bits, F32)


def _t5_bucket(n):
    max_exact = N_BUCKETS // 2
    n = jnp.maximum(n, 0)
    nf = jnp.maximum(n, 1).astype(F32)
    large = max_exact + jnp.floor(jnp.log(nf / max_exact) / math.log(MAX_DISTANCE / max_exact)
                                  * (N_BUCKETS - max_exact)).astype(I32)
    large = jnp.minimum(large, N_BUCKETS - 1)
    return jnp.where(n < max_exact, n, large)


def _bias_from_bucket(bkt, rb_ref, h):
    out = jnp.zeros(bkt.shape, F32)
    for k in range(N_BUCKETS):
        out = jnp.where(bkt == k, rb_ref[k, h], out)
    return out


def _proj_kernel(x_ref, g_ref, w_ref, cos_ref, sin_ref,
                 qa_ref, kat_ref, vat_ref, k16_ref, v16_ref, qi_ref, ga_ref, gb_ref,
                 qb_ref, kb_ref, vb_ref, kw_ref, kw16_ref, kit_ref, h_scr):
    x = x_ref[...]
    ms = jnp.mean(x * x, axis=-1, keepdims=True)
    h_scr[...] = (x * lax.rsqrt(ms + EPS) * g_ref[...]).astype(BF16)

    def mm(lo, hi):
        return _dot(h_scr[...], w_ref[:, lo:hi])

    qa_ref[...] = (mm(_C_QA, _C_KA) * (HEAD_DIM ** -0.5 * LOG2E)).astype(BF16)
    ka = mm(_C_KA, _C_VA)
    kat_ref[0] = ka.T
    k16_ref[...] = ka.astype(BF16)
    va = mm(_C_VA, _C_QI)
    vat_ref[0] = va.T
    v16_ref[...] = va.astype(BF16)
    qi_ref[...] = mm(_C_QI, _C_KW).astype(BF16)
    ga_ref[...] = mm(_C_GA, _C_QB).astype(BF16)
    gb_ref[...] = mm(_C_GB, _C_END).astype(BF16)
    vb_ref[...] = mm(_C_VB, _C_GB).astype(BF16)

    cos = cos_ref[...]
    sin = sin_ref[...]
    lane = lax.broadcasted_iota(I32, cos.shape, 1)
    first_half = (lane % HEAD_DIM) < (HEAD_DIM // 2)

    def rope(z, scale):
        outs = []
        for g in range(z.shape[1] // LANES):
            zg = z[:, g * LANES:(g + 1) * LANES]
            partner = jnp.where(first_half, pltpu.roll(zg, LANES - HEAD_DIM // 2, 1),
                                pltpu.roll(zg, HEAD_DIM // 2, 1))
            r = zg * cos + partner * sin
            if scale != 1.0:
                r = r * scale
            outs.append(r.astype(BF16))
        return jnp.concatenate(outs, axis=1)

    qb_ref[...] = rope(mm(_C_QB, _C_KB), 1.0)
    kb_ref[...] = rope(mm(_C_KB, _C_VB), HEAD_DIM ** -0.5)

    kw = mm(_C_KW, _C_GA)
    wscale = (N_IDX_HEADS ** -0.5) * (D_IDX ** -0.5)
    kw = kw * jnp.where(lane >= D_IDX, wscale, 1.0)
    kw_ref[...] = kw
    kw16_ref[...] = kw.astype(BF16)
    kit_ref[0] = kw.T[:D_IDX]


def _project(x2d, g_pre, w_cat, cos_t, sin_t, tm, seq):
    rows = x2d.shape[0]
    d_model = x2d.shape[1]
    n_tab = cos_t.shape[0] // tm
    tiles = seq // tm
    row_spec = lambda w: pl.BlockSpec((tm, w), lambda i: (i, 0))
    t_spec = lambda w: pl.BlockSpec((1, w, tm), lambda i: (i // tiles, 0, i % tiles))
    outs = [
        ("qa", D_A, BF16), ("ka_t", LANES, F32), ("va_t", LANES, F32), ("k16", LANES, BF16), ("v16", LANES, BF16),
        ("qi", N_IDX_HEADS * D_IDX, BF16), ("ga", D_A, BF16), ("gb", D_B, BF16),
        ("qb", D_B, BF16), ("kb", D_B, BF16), ("vb", D_B, BF16), ("kw", LANES, F32), ("kw16", LANES, BF16),
        ("ki_t", D_IDX, F32),
    ]
    transposed = lambda n: n.endswith("_t")
    res = pl.pallas_call(
        _proj_kernel,
        grid=(rows // tm,),
        in_specs=[
            row_spec(d_model),
            pl.BlockSpec((1, d_model), lambda i: (0, 0)),
            pl.BlockSpec(w_cat.shape, lambda i: (0, 0), pipeline_mode=pl.Buffered(1)),
            pl.BlockSpec((tm, LANES), lambda i: (i % n_tab, 0)),
            pl.BlockSpec((tm, LANES), lambda i: (i % n_tab, 0)),
        ],
        out_specs=[t_spec(w) if transposed(n) else row_spec(w) for n, w, _ in outs],
        out_shape=[jax.ShapeDtypeStruct((rows // seq, w, seq) if transposed(n) else (rows, w), dt)
                   for n, w, dt in outs],
        scratch_shapes=[pltpu.VMEM((tm, d_model), BF16)],
        compiler_params=pltpu.CompilerParams(dimension_semantics=("arbitrary",), vmem_limit_bytes=VMEM_LIMIT),
        name="proj",
    )(x2d, g_pre, w_cat, cos_t, sin_t)
    return {n: r for (n, _, _), r in zip(outs, res)}


def _dsa_prompt_kernel(rb_ref, qi_ref, kwq_ref, qin_ref, kwqn_ref, kw16_ref, qa_ref, k16_ref, v16_ref, oa_ref,
                       sc2, bt, vt, lgs, acc_scr, cut_scr, key_scr, *, topk, search_blk):
    b = pl.program_id(0)
    j = pl.program_id(1)
    nblk = pl.num_programs(1)
    sc = sc2.at[j % 2]
    sc_next = sc2.at[1 - j % 2]
    row = lax.broadcasted_iota(I32, (BLK, BLK), 0)
    col = lax.broadcasted_iota(I32, (BLK, BLK), 1)

    @pl.when((b == 0) & (j == 0))
    def _init_bias():
        for d in range(2):
            bkt = _t5_bucket(col - row + d * BLK)
            for h in range(N_HEADS_A):
                bt[d, h] = (_bias_from_bucket(bkt, rb_ref, h) - rb_ref[N_BUCKETS - 1, h]) * LOG2E
        bt[2] = jnp.zeros(bt.shape[1:], F32)

    @pl.when(j == 0)
    def _clear_values():
        vt[...] = jnp.zeros(vt.shape, BF16)

    vblk = v16_ref[pl.ds(pl.multiple_of(j * BLK, BLK), BLK), :]
    vtj = vblk.astype(F32).T.astype(BF16)
    ones = jnp.ones((ONES_ROWS, BLK), BF16)
    vt[j] = jnp.concatenate([piece for n in range(N_KV_A)
                             for piece in (vtj[n * HEAD_DIM:(n + 1) * HEAD_DIM], ones)], axis=0)

    def indexer_queries(qi_blk_ref, kw_blk_ref):
        qi = qi_blk_ref[...]
        qi_stack = jnp.concatenate([qi[:, h * D_IDX:(h + 1) * D_IDX] for h in range(N_IDX_HEADS)], axis=0)
        return qi_stack, kw_blk_ref[...].T

    def score_group(c, dst, queries, first_q):
        qi_stack, wi_t = queries
        for u in range(search_blk):
            i = c * search_blk + u
            kib = kw16_ref[pl.ds(pl.multiple_of(i * BLK, BLK), BLK), :][:, :D_IDX]
            s = _nt_dot(kib, qi_stack)
            acc = jnp.zeros((BLK, BLK), F32)
            for h in range(N_IDX_HEADS):
                acc = acc + wi_t[D_IDX + h:D_IDX + h + 1, :] * jnp.maximum(s[:, h * BLK:(h + 1) * BLK], 0.0)
            dst[i] = jnp.where(row + i * BLK <= col + first_q, acc, -jnp.inf)

    @pl.when(j == 0)
    def _own_scores():
        score_group(0, sc, indexer_queries(qi_ref, kwq_ref), 0)

    n_chunk = (j + search_blk) // search_blk

    def col_sum(w):
        return jnp.sum(w.reshape(BLK // SUBLANES, SUBLANES, BLK), axis=0)

    qa = qa_ref[...]
    g_per_kv = N_HEADS_A // N_KV_A
    q_stack = [jnp.concatenate([qa[:, (n * g_per_kv + g) * HEAD_DIM:(n * g_per_kv + g + 1) * HEAD_DIM]
                                for g in range(g_per_kv)], axis=0) for n in range(N_KV_A)]

    kf = float(topk)
    n_units = search_blk * N_KV_A

    def search(n_groups):
        def count_ge(cand, unit=None):
            acc = jnp.zeros((SUBLANES, BLK), F32)
            for c in range(n_groups):
                for u in range(search_blk):
                    acc = acc + col_sum(jnp.where(sc[c * search_blk + u] >= cand, 1.0, 0.0))
                if unit is not None:
                    i = c * search_blk + unit[0]
                    n = unit[1]
                    lg = _nt_dot(k16_ref[i * BLK:(i + 1) * BLK, n * HEAD_DIM:(n + 1) * HEAD_DIM], q_stack[n])
                    for g in range(g_per_kv):
                        lgs[i, n * g_per_kv + g] = lg[:, g * BLK:(g + 1) * BLK]
            return jnp.sum(acc, axis=0, keepdims=True)

        def bit_step(bit, key, unit=None):
            cand = key | jnp.left_shift(jnp.int32(1), bit)
            cnt = count_ge(_key_to_float(cand), unit)
            return jnp.where(cnt >= kf, cand, key)

        c0 = count_ge(jnp.zeros((1, BLK), F32))
        key = jnp.where(c0 >= kf, 0, INT_MIN).astype(I32)
        for it in range(n_units):
            key = bit_step(30 - it, key, (it // N_KV_A, it % N_KV_A))
        n_rest = jnp.where((j + 1) * BLK <= topk, 0, 31 - n_units)
        key = lax.fori_loop(0, n_rest, lambda it, k: bit_step(30 - n_units - it, k), key)
        key_scr[...] = jnp.broadcast_to(key, key_scr.shape)

    for n_groups in range(1, sc.shape[0] // search_blk + 1):
        pl.when(n_chunk == n_groups)(functools.partial(search, n_groups))

    key = jnp.maximum(key_scr[0:1, :], KEY_NEG_FLT_MAX)
    thr = _key_to_float(key)

    def count2_body(c, accs):
        a_gt, a_ge = accs
        for u in range(search_blk):
            t = sc[c * search_blk + u]
            a_gt = a_gt + col_sum(jnp.where(t > thr, 1.0, 0.0))
            a_ge = a_ge + col_sum(jnp.where(t >= thr, 1.0, 0.0))
        return a_gt, a_ge

    z8 = jnp.zeros((SUBLANES, BLK), F32)
    a_gt, a_ge = lax.fori_loop(0, n_chunk, count2_body, (z8, z8))
    cnt_gt = jnp.sum(a_gt, axis=0, keepdims=True)
    cnt_ge = jnp.sum(a_ge, axis=0, keepdims=True)
    need = kf - cnt_gt
    straddle = jnp.max(jnp.where(cnt_ge > kf, 1.0, 0.0))

    n_keys_max = sc.shape[0] * BLK
    cut_scr[...] = jnp.full(cut_scr.shape, 2 * n_keys_max, I32)

    @pl.when(straddle > 0.0)
    def _tie_cut():
        def count_tie(cand):
            def body(i, acc):
                t = sc[i]
                hit = jnp.where(t == thr, jnp.where(row + i * BLK < cand, 1.0, 0.0), 0.0)
                return acc + col_sum(hit)
            acc = lax.fori_loop(0, j + 1, body, z8)
            return jnp.sum(acc, axis=0, keepdims=True)

        def cut_body(it, cut):
            cand = cut | jnp.left_shift(jnp.int32(1), (n_keys_max.bit_length() - 1) - it)
            return jnp.where(count_tie(cand) <= need, cand, cut)

        cut = lax.fori_loop(0, n_keys_max.bit_length(), cut_body, jnp.zeros((1, BLK), I32))
        cut_scr[...] = jnp.broadcast_to(cut, cut_scr.shape)

    cut = cut_scr[0:1, :]

    def col_max(w):
        return jnp.max(w.reshape(BLK // SUBLANES, SUBLANES, BLK), axis=0)

    def mask_body(c, m8, biased):
        m8 = list(m8)
        for u in range(search_blk):
            i = c * search_blk + u
            t = sc[i]
            tie_keep = jnp.where(row + i * BLK < cut, 0.0, -jnp.inf)
            nm = jnp.where(t > thr, 0.0, jnp.where(t == thr, tie_keep, -jnp.inf))
            near = jnp.clip(j - i, 0, 2)
            for h in range(N_HEADS_A):
                x = lgs[i, h] + nm
                if biased:
                    x = x + bt[near, h]
                lgs[i, h] = x
                m8[h] = jnp.maximum(m8[h], col_max(x))
        return tuple(m8)

    n_far = jnp.maximum(n_chunk - 2, 0)
    neg8 = jnp.full((SUBLANES, BLK), NEG_BIG, F32)
    m8 = lax.fori_loop(0, n_far, functools.partial(mask_body, biased=False), (neg8,) * N_HEADS_A)
    m8 = lax.fori_loop(n_far, n_chunk, functools.partial(mask_body, biased=True), m8)
    m_row = [jnp.max(m8[h], axis=0, keepdims=True) for h in range(N_HEADS_A)]

    acc_scr[...] = jnp.zeros(acc_scr.shape, F32)

    vrows = HEAD_DIM + ONES_ROWS

    next_queries = indexer_queries(qin_ref, kwqn_ref)
    next_first_q = (j + 1) * BLK

    def pv_body(c, carry):
        for u in range(search_blk):
            i = c * search_blk + u
            vti = vt[i]
            for n in range(N_KV_A):
                ps = [jnp.exp2(lgs[i, n * g_per_kv + g] - m_row[n * g_per_kv + g]).astype(BF16)
                      for g in range(g_per_kv)]
                acc_scr[n] += _dot(vti[n * vrows:(n + 1) * vrows, :], jnp.concatenate(ps, axis=1))
        score_group(c, sc_next, next_queries, next_first_q)
        return carry

    lax.fori_loop(0, n_chunk, pv_body, 0)

    @pl.when(((j + 1) % search_blk == 0) & (j + 1 < nblk))
    def _next_reaches_one_more_group():
        score_group(n_chunk, sc_next, next_queries, next_first_q)

    parts = []
    for n in range(N_KV_A):
        acc = acc_scr[n]
        out = acc[:HEAD_DIM] * (1.0 / acc[HEAD_DIM:HEAD_DIM + 1])
        parts += [out[:, g * BLK:(g + 1) * BLK] for g in range(g_per_kv)]
    oa_ref[...] = jnp.concatenate(parts, axis=0).T.astype(BF16)


def _dsa_prompt(p, rel_bias, batch, seq):
    nblk = seq // BLK
    topk = min(TOPK_MAX, seq // 4)
    search_blk = 4 if nblk % 4 == 0 else 1
    qrow = lambda w: pl.BlockSpec((BLK, w), lambda b, j: (b * nblk + j, 0))
    qnext = lambda w: pl.BlockSpec((BLK, w), lambda b, j: (b * nblk + jnp.minimum(j + 1, nblk - 1), 0))
    brow = lambda w: pl.BlockSpec((seq, w), lambda b, j: (b, 0))
    return pl.pallas_call(
        functools.partial(_dsa_prompt_kernel, topk=topk, search_blk=search_blk),
        grid=(batch, nblk),
        in_specs=[
            pl.BlockSpec(memory_space=pltpu.SMEM),
            qrow(N_IDX_HEADS * D_IDX), qrow(LANES), qnext(N_IDX_HEADS * D_IDX), qnext(LANES),
            brow(LANES), qrow(D_A), brow(LANES), brow(LANES),
        ],
        out_specs=qrow(D_A),
        out_shape=jax.ShapeDtypeStruct((batch * seq, D_A), BF16),
        scratch_shapes=[
            pltpu.VMEM((2, nblk, BLK, BLK), F32),
            pltpu.VMEM((3, N_HEADS_A, BLK, BLK), F32),
            pltpu.VMEM((nblk, N_KV_A * (HEAD_DIM + ONES_ROWS), BLK), BF16),
            pltpu.VMEM((nblk, N_HEADS_A, BLK, BLK), F32),
            pltpu.VMEM((N_KV_A, HEAD_DIM + ONES_ROWS, (N_HEADS_A // N_KV_A) * BLK), F32),
            pltpu.VMEM((SUBLANES, BLK), I32),
            pltpu.VMEM((SUBLANES, BLK), I32),
        ],
        compiler_params=pltpu.CompilerParams(dimension_semantics=("arbitrary", "arbitrary"),
                                             vmem_limit_bytes=VMEM_LIMIT),
        name="dsa_prompt",
    )(rel_bias, p["qi"], p["kw"], p["qi"], p["kw"], p["kw16"], p["qa"], p["k16"], p["v16"])


def _retention_kernel(q_ref, k_ref, v_ref, s0_ref, dmat_ref, din_ref, dout_ref, decm_ref, bdm_ref,
                      yn_ref, s_out_ref, s_scr, *, n_sub):
    c = pl.program_id(1)
    chunk = din_ref.shape[0]

    @pl.when(c == 0)
    def _load_state():
        z = jnp.zeros((HEAD_DIM, HEAD_DIM), F32)
        for p in range(N_HEADS_B // 2):
            pair = jnp.concatenate([jnp.concatenate([s0_ref[0, 2 * p], z], axis=1),
                                    jnp.concatenate([z, s0_ref[0, 2 * p + 1]], axis=1)], axis=0)
            s_scr[p] = pair.T

    lo = lax.broadcasted_iota(I32, (chunk, LANES), 1) < HEAD_DIM
    zero = jnp.zeros((), BF16)
    for p in range(N_HEADS_B // 2):
        sl = slice(p * LANES, (p + 1) * LANES)
        st = s_scr[p]
        for u in range(n_sub):
            rows = slice(u * chunk, (u + 1) * chunk)
            q, k, v = q_ref[0, rows, sl], k_ref[0, rows, sl], v_ref[0, rows, sl]
            qd = (q.astype(F32) * din_ref[:, sl]).astype(BF16)
            kd = (k.astype(F32) * dout_ref[:, sl]).astype(BF16)
            vt = v.astype(F32).T.astype(BF16)
            s_a = (_nt_dot(jnp.where(lo, k, zero), q) * dmat_ref[2 * p]).astype(BF16)
            s_b = (_nt_dot(jnp.where(lo, zero, k), q) * dmat_ref[2 * p + 1]).astype(BF16)
            intra = jnp.concatenate([_dot(vt[:HEAD_DIM], s_a), _dot(vt[HEAD_DIM:], s_b)], axis=0)
            o = (_nt_dot(st.astype(BF16), qd) + intra).reshape(2, HEAD_DIM, chunk)
            dlt = o - jnp.mean(o, axis=1, keepdims=True)
            var = jnp.mean(dlt * dlt, axis=1, keepdims=True)
            yn_ref[0, rows, sl] = (dlt * lax.rsqrt(var + EPS)).reshape(LANES, chunk).T.astype(BF16)
            st = st * decm_ref[p] + _dot(vt, kd) * bdm_ref[...]
        s_scr[p] = st
        pair = st.T
        s_out_ref[0, 2 * p] = pair[:HEAD_DIM, :HEAD_DIM]
        s_out_ref[0, 2 * p + 1] = pair[HEAD_DIM:, HEAD_DIM:]


def _ret_tables(chunk, n_real):
    h = N_HEADS_B
    log_g = jnp.log1p(-jnp.exp2(-5.0 - jnp.arange(h, dtype=F32)))
    i = jnp.arange(chunk, dtype=F32)
    diff = i[:, None] - i[None, :]
    dmat = jnp.where(diff >= 0, jnp.exp(jnp.maximum(diff, 0.0)[None] * log_g[:, None, None]), 0.0)
    dmat_t = jnp.swapaxes(dmat, 1, 2)
    dec_in = jnp.exp((i + 1.0)[:, None] * log_g[None, :])
    dec_out = jnp.exp((n_real - 1.0 - i)[:, None] * log_g[None, :])
    dec_chunk = jnp.exp(n_real * log_g)
    din = jnp.repeat(dec_in, HEAD_DIM, axis=1)
    dout = jnp.repeat(dec_out, HEAD_DIM, axis=1)
    head_of = np.arange(LANES) // HEAD_DIM
    bd = jnp.asarray((head_of[:, None] == head_of[None, :]).astype(np.float32))
    dc = jnp.repeat(dec_chunk, HEAD_DIM).reshape(h // 2, LANES)
    decm = dc[:, :, None] * bd[None]
    return dmat_t, din, dout, decm, bd


def _retention(q3, k3, v3, s0, n_real):
    batch, seq, _ = q3.shape
    chunk = min(RET_CHUNK, seq)
    n_sub = _largest_divisor(seq // chunk, 4)
    dmat_t, din, dout, decm, bd = _ret_tables(chunk, n_real)
    tok = pl.BlockSpec((1, n_sub * chunk, D_B), lambda b, c: (b, c, 0))
    st = pl.BlockSpec((1,) + s0.shape[1:], lambda b, c: (b, 0, 0, 0))
    full = lambda a: pl.BlockSpec(a.shape, lambda b, c: (0,) * a.ndim)
    return pl.pallas_call(
        functools.partial(_retention_kernel, n_sub=n_sub),
        grid=(batch, seq // (n_sub * chunk)),
        in_specs=[tok, tok, tok, st, full(dmat_t), full(din), full(dout), full(decm), full(bd)],
        out_specs=[tok, st],
        out_shape=[jax.ShapeDtypeStruct((batch, seq, D_B), BF16),
                   jax.ShapeDtypeStruct(s0.shape, F32)],
        scratch_shapes=[pltpu.VMEM((N_HEADS_B // 2, LANES, LANES), F32)],
        compiler_params=pltpu.CompilerParams(dimension_semantics=("arbitrary", "arbitrary"),
                                             vmem_limit_bytes=VMEM_LIMIT),
        name="retention",
    )(q3, k3, v3, s0, dmat_t, din, dout, decm, bd)


def _merge_kernel(x_ref, oa_ref, yn_ref, ga_ref, gb_ref, p_ref, wo_ref, gp_ref, wup_ref, wg_ref, out_ref):
    ga = ga_ref[...].astype(F32)
    gb = gb_ref[...].astype(F32)
    ya = (ga * _sigmoid(ga) * oa_ref[...].astype(F32)).astype(BF16)
    yb = (gb * _sigmoid(gb) * yn_ref[...].astype(F32)).astype(BF16)
    y = _dot(ya, wo_ref[0:D_A, :]) + _dot(yb, wo_ref[D_A:D_A + D_B, :])
    ms = jnp.mean(y * y, axis=-1, keepdims=True)
    x1 = x_ref[...] + y * lax.rsqrt(ms + EPS) * gp_ref[...]
    ple = _dot(p_ref[...].astype(BF16), wup_ref[...])
    gate = _sigmoid(_dot(x1.astype(BF16), wg_ref[...]))
    out_ref[...] = x1 + ple * gate


def _merge(x2d, oa, yn, ga, gb, p2d, wo16, g_post, wup16, wg16, tm):
    rows, d_model = x2d.shape
    row = lambda w: pl.BlockSpec((tm, w), lambda i: (i, 0))
    full = lambda a: pl.BlockSpec(a.shape, lambda i: (0, 0))
    return pl.pallas_call(
        _merge_kernel,
        grid=(rows // tm,),
        in_specs=[row(d_model), row(D_A), row(D_B), row(D_A), row(D_B), row(p2d.shape[1]),
                  full(wo16), full(g_post), full(wup16), full(wg16)],
        out_specs=row(d_model),
        out_shape=jax.ShapeDtypeStruct((rows, d_model), F32),
        compiler_params=pltpu.CompilerParams(dimension_semantics=("arbitrary",), vmem_limit_bytes=VMEM_LIMIT),
        name="merge",
    )(x2d, oa, yn, ga, gb, p2d, wo16, g_post, wup16, wg16)


def _page_copy(pages_hbm, buf, sem, pt_ref, b, p, slot):
    return pltpu.make_async_copy(pages_hbm.at[pt_ref[b, p]], buf.at[slot, p], sem.at[slot])


def _start_batch_pages(streams, pt_ref, b, slot, n_pages):
    def body(p, c):
        for pages_hbm, buf, sem in streams:
            _page_copy(pages_hbm, buf, sem, pt_ref, b, p, slot).start()
        return c
    lax.fori_loop(0, n_pages, body, 0)


def _wait_batch_pages(streams, pt_ref, b, slot, n_pages):
    for pages_hbm, buf, sem in streams:
        for p in range(n_pages):
            _page_copy(pages_hbm, buf, sem, pt_ref, b, p, slot).wait()


def _sample_index_kernel(pt_ref, qi_ref, w_ref, pages_hbm, out_ref, buf, sem, *, n_pages, unroll):
    b = pl.program_id(0)
    slot = b % 2
    streams = [(pages_hbm, buf, sem)]

    @pl.when(b == 0)
    def _first():
        _start_batch_pages(streams, pt_ref, 0, 0, n_pages)

    _wait_batch_pages(streams, pt_ref, b, slot, n_pages)

    qi = qi_ref[0]
    w = w_ref[0]
    n_q = qi.shape[0] // N_IDX_HEADS

    def run(prefetch):
        def body(c, carry):
            for u in range(unroll):
                p = c * unroll + u
                if prefetch:
                    _page_copy(pages_hbm, buf, sem, pt_ref, b + 1, p, 1 - slot).start()
                s = _dot(qi, buf[slot, p].astype(BF16))
                r = jnp.maximum(s, 0.0) * w
                out_ref[0, c, :, u * PAGE_SIZE:(u + 1) * PAGE_SIZE] = jnp.sum(
                    r.reshape(n_q, N_IDX_HEADS, PAGE_SIZE), axis=1)
            return carry
        lax.fori_loop(0, n_pages // unroll, body, 0)

    has_next = b + 1 < pl.num_programs(0)
    pl.when(has_next)(lambda: run(True))
    pl.when(jnp.logical_not(has_next))(lambda: run(False))


def _sample_index(page_table, qi_qh, w_qh, kidx_t, unroll):
    batch, n_pages = page_table.shape
    n_q = qi_qh.shape[1] // N_IDX_HEADS
    groups = n_pages // unroll
    grid_spec = pltpu.PrefetchScalarGridSpec(
        num_scalar_prefetch=1,
        grid=(batch,),
        in_specs=[pl.BlockSpec((1,) + qi_qh.shape[1:], lambda b, pt: (b, 0, 0)),
                  pl.BlockSpec((1,) + w_qh.shape[1:], lambda b, pt: (b, 0, 0)),
                  pl.BlockSpec(memory_space=pl.ANY)],
        out_specs=pl.BlockSpec((1, groups, n_q, unroll * PAGE_SIZE), lambda b, pt: (b, 0, 0, 0)),
        scratch_shapes=[pltpu.VMEM((2, n_pages) + kidx_t.shape[1:], kidx_t.dtype),
                        pltpu.SemaphoreType.DMA((2,))],
    )
    return pl.pallas_call(
        functools.partial(_sample_index_kernel, n_pages=n_pages, unroll=unroll),
        grid_spec=grid_spec,
        out_shape=jax.ShapeDtypeStruct((batch, groups, n_q, unroll * PAGE_SIZE), F32),
        compiler_params=pltpu.CompilerParams(dimension_semantics=("arbitrary",), vmem_limit_bytes=VMEM_LIMIT),
        name="sample_index",
    )(page_table, qi_qh, w_qh, kidx_t)


def _sample_select_kernel(sp_ref, qi_ref, w_ref, kw16_ref, thr_ref, cut_ref, nmn_ref, snew_scr,
                          *, topk, n_q, lane_chunk):
    rows, past = sp_ref.shape
    r_i = lax.broadcasted_iota(I32, (rows, LANES), 0)
    l_i = lax.broadcasted_iota(I32, (rows, LANES), 1)

    s = _nt_dot(qi_ref[...], kw16_ref[...][:, :D_IDX])
    r = jnp.maximum(s, 0.0) * w_ref[...]
    s_new = jnp.sum(r.reshape(rows, N_IDX_HEADS, LANES), axis=1)
    ok = ((l_i // n_q) == (r_i // n_q)) & ((l_i % n_q) <= (r_i % n_q))
    snew_scr[...] = jnp.where(ok, s_new, -jnp.inf)

    n_chunks = past // lane_chunk
    tiles = lane_chunk // LANES
    kf = float(topk)

    def count(ind):
        acc = ind(snew_scr[...], l_i + past)
        for ch in range(n_chunks):
            x = sp_ref[:, ch * lane_chunk:(ch + 1) * lane_chunk]
            for t in range(tiles):
                idx = l_i + (ch * lane_chunk + t * LANES)
                acc = acc + ind(x[:, t * LANES:(t + 1) * LANES], idx)
        return jnp.broadcast_to(jnp.sum(acc, axis=1, keepdims=True), (rows, LANES))

    c0 = count(lambda x, idx: jnp.where(x >= 0.0, 1.0, 0.0))
    key0 = jnp.where(c0 >= kf, 0, INT_MIN).astype(I32)

    def bit_body(it, key):
        cand = key | jnp.left_shift(jnp.int32(1), 30 - it)
        cf = _key_to_float(cand)
        return jnp.where(count(lambda x, idx: jnp.where(x >= cf, 1.0, 0.0)) >= kf, cand, key)

    key = lax.fori_loop(0, 31, bit_body, key0)
    key = jnp.maximum(key, KEY_NEG_FLT_MAX)
    thr = _key_to_float(key)
    cnt_gt = count(lambda x, idx: jnp.where(x > thr, 1.0, 0.0))
    cnt_ge = count(lambda x, idx: jnp.where(x >= thr, 1.0, 0.0))
    need = kf - cnt_gt
    straddle = jnp.max(jnp.where(cnt_ge > kf, 1.0, 0.0))

    idx_bits = (past + LANES).bit_length()
    thr_ref[...] = thr
    cut_ref[...] = jnp.full((rows, LANES), 1 << idx_bits, I32)

    @pl.when(straddle > 0.0)
    def _tie_cut():
        def cut_body(it, cut):
            cand = cut | jnp.left_shift(jnp.int32(1), (idx_bits - 1) - it)
            cnt = count(lambda x, idx: jnp.where(x == thr, jnp.where(idx < cand, 1.0, 0.0), 0.0))
            return jnp.where(cnt <= need, cand, cut)
        cut_ref[...] = lax.fori_loop(0, idx_bits, cut_body, jnp.zeros((rows, LANES), I32))

    cut = cut_ref[...]
    sn = snew_scr[...]
    tie_keep = jnp.where(l_i + past < cut, 0.0, -jnp.inf)
    nmn_ref[...] = jnp.where(sn > thr, 0.0, jnp.where(sn == thr, tie_keep, -jnp.inf))


def _sample_select(scores_past, qi_rows, w_rows, kw16_s, topk, n_q):
    rows, past = scores_past.shape
    lane_chunk = 2048 if past % 2048 == 0 else LANES
    full = lambda a: pl.BlockSpec(a.shape, lambda i: (0,) * a.ndim)
    o = jax.ShapeDtypeStruct((rows, LANES), F32)
    return pl.pallas_call(
        functools.partial(_sample_select_kernel, topk=topk, n_q=n_q, lane_chunk=lane_chunk),
        grid=(1,),
        in_specs=[full(scores_past), full(qi_rows), full(w_rows), full(kw16_s)],
        out_specs=[pl.BlockSpec((rows, LANES), lambda i: (0, 0))] * 3,
        out_shape=[o, jax.ShapeDtypeStruct((rows, LANES), I32), o],
        scratch_shapes=[pltpu.VMEM((rows, LANES), F32)],
        compiler_params=pltpu.CompilerParams(dimension_semantics=("arbitrary",), vmem_limit_bytes=VMEM_LIMIT),
        name="sample_select",
    )(scores_past, qi_rows, w_rows, kw16_s)


def _sample_attn_kernel(pt_ref, rbc_ref, q_ref, sp_ref, thr_ref, cut_ref, nmn_ref, k16n_ref, v16n_ref,
                        k_hbm, v_hbm, o_ref, kbuf, vbuf, ksem, vsem, lgs, bias_scr, mx_scr,
                        *, n_pages, n_q, unroll):
    b = pl.program_id(0)
    slot = b % 2
    rows = n_q * N_HEADS_A
    groups = n_pages // unroll
    streams = [(k_hbm, kbuf, ksem), (v_hbm, vbuf, vsem)]
    lane = lax.broadcasted_iota(I32, (rows, LANES), 1)
    rq = lax.broadcasted_iota(I32, (rows, LANES), 0) // N_HEADS_A
    expand = lambda a: jnp.concatenate(
        [jnp.broadcast_to(a[t:t + 1], (N_HEADS_A, LANES)) for t in range(n_q)], axis=0)

    @pl.when(b == 0)
    def _first():
        _start_batch_pages(streams, pt_ref, 0, 0, n_pages)
        far = jnp.concatenate([rbc_ref[N_BUCKETS - 1]] * n_q, axis=0)
        for t, dist in enumerate((PAGE_SIZE + rq - lane, rq - lane % n_q)):
            bkt = _t5_bucket(dist)
            tile = jnp.zeros((rows, LANES), F32)
            for k in range(N_BUCKETS - 1):
                tile = jnp.where(bkt == k, jnp.concatenate([rbc_ref[k]] * n_q, axis=0) - far, tile)
            bias_scr[t] = tile * LOG2E

    _wait_batch_pages(streams, pt_ref, b, slot, n_pages)

    thr = expand(thr_ref[0])
    cut = expand(cut_ref[0])
    q = q_ref[0]

    def logits_pass(prefetch):
        def body(c, mx):
            for u in range(unroll):
                p = c * unroll + u
                if prefetch:
                    for pages_hbm, buf, sem in streams:
                        _page_copy(pages_hbm, buf, sem, pt_ref, b + 1, p, 1 - slot).start()
                lg = _dot(q, kbuf[slot, p].astype(BF16))
                sc = expand(sp_ref[0, c, :, u * PAGE_SIZE:(u + 1) * PAGE_SIZE])
                tie_keep = jnp.where(lane + p * PAGE_SIZE < cut, 0.0, -jnp.inf)
                x = lg + jnp.where(sc > thr, 0.0, jnp.where(sc == thr, tie_keep, -jnp.inf))
                if u == unroll - 1:
                    x = x + bias_scr[0] * jnp.where(c == groups - 1, 1.0, 0.0)
                lgs[p] = x
                mx = jnp.maximum(mx, x)
            return mx
        mx_scr[...] = lax.fori_loop(0, groups, body, jnp.full((rows, LANES), NEG_BIG, F32))

    has_next = b + 1 < pl.num_programs(0)
    pl.when(has_next)(lambda: logits_pass(True))
    pl.when(jnp.logical_not(has_next))(lambda: logits_pass(False))

    xn = _nt_dot(q, k16n_ref[...]) + bias_scr[1] + expand(nmn_ref[0])
    m = jnp.max(jnp.maximum(mx_scr[...], xn), axis=1, keepdims=True)

    def pv_body(c, carry):
        lsum, acc = carry
        for u in range(unroll):
            p = c * unroll + u
            pr = jnp.exp2(lgs[p] - m)
            lsum = lsum + pr
            acc = acc + _nt_dot(pr.astype(BF16), vbuf[slot, p].astype(BF16))
        return lsum, acc

    pn = jnp.exp2(xn - m)
    lsum, acc = lax.fori_loop(0, groups, pv_body, (pn, _dot(pn.astype(BF16), v16n_ref[...])))
    out = acc * (1.0 / jnp.sum(lsum, axis=1, keepdims=True))
    head = lax.broadcasted_iota(I32, (rows, HEAD_DIM), 0) % N_HEADS_A
    o_ref[0] = jnp.where(head < N_HEADS_A // N_KV_A, out[:, :HEAD_DIM], out[:, HEAD_DIM:])


def _sample_attn(page_table, rb_col, q_bd, scores4, thr3, cut3, nmn3, k16n, v16n, k_pages_t, v_pages_t, unroll):
    batch, n_pages = page_table.shape
    n_q = scores4.shape[2]
    rows = q_bd.shape[1]
    per_b = lambda a: pl.BlockSpec((1,) + a.shape[1:], lambda b, pt: (b,) + (0,) * (a.ndim - 1))
    full = lambda a: pl.BlockSpec(a.shape, lambda b, pt: (0,) * a.ndim)
    page_buf = pltpu.VMEM((2, n_pages) + k_pages_t.shape[1:], k_pages_t.dtype)
    grid_spec = pltpu.PrefetchScalarGridSpec(
        num_scalar_prefetch=1,
        grid=(batch,),
        in_specs=[full(rb_col), per_b(q_bd), per_b(scores4), per_b(thr3), per_b(cut3), per_b(nmn3),
                  full(k16n), full(v16n), pl.BlockSpec(memory_space=pl.ANY), pl.BlockSpec(memory_space=pl.ANY)],
        out_specs=pl.BlockSpec((1, rows, HEAD_DIM), lambda b, pt: (b, 0, 0)),
        scratch_shapes=[page_buf, page_buf, pltpu.SemaphoreType.DMA((2,)), pltpu.SemaphoreType.DMA((2,)),
                        pltpu.VMEM((n_pages, rows, LANES), F32), pltpu.VMEM((2, rows, LANES), F32),
                        pltpu.VMEM((rows, LANES), F32)],
    )
    return pl.pallas_call(
        functools.partial(_sample_attn_kernel, n_pages=n_pages, n_q=n_q, unroll=unroll),
        grid_spec=grid_spec,
        out_shape=jax.ShapeDtypeStruct((batch, rows, HEAD_DIM), F32),
        compiler_params=pltpu.CompilerParams(dimension_semantics=("arbitrary",),
                                             vmem_limit_bytes=VMEM_LIMIT_PAGED),
        name="sample_attn",
    )(page_table, rb_col, q_bd, scores4, thr3, cut3, nmn3, k16n, v16n, k_pages_t, v_pages_t)


def _rope_tables(pos):
    half = HEAD_DIM // 2
    inv = ROPE_BASE ** (-jnp.arange(half, dtype=F32) / half)
    ang = pos.astype(F32)[:, None] * inv[None, :]
    cos = jnp.cos(ang)
    sin = jnp.sin(ang)
    reps = LANES // HEAD_DIM
    cos_t = jnp.tile(jnp.concatenate([cos, cos], axis=1), (1, reps))
    sin_t = jnp.tile(jnp.concatenate([-sin, sin], axis=1), (1, reps))
    return cos_t, sin_t


def _cat_weight(w_in):
    split = int(np.sum(SPLIT_SIZES[:6]))
    pad = jnp.zeros((w_in.shape[0], LANES - D_IDX - N_IDX_HEADS), BF16)
    return jnp.concatenate([w_in[:, :split].astype(BF16), pad, w_in[:, split:].astype(BF16)], axis=1)


def _pages_t(cache):
    pool, page = cache.shape[:2]
    return jnp.transpose(cache, (0, 2, 3, 1)).reshape(pool, -1, page)


def _largest_divisor(n, cap):
    d = cap
    while n % d:
        d //= 2
    return d


def kernel(x_prompt, x_sample, cache_k, cache_v, cache_kidx, state_ret, page_table, p_prompt, p_sample,
           rel_bias, w_in, w_out, g_pre, g_post, w_ple_up, w_ple_gate):
    batch, seq, d_model = x_prompt.shape
    dec_b, dec_t, _ = x_sample.shape
    depth = w_in.shape[0]
    n_pages = page_table.shape[1]
    past = n_pages * PAGE_SIZE
    rows_s = dec_b * dec_t
    assert depth == 1 and rows_s == LANES and seq % BLK == 0

    w_cat = _cat_weight(w_in[0])
    gpre = g_pre[0].reshape(1, d_model)
    gpost = g_post[0].reshape(1, d_model)
    wo16 = w_out[0].astype(BF16)
    wup16 = w_ple_up[0].astype(BF16)
    wg16 = w_ple_gate[0].astype(BF16)

    tm = _largest_divisor(seq, 512)
    cos_p, sin_p = _rope_tables(jnp.arange(seq))
    xp2 = x_prompt.reshape(batch * seq, d_model)
    pp = _project(xp2, gpre, w_cat, cos_p, sin_p, _largest_divisor(seq, 1024), seq)
    oa_p = _dsa_prompt(pp, rel_bias, batch, seq)
    r3 = lambda a: a.reshape(batch, seq, D_B)
    s0_p = jnp.zeros((batch, N_HEADS_B, HEAD_DIM, HEAD_DIM), F32)
    yn_p, state_p = _retention(r3(pp["qb"]), r3(pp["kb"]), r3(pp["vb"]), s0_p, float(min(RET_CHUNK, seq)))
    y_prompt = _merge(xp2, oa_p, yn_p.reshape(batch * seq, D_B), pp["ga"], pp["gb"],
                      p_prompt[0].reshape(batch * seq, -1), wo16, gpost, wup16, wg16, tm)

    pos_s = past + jnp.arange(dec_t)
    cos_s, sin_s = _rope_tables(jnp.tile(pos_s, dec_b))
    xs2 = x_sample.reshape(rows_s, d_model)
    ps = _project(xs2, gpre, w_cat, cos_s, sin_s, rows_s, rows_s)
    topk_s = min(TOPK_MAX, (past + dec_t) // 4)

    qi_rows = ps["qi"].reshape(rows_s * N_IDX_HEADS, D_IDX)
    w_rows = jnp.broadcast_to(ps["kw"][:, D_IDX:D_IDX + N_IDX_HEADS].reshape(rows_s * N_IDX_HEADS, 1),
                              (rows_s * N_IDX_HEADS, LANES))
    unroll = _largest_divisor(n_pages, 32)
    scores4 = _sample_index(page_table, qi_rows.reshape(dec_b, dec_t * N_IDX_HEADS, D_IDX),
                            w_rows.reshape(dec_b, dec_t * N_IDX_HEADS, LANES),
                            jnp.transpose(cache_kidx[0], (0, 2, 1)), unroll)
    scores_past = scores4.transpose(0, 2, 1, 3).reshape(rows_s, past)
    thr, cut, nmn = _sample_select(scores_past, qi_rows, w_rows, ps["kw16"], topk_s, dec_t)

    kv_of_head = np.arange(N_HEADS_A) // (N_HEADS_A // N_KV_A)
    place = jnp.asarray((kv_of_head[:, None] == np.arange(N_KV_A)[None, :]).astype(np.float32)).astype(BF16)
    q_bd = (ps["qa"].reshape(dec_b, dec_t, N_HEADS_A, 1, HEAD_DIM) * place[None, None, :, :, None]).reshape(
        dec_b, dec_t * N_HEADS_A, N_KV_A * HEAD_DIM)
    rb_col = jnp.broadcast_to(rel_bias[:, :, None], rel_bias.shape + (LANES,))
    b3 = lambda a: a.reshape(dec_b, dec_t, LANES)
    o_qh = _sample_attn(page_table, rb_col, q_bd, scores4, b3(thr), b3(cut), b3(nmn),
                        ps["k16"], ps["v16"], _pages_t(cache_k[0]), _pages_t(cache_v[0]), unroll)
    oa_s = o_qh.reshape(rows_s, D_A)

    chunk_s = RET_CHUNK
    padt = lambda a: jnp.pad(a.reshape(dec_b, dec_t, D_B), ((0, 0), (0, chunk_s - dec_t), (0, 0)))
    yn_s, state_s = _retention(padt(ps["qb"]), padt(ps["kb"]), padt(ps["vb"]),
                               state_ret[0].astype(F32), float(math.gcd(dec_t, RET_CHUNK)))
    y_sample = _merge(xs2, oa_s, yn_s[:, :dec_t].reshape(rows_s, D_B), ps["ga"], ps["gb"],
                      p_sample[0].reshape(rows_s, -1), wo16, gpost, wup16, wg16, rows_s)

    def kv(a_t, b, t):
        a = a_t.reshape(a_t.shape[0], N_KV_A, HEAD_DIM, -1).transpose(0, 3, 1, 2)
        return a.reshape(1, b, t, N_KV_A, HEAD_DIM)

    def ki(a_t, b, t):
        return a_t.transpose(0, 2, 1).reshape(1, b, t, D_IDX)

    return (
        y_prompt.reshape(batch, seq, d_model),
        y_sample.reshape(dec_b, dec_t, d_model),
        kv(pp["ka_t"], batch, seq), kv(pp["va_t"], batch, seq),
        ki(pp["ki_t"], batch, seq),
        state_p[None].astype(state_ret.dtype),
        kv(ps["ka_t"], dec_b, dec_t), kv(ps["va_t"], dec_b, dec_t),
        ki(ps["ki_t"], dec_b, dec_t),
        state_s[None].astype(state_ret.dtype),
    )
```

```python
import functools
import math

import jax
import jax.numpy as jnp
import numpy as np
from jax import lax
from jax.experimental import pallas as pl
from jax.experimental.pallas import tpu as pltpu

F32 = jnp.float32
BF16 = jnp.bfloat16
I32 = jnp.int32

HEAD_DIM = 64
N_HEADS_A = 8
N_KV_A = 2
N_IDX_HEADS = 8
D_IDX = 64
TOPK_MAX = 256
N_HEADS_B = 8
D_A = N_HEADS_A * HEAD_DIM
D_B = N_HEADS_B * HEAD_DIM
N_BUCKETS = 32
MAX_DISTANCE = 128
ROPE_BASE = 10000.0
RET_CHUNK = 128
PAGE_SIZE = 128
EPS = 1e-6
SPLIT_SIZES = (D_A, N_KV_A * HEAD_DIM, N_KV_A * HEAD_DIM, N_IDX_HEADS * D_IDX, D_IDX, N_IDX_HEADS,
               D_A, D_B, D_B, D_B, D_B)

LANES = 128
SUBLANES = 8
BLK = 128
VMEM_LIMIT = 48 * 1024 * 1024
VMEM_LIMIT_PAGED = 56 * 1024 * 1024

INT_MIN = -(2 ** 31)
KEY_NEG_FLT_MAX = INT_MIN + 0x00800000
NEG_BIG = -1e30
LOG2E = math.log2(math.e)
N_DMA_THREADS = 2
ONES_ROWS = 16

_C_QA, _C_KA, _C_VA, _C_QI, _C_KW, _C_GA, _C_QB, _C_KB, _C_VB, _C_GB, _C_END = (
    0, 512, 640, 768, 1280, 1408, 1920, 2432, 2944, 3456, 3968)


def _nt_dot(a, b):
    return lax.dot_general(a, b, (((1,), (1,)), ((), ())), preferred_element_type=F32)


def _dot(a, b):
    return jnp.dot(a, b, preferred_element_type=F32)


def _sigmoid(x):
    return 1.0 / (1.0 + jnp.exp(-x))


def _key_to_float(k):
    bits = jnp.where(k >= 0, k, k ^ 0x7FFFFFFF)
    return lax.bitcast_convert_type(bits, F32)


def _t5_bucket(n):
    max_exact = N_BUCKETS // 2
    n = jnp.maximum(n, 0)
    nf = jnp.maximum(n, 1).astype(F32)
    large = max_exact + jnp.floor(jnp.log(nf / max_exact) / math.log(MAX_DISTANCE / max_exact)
                                  * (N_BUCKETS - max_exact)).astype(I32)
    large = jnp.minimum(large, N_BUCKETS - 1)
    return jnp.where(n < max_exact, n, large)


def _bias_from_bucket(bkt, rb_ref, h):
    out = jnp.zeros(bkt.shape, F32)
    for k in range(N_BUCKETS):
        out = jnp.where(bkt == k, rb_ref[k, h], out)
    return out


def _proj_kernel(x_ref, g_ref, w_ref, cos_ref, sin_ref,
                 qa_ref, kat_ref, vat_ref, k16_ref, v16_ref, qi_ref, ga_ref, gb_ref,
                 qb_ref, kb_ref, vb_ref, kw_ref, kw16_ref, kit_ref, h_scr):
    x = x_ref[...]
    ms = jnp.mean(x * x, axis=-1, keepdims=True)
    h_scr[...] = (x * lax.rsqrt(ms + EPS) * g_ref[...]).astype(BF16)

    def mm(lo, hi):
        return _dot(h_scr[...], w_ref[:, lo:hi])

    qa_ref[...] = (mm(_C_QA, _C_KA) * (HEAD_DIM ** -0.5 * LOG2E)).astype(BF16)
    ka = mm(_C_KA, _C_VA)
    kat_ref[0] = ka.T
    k16_ref[...] = ka.astype(BF16)
    va = mm(_C_VA, _C_QI)
    vat_ref[0] = va.T
    v16_ref[...] = va.astype(BF16)
    qi_ref[...] = mm(_C_QI, _C_KW).astype(BF16)
    ga_ref[...] = mm(_C_GA, _C_QB).astype(BF16)
    gb_ref[...] = mm(_C_GB, _C_END).astype(BF16)
    vb_ref[...] = mm(_C_VB, _C_GB).astype(BF16)

    cos = cos_ref[...]
    sin = sin_ref[...]
    lane = lax.broadcasted_iota(I32, cos.shape, 1)
    first_half = (lane % HEAD_DIM) < (HEAD_DIM // 2)

    def rope(z, scale):
        outs = []
        for g in range(z.shape[1] // LANES):
            zg = z[:, g * LANES:(g + 1) * LANES]
            partner = jnp.where(first_half, pltpu.roll(zg, LANES - HEAD_DIM // 2, 1),
                                pltpu.roll(zg, HEAD_DIM // 2, 1))
            r = zg * cos + partner * sin
            if scale != 1.0:
                r = r * scale
            outs.append(r.astype(BF16))
        return jnp.concatenate(outs, axis=1)

    qb_ref[...] = rope(mm(_C_QB, _C_KB), 1.0)
    kb_ref[...] = rope(mm(_C_KB, _C_VB), HEAD_DIM ** -0.5)

    kw = mm(_C_KW, _C_GA)
    wscale = (N_IDX_HEADS ** -0.5) * (D_IDX ** -0.5)
    kw = kw * jnp.where(lane >= D_IDX, wscale, 1.0)
    kw_ref[...] = kw
    kw16_ref[...] = kw.astype(BF16)
    kit_ref[0] = kw.T[:D_IDX]


def _project(x2d, g_pre, w_cat, cos_t, sin_t, tm, seq):
    rows = x2d.shape[0]
    d_model = x2d.shape[1]
    n_tab = cos_t.shape[0] // tm
    tiles = seq // tm
    row_spec = lambda w: pl.BlockSpec((tm, w), lambda i: (i, 0))
    t_spec = lambda w: pl.BlockSpec((1, w, tm), lambda i: (i // tiles, 0, i % tiles))
    outs = [
        ("qa", D_A, BF16), ("ka_t", LANES, F32), ("va_t", LANES, F32), ("k16", LANES, BF16), ("v16", LANES, BF16),
        ("qi", N_IDX_HEADS * D_IDX, BF16), ("ga", D_A, BF16), ("gb", D_B, BF16),
        ("qb", D_B, BF16), ("kb", D_B, BF16), ("vb", D_B, BF16), ("kw", LANES, F32), ("kw16", LANES, BF16),
        ("ki_t", D_IDX, F32),
    ]
    transposed = lambda n: n.endswith("_t")
    res = pl.pallas_call(
        _proj_kernel,
        grid=(rows // tm,),
        in_specs=[
            row_spec(d_model),
            pl.BlockSpec((1, d_model), lambda i: (0, 0)),
            pl.BlockSpec(w_cat.shape, lambda i: (0, 0), pipeline_mode=pl.Buffered(1)),
            pl.BlockSpec((tm, LANES), lambda i: (i % n_tab, 0)),
            pl.BlockSpec((tm, LANES), lambda i: (i % n_tab, 0)),
        ],
        out_specs=[t_spec(w) if transposed(n) else row_spec(w) for n, w, _ in outs],
        out_shape=[jax.ShapeDtypeStruct((rows // seq, w, seq) if transposed(n) else (rows, w), dt)
                   for n, w, dt in outs],
        scratch_shapes=[pltpu.VMEM((tm, d_model), BF16)],
        compiler_params=pltpu.CompilerParams(dimension_semantics=("arbitrary",), vmem_limit_bytes=VMEM_LIMIT),
        name="proj",
    )(x2d, g_pre, w_cat, cos_t, sin_t)
    return {n: r for (n, _, _), r in zip(outs, res)}


def _dsa_prompt_kernel(rb_ref, qi_ref, kwq_ref, qin_ref, kwqn_ref, kw16_ref, qa_ref, k16_ref, v16_ref, oa_ref,
                       sc2, bt, vt, lgs, acc_scr, cut_scr, key_scr, *, topk, search_blk):
    b = pl.program_id(0)
    j = pl.program_id(1)
    nblk = pl.num_programs(1)
    sc = sc2.at[j % 2]
    sc_next = sc2.at[1 - j % 2]
    row = lax.broadcasted_iota(I32, (BLK, BLK), 0)
    col = lax.broadcasted_iota(I32, (BLK, BLK), 1)

    @pl.when((b == 0) & (j == 0))
    def _init_bias():
        for d in range(2):
            bkt = _t5_bucket(col - row + d * BLK)
            for h in range(N_HEADS_A):
                bt[d, h] = (_bias_from_bucket(bkt, rb_ref, h) - rb_ref[N_BUCKETS - 1, h]) * LOG2E
        bt[2] = jnp.zeros(bt.shape[1:], F32)

    @pl.when(j == 0)
    def _clear_values():
        vt[...] = jnp.zeros(vt.shape, BF16)

    vblk = v16_ref[pl.ds(pl.multiple_of(j * BLK, BLK), BLK), :]
    vtj = vblk.astype(F32).T.astype(BF16)
    ones = jnp.ones((ONES_ROWS, BLK), BF16)
    vt[j] = jnp.concatenate([piece for n in range(N_KV_A)
                             for piece in (vtj[n * HEAD_DIM:(n + 1) * HEAD_DIM], ones)], axis=0)

    def indexer_queries(qi_blk_ref, kw_blk_ref):
        qi = qi_blk_ref[...]
        qi_stack = jnp.concatenate([qi[:, h * D_IDX:(h + 1) * D_IDX] for h in range(N_IDX_HEADS)], axis=0)
        return qi_stack, kw_blk_ref[...].T

    def score_group(c, dst, queries, first_q):
        qi_stack, wi_t = queries
        for u in range(search_blk):
            i = c * search_blk + u
            kib = kw16_ref[pl.ds(pl.multiple_of(i * BLK, BLK), BLK), :][:, :D_IDX]
            s = _nt_dot(kib, qi_stack)
            acc = jnp.zeros((BLK, BLK), F32)
            for h in range(N_IDX_HEADS):
                acc = acc + wi_t[D_IDX + h:D_IDX + h + 1, :] * jnp.maximum(s[:, h * BLK:(h + 1) * BLK], 0.0)
            dst[i] = jnp.where(row + i * BLK <= col + first_q, acc, -jnp.inf)

    @pl.when(j == 0)
    def _own_scores():
        score_group(0, sc, indexer_queries(qi_ref, kwq_ref), 0)

    n_chunk = (j + search_blk) // search_blk

    def col_sum(w):
        return jnp.sum(w.reshape(BLK // SUBLANES, SUBLANES, BLK), axis=0)

    qa = qa_ref[...]
    g_per_kv = N_HEADS_A // N_KV_A
    q_stack = [jnp.concatenate([qa[:, (n * g_per_kv + g) * HEAD_DIM:(n * g_per_kv + g + 1) * HEAD_DIM]
                                for g in range(g_per_kv)], axis=0) for n in range(N_KV_A)]

    kf = float(topk)
    n_units = search_blk * N_KV_A

    def search(n_groups):
        def count_ge(cand, unit=None):
            acc = jnp.zeros((SUBLANES, BLK), F32)
            for c in range(n_groups):
                for u in range(search_blk):
                    acc = acc + col_sum(jnp.where(sc[c * search_blk + u] >= cand, 1.0, 0.0))
                if unit is not None:
                    i = c * search_blk + unit[0]
                    n = unit[1]
                    lg = _nt_dot(k16_ref[i * BLK:(i + 1) * BLK, n * HEAD_DIM:(n + 1) * HEAD_DIM], q_stack[n])
                    for g in range(g_per_kv):
                        lgs[i, n * g_per_kv + g] = lg[:, g * BLK:(g + 1) * BLK]
            return jnp.sum(acc, axis=0, keepdims=True)

        def bit_step(bit, key, unit=None):
            cand = key | jnp.left_shift(jnp.int32(1), bit)
            cnt = count_ge(_key_to_float(cand), unit)
            return jnp.where(cnt >= kf, cand, key)

        c0 = count_ge(jnp.zeros((1, BLK), F32))
        key = jnp.where(c0 >= kf, 0, INT_MIN).astype(I32)
        for it in range(n_units):
            key = bit_step(30 - it, key, (it // N_KV_A, it % N_KV_A))
        n_rest = jnp.where((j + 1) * BLK <= topk, 0, 31 - n_units)
        key = lax.fori_loop(0, n_rest, lambda it, k: bit_step(30 - n_units - it, k), key)
        key_scr[...] = jnp.broadcast_to(key, key_scr.shape)

    for n_groups in range(1, sc.shape[0] // search_blk + 1):
        pl.when(n_chunk == n_groups)(functools.partial(search, n_groups))

    key = jnp.maximum(key_scr[0:1, :], KEY_NEG_FLT_MAX)
    thr = _key_to_float(key)

    def count2_body(c, accs):
        a_gt, a_ge = accs
        for u in range(search_blk):
            t = sc[c * search_blk + u]
            a_gt = a_gt + col_sum(jnp.where(t > thr, 1.0, 0.0))
            a_ge = a_ge + col_sum(jnp.where(t >= thr, 1.0, 0.0))
        return a_gt, a_ge

    z8 = jnp.zeros((SUBLANES, BLK), F32)
    a_gt, a_ge = lax.fori_loop(0, n_chunk, count2_body, (z8, z8))
    cnt_gt = jnp.sum(a_gt, axis=0, keepdims=True)
    cnt_ge = jnp.sum(a_ge, axis=0, keepdims=True)
    need = kf - cnt_gt
    straddle = jnp.max(jnp.where(cnt_ge > kf, 1.0, 0.0))

    n_keys_max = sc.shape[0] * BLK
    cut_scr[...] = jnp.full(cut_scr.shape, 2 * n_keys_max, I32)

    @pl.when(straddle > 0.0)
    def _tie_cut():
        def count_tie(cand):
            def body(i, acc):
                t = sc[i]
                hit = jnp.where(t == thr, jnp.where(row + i * BLK < cand, 1.0, 0.0), 0.0)
                return acc + col_sum(hit)
            acc = lax.fori_loop(0, j + 1, body, z8)
            return jnp.sum(acc, axis=0, keepdims=True)

        def cut_body(it, cut):
            cand = cut | jnp.left_shift(jnp.int32(1), (n_keys_max.bit_length() - 1) - it)
            return jnp.where(count_tie(cand) <= need, cand, cut)

        cut = lax.fori_loop(0, n_keys_max.bit_length(), cut_body, jnp.zeros((1, BLK), I32))
        cut_scr[...] = jnp.broadcast_to(cut, cut_scr.shape)

    cut = cut_scr[0:1, :]

    def col_max(w):
        return jnp.max(w.reshape(BLK // SUBLANES, SUBLANES, BLK), axis=0)

    def mask_body(c, m8, biased):
        m8 = list(m8)
        for u in range(search_blk):
            i = c * search_blk + u
            t = sc[i]
            tie_keep = jnp.where(row + i * BLK < cut, 0.0, -jnp.inf)
            nm = jnp.where(t > thr, 0.0, jnp.where(t == thr, tie_keep, -jnp.inf))
            near = jnp.clip(j - i, 0, 2)
            for h in range(N_HEADS_A):
                x = lgs[i, h] + nm
                if biased:
                    x = x + bt[near, h]
                lgs[i, h] = x
                m8[h] = jnp.maximum(m8[h], col_max(x))
        return tuple(m8)

    n_far = jnp.maximum(n_chunk - 2, 0)
    neg8 = jnp.full((SUBLANES, BLK), NEG_BIG, F32)
    m8 = lax.fori_loop(0, n_far, functools.partial(mask_body, biased=False), (neg8,) * N_HEADS_A)
    m8 = lax.fori_loop(n_far, n_chunk, functools.partial(mask_body, biased=True), m8)
    m_row = [jnp.max(m8[h], axis=0, keepdims=True) for h in range(N_HEADS_A)]

    acc_scr[...] = jnp.zeros(acc_scr.shape, F32)

    vrows = HEAD_DIM + ONES_ROWS

    next_queries = indexer_queries(qin_ref, kwqn_ref)
    next_first_q = (j + 1) * BLK

    def pv_body(c, carry):
        for u in range(search_blk):
            i = c * search_blk + u
            vti = vt[i]
            for n in range(N_KV_A):
                ps = [jnp.exp2(lgs[i, n * g_per_kv + g] - m_row[n * g_per_kv + g]).astype(BF16)
                      for g in range(g_per_kv)]
                acc_scr[n] += _dot(vti[n * vrows:(n + 1) * vrows, :], jnp.concatenate(ps, axis=1))
        score_group(c, sc_next, next_queries, next_first_q)
        return carry

    lax.fori_loop(0, n_chunk, pv_body, 0)

    @pl.when(((j + 1) % search_blk == 0) & (j + 1 < nblk))
    def _next_reaches_one_more_group():
        score_group(n_chunk, sc_next, next_queries, next_first_q)

    parts = []
    for n in range(N_KV_A):
        acc = acc_scr[n]
        out = acc[:HEAD_DIM] * (1.0 / acc[HEAD_DIM:HEAD_DIM + 1])
        parts += [out[:, g * BLK:(g + 1) * BLK] for g in range(g_per_kv)]
    oa_ref[...] = jnp.concatenate(parts, axis=0).T.astype(BF16)


def _dsa_prompt(p, rel_bias, batch, seq):
    nblk = seq // BLK
    topk = min(TOPK_MAX, seq // 4)
    search_blk = 4 if nblk % 4 == 0 else 1
    qrow = lambda w: pl.BlockSpec((BLK, w), lambda b, j: (b * nblk + j, 0))
    qnext = lambda w: pl.BlockSpec((BLK, w), lambda b, j: (b * nblk + jnp.minimum(j + 1, nblk - 1), 0))
    brow = lambda w: pl.BlockSpec((seq, w), lambda b, j: (b, 0))
    return pl.pallas_call(
        functools.partial(_dsa_prompt_kernel, topk=topk, search_blk=search_blk),
        grid=(batch, nblk),
        in_specs=[
            pl.BlockSpec(memory_space=pltpu.SMEM),
            qrow(N_IDX_HEADS * D_IDX), qrow(LANES), qnext(N_IDX_HEADS * D_IDX), qnext(LANES),
            brow(LANES), qrow(D_A), brow(LANES), brow(LANES),
        ],
        out_specs=qrow(D_A),
        out_shape=jax.ShapeDtypeStruct((batch * seq, D_A), BF16),
        scratch_shapes=[
            pltpu.VMEM((2, nblk, BLK, BLK), F32),
            pltpu.VMEM((3, N_HEADS_A, BLK, BLK), F32),
            pltpu.VMEM((nblk, N_KV_A * (HEAD_DIM + ONES_ROWS), BLK), BF16),
            pltpu.VMEM((nblk, N_HEADS_A, BLK, BLK), F32),
            pltpu.VMEM((N_KV_A, HEAD_DIM + ONES_ROWS, (N_HEADS_A // N_KV_A) * BLK), F32),
            pltpu.VMEM((SUBLANES, BLK), I32),
            pltpu.VMEM((SUBLANES, BLK), I32),
        ],
        compiler_params=pltpu.CompilerParams(dimension_semantics=("arbitrary", "arbitrary"),
                                             vmem_limit_bytes=VMEM_LIMIT),
        name="dsa_prompt",
    )(rel_bias, p["qi"], p["kw"], p["qi"], p["kw"], p["kw16"], p["qa"], p["k16"], p["v16"])


def _retention_kernel(q_ref, k_ref, v_ref, s0_ref, dmat_ref, din_ref, dout_ref, decm_ref, bdm_ref,
                      yn_ref, s_out_ref, s_scr, *, n_sub):
    c = pl.program_id(1)
    chunk = din_ref.shape[0]

    @pl.when(c == 0)
    def _load_state():
        z = jnp.zeros((HEAD_DIM, HEAD_DIM), F32)
        for p in range(N_HEADS_B // 2):
            pair = jnp.concatenate([jnp.concatenate([s0_ref[0, 2 * p], z], axis=1),
                                    jnp.concatenate([z, s0_ref[0, 2 * p + 1]], axis=1)], axis=0)
            s_scr[p] = pair.T

    lo = lax.broadcasted_iota(I32, (chunk, LANES), 1) < HEAD_DIM
    zero = jnp.zeros((), BF16)
    for p in range(N_HEADS_B // 2):
        sl = slice(p * LANES, (p + 1) * LANES)
        st = s_scr[p]
        for u in range(n_sub):
            rows = slice(u * chunk, (u + 1) * chunk)
            q, k, v = q_ref[0, rows, sl], k_ref[0, rows, sl], v_ref[0, rows, sl]
            qd = (q.astype(F32) * din_ref[:, sl]).astype(BF16)
            kd = (k.astype(F32) * dout_ref[:, sl]).astype(BF16)
            vt = v.astype(F32).T.astype(BF16)
            s_a = (_nt_dot(jnp.where(lo, k, zero), q) * dmat_ref[2 * p]).astype(BF16)
            s_b = (_nt_dot(jnp.where(lo, zero, k), q) * dmat_ref[2 * p + 1]).astype(BF16)
            intra = jnp.concatenate([_dot(vt[:HEAD_DIM], s_a), _dot(vt[HEAD_DIM:], s_b)], axis=0)
            o = (_nt_dot(st.astype(BF16), qd) + intra).reshape(2, HEAD_DIM, chunk)
            dlt = o - jnp.mean(o, axis=1, keepdims=True)
            var = jnp.mean(dlt * dlt, axis=1, keepdims=True)
            yn_ref[0, rows, sl] = (dlt * lax.rsqrt(var + EPS)).reshape(LANES, chunk).T.astype(BF16)
            st = st * decm_ref[p] + _dot(vt, kd) * bdm_ref[...]
        s_scr[p] = st
        pair = st.T
        s_out_ref[0, 2 * p] = pair[:HEAD_DIM, :HEAD_DIM]
        s_out_ref[0, 2 * p + 1] = pair[HEAD_DIM:, HEAD_DIM:]


def _ret_tables(chunk, n_real):
    h = N_HEADS_B
    log_g = jnp.log1p(-jnp.exp2(-5.0 - jnp.arange(h, dtype=F32)))
    i = jnp.arange(chunk, dtype=F32)
    diff = i[:, None] - i[None, :]
    dmat = jnp.where(diff >= 0, jnp.exp(jnp.maximum(diff, 0.0)[None] * log_g[:, None, None]), 0.0)
    dmat_t = jnp.swapaxes(dmat, 1, 2)
    dec_in = jnp.exp((i + 1.0)[:, None] * log_g[None, :])
    dec_out = jnp.exp((n_real - 1.0 - i)[:, None] * log_g[None, :])
    dec_chunk = jnp.exp(n_real * log_g)
    din = jnp.repeat(dec_in, HEAD_DIM, axis=1)
    dout = jnp.repeat(dec_out, HEAD_DIM, axis=1)
    head_of = np.arange(LANES) // HEAD_DIM
    bd = jnp.asarray((head_of[:, None] == head_of[None, :]).astype(np.float32))
    dc = jnp.repeat(dec_chunk, HEAD_DIM).reshape(h // 2, LANES)
    decm = dc[:, :, None] * bd[None]
    return dmat_t, din, dout, decm, bd


def _retention(q3, k3, v3, s0, n_real):
    batch, seq, _ = q3.shape
    chunk = min(RET_CHUNK, seq)
    n_sub = _largest_divisor(seq // chunk, 4)
    dmat_t, din, dout, decm, bd = _ret_tables(chunk, n_real)
    tok = pl.BlockSpec((1, n_sub * chunk, D_B), lambda b, c: (b, c, 0))
    st = pl.BlockSpec((1,) + s0.shape[1:], lambda b, c: (b, 0, 0, 0))
    full = lambda a: pl.BlockSpec(a.shape, lambda b, c: (0,) * a.ndim)
    return pl.pallas_call(
        functools.partial(_retention_kernel, n_sub=n_sub),
        grid=(batch, seq // (n_sub * chunk)),
        in_specs=[tok, tok, tok, st, full(dmat_t), full(din), full(dout), full(decm), full(bd)],
        out_specs=[tok, st],
        out_shape=[jax.ShapeDtypeStruct((batch, seq, D_B), BF16),
                   jax.ShapeDtypeStruct(s0.shape, F32)],
        scratch_shapes=[pltpu.VMEM((N_HEADS_B // 2, LANES, LANES), F32)],
        compiler_params=pltpu.CompilerParams(dimension_semantics=("arbitrary", "arbitrary"),
                                             vmem_limit_bytes=VMEM_LIMIT),
        name="retention",
    )(q3, k3, v3, s0, dmat_t, din, dout, decm, bd)


def _merge_kernel(x_ref, oa_ref, yn_ref, ga_ref, gb_ref, p_ref, wo_ref, gp_ref, wup_ref, wg_ref, out_ref):
    ga = ga_ref[...].astype(F32)
    gb = gb_ref[...].astype(F32)
    ya = (ga * _sigmoid(ga) * oa_ref[...].astype(F32)).astype(BF16)
    yb = (gb * _sigmoid(gb) * yn_ref[...].astype(F32)).astype(BF16)
    y = _dot(ya, wo_ref[0:D_A, :]) + _dot(yb, wo_ref[D_A:D_A + D_B, :])
    ms = jnp.mean(y * y, axis=-1, keepdims=True)
    x1 = x_ref[...] + y * lax.rsqrt(ms + EPS) * gp_ref[...]
    ple = _dot(p_ref[...].astype(BF16), wup_ref[...])
    gate = _sigmoid(_dot(x1.astype(BF16), wg_ref[...]))
    out_ref[...] = x1 + ple * gate


def _merge(x2d, oa, yn, ga, gb, p2d, wo16, g_post, wup16, wg16, tm):
    rows, d_model = x2d.shape
    row = lambda w: pl.BlockSpec((tm, w), lambda i: (i, 0))
    full = lambda a: pl.BlockSpec(a.shape, lambda i: (0, 0))
    return pl.pallas_call(
        _merge_kernel,
        grid=(rows // tm,),
        in_specs=[row(d_model), row(D_A), row(D_B), row(D_A), row(D_B), row(p2d.shape[1]),
                  full(wo16), full(g_post), full(wup16), full(wg16)],
        out_specs=row(d_model),
        out_shape=jax.ShapeDtypeStruct((rows, d_model), F32),
        compiler_params=pltpu.CompilerParams(dimension_semantics=("arbitrary",), vmem_limit_bytes=VMEM_LIMIT),
        name="merge",
    )(x2d, oa, yn, ga, gb, p2d, wo16, g_post, wup16, wg16)


def _page_copy(pages_hbm, buf, sem, pt_ref, b, p, slot):
    return pltpu.make_async_copy(pages_hbm.at[pt_ref[b, p]], buf.at[slot, p], sem.at[slot])


def _start_batch_pages(streams, pt_ref, b, slot, n_pages):
    def body(p, c):
        for s, (pages_hbm, buf, sem) in enumerate(streams):
            _page_copy(pages_hbm, buf, sem, pt_ref, b, p, slot).start(priority=s % N_DMA_THREADS)
        return c
    lax.fori_loop(0, n_pages, body, 0)


def _wait_batch_pages(streams, pt_ref, b, slot, n_pages):
    for pages_hbm, buf, sem in streams:
        for p in range(n_pages):
            _page_copy(pages_hbm, buf, sem, pt_ref, b, p, slot).wait()


def _sample_index_kernel(pt_ref, qi_ref, w_ref, pages_hbm, out_ref, buf, sem, *, n_pages, unroll):
    b = pl.program_id(0)
    slot = b % 2
    streams = [(pages_hbm, buf, sem)]

    @pl.when(b == 0)
    def _first():
        _start_batch_pages(streams, pt_ref, 0, 0, n_pages)

    _wait_batch_pages(streams, pt_ref, b, slot, n_pages)

    qi = qi_ref[0]
    w = w_ref[0]
    n_q = qi.shape[0] // N_IDX_HEADS

    def run(prefetch):
        def body(c, carry):
            for u in range(unroll):
                p = c * unroll + u
                if prefetch:
                    _page_copy(pages_hbm, buf, sem, pt_ref, b + 1, p, 1 - slot).start(priority=u % N_DMA_THREADS)
                s = _dot(qi, buf[slot, p].astype(BF16))
                r = jnp.maximum(s, 0.0) * w
                out_ref[0, c, :, u * PAGE_SIZE:(u + 1) * PAGE_SIZE] = jnp.sum(
                    r.reshape(n_q, N_IDX_HEADS, PAGE_SIZE), axis=1)
            return carry
        lax.fori_loop(0, n_pages // unroll, body, 0)

    has_next = b + 1 < pl.num_programs(0)
    pl.when(has_next)(lambda: run(True))
    pl.when(jnp.logical_not(has_next))(lambda: run(False))


def _sample_index(page_table, qi_qh, w_qh, kidx_t, unroll):
    batch, n_pages = page_table.shape
    n_q = qi_qh.shape[1] // N_IDX_HEADS
    groups = n_pages // unroll
    grid_spec = pltpu.PrefetchScalarGridSpec(
        num_scalar_prefetch=1,
        grid=(batch,),
        in_specs=[pl.BlockSpec((1,) + qi_qh.shape[1:], lambda b, pt: (b, 0, 0)),
                  pl.BlockSpec((1,) + w_qh.shape[1:], lambda b, pt: (b, 0, 0)),
                  pl.BlockSpec(memory_space=pl.ANY)],
        out_specs=pl.BlockSpec((1, groups, n_q, unroll * PAGE_SIZE), lambda b, pt: (b, 0, 0, 0)),
        scratch_shapes=[pltpu.VMEM((2, n_pages) + kidx_t.shape[1:], kidx_t.dtype),
                        pltpu.SemaphoreType.DMA((2,))],
    )
    return pl.pallas_call(
        functools.partial(_sample_index_kernel, n_pages=n_pages, unroll=unroll),
        grid_spec=grid_spec,
        out_shape=jax.ShapeDtypeStruct((batch, groups, n_q, unroll * PAGE_SIZE), F32),
        compiler_params=pltpu.CompilerParams(dimension_semantics=("arbitrary",), vmem_limit_bytes=VMEM_LIMIT),
        name="sample_index",
    )(page_table, qi_qh, w_qh, kidx_t)


def _sample_select_kernel(sp_ref, qi_ref, w_ref, kw16_ref, thr_ref, cut_ref, nmn_ref, snew_scr,
                          *, topk, n_q, lane_chunk):
    rows, past = sp_ref.shape
    r_i = lax.broadcasted_iota(I32, (rows, LANES), 0)
    l_i = lax.broadcasted_iota(I32, (rows, LANES), 1)

    s = _nt_dot(qi_ref[...], kw16_ref[...][:, :D_IDX])
    r = jnp.maximum(s, 0.0) * w_ref[...]
    s_new = jnp.sum(r.reshape(rows, N_IDX_HEADS, LANES), axis=1)
    ok = ((l_i // n_q) == (r_i // n_q)) & ((l_i % n_q) <= (r_i % n_q))
    snew_scr[...] = jnp.where(ok, s_new, -jnp.inf)

    n_chunks = past // lane_chunk
    tiles = lane_chunk // LANES
    kf = float(topk)

    def count(ind):
        acc = ind(snew_scr[...], l_i + past)
        for ch in range(n_chunks):
            x = sp_ref[:, ch * lane_chunk:(ch + 1) * lane_chunk]
            for t in range(tiles):
                idx = l_i + (ch * lane_chunk + t * LANES)
                acc = acc + ind(x[:, t * LANES:(t + 1) * LANES], idx)
        return jnp.broadcast_to(jnp.sum(acc, axis=1, keepdims=True), (rows, LANES))

    c0 = count(lambda x, idx: jnp.where(x >= 0.0, 1.0, 0.0))
    key0 = jnp.where(c0 >= kf, 0, INT_MIN).astype(I32)

    def bit_body(it, key):
        cand = key | jnp.left_shift(jnp.int32(1), 30 - it)
        cf = _key_to_float(cand)
        return jnp.where(count(lambda x, idx: jnp.where(x >= cf, 1.0, 0.0)) >= kf, cand, key)

    key = lax.fori_loop(0, 31, bit_body, key0)
    key = jnp.maximum(key, KEY_NEG_FLT_MAX)
    thr = _key_to_float(key)
    cnt_gt = count(lambda x, idx: jnp.where(x > thr, 1.0, 0.0))
    cnt_ge = count(lambda x, idx: jnp.where(x >= thr, 1.0, 0.0))
    need = kf - cnt_gt
    straddle = jnp.max(jnp.where(cnt_ge > kf, 1.0, 0.0))

    idx_bits = (past + LANES).bit_length()
    thr_ref[...] = thr
    cut_ref[...] = jnp.full((rows, LANES), 1 << idx_bits, I32)

    @pl.when(straddle > 0.0)
    def _tie_cut():
        def cut_body(it, cut):
            cand = cut | jnp.left_shift(jnp.int32(1), (idx_bits - 1) - it)
            cnt = count(lambda x, idx: jnp.where(x == thr, jnp.where(idx < cand, 1.0, 0.0), 0.0))
            return jnp.where(cnt <= need, cand, cut)
        cut_ref[...] = lax.fori_loop(0, idx_bits, cut_body, jnp.zeros((rows, LANES), I32))

    cut = cut_ref[...]
    sn = snew_scr[...]
    tie_keep = jnp.where(l_i + past < cut, 0.0, -jnp.inf)
    nmn_ref[...] = jnp.where(sn > thr, 0.0, jnp.where(sn == thr, tie_keep, -jnp.inf))


def _sample_select(scores_past, qi_rows, w_rows, kw16_s, topk, n_q):
    rows, past = scores_past.shape
    lane_chunk = 2048 if past % 2048 == 0 else LANES
    full = lambda a: pl.BlockSpec(a.shape, lambda i: (0,) * a.ndim)
    o = jax.ShapeDtypeStruct((rows, LANES), F32)
    return pl.pallas_call(
        functools.partial(_sample_select_kernel, topk=topk, n_q=n_q, lane_chunk=lane_chunk),
        grid=(1,),
        in_specs=[full(scores_past), full(qi_rows), full(w_rows), full(kw16_s)],
        out_specs=[pl.BlockSpec((rows, LANES), lambda i: (0, 0))] * 3,
        out_shape=[o, jax.ShapeDtypeStruct((rows, LANES), I32), o],
        scratch_shapes=[pltpu.VMEM((rows, LANES), F32)],
        compiler_params=pltpu.CompilerParams(dimension_semantics=("arbitrary",), vmem_limit_bytes=VMEM_LIMIT),
        name="sample_select",
    )(scores_past, qi_rows, w_rows, kw16_s)


def _sample_attn_kernel(pt_ref, rbc_ref, q_ref, sp_ref, thr_ref, cut_ref, nmn_ref, k16n_ref, v16n_ref,
                        k_hbm, v_hbm, o_ref, kbuf, vbuf, ksem, vsem, lgs, bias_scr, mx_scr,
                        *, n_pages, n_q, unroll):
    b = pl.program_id(0)
    slot = b % 2
    rows = n_q * N_HEADS_A
    groups = n_pages // unroll
    streams = [(k_hbm, kbuf, ksem), (v_hbm, vbuf, vsem)]
    lane = lax.broadcasted_iota(I32, (rows, LANES), 1)
    rq = lax.broadcasted_iota(I32, (rows, LANES), 0) // N_HEADS_A
    expand = lambda a: jnp.concatenate(
        [jnp.broadcast_to(a[t:t + 1], (N_HEADS_A, LANES)) for t in range(n_q)], axis=0)

    @pl.when(b == 0)
    def _first():
        _start_batch_pages(streams, pt_ref, 0, 0, n_pages)
        far = jnp.concatenate([rbc_ref[N_BUCKETS - 1]] * n_q, axis=0)
        for t, dist in enumerate((PAGE_SIZE + rq - lane, rq - lane % n_q)):
            bkt = _t5_bucket(dist)
            tile = jnp.zeros((rows, LANES), F32)
            for k in range(N_BUCKETS - 1):
                tile = jnp.where(bkt == k, jnp.concatenate([rbc_ref[k]] * n_q, axis=0) - far, tile)
            bias_scr[t] = tile * LOG2E

    _wait_batch_pages(streams, pt_ref, b, slot, n_pages)

    thr = expand(thr_ref[0])
    cut = expand(cut_ref[0])
    q = q_ref[0]

    def logits_pass(prefetch):
        def body(c, mx):
            for u in range(unroll):
                p = c * unroll + u
                if prefetch:
                    for s, (pages_hbm, buf, sem) in enumerate(streams):
                        _page_copy(pages_hbm, buf, sem, pt_ref, b + 1, p, 1 - slot).start(
                            priority=s % N_DMA_THREADS)
                lg = _dot(q, kbuf[slot, p].astype(BF16))
                sc = expand(sp_ref[0, c, :, u * PAGE_SIZE:(u + 1) * PAGE_SIZE])
                tie_keep = jnp.where(lane + p * PAGE_SIZE < cut, 0.0, -jnp.inf)
                x = lg + jnp.where(sc > thr, 0.0, jnp.where(sc == thr, tie_keep, -jnp.inf))
                if u == unroll - 1:
                    x = x + bias_scr[0] * jnp.where(c == groups - 1, 1.0, 0.0)
                lgs[p] = x
                mx = jnp.maximum(mx, x)
            return mx
        mx_scr[...] = lax.fori_loop(0, groups, body, jnp.full((rows, LANES), NEG_BIG, F32))

    has_next = b + 1 < pl.num_programs(0)
    pl.when(has_next)(lambda: logits_pass(True))
    pl.when(jnp.logical_not(has_next))(lambda: logits_pass(False))

    xn = _nt_dot(q, k16n_ref[...]) + bias_scr[1] + expand(nmn_ref[0])
    m = jnp.max(jnp.maximum(mx_scr[...], xn), axis=1, keepdims=True)

    def pv_body(c, carry):
        lsum, acc = carry
        for u in range(unroll):
            p = c * unroll + u
            pr = jnp.exp2(lgs[p] - m)
            lsum = lsum + pr
            acc = acc + _nt_dot(pr.astype(BF16), vbuf[slot, p].astype(BF16))
        return lsum, acc

    pn = jnp.exp2(xn - m)
    lsum, acc = lax.fori_loop(0, groups, pv_body, (pn, _dot(pn.astype(BF16), v16n_ref[...])))
    out = acc * (1.0 / jnp.sum(lsum, axis=1, keepdims=True))
    head = lax.broadcasted_iota(I32, (rows, HEAD_DIM), 0) % N_HEADS_A
    o_ref[0] = jnp.where(head < N_HEADS_A // N_KV_A, out[:, :HEAD_DIM], out[:, HEAD_DIM:])


def _sample_attn(page_table, rb_col, q_bd, scores4, thr3, cut3, nmn3, k16n, v16n, k_pages_t, v_pages_t, unroll):
    batch, n_pages = page_table.shape
    n_q = scores4.shape[2]
    rows = q_bd.shape[1]
    per_b = lambda a: pl.BlockSpec((1,) + a.shape[1:], lambda b, pt: (b,) + (0,) * (a.ndim - 1))
    full = lambda a: pl.BlockSpec(a.shape, lambda b, pt: (0,) * a.ndim)
    page_buf = pltpu.VMEM((2, n_pages) + k_pages_t.shape[1:], k_pages_t.dtype)
    grid_spec = pltpu.PrefetchScalarGridSpec(
        num_scalar_prefetch=1,
        grid=(batch,),
        in_specs=[full(rb_col), per_b(q_bd), per_b(scores4), per_b(thr3), per_b(cut3), per_b(nmn3),
                  full(k16n), full(v16n), pl.BlockSpec(memory_space=pl.ANY), pl.BlockSpec(memory_space=pl.ANY)],
        out_specs=pl.BlockSpec((1, rows, HEAD_DIM), lambda b, pt: (b, 0, 0)),
        scratch_shapes=[page_buf, page_buf, pltpu.SemaphoreType.DMA((2,)), pltpu.SemaphoreType.DMA((2,)),
                        pltpu.VMEM((n_pages, rows, LANES), F32), pltpu.VMEM((2, rows, LANES), F32),
                        pltpu.VMEM((rows, LANES), F32)],
    )
    return pl.pallas_call(
        functools.partial(_sample_attn_kernel, n_pages=n_pages, n_q=n_q, unroll=unroll),
        grid_spec=grid_spec,
        out_shape=jax.ShapeDtypeStruct((batch, rows, HEAD_DIM), F32),
        compiler_params=pltpu.CompilerParams(dimension_semantics=("arbitrary",),
                                             vmem_limit_bytes=VMEM_LIMIT_PAGED),
        name="sample_attn",
    )(page_table, rb_col, q_bd, scores4, thr3, cut3, nmn3, k16n, v16n, k_pages_t, v_pages_t)


def _rope_tables(pos):
    half = HEAD_DIM // 2
    inv = ROPE_BASE ** (-jnp.arange(half, dtype=F32) / half)
    ang = pos.astype(F32)[:, None] * inv[None, :]
    cos = jnp.cos(ang)
    sin = jnp.sin(ang)
    reps = LANES // HEAD_DIM
    cos_t = jnp.tile(jnp.concatenate([cos, cos], axis=1), (1, reps))
    sin_t = jnp.tile(jnp.concatenate([-sin, sin], axis=1), (1, reps))
    return cos_t, sin_t


def _cat_weight(w_in):
    split = int(np.sum(SPLIT_SIZES[:6]))
    pad = jnp.zeros((w_in.shape[0], LANES - D_IDX - N_IDX_HEADS), BF16)
    return jnp.concatenate([w_in[:, :split].astype(BF16), pad, w_in[:, split:].astype(BF16)], axis=1)


def _pages_t(cache):
    pool, page = cache.shape[:2]
    return jnp.transpose(cache, (0, 2, 3, 1)).reshape(pool, -1, page)


def _largest_divisor(n, cap):
    d = cap
    while n % d:
        d //= 2
    return d


def kernel(x_prompt, x_sample, cache_k, cache_v, cache_kidx, state_ret, page_table, p_prompt, p_sample,
           rel_bias, w_in, w_out, g_pre, g_post, w_ple_up, w_ple_gate):
    batch, seq, d_model = x_prompt.shape
    dec_b, dec_t, _ = x_sample.shape
    depth = w_in.shape[0]
    n_pages = page_table.shape[1]
    past = n_pages * PAGE_SIZE
    rows_s = dec_b * dec_t
    assert depth == 1 and rows_s == LANES and seq % BLK == 0

    w_cat = _cat_weight(w_in[0])
    gpre = g_pre[0].reshape(1, d_model)
    gpost = g_post[0].reshape(1, d_model)
    wo16 = w_out[0].astype(BF16)
    wup16 = w_ple_up[0].astype(BF16)
    wg16 = w_ple_gate[0].astype(BF16)

    tm = _largest_divisor(seq, 512)
    cos_p, sin_p = _rope_tables(jnp.arange(seq))
    xp2 = x_prompt.reshape(batch * seq, d_model)
    pp = _project(xp2, gpre, w_cat, cos_p, sin_p, _largest_divisor(seq, 1024), seq)
    oa_p = _dsa_prompt(pp, rel_bias, batch, seq)
    r3 = lambda a: a.reshape(batch, seq, D_B)
    s0_p = jnp.zeros((batch, N_HEADS_B, HEAD_DIM, HEAD_DIM), F32)
    yn_p, state_p = _retention(r3(pp["qb"]), r3(pp["kb"]), r3(pp["vb"]), s0_p, float(min(RET_CHUNK, seq)))
    y_prompt = _merge(xp2, oa_p, yn_p.reshape(batch * seq, D_B), pp["ga"], pp["gb"],
                      p_prompt[0].reshape(batch * seq, -1), wo16, gpost, wup16, wg16, tm)

    pos_s = past + jnp.arange(dec_t)
    cos_s, sin_s = _rope_tables(jnp.tile(pos_s, dec_b))
    xs2 = x_sample.reshape(rows_s, d_model)
    ps = _project(xs2, gpre, w_cat, cos_s, sin_s, rows_s, rows_s)
    topk_s = min(TOPK_MAX, (past + dec_t) // 4)

    qi_rows = ps["qi"].reshape(rows_s * N_IDX_HEADS, D_IDX)
    w_rows = jnp.broadcast_to(ps["kw"][:, D_IDX:D_IDX + N_IDX_HEADS].reshape(rows_s * N_IDX_HEADS, 1),
                              (rows_s * N_IDX_HEADS, LANES))
    unroll = _largest_divisor(n_pages, 32)
    scores4 = _sample_index(page_table, qi_rows.reshape(dec_b, dec_t * N_IDX_HEADS, D_IDX),
                            w_rows.reshape(dec_b, dec_t * N_IDX_HEADS, LANES),
                            jnp.transpose(cache_kidx[0], (0, 2, 1)), unroll)
    scores_past = scores4.transpose(0, 2, 1, 3).reshape(rows_s, past)
    thr, cut, nmn = _sample_select(scores_past, qi_rows, w_rows, ps["kw16"], topk_s, dec_t)

    kv_of_head = np.arange(N_HEADS_A) // (N_HEADS_A // N_KV_A)
    place = jnp.asarray((kv_of_head[:, None] == np.arange(N_KV_A)[None, :]).astype(np.float32)).astype(BF16)
    q_bd = (ps["qa"].reshape(dec_b, dec_t, N_HEADS_A, 1, HEAD_DIM) * place[None, None, :, :, None]).reshape(
        dec_b, dec_t * N_HEADS_A, N_KV_A * HEAD_DIM)
    rb_col = jnp.broadcast_to(rel_bias[:, :, None], rel_bias.shape + (LANES,))
    b3 = lambda a: a.reshape(dec_b, dec_t, LANES)
    o_qh = _sample_attn(page_table, rb_col, q_bd, scores4, b3(thr), b3(cut), b3(nmn),
                        ps["k16"], ps["v16"], _pages_t(cache_k[0]), _pages_t(cache_v[0]), unroll)
    oa_s = o_qh.reshape(rows_s, D_A)

    chunk_s = RET_CHUNK
    padt = lambda a: jnp.pad(a.reshape(dec_b, dec_t, D_B), ((0, 0), (0, chunk_s - dec_t), (0, 0)))
    yn_s, state_s = _retention(padt(ps["qb"]), padt(ps["kb"]), padt(ps["vb"]),
                               state_ret[0].astype(F32), float(math.gcd(dec_t, RET_CHUNK)))
    y_sample = _merge(xs2, oa_s, yn_s[:, :dec_t].reshape(rows_s, D_B), ps["ga"], ps["gb"],
                      p_sample[0].reshape(rows_s, -1), wo16, gpost, wup16, wg16, rows_s)

    def kv(a_t, b, t):
        a = a_t.reshape(a_t.shape[0], N_KV_A, HEAD_DIM, -1).transpose(0, 3, 1, 2)
        return a.reshape(1, b, t, N_KV_A, HEAD_DIM)

    def ki(a_t, b, t):
        return a_t.transpose(0, 2, 1).reshape(1, b, t, D_IDX)

    return (
        y_prompt.reshape(batch, seq, d_model),
        y_sample.reshape(dec_b, dec_t, d_model),
        kv(pp["ka_t"], batch, seq), kv(pp["va_t"], batch, seq),
        ki(pp["ki_t"], batch, seq),
        state_p[None].astype(state_ret.dtype),
        kv(ps["ka_t"], dec_b, dec_t), kv(ps["va_t"], dec_b, dec_t),
        ki(ps["ki_t"], dec_b, dec_t),
        state_s[None].astype(state_ret.dtype),
    )
```

```python
import functools
import math

import jax
import jax.numpy as jnp
import numpy as np
from jax import lax
from jax.experimental import pallas as pl
from jax.experimental.pallas import tpu as pltpu

F32 = jnp.float32
BF16 = jnp.bfloat16
I32 = jnp.int32

HEAD_DIM = 64
N_HEADS_A = 8
N_KV_A = 2
N_IDX_HEADS = 8
D_IDX = 64
TOPK_MAX = 256
N_HEADS_B = 8
D_A = N_HEADS_A * HEAD_DIM
D_B = N_HEADS_B * HEAD_DIM
N_BUCKETS = 32
MAX_DISTANCE = 128
ROPE_BASE = 10000.0
RET_CHUNK = 128
PAGE_SIZE = 128
EPS = 1e-6
SPLIT_SIZES = (D_A, N_KV_A * HEAD_DIM, N_KV_A * HEAD_DIM, N_IDX_HEADS * D_IDX, D_IDX, N_IDX_HEADS,
               D_A, D_B, D_B, D_B, D_B)

LANES = 128
SUBLANES = 8
BLK = 128
VMEM_LIMIT = 48 * 1024 * 1024
VMEM_LIMIT_PAGED = 56 * 1024 * 1024

INT_MIN = -(2 ** 31)
KEY_NEG_FLT_MAX = INT_MIN + 0x00800000
NEG_BIG = -1e30
LOG2E = math.log2(math.e)
N_DMA_THREADS = 2
ONES_ROWS = 16

_C_QA, _C_KA, _C_VA, _C_QI, _C_KW, _C_GA, _C_QB, _C_KB, _C_VB, _C_GB, _C_END = (
    0, 512, 640, 768, 1280, 1408, 1920, 2432, 2944, 3456, 3968)


def _nt_dot(a, b):
    return lax.dot_general(a, b, (((1,), (1,)), ((), ())), preferred_element_type=F32)


def _dot(a, b):
    return jnp.dot(a, b, preferred_element_type=F32)


def _sigmoid(x):
    return 1.0 / (1.0 + jnp.exp(-x))


def _key_to_float(k):
    bits = jnp.where(k >= 0, k, k ^ 0x7FFFFFFF)
    return lax.bitcast_convert_type(bits, F32)


def _t5_bucket(n):
    max_exact = N_BUCKETS // 2
    n = jnp.maximum(n, 0)
    nf = jnp.maximum(n, 1).astype(F32)
    large = max_exact + jnp.floor(jnp.log(nf / max_exact) / math.log(MAX_DISTANCE / max_exact)
                                  * (N_BUCKETS - max_exact)).astype(I32)
    large = jnp.minimum(large, N_BUCKETS - 1)
    return jnp.where(n < max_exact, n, large)


def _bias_from_bucket(bkt, rb_ref, h):
    out = jnp.zeros(bkt.shape, F32)
    for k in range(N_BUCKETS):
        out = jnp.where(bkt == k, rb_ref[k, h], out)
    return out


def _proj_kernel(x_ref, g_ref, w_ref, cos_ref, sin_ref,
                 qa_ref, kat_ref, vat_ref, k16_ref, v16_ref, qi_ref, ga_ref, gb_ref,
                 qb_ref, kb_ref, vb_ref, kw_ref, kw16_ref, kit_ref, h_scr):
    x = x_ref[...]
    ms = jnp.mean(x * x, axis=-1, keepdims=True)
    h_scr[...] = (x * lax.rsqrt(ms + EPS) * g_ref[...]).astype(BF16)

    def mm(lo, hi):
        return _dot(h_scr[...], w_ref[:, lo:hi])

    qa_ref[...] = (mm(_C_QA, _C_KA) * (HEAD_DIM ** -0.5 * LOG2E)).astype(BF16)
    ka = mm(_C_KA, _C_VA)
    kat_ref[0] = ka.T
    k16_ref[...] = ka.astype(BF16)
    va = mm(_C_VA, _C_QI)
    vat_ref[0] = va.T
    v16_ref[...] = va.astype(BF16)
    qi_ref[...] = mm(_C_QI, _C_KW).astype(BF16)
    ga_ref[...] = mm(_C_GA, _C_QB).astype(BF16)
    gb_ref[...] = mm(_C_GB, _C_END).astype(BF16)
    vb_ref[...] = mm(_C_VB, _C_GB).astype(BF16)

    cos = cos_ref[...]
    sin = sin_ref[...]
    lane = lax.broadcasted_iota(I32, cos.shape, 1)
    first_half = (lane % HEAD_DIM) < (HEAD_DIM // 2)

    def rope(z, scale):
        outs = []
        for g in range(z.shape[1] // LANES):
            zg = z[:, g * LANES:(g + 1) * LANES]
            partner = jnp.where(first_half, pltpu.roll(zg, LANES - HEAD_DIM // 2, 1),
                                pltpu.roll(zg, HEAD_DIM // 2, 1))
            r = zg * cos + partner * sin
            if scale != 1.0:
                r = r * scale
            outs.append(r.astype(BF16))
        return jnp.concatenate(outs, axis=1)

    qb_ref[...] = rope(mm(_C_QB, _C_KB), 1.0)
    kb_ref[...] = rope(mm(_C_KB, _C_VB), HEAD_DIM ** -0.5)

    kw = mm(_C_KW, _C_GA)
    wscale = (N_IDX_HEADS ** -0.5) * (D_IDX ** -0.5)
    kw = kw * jnp.where(lane >= D_IDX, wscale, 1.0)
    kw_ref[...] = kw
    kw16_ref[...] = kw.astype(BF16)
    kit_ref[0] = kw.T[:D_IDX]


def _project(x2d, g_pre, w_cat, cos_t, sin_t, tm, seq):
    rows = x2d.shape[0]
    d_model = x2d.shape[1]
    n_tab = cos_t.shape[0] // tm
    tiles = seq // tm
    row_spec = lambda w: pl.BlockSpec((tm, w), lambda i: (i, 0))
    t_spec = lambda w: pl.BlockSpec((1, w, tm), lambda i: (i // tiles, 0, i % tiles))
    outs = [
        ("qa", D_A, BF16), ("ka_t", LANES, F32), ("va_t", LANES, F32), ("k16", LANES, BF16), ("v16", LANES, BF16),
        ("qi", N_IDX_HEADS * D_IDX, BF16), ("ga", D_A, BF16), ("gb", D_B, BF16),
        ("qb", D_B, BF16), ("kb", D_B, BF16), ("vb", D_B, BF16), ("kw", LANES, F32), ("kw16", LANES, BF16),
        ("ki_t", D_IDX, F32),
    ]
    transposed = lambda n: n.endswith("_t")
    res = pl.pallas_call(
        _proj_kernel,
        grid=(rows // tm,),
        in_specs=[
            row_spec(d_model),
            pl.BlockSpec((1, d_model), lambda i: (0, 0)),
            pl.BlockSpec(w_cat.shape, lambda i: (0, 0), pipeline_mode=pl.Buffered(1)),
            pl.BlockSpec((tm, LANES), lambda i: (i % n_tab, 0)),
            pl.BlockSpec((tm, LANES), lambda i: (i % n_tab, 0)),
        ],
        out_specs=[t_spec(w) if transposed(n) else row_spec(w) for n, w, _ in outs],
        out_shape=[jax.ShapeDtypeStruct((rows // seq, w, seq) if transposed(n) else (rows, w), dt)
                   for n, w, dt in outs],
        scratch_shapes=[pltpu.VMEM((tm, d_model), BF16)],
        compiler_params=pltpu.CompilerParams(dimension_semantics=("arbitrary",), vmem_limit_bytes=VMEM_LIMIT),
        name="proj",
    )(x2d, g_pre, w_cat, cos_t, sin_t)
    return {n: r for (n, _, _), r in zip(outs, res)}


def _dsa_prompt_kernel(rb_ref, qi_ref, kwq_ref, qin_ref, kwqn_ref, kw16_ref, qa_ref, k16_ref, v16_ref, oa_ref,
                       sc2, bt, vt, lgs, acc_scr, cut_scr, key_scr, *, topk, search_blk):
    b = pl.program_id(0)
    j = pl.program_id(1)
    nblk = pl.num_programs(1)
    sc = sc2.at[j % 2]
    sc_next = sc2.at[1 - j % 2]
    row = lax.broadcasted_iota(I32, (BLK, BLK), 0)
    col = lax.broadcasted_iota(I32, (BLK, BLK), 1)

    @pl.when((b == 0) & (j == 0))
    def _init_bias():
        for d in range(2):
            bkt = _t5_bucket(col - row + d * BLK)
            for h in range(N_HEADS_A):
                bt[d, h] = (_bias_from_bucket(bkt, rb_ref, h) - rb_ref[N_BUCKETS - 1, h]) * LOG2E
        bt[2] = jnp.zeros(bt.shape[1:], F32)

    @pl.when(j == 0)
    def _clear_values():
        vt[...] = jnp.zeros(vt.shape, BF16)

    vblk = v16_ref[pl.ds(pl.multiple_of(j * BLK, BLK), BLK), :]
    vtj = vblk.astype(F32).T.astype(BF16)
    ones = jnp.ones((ONES_ROWS, BLK), BF16)
    vt[j] = jnp.concatenate([piece for n in range(N_KV_A)
                             for piece in (vtj[n * HEAD_DIM:(n + 1) * HEAD_DIM], ones)], axis=0)

    def indexer_queries(qi_blk_ref, kw_blk_ref):
        qi = qi_blk_ref[...]
        qi_stack = jnp.concatenate([qi[:, h * D_IDX:(h + 1) * D_IDX] for h in range(N_IDX_HEADS)], axis=0)
        return qi_stack, kw_blk_ref[...].T

    def score_products(c, queries):
        out = []
        for u in range(search_blk):
            i = c * search_blk + u
            kib = kw16_ref[pl.ds(pl.multiple_of(i * BLK, BLK), BLK), :][:, :D_IDX]
            out.append(_nt_dot(kib, queries[0]))
        return out

    def score_finish(c, products, dst, queries, first_q):
        wi_t = queries[1]
        for u, s in enumerate(products):
            i = c * search_blk + u
            acc = jnp.zeros((BLK, BLK), F32)
            for h in range(N_IDX_HEADS):
                acc = acc + wi_t[D_IDX + h:D_IDX + h + 1, :] * jnp.maximum(s[:, h * BLK:(h + 1) * BLK], 0.0)
            dst[i] = jnp.where(row + i * BLK <= col + first_q, acc, -jnp.inf)

    def score_group(c, dst, queries, first_q):
        score_finish(c, score_products(c, queries), dst, queries, first_q)

    @pl.when(j == 0)
    def _own_scores():
        score_group(0, sc, indexer_queries(qi_ref, kwq_ref), 0)

    n_chunk = (j + search_blk) // search_blk

    def col_sum(w):
        return jnp.sum(w.reshape(BLK // SUBLANES, SUBLANES, BLK), axis=0)

    qa = qa_ref[...]
    g_per_kv = N_HEADS_A // N_KV_A
    q_stack = [jnp.concatenate([qa[:, (n * g_per_kv + g) * HEAD_DIM:(n * g_per_kv + g + 1) * HEAD_DIM]
                                for g in range(g_per_kv)], axis=0) for n in range(N_KV_A)]

    kf = float(topk)
    n_units = search_blk * N_KV_A

    def search(n_groups):
        def count_ge(cand, unit=None):
            raw = []
            if unit is not None:
                n = unit[1]
                for c in range(n_groups):
                    i = c * search_blk + unit[0]
                    raw.append((i, _nt_dot(k16_ref[i * BLK:(i + 1) * BLK, n * HEAD_DIM:(n + 1) * HEAD_DIM],
                                           q_stack[n])))
            acc = jnp.zeros((SUBLANES, BLK), F32)
            for c in range(n_groups):
                for u in range(search_blk):
                    acc = acc + col_sum(jnp.where(sc[c * search_blk + u] >= cand, 1.0, 0.0))
            for i, lg in raw:
                for g in range(g_per_kv):
                    lgs[i, unit[1] * g_per_kv + g] = lg[:, g * BLK:(g + 1) * BLK]
            return jnp.sum(acc, axis=0, keepdims=True)

        def bit_step(bit, key, unit=None):
            cand = key | jnp.left_shift(jnp.int32(1), bit)
            cnt = count_ge(_key_to_float(cand), unit)
            return jnp.where(cnt >= kf, cand, key)

        c0 = count_ge(jnp.zeros((1, BLK), F32))
        key = jnp.where(c0 >= kf, 0, INT_MIN).astype(I32)
        for it in range(n_units):
            key = bit_step(30 - it, key, (it // N_KV_A, it % N_KV_A))
        n_rest = jnp.where((j + 1) * BLK <= topk, 0, 31 - n_units)
        key = lax.fori_loop(0, n_rest, lambda it, k: bit_step(30 - n_units - it, k), key)
        key_scr[...] = jnp.broadcast_to(key, key_scr.shape)

    for n_groups in range(1, sc.shape[0] // search_blk + 1):
        pl.when(n_chunk == n_groups)(functools.partial(search, n_groups))

    key = jnp.maximum(key_scr[0:1, :], KEY_NEG_FLT_MAX)
    thr = _key_to_float(key)

    def count2_body(c, accs):
        a_gt, a_ge = accs
        for u in range(search_blk):
            t = sc[c * search_blk + u]
            a_gt = a_gt + col_sum(jnp.where(t > thr, 1.0, 0.0))
            a_ge = a_ge + col_sum(jnp.where(t >= thr, 1.0, 0.0))
        return a_gt, a_ge

    z8 = jnp.zeros((SUBLANES, BLK), F32)
    a_gt, a_ge = lax.fori_loop(0, n_chunk, count2_body, (z8, z8))
    cnt_gt = jnp.sum(a_gt, axis=0, keepdims=True)
    cnt_ge = jnp.sum(a_ge, axis=0, keepdims=True)
    need = kf - cnt_gt
    straddle = jnp.max(jnp.where(cnt_ge > kf, 1.0, 0.0))

    n_keys_max = sc.shape[0] * BLK
    cut_scr[...] = jnp.full(cut_scr.shape, 2 * n_keys_max, I32)

    @pl.when(straddle > 0.0)
    def _tie_cut():
        def count_tie(cand):
            def body(i, acc):
                t = sc[i]
                hit = jnp.where(t == thr, jnp.where(row + i * BLK < cand, 1.0, 0.0), 0.0)
                return acc + col_sum(hit)
            acc = lax.fori_loop(0, j + 1, body, z8)
            return jnp.sum(acc, axis=0, keepdims=True)

        def cut_body(it, cut):
            cand = cut | jnp.left_shift(jnp.int32(1), (n_keys_max.bit_length() - 1) - it)
            return jnp.where(count_tie(cand) <= need, cand, cut)

        cut = lax.fori_loop(0, n_keys_max.bit_length(), cut_body, jnp.zeros((1, BLK), I32))
        cut_scr[...] = jnp.broadcast_to(cut, cut_scr.shape)

    cut = cut_scr[0:1, :]

    def col_max(w):
        return jnp.max(w.reshape(BLK // SUBLANES, SUBLANES, BLK), axis=0)

    def mask_body(c, m8, biased):
        m8 = list(m8)
        for u in range(search_blk):
            i = c * search_blk + u
            t = sc[i]
            tie_keep = jnp.where(row + i * BLK < cut, 0.0, -jnp.inf)
            nm = jnp.where(t > thr, 0.0, jnp.where(t == thr, tie_keep, -jnp.inf))
            near = jnp.clip(j - i, 0, 2)
            for h in range(N_HEADS_A):
                x = lgs[i, h] + nm
                if biased:
                    x = x + bt[near, h]
                lgs[i, h] = x
                m8[h] = jnp.maximum(m8[h], col_max(x))
        return tuple(m8)

    n_far = jnp.maximum(n_chunk - 2, 0)
    neg8 = jnp.full((SUBLANES, BLK), NEG_BIG, F32)
    m8 = lax.fori_loop(0, n_far, functools.partial(mask_body, biased=False), (neg8,) * N_HEADS_A)
    m8 = lax.fori_loop(n_far, n_chunk, functools.partial(mask_body, biased=True), m8)
    m_row = [jnp.max(m8[h], axis=0, keepdims=True) for h in range(N_HEADS_A)]

    acc_scr[...] = jnp.zeros(acc_scr.shape, F32)

    vrows = HEAD_DIM + ONES_ROWS

    next_queries = indexer_queries(qin_ref, kwqn_ref)
    next_first_q = (j + 1) * BLK

    def pv_body(c, carry):
        pv = [None] * N_KV_A
        for u in range(search_blk):
            i = c * search_blk + u
            vti = vt[i]
            for n in range(N_KV_A):
                ps = [jnp.exp2(lgs[i, n * g_per_kv + g] - m_row[n * g_per_kv + g]).astype(BF16)
                      for g in range(g_per_kv)]
                t = _dot(vti[n * vrows:(n + 1) * vrows, :], jnp.concatenate(ps, axis=1))
                pv[n] = t if pv[n] is None else pv[n] + t
        for n in range(N_KV_A):
            acc_scr[n] += pv[n]
        score_group(c, sc_next, next_queries, next_first_q)
        return carry

    lax.fori_loop(0, n_chunk, pv_body, 0)

    @pl.when(((j + 1) % search_blk == 0) & (j + 1 < nblk))
    def _next_reaches_one_more_group():
        score_group(n_chunk, sc_next, next_queries, next_first_q)

    parts = []
    for n in range(N_KV_A):
        acc = acc_scr[n]
        out = acc[:HEAD_DIM] * (1.0 / acc[HEAD_DIM:HEAD_DIM + 1])
        parts += [out[:, g * BLK:(g + 1) * BLK] for g in range(g_per_kv)]
    oa_ref[...] = jnp.concatenate(parts, axis=0).T.astype(BF16)


def _dsa_prompt(p, rel_bias, batch, seq):
    nblk = seq // BLK
    topk = min(TOPK_MAX, seq // 4)
    search_blk = 4 if nblk % 4 == 0 else 1
    qrow = lambda w: pl.BlockSpec((BLK, w), lambda b, j: (b * nblk + j, 0))
    qnext = lambda w: pl.BlockSpec((BLK, w), lambda b, j: (b * nblk + jnp.minimum(j + 1, nblk - 1), 0))
    brow = lambda w: pl.BlockSpec((seq, w), lambda b, j: (b, 0))
    return pl.pallas_call(
        functools.partial(_dsa_prompt_kernel, topk=topk, search_blk=search_blk),
        grid=(batch, nblk),
        in_specs=[
            pl.BlockSpec(memory_space=pltpu.SMEM),
            qrow(N_IDX_HEADS * D_IDX), qrow(LANES), qnext(N_IDX_HEADS * D_IDX), qnext(LANES),
            brow(LANES), qrow(D_A), brow(LANES), brow(LANES),
        ],
        out_specs=qrow(D_A),
        out_shape=jax.ShapeDtypeStruct((batch * seq, D_A), BF16),
        scratch_shapes=[
            pltpu.VMEM((2, nblk, BLK, BLK), F32),
            pltpu.VMEM((3, N_HEADS_A, BLK, BLK), F32),
            pltpu.VMEM((nblk, N_KV_A * (HEAD_DIM + ONES_ROWS), BLK), BF16),
            pltpu.VMEM((nblk, N_HEADS_A, BLK, BLK), F32),
            pltpu.VMEM((N_KV_A, HEAD_DIM + ONES_ROWS, (N_HEADS_A // N_KV_A) * BLK), F32),
            pltpu.VMEM((SUBLANES, BLK), I32),
            pltpu.VMEM((SUBLANES, BLK), I32),
        ],
        compiler_params=pltpu.CompilerParams(dimension_semantics=("arbitrary", "arbitrary"),
                                             vmem_limit_bytes=VMEM_LIMIT),
        name="dsa_prompt",
    )(rel_bias, p["qi"], p["kw"], p["qi"], p["kw"], p["kw16"], p["qa"], p["k16"], p["v16"])


def _retention_kernel(q_ref, k_ref, v_ref, s0_ref, dmat_ref, din_ref, dout_ref, decm_ref, bdm_ref,
                      yn_ref, s_out_ref, s_scr, *, n_sub, pairs_per_phase):
    c = pl.program_id(1)
    chunk = din_ref.shape[0]

    @pl.when(c == 0)
    def _load_state():
        z = jnp.zeros((HEAD_DIM, HEAD_DIM), F32)
        for p in range(N_HEADS_B // 2):
            pair = jnp.concatenate([jnp.concatenate([s0_ref[0, 2 * p], z], axis=1),
                                    jnp.concatenate([z, s0_ref[0, 2 * p + 1]], axis=1)], axis=0)
            s_scr[p] = pair.T

    lo = lax.broadcasted_iota(I32, (chunk, LANES), 1) < HEAD_DIM
    zero = jnp.zeros((), BF16)
    n_pairs = N_HEADS_B // 2
    lanes = [slice(p * LANES, (p + 1) * LANES) for p in range(n_pairs)]
    rows = [slice(u * chunk, (u + 1) * chunk) for u in range(n_sub)]

    def run(pairs):
        units = [(p, u) for p in pairs for u in range(n_sub)]
        qd, kd, vt, s_a, s_b, intra, upd, inter = {}, {}, {}, {}, {}, {}, {}, {}
        for p, u in units:
            q, k, v = q_ref[0, rows[u], lanes[p]], k_ref[0, rows[u], lanes[p]], v_ref[0, rows[u], lanes[p]]
            qd[p, u] = (q.astype(F32) * din_ref[:, lanes[p]]).astype(BF16)
            kd[p, u] = (k.astype(F32) * dout_ref[:, lanes[p]]).astype(BF16)
            vt[p, u] = v.astype(F32).T.astype(BF16)
            s_a[p, u] = (_nt_dot(jnp.where(lo, k, zero), q) * dmat_ref[2 * p]).astype(BF16)
            s_b[p, u] = (_nt_dot(jnp.where(lo, zero, k), q) * dmat_ref[2 * p + 1]).astype(BF16)
        for p, u in units:
            intra[p, u] = jnp.concatenate([_dot(vt[p, u][:HEAD_DIM], s_a[p, u]),
                                           _dot(vt[p, u][HEAD_DIM:], s_b[p, u])], axis=0)
            upd[p, u] = _dot(vt[p, u], kd[p, u]) * bdm_ref[...]
        for p in pairs:
            st = s_scr[p]
            for u in range(n_sub):
                inter[p, u] = _nt_dot(st.astype(BF16), qd[p, u])
                st = st * decm_ref[p] + upd[p, u]
            s_scr[p] = st
            pair = st.T
            s_out_ref[0, 2 * p] = pair[:HEAD_DIM, :HEAD_DIM]
            s_out_ref[0, 2 * p + 1] = pair[HEAD_DIM:, HEAD_DIM:]
        for p, u in units:
            o = (inter[p, u] + intra[p, u]).reshape(2, HEAD_DIM, chunk)
            dlt = o - jnp.mean(o, axis=1, keepdims=True)
            var = jnp.mean(dlt * dlt, axis=1, keepdims=True)
            yn_ref[0, rows[u], lanes[p]] = (dlt * lax.rsqrt(var + EPS)).reshape(LANES, chunk).T.astype(BF16)

    for group in range(0, n_pairs, pairs_per_phase):
        run(range(group, group + pairs_per_phase))


def _ret_tables(chunk, n_real):
    h = N_HEADS_B
    log_g = jnp.log1p(-jnp.exp2(-5.0 - jnp.arange(h, dtype=F32)))
    i = jnp.arange(chunk, dtype=F32)
    diff = i[:, None] - i[None, :]
    dmat = jnp.where(diff >= 0, jnp.exp(jnp.maximum(diff, 0.0)[None] * log_g[:, None, None]), 0.0)
    dmat_t = jnp.swapaxes(dmat, 1, 2)
    dec_in = jnp.exp((i + 1.0)[:, None] * log_g[None, :])
    dec_out = jnp.exp((n_real - 1.0 - i)[:, None] * log_g[None, :])
    dec_chunk = jnp.exp(n_real * log_g)
    din = jnp.repeat(dec_in, HEAD_DIM, axis=1)
    dout = jnp.repeat(dec_out, HEAD_DIM, axis=1)
    head_of = np.arange(LANES) // HEAD_DIM
    bd = jnp.asarray((head_of[:, None] == head_of[None, :]).astype(np.float32))
    dc = jnp.repeat(dec_chunk, HEAD_DIM).reshape(h // 2, LANES)
    decm = dc[:, :, None] * bd[None]
    return dmat_t, din, dout, decm, bd


def _retention(q3, k3, v3, s0, n_real):
    batch, seq, _ = q3.shape
    chunk = min(RET_CHUNK, seq)
    n_sub = _largest_divisor(seq // chunk, 8)
    dmat_t, din, dout, decm, bd = _ret_tables(chunk, n_real)
    tok = pl.BlockSpec((1, n_sub * chunk, D_B), lambda b, c: (b, c, 0))
    st = pl.BlockSpec((1,) + s0.shape[1:], lambda b, c: (b, 0, 0, 0))
    full = lambda a: pl.BlockSpec(a.shape, lambda b, c: (0,) * a.ndim)
    return pl.pallas_call(
        functools.partial(_retention_kernel, n_sub=n_sub,
                          pairs_per_phase=max(1, min(N_HEADS_B // 2, 8 // n_sub))),
        grid=(batch, seq // (n_sub * chunk)),
        in_specs=[tok, tok, tok, st, full(dmat_t), full(din), full(dout), full(decm), full(bd)],
        out_specs=[tok, st],
        out_shape=[jax.ShapeDtypeStruct((batch, seq, D_B), BF16),
                   jax.ShapeDtypeStruct(s0.shape, F32)],
        scratch_shapes=[pltpu.VMEM((N_HEADS_B // 2, LANES, LANES), F32)],
        compiler_params=pltpu.CompilerParams(dimension_semantics=("arbitrary", "arbitrary"),
                                             vmem_limit_bytes=VMEM_LIMIT),
        name="retention",
    )(q3, k3, v3, s0, dmat_t, din, dout, decm, bd)


def _merge_kernel(x_ref, oa_ref, yn_ref, ga_ref, gb_ref, p_ref, wo_ref, gp_ref, wup_ref, wg_ref, out_ref):
    ga = ga_ref[...].astype(F32)
    gb = gb_ref[...].astype(F32)
    ya = (ga * _sigmoid(ga) * oa_ref[...].astype(F32)).astype(BF16)
    yb = (gb * _sigmoid(gb) * yn_ref[...].astype(F32)).astype(BF16)
    y = _dot(ya, wo_ref[0:D_A, :]) + _dot(yb, wo_ref[D_A:D_A + D_B, :])
    ms = jnp.mean(y * y, axis=-1, keepdims=True)
    x1 = x_ref[...] + y * lax.rsqrt(ms + EPS) * gp_ref[...]
    ple = _dot(p_ref[...].astype(BF16), wup_ref[...])
    gate = _sigmoid(_dot(x1.astype(BF16), wg_ref[...]))
    out_ref[...] = x1 + ple * gate


def _merge(x2d, oa, yn, ga, gb, p2d, wo16, g_post, wup16, wg16, tm):
    rows, d_model = x2d.shape
    row = lambda w: pl.BlockSpec((tm, w), lambda i: (i, 0))
    full = lambda a: pl.BlockSpec(a.shape, lambda i: (0, 0))
    return pl.pallas_call(
        _merge_kernel,
        grid=(rows // tm,),
        in_specs=[row(d_model), row(D_A), row(D_B), row(D_A), row(D_B), row(p2d.shape[1]),
                  full(wo16), full(g_post), full(wup16), full(wg16)],
        out_specs=row(d_model),
        out_shape=jax.ShapeDtypeStruct((rows, d_model), F32),
        compiler_params=pltpu.CompilerParams(dimension_semantics=("arbitrary",), vmem_limit_bytes=VMEM_LIMIT),
        name="merge",
    )(x2d, oa, yn, ga, gb, p2d, wo16, g_post, wup16, wg16)


def _page_copy(pages_hbm, buf, sem, pt_ref, b, p, slot):
    return pltpu.make_async_copy(pages_hbm.at[pt_ref[b, p]], buf.at[slot, p], sem.at[slot])


def _start_batch_pages(streams, pt_ref, b, slot, n_pages):
    def body(p, c):
        for s, (pages_hbm, buf, sem) in enumerate(streams):
            _page_copy(pages_hbm, buf, sem, pt_ref, b, p, slot).start(priority=s % N_DMA_THREADS)
        return c
    lax.fori_loop(0, n_pages, body, 0)


def _wait_batch_pages(streams, pt_ref, b, slot, n_pages):
    for pages_hbm, buf, sem in streams:
        for p in range(n_pages):
            _page_copy(pages_hbm, buf, sem, pt_ref, b, p, slot).wait()


def _sample_index_kernel(pt_ref, qi_ref, w_ref, pages_hbm, out_ref, buf, sem, *, n_pages, unroll):
    b = pl.program_id(0)
    slot = b % 2
    streams = [(pages_hbm, buf, sem)]

    @pl.when(b == 0)
    def _first():
        _start_batch_pages(streams, pt_ref, 0, 0, n_pages)

    _wait_batch_pages(streams, pt_ref, b, slot, n_pages)

    qi = qi_ref[0]
    w = w_ref[0]
    n_q = qi.shape[0] // N_IDX_HEADS

    def run(prefetch):
        def body(c, carry):
            for u in range(unroll):
                p = c * unroll + u
                if prefetch:
                    _page_copy(pages_hbm, buf, sem, pt_ref, b + 1, p, 1 - slot).start(priority=u % N_DMA_THREADS)
                s = _dot(qi, buf[slot, p].astype(BF16))
                r = jnp.maximum(s, 0.0) * w
                out_ref[0, c, :, u * PAGE_SIZE:(u + 1) * PAGE_SIZE] = jnp.sum(
                    r.reshape(n_q, N_IDX_HEADS, PAGE_SIZE), axis=1)
            return carry
        lax.fori_loop(0, n_pages // unroll, body, 0)

    has_next = b + 1 < pl.num_programs(0)
    pl.when(has_next)(lambda: run(True))
    pl.when(jnp.logical_not(has_next))(lambda: run(False))


def _sample_index(page_table, qi_qh, w_qh, kidx_t, unroll):
    batch, n_pages = page_table.shape
    n_q = qi_qh.shape[1] // N_IDX_HEADS
    groups = n_pages // unroll
    grid_spec = pltpu.PrefetchScalarGridSpec(
        num_scalar_prefetch=1,
        grid=(batch,),
        in_specs=[pl.BlockSpec((1,) + qi_qh.shape[1:], lambda b, pt: (b, 0, 0)),
                  pl.BlockSpec((1,) + w_qh.shape[1:], lambda b, pt: (b, 0, 0)),
                  pl.BlockSpec(memory_space=pl.ANY)],
        out_specs=pl.BlockSpec((1, groups, n_q, unroll * PAGE_SIZE), lambda b, pt: (b, 0, 0, 0)),
        scratch_shapes=[pltpu.VMEM((2, n_pages) + kidx_t.shape[1:], kidx_t.dtype),
                        pltpu.SemaphoreType.DMA((2,))],
    )
    return pl.pallas_call(
        functools.partial(_sample_index_kernel, n_pages=n_pages, unroll=unroll),
        grid_spec=grid_spec,
        out_shape=jax.ShapeDtypeStruct((batch, groups, n_q, unroll * PAGE_SIZE), F32),
        compiler_params=pltpu.CompilerParams(dimension_semantics=("arbitrary",), vmem_limit_bytes=VMEM_LIMIT),
        name="sample_index",
    )(page_table, qi_qh, w_qh, kidx_t)


def _sample_select_kernel(sp_ref, qi_ref, w_ref, kw16_ref, thr_ref, cut_ref, nmn_ref, snew_scr,
                          *, topk, n_q, lane_chunk):
    rows, past = sp_ref.shape
    r_i = lax.broadcasted_iota(I32, (rows, LANES), 0)
    l_i = lax.broadcasted_iota(I32, (rows, LANES), 1)

    s = _nt_dot(qi_ref[...], kw16_ref[...][:, :D_IDX])
    r = jnp.maximum(s, 0.0) * w_ref[...]
    s_new = jnp.sum(r.reshape(rows, N_IDX_HEADS, LANES), axis=1)
    ok = ((l_i // n_q) == (r_i // n_q)) & ((l_i % n_q) <= (r_i % n_q))
    snew_scr[...] = jnp.where(ok, s_new, -jnp.inf)

    n_chunks = past // lane_chunk
    tiles = lane_chunk // LANES
    kf = float(topk)

    def count(ind):
        acc = ind(snew_scr[...], l_i + past)
        for ch in range(n_chunks):
            x = sp_ref[:, ch * lane_chunk:(ch + 1) * lane_chunk]
            for t in range(tiles):
                idx = l_i + (ch * lane_chunk + t * LANES)
                acc = acc + ind(x[:, t * LANES:(t + 1) * LANES], idx)
        return jnp.broadcast_to(jnp.sum(acc, axis=1, keepdims=True), (rows, LANES))

    c0 = count(lambda x, idx: jnp.where(x >= 0.0, 1.0, 0.0))
    key0 = jnp.where(c0 >= kf, 0, INT_MIN).astype(I32)

    def bit_body(it, key):
        cand = key | jnp.left_shift(jnp.int32(1), 30 - it)
        cf = _key_to_float(cand)
        return jnp.where(count(lambda x, idx: jnp.where(x >= cf, 1.0, 0.0)) >= kf, cand, key)

    key = lax.fori_loop(0, 31, bit_body, key0)
    key = jnp.maximum(key, KEY_NEG_FLT_MAX)
    thr = _key_to_float(key)
    cnt_gt = count(lambda x, idx: jnp.where(x > thr, 1.0, 0.0))
    cnt_ge = count(lambda x, idx: jnp.where(x >= thr, 1.0, 0.0))
    need = kf - cnt_gt
    straddle = jnp.max(jnp.where(cnt_ge > kf, 1.0, 0.0))

    idx_bits = (past + LANES).bit_length()
    thr_ref[...] = thr
    cut_ref[...] = jnp.full((rows, LANES), 1 << idx_bits, I32)

    @pl.when(straddle > 0.0)
    def _tie_cut():
        def cut_body(it, cut):
            cand = cut | jnp.left_shift(jnp.int32(1), (idx_bits - 1) - it)
            cnt = count(lambda x, idx: jnp.where(x == thr, jnp.where(idx < cand, 1.0, 0.0), 0.0))
            return jnp.where(cnt <= need, cand, cut)
        cut_ref[...] = lax.fori_loop(0, idx_bits, cut_body, jnp.zeros((rows, LANES), I32))

    cut = cut_ref[...]
    sn = snew_scr[...]
    tie_keep = jnp.where(l_i + past < cut, 0.0, -jnp.inf)
    nmn_ref[...] = jnp.where(sn > thr, 0.0, jnp.where(sn == thr, tie_keep, -jnp.inf))


def _sample_select(scores_past, qi_rows, w_rows, kw16_s, topk, n_q):
    rows, past = scores_past.shape
    lane_chunk = 2048 if past % 2048 == 0 else LANES
    full = lambda a: pl.BlockSpec(a.shape, lambda i: (0,) * a.ndim)
    o = jax.ShapeDtypeStruct((rows, LANES), F32)
    return pl.pallas_call(
        functools.partial(_sample_select_kernel, topk=topk, n_q=n_q, lane_chunk=lane_chunk),
        grid=(1,),
        in_specs=[full(scores_past), full(qi_rows), full(w_rows), full(kw16_s)],
        out_specs=[pl.BlockSpec((rows, LANES), lambda i: (0, 0))] * 3,
        out_shape=[o, jax.ShapeDtypeStruct((rows, LANES), I32), o],
        scratch_shapes=[pltpu.VMEM((rows, LANES), F32)],
        compiler_params=pltpu.CompilerParams(dimension_semantics=("arbitrary",), vmem_limit_bytes=VMEM_LIMIT),
        name="sample_select",
    )(scores_past, qi_rows, w_rows, kw16_s)


def _sample_attn_kernel(pt_ref, rbc_ref, q_ref, sp_ref, thr_ref, cut_ref, nmn_ref, k16n_ref, v16n_ref,
                        k_hbm, v_hbm, o_ref, kbuf, vbuf, ksem, vsem, lgs, bias_scr, mx_scr,
                        *, n_pages, n_q, unroll):
    b = pl.program_id(0)
    slot = b % 2
    rows = n_q * N_HEADS_A
    groups = n_pages // unroll
    streams = [(k_hbm, kbuf, ksem), (v_hbm, vbuf, vsem)]
    lane = lax.broadcasted_iota(I32, (rows, LANES), 1)
    rq = lax.broadcasted_iota(I32, (rows, LANES), 0) // N_HEADS_A
    expand = lambda a: jnp.concatenate(
        [jnp.broadcast_to(a[t:t + 1], (N_HEADS_A, LANES)) for t in range(n_q)], axis=0)

    @pl.when(b == 0)
    def _first():
        _start_batch_pages(streams, pt_ref, 0, 0, n_pages)
        far = jnp.concatenate([rbc_ref[N_BUCKETS - 1]] * n_q, axis=0)
        for t, dist in enumerate((PAGE_SIZE + rq - lane, rq - lane % n_q)):
            bkt = _t5_bucket(dist)
            tile = jnp.zeros((rows, LANES), F32)
            for k in range(N_BUCKETS - 1):
                tile = jnp.where(bkt == k, jnp.concatenate([rbc_ref[k]] * n_q, axis=0) - far, tile)
            bias_scr[t] = tile * LOG2E

    _wait_batch_pages(streams, pt_ref, b, slot, n_pages)

    thr = expand(thr_ref[0])
    cut = expand(cut_ref[0])
    q = q_ref[0]

    def logits_pass(prefetch):
        def body(c, mx):
            for u in range(unroll):
                p = c * unroll + u
                if prefetch:
                    for s, (pages_hbm, buf, sem) in enumerate(streams):
                        _page_copy(pages_hbm, buf, sem, pt_ref, b + 1, p, 1 - slot).start(
                            priority=s % N_DMA_THREADS)
                lg = _dot(q, kbuf[slot, p].astype(BF16))
                sc = expand(sp_ref[0, c, :, u * PAGE_SIZE:(u + 1) * PAGE_SIZE])
                tie_keep = jnp.where(lane + p * PAGE_SIZE < cut, 0.0, -jnp.inf)
                x = lg + jnp.where(sc > thr, 0.0, jnp.where(sc == thr, tie_keep, -jnp.inf))
                if u == unroll - 1:
                    x = x + bias_scr[0] * jnp.where(c == groups - 1, 1.0, 0.0)
                lgs[p] = x
                mx = jnp.maximum(mx, x)
            return mx
        mx_scr[...] = lax.fori_loop(0, groups, body, jnp.full((rows, LANES), NEG_BIG, F32))

    has_next = b + 1 < pl.num_programs(0)
    pl.when(has_next)(lambda: logits_pass(True))
    pl.when(jnp.logical_not(has_next))(lambda: logits_pass(False))

    xn = _nt_dot(q, k16n_ref[...]) + bias_scr[1] + expand(nmn_ref[0])
    m = jnp.max(jnp.maximum(mx_scr[...], xn), axis=1, keepdims=True)

    def pv_body(c, carry):
        lsum, acc = carry
        for u in range(unroll):
            p = c * unroll + u
            pr = jnp.exp2(lgs[p] - m)
            lsum = lsum + pr
            acc = acc + _nt_dot(pr.astype(BF16), vbuf[slot, p].astype(BF16))
        return lsum, acc

    pn = jnp.exp2(xn - m)
    lsum, acc = lax.fori_loop(0, groups, pv_body, (pn, _dot(pn.astype(BF16), v16n_ref[...])))
    out = acc * (1.0 / jnp.sum(lsum, axis=1, keepdims=True))
    head = lax.broadcasted_iota(I32, (rows, HEAD_DIM), 0) % N_HEADS_A
    o_ref[0] = jnp.where(head < N_HEADS_A // N_KV_A, out[:, :HEAD_DIM], out[:, HEAD_DIM:])


def _sample_attn(page_table, rb_col, q_bd, scores4, thr3, cut3, nmn3, k16n, v16n, k_pages_t, v_pages_t, unroll):
    batch, n_pages = page_table.shape
    n_q = scores4.shape[2]
    rows = q_bd.shape[1]
    per_b = lambda a: pl.BlockSpec((1,) + a.shape[1:], lambda b, pt: (b,) + (0,) * (a.ndim - 1))
    full = lambda a: pl.BlockSpec(a.shape, lambda b, pt: (0,) * a.ndim)
    page_buf = pltpu.VMEM((2, n_pages) + k_pages_t.shape[1:], k_pages_t.dtype)
    grid_spec = pltpu.PrefetchScalarGridSpec(
        num_scalar_prefetch=1,
        grid=(batch,),
        in_specs=[full(rb_col), per_b(q_bd), per_b(scores4), per_b(thr3), per_b(cut3), per_b(nmn3),
                  full(k16n), full(v16n), pl.BlockSpec(memory_space=pl.ANY), pl.BlockSpec(memory_space=pl.ANY)],
        out_specs=pl.BlockSpec((1, rows, HEAD_DIM), lambda b, pt: (b, 0, 0)),
        scratch_shapes=[page_buf, page_buf, pltpu.SemaphoreType.DMA((2,)), pltpu.SemaphoreType.DMA((2,)),
                        pltpu.VMEM((n_pages, rows, LANES), F32), pltpu.VMEM((2, rows, LANES), F32),
                        pltpu.VMEM((rows, LANES), F32)],
    )
    return pl.pallas_call(
        functools.partial(_sample_attn_kernel, n_pages=n_pages, n_q=n_q, unroll=unroll),
        grid_spec=grid_spec,
        out_shape=jax.ShapeDtypeStruct((batch, rows, HEAD_DIM), F32),
        compiler_params=pltpu.CompilerParams(dimension_semantics=("arbitrary",),
                                             vmem_limit_bytes=VMEM_LIMIT_PAGED),
        name="sample_attn",
    )(page_table, rb_col, q_bd, scores4, thr3, cut3, nmn3, k16n, v16n, k_pages_t, v_pages_t)


def _rope_tables(pos):
    half = HEAD_DIM // 2
    inv = ROPE_BASE ** (-jnp.arange(half, dtype=F32) / half)
    ang = pos.astype(F32)[:, None] * inv[None, :]
    cos = jnp.cos(ang)
    sin = jnp.sin(ang)
    reps = LANES // HEAD_DIM
    cos_t = jnp.tile(jnp.concatenate([cos, cos], axis=1), (1, reps))
    sin_t = jnp.tile(jnp.concatenate([-sin, sin], axis=1), (1, reps))
    return cos_t, sin_t


def _cat_weight(w_in):
    split = int(np.sum(SPLIT_SIZES[:6]))
    pad = jnp.zeros((w_in.shape[0], LANES - D_IDX - N_IDX_HEADS), BF16)
    return jnp.concatenate([w_in[:, :split].astype(BF16), pad, w_in[:, split:].astype(BF16)], axis=1)


def _pages_t(cache):
    pool, page = cache.shape[:2]
    return jnp.transpose(cache, (0, 2, 3, 1)).reshape(pool, -1, page)


def _largest_divisor(n, cap):
    d = cap
    while n % d:
        d //= 2
    return d


def kernel(x_prompt, x_sample, cache_k, cache_v, cache_kidx, state_ret, page_table, p_prompt, p_sample,
           rel_bias, w_in, w_out, g_pre, g_post, w_ple_up, w_ple_gate):
    batch, seq, d_model = x_prompt.shape
    dec_b, dec_t, _ = x_sample.shape
    depth = w_in.shape[0]
    n_pages = page_table.shape[1]
    past = n_pages * PAGE_SIZE
    rows_s = dec_b * dec_t
    assert depth == 1 and rows_s == LANES and seq % BLK == 0

    w_cat = _cat_weight(w_in[0])
    gpre = g_pre[0].reshape(1, d_model)
    gpost = g_post[0].reshape(1, d_model)
    wo16 = w_out[0].astype(BF16)
    wup16 = w_ple_up[0].astype(BF16)
    wg16 = w_ple_gate[0].astype(BF16)

    tm = _largest_divisor(seq, 512)
    cos_p, sin_p = _rope_tables(jnp.arange(seq))
    xp2 = x_prompt.reshape(batch * seq, d_model)
    pp = _project(xp2, gpre, w_cat, cos_p, sin_p, _largest_divisor(seq, 1024), seq)
    oa_p = _dsa_prompt(pp, rel_bias, batch, seq)
    r3 = lambda a: a.reshape(batch, seq, D_B)
    s0_p = jnp.zeros((batch, N_HEADS_B, HEAD_DIM, HEAD_DIM), F32)
    yn_p, state_p = _retention(r3(pp["qb"]), r3(pp["kb"]), r3(pp["vb"]), s0_p, float(min(RET_CHUNK, seq)))
    y_prompt = _merge(xp2, oa_p, yn_p.reshape(batch * seq, D_B), pp["ga"], pp["gb"],
                      p_prompt[0].reshape(batch * seq, -1), wo16, gpost, wup16, wg16, tm)

    pos_s = past + jnp.arange(dec_t)
    cos_s, sin_s = _rope_tables(jnp.tile(pos_s, dec_b))
    xs2 = x_sample.reshape(rows_s, d_model)
    ps = _project(xs2, gpre, w_cat, cos_s, sin_s, rows_s, rows_s)
    topk_s = min(TOPK_MAX, (past + dec_t) // 4)

    qi_rows = ps["qi"].reshape(rows_s * N_IDX_HEADS, D_IDX)
    w_rows = jnp.broadcast_to(ps["kw"][:, D_IDX:D_IDX + N_IDX_HEADS].reshape(rows_s * N_IDX_HEADS, 1),
                              (rows_s * N_IDX_HEADS, LANES))
    unroll = _largest_divisor(n_pages, 32)
    scores4 = _sample_index(page_table, qi_rows.reshape(dec_b, dec_t * N_IDX_HEADS, D_IDX),
                            w_rows.reshape(dec_b, dec_t * N_IDX_HEADS, LANES),
                            jnp.transpose(cache_kidx[0], (0, 2, 1)), unroll)
    scores_past = scores4.transpose(0, 2, 1, 3).reshape(rows_s, past)
    thr, cut, nmn = _sample_select(scores_past, qi_rows, w_rows, ps["kw16"], topk_s, dec_t)

    kv_of_head = np.arange(N_HEADS_A) // (N_HEADS_A // N_KV_A)
    place = jnp.asarray((kv_of_head[:, None] == np.arange(N_KV_A)[None, :]).astype(np.float32)).astype(BF16)
    q_bd = (ps["qa"].reshape(dec_b, dec_t, N_HEADS_A, 1, HEAD_DIM) * place[None, None, :, :, None]).reshape(
        dec_b, dec_t * N_HEADS_A, N_KV_A * HEAD_DIM)
    rb_col = jnp.broadcast_to(rel_bias[:, :, None], rel_bias.shape + (LANES,))
    b3 = lambda a: a.reshape(dec_b, dec_t, LANES)
    o_qh = _sample_attn(page_table, rb_col, q_bd, scores4, b3(thr), b3(cut), b3(nmn),
                        ps["k16"], ps["v16"], _pages_t(cache_k[0]), _pages_t(cache_v[0]), unroll)
    oa_s = o_qh.reshape(rows_s, D_A)

    chunk_s = RET_CHUNK
    padt = lambda a: jnp.pad(a.reshape(dec_b, dec_t, D_B), ((0, 0), (0, chunk_s - dec_t), (0, 0)))
    yn_s, state_s = _retention(padt(ps["qb"]), padt(ps["kb"]), padt(ps["vb"]),
                               state_ret[0].astype(F32), float(math.gcd(dec_t, RET_CHUNK)))
    y_sample = _merge(xs2, oa_s, yn_s[:, :dec_t].reshape(rows_s, D_B), ps["ga"], ps["gb"],
                      p_sample[0].reshape(rows_s, -1), wo16, gpost, wup16, wg16, rows_s)

    def kv(a_t, b, t):
        a = a_t.reshape(a_t.shape[0], N_KV_A, HEAD_DIM, -1).transpose(0, 3, 1, 2)
        return a.reshape(1, b, t, N_KV_A, HEAD_DIM)

    def ki(a_t, b, t):
        return a_t.transpose(0, 2, 1).reshape(1, b, t, D_IDX)

    return (
        y_prompt.reshape(batch, seq, d_model),
        y_sample.reshape(dec_b, dec_t, d_model),
        kv(pp["ka_t"], batch, seq), kv(pp["va_t"], batch, seq),
        ki(pp["ki_t"], batch, seq),
        state_p[None].astype(state_ret.dtype),
        kv(ps["ka_t"], dec_b, dec_t), kv(ps["va_t"], dec_b, dec_t),
        ki(ps["ki_t"], dec_b, dec_t),
        state_s[None].astype(state_ret.dtype),
    )
```

```python
import functools
import math

import jax
import jax.numpy as jnp
import numpy as np
from jax import lax
from jax.experimental import pallas as pl
from jax.experimental.pallas import tpu as pltpu

F32 = jnp.float32
BF16 = jnp.bfloat16
I32 = jnp.int32

HEAD_DIM = 64
N_HEADS_A = 8
N_KV_A = 2
N_IDX_HEADS = 8
D_IDX = 64
TOPK_MAX = 256
N_HEADS_B = 8
D_A = N_HEADS_A * HEAD_DIM
D_B = N_HEADS_B * HEAD_DIM
N_BUCKETS = 32
MAX_DISTANCE = 128
ROPE_BASE = 10000.0
RET_CHUNK = 128
PAGE_SIZE = 128
EPS = 1e-6
SPLIT_SIZES = (D_A, N_KV_A * HEAD_DIM, N_KV_A * HEAD_DIM, N_IDX_HEADS * D_IDX, D_IDX, N_IDX_HEADS,
               D_A, D_B, D_B, D_B, D_B)

LANES = 128
SUBLANES = 8
BLK = 128
VMEM_LIMIT = 48 * 1024 * 1024
VMEM_LIMIT_PAGED = 56 * 1024 * 1024

INT_MIN = -(2 ** 31)
KEY_NEG_FLT_MAX = INT_MIN + 0x00800000
NEG_BIG = -1e30
LOG2E = math.log2(math.e)
N_DMA_THREADS = 2
ONES_ROWS = 16

_C_QA, _C_KA, _C_VA, _C_QI, _C_KW, _C_GA, _C_QB, _C_KB, _C_VB, _C_GB, _C_END = (
    0, 512, 640, 768, 1280, 1408, 1920, 2432, 2944, 3456, 3968)


def _nt_dot(a, b):
    return lax.dot_general(a, b, (((1,), (1,)), ((), ())), preferred_element_type=F32)


def _dot(a, b):
    return jnp.dot(a, b, preferred_element_type=F32)


def _sigmoid(x):
    return 1.0 / (1.0 + jnp.exp(-x))


def _key_to_float(k):
    bits = jnp.where(k >= 0, k, k ^ 0x7FFFFFFF)
    return lax.bitcast_convert_type(bits, F32)


def _t5_bucket(n):
    max_exact = N_BUCKETS // 2
    n = jnp.maximum(n, 0)
    nf = jnp.maximum(n, 1).astype(F32)
    large = max_exact + jnp.floor(jnp.log(nf / max_exact) / math.log(MAX_DISTANCE / max_exact)
                                  * (N_BUCKETS - max_exact)).astype(I32)
    large = jnp.minimum(large, N_BUCKETS - 1)
    return jnp.where(n < max_exact, n, large)


def _bias_from_bucket(bkt, rb_ref, h):
    out = jnp.zeros(bkt.shape, F32)
    for k in range(N_BUCKETS):
        out = jnp.where(bkt == k, rb_ref[k, h], out)
    return out


def _proj_kernel(x_ref, g_ref, w_ref, cos_ref, sin_ref,
                 qa_ref, kat_ref, vat_ref, k16_ref, v16_ref, qi_ref, ga_ref, gb_ref,
                 qb_ref, kb_ref, vb_ref, kw_ref, kw16_ref, kit_ref, h_scr):
    x = x_ref[...]
    ms = jnp.mean(x * x, axis=-1, keepdims=True)
    h_scr[...] = (x * lax.rsqrt(ms + EPS) * g_ref[...]).astype(BF16)

    def mm(lo, hi):
        return _dot(h_scr[...], w_ref[:, lo:hi])

    qa_ref[...] = (mm(_C_QA, _C_KA) * (HEAD_DIM ** -0.5 * LOG2E)).astype(BF16)
    ka = mm(_C_KA, _C_VA)
    kat_ref[0] = ka.T
    k16_ref[...] = ka.astype(BF16)
    va = mm(_C_VA, _C_QI)
    vat_ref[0] = va.T
    v16_ref[...] = va.astype(BF16)
    qi_ref[...] = mm(_C_QI, _C_KW).astype(BF16)
    ga_ref[...] = mm(_C_GA, _C_QB).astype(BF16)
    gb_ref[...] = mm(_C_GB, _C_END).astype(BF16)
    vb_ref[...] = mm(_C_VB, _C_GB).astype(BF16)

    cos = cos_ref[...]
    sin = sin_ref[...]
    lane = lax.broadcasted_iota(I32, cos.shape, 1)
    first_half = (lane % HEAD_DIM) < (HEAD_DIM // 2)

    def rope(z, scale):
        outs = []
        for g in range(z.shape[1] // LANES):
            zg = z[:, g * LANES:(g + 1) * LANES]
            partner = jnp.where(first_half, pltpu.roll(zg, LANES - HEAD_DIM // 2, 1),
                                pltpu.roll(zg, HEAD_DIM // 2, 1))
            r = zg * cos + partner * sin
            if scale != 1.0:
                r = r * scale
            outs.append(r.astype(BF16))
        return jnp.concatenate(outs, axis=1)

    qb_ref[...] = rope(mm(_C_QB, _C_KB), 1.0)
    kb_ref[...] = rope(mm(_C_KB, _C_VB), HEAD_DIM ** -0.5)

    kw = mm(_C_KW, _C_GA)
    wscale = (N_IDX_HEADS ** -0.5) * (D_IDX ** -0.5)
    kw = kw * jnp.where(lane >= D_IDX, wscale, 1.0)
    kw_ref[...] = kw
    kw16_ref[...] = kw.astype(BF16)
    kit_ref[0] = kw.T[:D_IDX]


def _project(x2d, g_pre, w_cat, cos_t, sin_t, tm, seq):
    rows = x2d.shape[0]
    d_model = x2d.shape[1]
    n_tab = cos_t.shape[0] // tm
    tiles = seq // tm
    row_spec = lambda w: pl.BlockSpec((tm, w), lambda i: (i, 0))
    t_spec = lambda w: pl.BlockSpec((1, w, tm), lambda i: (i // tiles, 0, i % tiles))
    outs = [
        ("qa", D_A, BF16), ("ka_t", LANES, F32), ("va_t", LANES, F32), ("k16", LANES, BF16), ("v16", LANES, BF16),
        ("qi", N_IDX_HEADS * D_IDX, BF16), ("ga", D_A, BF16), ("gb", D_B, BF16),
        ("qb", D_B, BF16), ("kb", D_B, BF16), ("vb", D_B, BF16), ("kw", LANES, F32), ("kw16", LANES, BF16),
        ("ki_t", D_IDX, F32),
    ]
    transposed = lambda n: n.endswith("_t")
    res = pl.pallas_call(
        _proj_kernel,
        grid=(rows // tm,),
        in_specs=[
            row_spec(d_model),
            pl.BlockSpec((1, d_model), lambda i: (0, 0)),
            pl.BlockSpec(w_cat.shape, lambda i: (0, 0), pipeline_mode=pl.Buffered(1)),
            pl.BlockSpec((tm, LANES), lambda i: (i % n_tab, 0)),
            pl.BlockSpec((tm, LANES), lambda i: (i % n_tab, 0)),
        ],
        out_specs=[t_spec(w) if transposed(n) else row_spec(w) for n, w, _ in outs],
        out_shape=[jax.ShapeDtypeStruct((rows // seq, w, seq) if transposed(n) else (rows, w), dt)
                   for n, w, dt in outs],
        scratch_shapes=[pltpu.VMEM((tm, d_model), BF16)],
        compiler_params=pltpu.CompilerParams(dimension_semantics=("arbitrary",), vmem_limit_bytes=VMEM_LIMIT),
        name="proj",
    )(x2d, g_pre, w_cat, cos_t, sin_t)
    return {n: r for (n, _, _), r in zip(outs, res)}


def _dsa_prompt_kernel(rb_ref, qi_ref, kwq_ref, qin_ref, kwqn_ref, kw16_ref, qa_ref, k16_ref, v16_ref, oa_ref,
                       sc2, bt, vt, lgs, acc_scr, cut_scr, key_scr, *, topk, search_blk):
    b = pl.program_id(0)
    j = pl.program_id(1)
    nblk = pl.num_programs(1)
    sc = sc2.at[j % 2]
    sc_next = sc2.at[1 - j % 2]
    row = lax.broadcasted_iota(I32, (BLK, BLK), 0)
    col = lax.broadcasted_iota(I32, (BLK, BLK), 1)

    @pl.when((b == 0) & (j == 0))
    def _init_bias():
        for d in range(2):
            bkt = _t5_bucket(col - row + d * BLK)
            for h in range(N_HEADS_A):
                bt[d, h] = (_bias_from_bucket(bkt, rb_ref, h) - rb_ref[N_BUCKETS - 1, h]) * LOG2E
        bt[2] = jnp.zeros(bt.shape[1:], F32)

    @pl.when(j == 0)
    def _clear_values():
        vt[...] = jnp.zeros(vt.shape, BF16)

    vblk = v16_ref[pl.ds(pl.multiple_of(j * BLK, BLK), BLK), :]
    vtj = vblk.astype(F32).T.astype(BF16)
    ones = jnp.ones((ONES_ROWS, BLK), BF16)
    vt[j] = jnp.concatenate([piece for n in range(N_KV_A)
                             for piece in (vtj[n * HEAD_DIM:(n + 1) * HEAD_DIM], ones)], axis=0)

    def indexer_queries(qi_blk_ref, kw_blk_ref):
        qi = qi_blk_ref[...]
        qi_stack = jnp.concatenate([qi[:, h * D_IDX:(h + 1) * D_IDX] for h in range(N_IDX_HEADS)], axis=0)
        return qi_stack, kw_blk_ref[...].T

    def score_products(c, queries):
        out = []
        for u in range(search_blk):
            i = c * search_blk + u
            kib = kw16_ref[pl.ds(pl.multiple_of(i * BLK, BLK), BLK), :][:, :D_IDX]
            out.append(_nt_dot(kib, queries[0]))
        return out

    def score_finish(c, products, dst, queries, first_q):
        wi_t = queries[1]
        for u, s in enumerate(products):
            i = c * search_blk + u
            acc = jnp.zeros((BLK, BLK), F32)
            for h in range(N_IDX_HEADS):
                acc = acc + wi_t[D_IDX + h:D_IDX + h + 1, :] * jnp.maximum(s[:, h * BLK:(h + 1) * BLK], 0.0)
            dst[i] = jnp.where(row + i * BLK <= col + first_q, acc, -jnp.inf)

    def score_group(c, dst, queries, first_q):
        score_finish(c, score_products(c, queries), dst, queries, first_q)

    @pl.when(j == 0)
    def _own_scores():
        score_group(0, sc, indexer_queries(qi_ref, kwq_ref), 0)

    n_chunk = (j + search_blk) // search_blk

    def col_sum(w):
        return jnp.sum(w.reshape(BLK // SUBLANES, SUBLANES, BLK), axis=0)

    qa = qa_ref[...]
    g_per_kv = N_HEADS_A // N_KV_A
    q_stack = [jnp.concatenate([qa[:, (n * g_per_kv + g) * HEAD_DIM:(n * g_per_kv + g + 1) * HEAD_DIM]
                                for g in range(g_per_kv)], axis=0) for n in range(N_KV_A)]

    kf = float(topk)
    n_units = search_blk * N_KV_A

    def search(n_groups):
        def count_ge(cand, unit=None):
            raw = []
            if unit is not None:
                n = unit[1]
                for c in range(n_groups):
                    i = c * search_blk + unit[0]
                    raw.append((i, _nt_dot(k16_ref[i * BLK:(i + 1) * BLK, n * HEAD_DIM:(n + 1) * HEAD_DIM],
                                           q_stack[n])))
            acc = jnp.zeros((SUBLANES, BLK), F32)
            for c in range(n_groups):
                for u in range(search_blk):
                    acc = acc + col_sum(jnp.where(sc[c * search_blk + u] >= cand, 1.0, 0.0))
            for i, lg in raw:
                for g in range(g_per_kv):
                    lgs[i, unit[1] * g_per_kv + g] = lg[:, g * BLK:(g + 1) * BLK]
            return jnp.sum(acc, axis=0, keepdims=True)

        def bit_step(bit, key, unit=None):
            cand = key | jnp.left_shift(jnp.int32(1), bit)
            cnt = count_ge(_key_to_float(cand), unit)
            return jnp.where(cnt >= kf, cand, key)

        c0 = count_ge(jnp.zeros((1, BLK), F32))
        key = jnp.where(c0 >= kf, 0, INT_MIN).astype(I32)
        for it in range(n_units):
            key = bit_step(30 - it, key, (it // N_KV_A, it % N_KV_A))
        n_rest = jnp.where((j + 1) * BLK <= topk, 0, 31 - n_units)
        key = lax.fori_loop(0, n_rest, lambda it, k: bit_step(30 - n_units - it, k), key)
        key_scr[...] = jnp.broadcast_to(key, key_scr.shape)

    for n_groups in range(1, sc.shape[0] // search_blk + 1):
        pl.when(n_chunk == n_groups)(functools.partial(search, n_groups))

    key = jnp.maximum(key_scr[0:1, :], KEY_NEG_FLT_MAX)
    thr = _key_to_float(key)

    def count2_body(c, accs):
        a_gt, a_ge = accs
        for u in range(search_blk):
            t = sc[c * search_blk + u]
            a_gt = a_gt + col_sum(jnp.where(t > thr, 1.0, 0.0))
            a_ge = a_ge + col_sum(jnp.where(t >= thr, 1.0, 0.0))
        return a_gt, a_ge

    z8 = jnp.zeros((SUBLANES, BLK), F32)
    a_gt, a_ge = lax.fori_loop(0, n_chunk, count2_body, (z8, z8))
    cnt_gt = jnp.sum(a_gt, axis=0, keepdims=True)
    cnt_ge = jnp.sum(a_ge, axis=0, keepdims=True)
    need = kf - cnt_gt
    straddle = jnp.max(jnp.where(cnt_ge > kf, 1.0, 0.0))

    n_keys_max = sc.shape[0] * BLK
    cut_scr[...] = jnp.full(cut_scr.shape, 2 * n_keys_max, I32)

    @pl.when(straddle > 0.0)
    def _tie_cut():
        def count_tie(cand):
            def body(i, acc):
                t = sc[i]
                hit = jnp.where(t == thr, jnp.where(row + i * BLK < cand, 1.0, 0.0), 0.0)
                return acc + col_sum(hit)
            acc = lax.fori_loop(0, j + 1, body, z8)
            return jnp.sum(acc, axis=0, keepdims=True)

        def cut_body(it, cut):
            cand = cut | jnp.left_shift(jnp.int32(1), (n_keys_max.bit_length() - 1) - it)
            return jnp.where(count_tie(cand) <= need, cand, cut)

        cut = lax.fori_loop(0, n_keys_max.bit_length(), cut_body, jnp.zeros((1, BLK), I32))
        cut_scr[...] = jnp.broadcast_to(cut, cut_scr.shape)

    cut = cut_scr[0:1, :]

    def col_max(w):
        return jnp.max(w.reshape(BLK // SUBLANES, SUBLANES, BLK), axis=0)

    def mask_body(c, m8, biased):
        m8 = list(m8)
        for u in range(search_blk):
            i = c * search_blk + u
            t = sc[i]
            tie_keep = jnp.where(row + i * BLK < cut, 0.0, -jnp.inf)
            nm = jnp.where(t > thr, 0.0, jnp.where(t == thr, tie_keep, -jnp.inf))
            near = jnp.clip(j - i, 0, 2)
            for h in range(N_HEADS_A):
                x = lgs[i, h] + nm
                if biased:
                    x = x + bt[near, h]
                lgs[i, h] = x
                m8[h] = jnp.maximum(m8[h], col_max(x))
        return tuple(m8)

    n_far = jnp.maximum(n_chunk - 2, 0)
    neg8 = jnp.full((SUBLANES, BLK), NEG_BIG, F32)
    m8 = lax.fori_loop(0, n_far, functools.partial(mask_body, biased=False), (neg8,) * N_HEADS_A)
    m8 = lax.fori_loop(n_far, n_chunk, functools.partial(mask_body, biased=True), m8)
    m_row = [jnp.max(m8[h], axis=0, keepdims=True) for h in range(N_HEADS_A)]

    acc_scr[...] = jnp.zeros(acc_scr.shape, F32)

    vrows = HEAD_DIM + ONES_ROWS

    next_queries = indexer_queries(qin_ref, kwqn_ref)
    next_first_q = (j + 1) * BLK

    def pv_body(c, carry):
        pv = [None] * N_KV_A
        for u in range(search_blk):
            i = c * search_blk + u
            vti = vt[i]
            for n in range(N_KV_A):
                ps = [jnp.exp2(lgs[i, n * g_per_kv + g] - m_row[n * g_per_kv + g]).astype(BF16)
                      for g in range(g_per_kv)]
                t = _dot(vti[n * vrows:(n + 1) * vrows, :], jnp.concatenate(ps, axis=1))
                pv[n] = t if pv[n] is None else pv[n] + t
        for n in range(N_KV_A):
            acc_scr[n] += pv[n]
        score_group(c, sc_next, next_queries, next_first_q)
        return carry

    lax.fori_loop(0, n_chunk, pv_body, 0)

    @pl.when(((j + 1) % search_blk == 0) & (j + 1 < nblk))
    def _next_reaches_one_more_group():
        score_group(n_chunk, sc_next, next_queries, next_first_q)

    parts = []
    for n in range(N_KV_A):
        acc = acc_scr[n]
        out = acc[:HEAD_DIM] * (1.0 / acc[HEAD_DIM:HEAD_DIM + 1])
        parts += [out[:, g * BLK:(g + 1) * BLK] for g in range(g_per_kv)]
    oa_ref[...] = jnp.concatenate(parts, axis=0).T.astype(BF16)


def _dsa_prompt(p, rel_bias, batch, seq):
    nblk = seq // BLK
    topk = min(TOPK_MAX, seq // 4)
    search_blk = 4 if nblk % 4 == 0 else 1
    qrow = lambda w: pl.BlockSpec((BLK, w), lambda b, j: (b * nblk + j, 0))
    qnext = lambda w: pl.BlockSpec((BLK, w), lambda b, j: (b * nblk + jnp.minimum(j + 1, nblk - 1), 0))
    brow = lambda w: pl.BlockSpec((seq, w), lambda b, j: (b, 0))
    return pl.pallas_call(
        functools.partial(_dsa_prompt_kernel, topk=topk, search_blk=search_blk),
        grid=(batch, nblk),
        in_specs=[
            pl.BlockSpec(memory_space=pltpu.SMEM),
            qrow(N_IDX_HEADS * D_IDX), qrow(LANES), qnext(N_IDX_HEADS * D_IDX), qnext(LANES),
            brow(LANES), qrow(D_A), brow(LANES), brow(LANES),
        ],
        out_specs=qrow(D_A),
        out_shape=jax.ShapeDtypeStruct((batch * seq, D_A), BF16),
        scratch_shapes=[
            pltpu.VMEM((2, nblk, BLK, BLK), F32),
            pltpu.VMEM((3, N_HEADS_A, BLK, BLK), F32),
            pltpu.VMEM((nblk, N_KV_A * (HEAD_DIM + ONES_ROWS), BLK), BF16),
            pltpu.VMEM((nblk, N_HEADS_A, BLK, BLK), F32),
            pltpu.VMEM((N_KV_A, HEAD_DIM + ONES_ROWS, (N_HEADS_A // N_KV_A) * BLK), F32),
            pltpu.VMEM((SUBLANES, BLK), I32),
            pltpu.VMEM((SUBLANES, BLK), I32),
        ],
        compiler_params=pltpu.CompilerParams(dimension_semantics=("arbitrary", "arbitrary"),
                                             vmem_limit_bytes=VMEM_LIMIT),
        name="dsa_prompt",
    )(rel_bias, p["qi"], p["kw"], p["qi"], p["kw"], p["kw16"], p["qa"], p["k16"], p["v16"])


def _retention_kernel(q_ref, k_ref, v_ref, s0_ref, dmat_ref, din_ref, dout_ref, decm_ref, bdm_ref,
                      yn_ref, s_out_ref, s_scr, *, n_sub, pairs_per_phase):
    c = pl.program_id(1)
    chunk = din_ref.shape[0]

    @pl.when(c == 0)
    def _load_state():
        z = jnp.zeros((HEAD_DIM, HEAD_DIM), F32)
        for p in range(N_HEADS_B // 2):
            pair = jnp.concatenate([jnp.concatenate([s0_ref[0, 2 * p], z], axis=1),
                                    jnp.concatenate([z, s0_ref[0, 2 * p + 1]], axis=1)], axis=0)
            s_scr[p] = pair.T

    lo = lax.broadcasted_iota(I32, (chunk, LANES), 1) < HEAD_DIM
    zero = jnp.zeros((), BF16)
    n_pairs = N_HEADS_B // 2
    lanes = [slice(p * LANES, (p + 1) * LANES) for p in range(n_pairs)]
    rows = [slice(u * chunk, (u + 1) * chunk) for u in range(n_sub)]

    def run(pairs):
        units = [(p, u) for p in pairs for u in range(n_sub)]
        qd, kd, vt, s_a, s_b, intra, upd, inter = {}, {}, {}, {}, {}, {}, {}, {}
        for p, u in units:
            q, k, v = q_ref[0, rows[u], lanes[p]], k_ref[0, rows[u], lanes[p]], v_ref[0, rows[u], lanes[p]]
            qd[p, u] = (q.astype(F32) * din_ref[:, lanes[p]]).astype(BF16)
            kd[p, u] = (k.astype(F32) * dout_ref[:, lanes[p]]).astype(BF16)
            vt[p, u] = v.astype(F32).T.astype(BF16)
            s_a[p, u] = (_nt_dot(jnp.where(lo, k, zero), q) * dmat_ref[2 * p]).astype(BF16)
            s_b[p, u] = (_nt_dot(jnp.where(lo, zero, k), q) * dmat_ref[2 * p + 1]).astype(BF16)
        for p, u in units:
            intra[p, u] = jnp.concatenate([_dot(vt[p, u][:HEAD_DIM], s_a[p, u]),
                                           _dot(vt[p, u][HEAD_DIM:], s_b[p, u])], axis=0)
            upd[p, u] = _dot(vt[p, u], kd[p, u]) * bdm_ref[...]
        for p in pairs:
            st = s_scr[p]
            for u in range(n_sub):
                inter[p, u] = _nt_dot(st.astype(BF16), qd[p, u])
                st = st * decm_ref[p] + upd[p, u]
            s_scr[p] = st
            pair = st.T
            s_out_ref[0, 2 * p] = pair[:HEAD_DIM, :HEAD_DIM]
            s_out_ref[0, 2 * p + 1] = pair[HEAD_DIM:, HEAD_DIM:]
        for p, u in units:
            o = (inter[p, u] + intra[p, u]).reshape(2, HEAD_DIM, chunk)
            dlt = o - jnp.mean(o, axis=1, keepdims=True)
            var = jnp.mean(dlt * dlt, axis=1, keepdims=True)
            yn_ref[0, rows[u], lanes[p]] = (dlt * lax.rsqrt(var + EPS)).reshape(LANES, chunk).T.astype(BF16)

    for group in range(0, n_pairs, pairs_per_phase):
        run(range(group, group + pairs_per_phase))


def _ret_tables(chunk, n_real):
    h = N_HEADS_B
    log_g = jnp.log1p(-jnp.exp2(-5.0 - jnp.arange(h, dtype=F32)))
    i = jnp.arange(chunk, dtype=F32)
    diff = i[:, None] - i[None, :]
    dmat = jnp.where(diff >= 0, jnp.exp(jnp.maximum(diff, 0.0)[None] * log_g[:, None, None]), 0.0)
    dmat_t = jnp.swapaxes(dmat, 1, 2)
    dec_in = jnp.exp((i + 1.0)[:, None] * log_g[None, :])
    dec_out = jnp.exp((n_real - 1.0 - i)[:, None] * log_g[None, :])
    dec_chunk = jnp.exp(n_real * log_g)
    din = jnp.repeat(dec_in, HEAD_DIM, axis=1)
    dout = jnp.repeat(dec_out, HEAD_DIM, axis=1)
    head_of = np.arange(LANES) // HEAD_DIM
    bd = jnp.asarray((head_of[:, None] == head_of[None, :]).astype(np.float32))
    dc = jnp.repeat(dec_chunk, HEAD_DIM).reshape(h // 2, LANES)
    decm = dc[:, :, None] * bd[None]
    return dmat_t, din, dout, decm, bd


def _retention(q3, k3, v3, s0, n_real):
    batch, seq, _ = q3.shape
    chunk = min(RET_CHUNK, seq)
    n_sub = _largest_divisor(seq // chunk, 8)
    dmat_t, din, dout, decm, bd = _ret_tables(chunk, n_real)
    tok = pl.BlockSpec((1, n_sub * chunk, D_B), lambda b, c: (b, c, 0))
    st = pl.BlockSpec((1,) + s0.shape[1:], lambda b, c: (b, 0, 0, 0))
    full = lambda a: pl.BlockSpec(a.shape, lambda b, c: (0,) * a.ndim)
    return pl.pallas_call(
        functools.partial(_retention_kernel, n_sub=n_sub,
                          pairs_per_phase=max(1, min(N_HEADS_B // 2, 8 // n_sub))),
        grid=(batch, seq // (n_sub * chunk)),
        in_specs=[tok, tok, tok, st, full(dmat_t), full(din), full(dout), full(decm), full(bd)],
        out_specs=[tok, st],
        out_shape=[jax.ShapeDtypeStruct((batch, seq, D_B), BF16),
                   jax.ShapeDtypeStruct(s0.shape, F32)],
        scratch_shapes=[pltpu.VMEM((N_HEADS_B // 2, LANES, LANES), F32)],
        compiler_params=pltpu.CompilerParams(dimension_semantics=("arbitrary", "arbitrary"),
                                             vmem_limit_bytes=VMEM_LIMIT),
        name="retention",
    )(q3, k3, v3, s0, dmat_t, din, dout, decm, bd)


def _merge_kernel(x_ref, oa_ref, yn_ref, ga_ref, gb_ref, p_ref, wo_ref, gp_ref, wup_ref, wg_ref, out_ref):
    ga = ga_ref[...].astype(F32)
    gb = gb_ref[...].astype(F32)
    ya = (ga * _sigmoid(ga) * oa_ref[...].astype(F32)).astype(BF16)
    yb = (gb * _sigmoid(gb) * yn_ref[...].astype(F32)).astype(BF16)
    y = _dot(ya, wo_ref[0:D_A, :]) + _dot(yb, wo_ref[D_A:D_A + D_B, :])
    ms = jnp.mean(y * y, axis=-1, keepdims=True)
    x1 = x_ref[...] + y * lax.rsqrt(ms + EPS) * gp_ref[...]
    ple = _dot(p_ref[...].astype(BF16), wup_ref[...])
    gate = _sigmoid(_dot(x1.astype(BF16), wg_ref[...]))
    out_ref[...] = x1 + ple * gate


def _merge(x2d, oa, yn, ga, gb, p2d, wo16, g_post, wup16, wg16, tm):
    rows, d_model = x2d.shape
    row = lambda w: pl.BlockSpec((tm, w), lambda i: (i, 0))
    full = lambda a: pl.BlockSpec(a.shape, lambda i: (0, 0))
    return pl.pallas_call(
        _merge_kernel,
        grid=(rows // tm,),
        in_specs=[row(d_model), row(D_A), row(D_B), row(D_A), row(D_B), row(p2d.shape[1]),
                  full(wo16), full(g_post), full(wup16), full(wg16)],
        out_specs=row(d_model),
        out_shape=jax.ShapeDtypeStruct((rows, d_model), F32),
        compiler_params=pltpu.CompilerParams(dimension_semantics=("arbitrary",), vmem_limit_bytes=VMEM_LIMIT),
        name="merge",
    )(x2d, oa, yn, ga, gb, p2d, wo16, g_post, wup16, wg16)


def _page_copy(pages_hbm, buf, sem, pt_ref, b, p, slot):
    return pltpu.make_async_copy(pages_hbm.at[pt_ref[b, p]], buf.at[slot, p], sem.at[slot])


def _start_batch_pages(streams, pt_ref, b, slot, n_pages):
    def body(p, c):
        for s, (pages_hbm, buf, sem) in enumerate(streams):
            _page_copy(pages_hbm, buf, sem, pt_ref, b, p, slot).start(priority=s % N_DMA_THREADS)
        return c
    lax.fori_loop(0, n_pages, body, 0)


def _wait_batch_pages(streams, pt_ref, b, slot, n_pages):
    for pages_hbm, buf, sem in streams:
        for p in range(n_pages):
            _page_copy(pages_hbm, buf, sem, pt_ref, b, p, slot).wait()


def _sample_index_kernel(pt_ref, qi_ref, w_ref, pages_hbm, out_ref, buf, sem, *, n_pages, unroll):
    b = pl.program_id(0)
    slot = b % 2
    streams = [(pages_hbm, buf, sem)]

    @pl.when(b == 0)
    def _first():
        _start_batch_pages(streams, pt_ref, 0, 0, n_pages)

    _wait_batch_pages(streams, pt_ref, b, slot, n_pages)

    qi = qi_ref[0]
    w = w_ref[0]
    n_q = qi.shape[0] // N_IDX_HEADS

    def run(prefetch):
        def body(c, carry):
            for u in range(unroll):
                p = c * unroll + u
                if prefetch:
                    _page_copy(pages_hbm, buf, sem, pt_ref, b + 1, p, 1 - slot).start(priority=u % N_DMA_THREADS)
                s = _dot(qi, buf[slot, p].astype(BF16))
                r = jnp.maximum(s, 0.0) * w
                out_ref[0, c, :, u * PAGE_SIZE:(u + 1) * PAGE_SIZE] = jnp.sum(
                    r.reshape(n_q, N_IDX_HEADS, PAGE_SIZE), axis=1)
            return carry
        lax.fori_loop(0, n_pages // unroll, body, 0)

    has_next = b + 1 < pl.num_programs(0)
    pl.when(has_next)(lambda: run(True))
    pl.when(jnp.logical_not(has_next))(lambda: run(False))


def _sample_index(page_table, qi_qh, w_qh, kidx_t, unroll):
    batch, n_pages = page_table.shape
    n_q = qi_qh.shape[1] // N_IDX_HEADS
    groups = n_pages // unroll
    grid_spec = pltpu.PrefetchScalarGridSpec(
        num_scalar_prefetch=1,
        grid=(batch,),
        in_specs=[pl.BlockSpec((1,) + qi_qh.shape[1:], lambda b, pt: (b, 0, 0)),
                  pl.BlockSpec((1,) + w_qh.shape[1:], lambda b, pt: (b, 0, 0)),
                  pl.BlockSpec(memory_space=pl.ANY)],
        out_specs=pl.BlockSpec((1, groups, n_q, unroll * PAGE_SIZE), lambda b, pt: (b, 0, 0, 0)),
        scratch_shapes=[pltpu.VMEM((2, n_pages) + kidx_t.shape[1:], kidx_t.dtype),
                        pltpu.SemaphoreType.DMA((2,))],
    )
    return pl.pallas_call(
        functools.partial(_sample_index_kernel, n_pages=n_pages, unroll=unroll),
        grid_spec=grid_spec,
        out_shape=jax.ShapeDtypeStruct((batch, groups, n_q, unroll * PAGE_SIZE), F32),
        compiler_params=pltpu.CompilerParams(dimension_semantics=("arbitrary",), vmem_limit_bytes=VMEM_LIMIT),
        name="sample_index",
    )(page_table, qi_qh, w_qh, kidx_t)


def _sample_select_kernel(sp_ref, qi_ref, w_ref, kw16_ref, thr_ref, cut_ref, nmn_ref, snew_scr,
                          *, topk, n_q, lane_chunk):
    rows, past = sp_ref.shape
    r_i = lax.broadcasted_iota(I32, (rows, LANES), 0)
    l_i = lax.broadcasted_iota(I32, (rows, LANES), 1)

    s = _nt_dot(qi_ref[...], kw16_ref[...][:, :D_IDX])
    r = jnp.maximum(s, 0.0) * w_ref[...]
    s_new = jnp.sum(r.reshape(rows, N_IDX_HEADS, LANES), axis=1)
    ok = ((l_i // n_q) == (r_i // n_q)) & ((l_i % n_q) <= (r_i % n_q))
    snew_scr[...] = jnp.where(ok, s_new, -jnp.inf)

    n_chunks = past // lane_chunk
    tiles = lane_chunk // LANES
    kf = float(topk)

    def count(ind):
        acc = ind(snew_scr[...], l_i + past)
        for ch in range(n_chunks):
            x = sp_ref[:, ch * lane_chunk:(ch + 1) * lane_chunk]
            for t in range(tiles):
                idx = l_i + (ch * lane_chunk + t * LANES)
                acc = acc + ind(x[:, t * LANES:(t + 1) * LANES], idx)
        return jnp.broadcast_to(jnp.sum(acc, axis=1, keepdims=True), (rows, LANES))

    c0 = count(lambda x, idx: jnp.where(x >= 0.0, 1.0, 0.0))
    key0 = jnp.where(c0 >= kf, 0, INT_MIN).astype(I32)

    def bit_body(it, key):
        cand = key | jnp.left_shift(jnp.int32(1), 30 - it)
        cf = _key_to_float(cand)
        return jnp.where(count(lambda x, idx: jnp.where(x >= cf, 1.0, 0.0)) >= kf, cand, key)

    key = lax.fori_loop(0, 31, bit_body, key0)
    key = jnp.maximum(key, KEY_NEG_FLT_MAX)
    thr = _key_to_float(key)
    cnt_gt = count(lambda x, idx: jnp.where(x > thr, 1.0, 0.0))
    cnt_ge = count(lambda x, idx: jnp.where(x >= thr, 1.0, 0.0))
    need = kf - cnt_gt
    straddle = jnp.max(jnp.where(cnt_ge > kf, 1.0, 0.0))

    idx_bits = (past + LANES).bit_length()
    thr_ref[...] = thr
    cut_ref[...] = jnp.full((rows, LANES), 1 << idx_bits, I32)

    @pl.when(straddle > 0.0)
    def _tie_cut():
        def cut_body(it, cut):
            cand = cut | jnp.left_shift(jnp.int32(1), (idx_bits - 1) - it)
            cnt = count(lambda x, idx: jnp.where(x == thr, jnp.where(idx < cand, 1.0, 0.0), 0.0))
            return jnp.where(cnt <= need, cand, cut)
        cut_ref[...] = lax.fori_loop(0, idx_bits, cut_body, jnp.zeros((rows, LANES), I32))

    cut = cut_ref[...]
    sn = snew_scr[...]
    tie_keep = jnp.where(l_i + past < cut, 0.0, -jnp.inf)
    nmn_ref[...] = jnp.where(sn > thr, 0.0, jnp.where(sn == thr, tie_keep, -jnp.inf))


def _sample_select(scores_past, qi_rows, w_rows, kw16_s, topk, n_q):
    rows, past = scores_past.shape
    lane_chunk = 2048 if past % 2048 == 0 else LANES
    full = lambda a: pl.BlockSpec(a.shape, lambda i: (0,) * a.ndim)
    o = jax.ShapeDtypeStruct((rows, LANES), F32)
    return pl.pallas_call(
        functools.partial(_sample_select_kernel, topk=topk, n_q=n_q, lane_chunk=lane_chunk),
        grid=(1,),
        in_specs=[full(scores_past), full(qi_rows), full(w_rows), full(kw16_s)],
        out_specs=[pl.BlockSpec((rows, LANES), lambda i: (0, 0))] * 3,
        out_shape=[o, jax.ShapeDtypeStruct((rows, LANES), I32), o],
        scratch_shapes=[pltpu.VMEM((rows, LANES), F32)],
        compiler_params=pltpu.CompilerParams(dimension_semantics=("arbitrary",), vmem_limit_bytes=VMEM_LIMIT),
        name="sample_select",
    )(scores_past, qi_rows, w_rows, kw16_s)


def _sample_attn_kernel(pt_ref, rbc_ref, q_ref, sp_ref, thr_ref, cut_ref, nmn_ref, k16n_ref, v16n_ref,
                        k_hbm, v_hbm, o_ref, kbuf, vbuf, ksem, vsem, lgs, bias_scr, mx_scr,
                        *, n_pages, n_q, unroll):
    b = pl.program_id(0)
    slot = b % 2
    rows = n_q * N_HEADS_A
    groups = n_pages // unroll
    streams = [(k_hbm, kbuf, ksem), (v_hbm, vbuf, vsem)]
    lane = lax.broadcasted_iota(I32, (rows, LANES), 1)
    rq = lax.broadcasted_iota(I32, (rows, LANES), 0) // N_HEADS_A
    expand = lambda a: jnp.concatenate(
        [jnp.broadcast_to(a[t:t + 1], (N_HEADS_A, LANES)) for t in range(n_q)], axis=0)

    @pl.when(b == 0)
    def _first():
        _start_batch_pages(streams, pt_ref, 0, 0, n_pages)
        far = jnp.concatenate([rbc_ref[N_BUCKETS - 1]] * n_q, axis=0)
        for t, dist in enumerate((PAGE_SIZE + rq - lane, rq - lane % n_q)):
            bkt = _t5_bucket(dist)
            tile = jnp.zeros((rows, LANES), F32)
            for k in range(N_BUCKETS - 1):
                tile = jnp.where(bkt == k, jnp.concatenate([rbc_ref[k]] * n_q, axis=0) - far, tile)
            bias_scr[t] = tile * LOG2E

    _wait_batch_pages(streams, pt_ref, b, slot, n_pages)

    thr = expand(thr_ref[0])
    cut = expand(cut_ref[0])
    q = q_ref[0]

    def logits_pass(prefetch):
        def body(c, mx):
            for u in range(unroll):
                p = c * unroll + u
                if prefetch:
                    for s, (pages_hbm, buf, sem) in enumerate(streams):
                        _page_copy(pages_hbm, buf, sem, pt_ref, b + 1, p, 1 - slot).start(
                            priority=s % N_DMA_THREADS)
                lg = _dot(q, kbuf[slot, p].astype(BF16))
                sc = expand(sp_ref[0, c, :, u * PAGE_SIZE:(u + 1) * PAGE_SIZE])
                tie_keep = jnp.where(lane + p * PAGE_SIZE < cut, 0.0, -jnp.inf)
                x = lg + jnp.where(sc > thr, 0.0, jnp.where(sc == thr, tie_keep, -jnp.inf))
                if u == unroll - 1:
                    x = x + bias_scr[0] * jnp.where(c == groups - 1, 1.0, 0.0)
                lgs[p] = x
                mx = jnp.maximum(mx, x)
            return mx
        mx_scr[...] = lax.fori_loop(0, groups, body, jnp.full((rows, LANES), NEG_BIG, F32))

    has_next = b + 1 < pl.num_programs(0)
    pl.when(has_next)(lambda: logits_pass(True))
    pl.when(jnp.logical_not(has_next))(lambda: logits_pass(False))

    xn = _nt_dot(q, k16n_ref[...]) + bias_scr[1] + expand(nmn_ref[0])
    m = jnp.max(jnp.maximum(mx_scr[...], xn), axis=1, keepdims=True)

    def pv_body(c, carry):
        lsum, acc = carry
        for u in range(unroll):
            p = c * unroll + u
            pr = jnp.exp2(lgs[p] - m)
            lsum = lsum + pr
            acc = acc + _nt_dot(pr.astype(BF16), vbuf[slot, p].astype(BF16))
        return lsum, acc

    pn = jnp.exp2(xn - m)
    lsum, acc = lax.fori_loop(0, groups, pv_body, (pn, _dot(pn.astype(BF16), v16n_ref[...])))
    out = acc * (1.0 / jnp.sum(lsum, axis=1, keepdims=True))
    head = lax.broadcasted_iota(I32, (rows, HEAD_DIM), 0) % N_HEADS_A
    o_ref[0] = jnp.where(head < N_HEADS_A // N_KV_A, out[:, :HEAD_DIM], out[:, HEAD_DIM:])


def _sample_attn(page_table, rb_col, q_bd, scores4, thr3, cut3, nmn3, k16n, v16n, k_pages_t, v_pages_t, unroll):
    batch, n_pages = page_table.shape
    n_q = scores4.shape[2]
    rows = q_bd.shape[1]
    per_b = lambda a: pl.BlockSpec((1,) + a.shape[1:], lambda b, pt: (b,) + (0,) * (a.ndim - 1))
    full = lambda a: pl.BlockSpec(a.shape, lambda b, pt: (0,) * a.ndim)
    page_buf = pltpu.VMEM((2, n_pages) + k_pages_t.shape[1:], k_pages_t.dtype)
    grid_spec = pltpu.PrefetchScalarGridSpec(
        num_scalar_prefetch=1,
        grid=(batch,),
        in_specs=[full(rb_col), per_b(q_bd), per_b(scores4), per_b(thr3), per_b(cut3), per_b(nmn3),
                  full(k16n), full(v16n), pl.BlockSpec(memory_space=pl.ANY), pl.BlockSpec(memory_space=pl.ANY)],
        out_specs=pl.BlockSpec((1, rows, HEAD_DIM), lambda b, pt: (b, 0, 0)),
        scratch_shapes=[page_buf, page_buf, pltpu.SemaphoreType.DMA((2,)), pltpu.SemaphoreType.DMA((2,)),
                        pltpu.VMEM((n_pages, rows, LANES), F32), pltpu.VMEM((2, rows, LANES), F32),
                        pltpu.VMEM((rows, LANES), F32)],
    )
    return pl.pallas_call(
        functools.partial(_sample_attn_kernel, n_pages=n_pages, n_q=n_q, unroll=unroll),
        grid_spec=grid_spec,
        out_shape=jax.ShapeDtypeStruct((batch, rows, HEAD_DIM), F32),
        compiler_params=pltpu.CompilerParams(dimension_semantics=("arbitrary",),
                                             vmem_limit_bytes=VMEM_LIMIT_PAGED),
        name="sample_attn",
    )(page_table, rb_col, q_bd, scores4, thr3, cut3, nmn3, k16n, v16n, k_pages_t, v_pages_t)


def _rope_tables(pos):
    half = HEAD_DIM // 2
    inv = ROPE_BASE ** (-jnp.arange(half, dtype=F32) / half)
    ang = pos.astype(F32)[:, None] * inv[None, :]
    cos = jnp.cos(ang)
    sin = jnp.sin(ang)
    reps = LANES // HEAD_DIM
    cos_t = jnp.tile(jnp.concatenate([cos, cos], axis=1), (1, reps))
    sin_t = jnp.tile(jnp.concatenate([-sin, sin], axis=1), (1, reps))
    return cos_t, sin_t


def _cat_weight(w_in):
    split = int(np.sum(SPLIT_SIZES[:6]))
    pad = jnp.zeros((w_in.shape[0], LANES - D_IDX - N_IDX_HEADS), BF16)
    return jnp.concatenate([w_in[:, :split].astype(BF16), pad, w_in[:, split:].astype(BF16)], axis=1)


def _pages_t(cache):
    pool, page = cache.shape[:2]
    return jnp.transpose(cache, (0, 2, 3, 1)).reshape(pool, -1, page)


def _largest_divisor(n, cap):
    d = cap
    while n % d:
        d //= 2
    return d


def kernel(x_prompt, x_sample, cache_k, cache_v, cache_kidx, state_ret, page_table, p_prompt, p_sample,
           rel_bias, w_in, w_out, g_pre, g_post, w_ple_up, w_ple_gate):
    batch, seq, d_model = x_prompt.shape
    dec_b, dec_t, _ = x_sample.shape
    depth = w_in.shape[0]
    n_pages = page_table.shape[1]
    past = n_pages * PAGE_SIZE
    rows_s = dec_b * dec_t
    assert depth == 1 and rows_s == LANES and seq % BLK == 0

    w_cat = _cat_weight(w_in[0])
    gpre = g_pre[0].reshape(1, d_model)
    gpost = g_post[0].reshape(1, d_model)
    wo16 = w_out[0].astype(BF16)
    wup16 = w_ple_up[0].astype(BF16)
    wg16 = w_ple_gate[0].astype(BF16)

    tm = _largest_divisor(seq, 1024)
    cos_p, sin_p = _rope_tables(jnp.arange(seq))
    xp2 = x_prompt.reshape(batch * seq, d_model)
    pp = _project(xp2, gpre, w_cat, cos_p, sin_p, tm, seq)
    oa_p = _dsa_prompt(pp, rel_bias, batch, seq)
    r3 = lambda a: a.reshape(batch, seq, D_B)
    s0_p = jnp.zeros((batch, N_HEADS_B, HEAD_DIM, HEAD_DIM), F32)
    yn_p, state_p = _retention(r3(pp["qb"]), r3(pp["kb"]), r3(pp["vb"]), s0_p, float(min(RET_CHUNK, seq)))
    y_prompt = _merge(xp2, oa_p, yn_p.reshape(batch * seq, D_B), pp["ga"], pp["gb"],
                      p_prompt[0].reshape(batch * seq, -1), wo16, gpost, wup16, wg16, tm)

    pos_s = past + jnp.arange(dec_t)
    cos_s, sin_s = _rope_tables(jnp.tile(pos_s, dec_b))
    xs2 = x_sample.reshape(rows_s, d_model)
    ps = _project(xs2, gpre, w_cat, cos_s, sin_s, rows_s, rows_s)
    topk_s = min(TOPK_MAX, (past + dec_t) // 4)

    qi_rows = ps["qi"].reshape(rows_s * N_IDX_HEADS, D_IDX)
    w_rows = jnp.broadcast_to(ps["kw"][:, D_IDX:D_IDX + N_IDX_HEADS].reshape(rows_s * N_IDX_HEADS, 1),
                              (rows_s * N_IDX_HEADS, LANES))
    unroll = _largest_divisor(n_pages, 64)
    scores4 = _sample_index(page_table, qi_rows.reshape(dec_b, dec_t * N_IDX_HEADS, D_IDX),
                            w_rows.reshape(dec_b, dec_t * N_IDX_HEADS, LANES),
                            jnp.transpose(cache_kidx[0], (0, 2, 1)), unroll)
    scores_past = scores4.transpose(0, 2, 1, 3).reshape(rows_s, past)
    thr, cut, nmn = _sample_select(scores_past, qi_rows, w_rows, ps["kw16"], topk_s, dec_t)

    kv_of_head = np.arange(N_HEADS_A) // (N_HEADS_A // N_KV_A)
    place = jnp.asarray((kv_of_head[:, None] == np.arange(N_KV_A)[None, :]).astype(np.float32)).astype(BF16)
    q_bd = (ps["qa"].reshape(dec_b, dec_t, N_HEADS_A, 1, HEAD_DIM) * place[None, None, :, :, None]).reshape(
        dec_b, dec_t * N_HEADS_A, N_KV_A * HEAD_DIM)
    rb_col = jnp.broadcast_to(rel_bias[:, :, None], rel_bias.shape + (LANES,))
    b3 = lambda a: a.reshape(dec_b, dec_t, LANES)
    o_qh = _sample_attn(page_table, rb_col, q_bd, scores4, b3(thr), b3(cut), b3(nmn),
                        ps["k16"], ps["v16"], _pages_t(cache_k[0]), _pages_t(cache_v[0]), unroll)
    oa_s = o_qh.reshape(rows_s, D_A)

    chunk_s = RET_CHUNK
    padt = lambda a: jnp.pad(a.reshape(dec_b, dec_t, D_B), ((0, 0), (0, chunk_s - dec_t), (0, 0)))
    yn_s, state_s = _retention(padt(ps["qb"]), padt(ps["kb"]), padt(ps["vb"]),
                               state_ret[0].astype(F32), float(math.gcd(dec_t, RET_CHUNK)))
    y_sample = _merge(xs2, oa_s, yn_s[:, :dec_t].reshape(rows_s, D_B), ps["ga"], ps["gb"],
                      p_sample[0].reshape(rows_s, -1), wo16, gpost, wup16, wg16, rows_s)

    def kv(a_t, b, t):
        a = a_t.reshape(a_t.shape[0], N_KV_A, HEAD_DIM, -1).transpose(0, 3, 1, 2)
        return a.reshape(1, b, t, N_KV_A, HEAD_DIM)

    def ki(a_t, b, t):
        return a_t.transpose(0, 2, 1).reshape(1, b, t, D_IDX)

    return (
        y_prompt.reshape(batch, seq, d_model),
        y_sample.reshape(dec_b, dec_t, d_model),
        kv(pp["ka_t"], batch, seq), kv(pp["va_t"], batch, seq),
        ki(pp["ki_t"], batch, seq),
        state_p[None].astype(state_ret.dtype),
        kv(ps["ka_t"], dec_b, dec_t), kv(ps["va_t"], dec_b, dec_t),
        ki(ps["ki_t"], dec_b, dec_t),
        state_s[None].astype(state_ret.dtype),
    )
```

```python
import functools
import math

import jax
import jax.numpy as jnp
import numpy as np
from jax import lax
from jax.experimental import pallas as pl
from jax.experimental.pallas import tpu as pltpu

F32 = jnp.float32
BF16 = jnp.bfloat16
I32 = jnp.int32

HEAD_DIM = 64
N_HEADS_A = 8
N_KV_A = 2
N_IDX_HEADS = 8
D_IDX = 64
TOPK_MAX = 256
N_HEADS_B = 8
D_A = N_HEADS_A * HEAD_DIM
D_B = N_HEADS_B * HEAD_DIM
N_BUCKETS = 32
MAX_DISTANCE = 128
ROPE_BASE = 10000.0
RET_CHUNK = 128
PAGE_SIZE = 128
EPS = 1e-6
SPLIT_SIZES = (D_A, N_KV_A * HEAD_DIM, N_KV_A * HEAD_DIM, N_IDX_HEADS * D_IDX, D_IDX, N_IDX_HEADS,
               D_A, D_B, D_B, D_B, D_B)

LANES = 128
SUBLANES = 8
BLK = 128
VMEM_LIMIT = 48 * 1024 * 1024
VMEM_LIMIT_PAGED = 56 * 1024 * 1024

INT_MIN = -(2 ** 31)
KEY_NEG_FLT_MAX = INT_MIN + 0x00800000
NEG_BIG = -1e30
LOG2E = math.log2(math.e)
ONES_ROWS = 16

_C_QA, _C_KA, _C_VA, _C_QI, _C_KW, _C_GA, _C_QB, _C_KB, _C_VB, _C_GB, _C_END = (
    0, 512, 640, 768, 1280, 1408, 1920, 2432, 2944, 3456, 3968)


def _nt_dot(a, b):
    return lax.dot_general(a, b, (((1,), (1,)), ((), ())), preferred_element_type=F32)


def _dot(a, b):
    return jnp.dot(a, b, preferred_element_type=F32)


def _sigmoid(x):
    return 1.0 / (1.0 + jnp.exp(-x))


def _key_to_float(k):
    bits = jnp.where(k >= 0, k, k ^ 0x7FFFFFFF)
    return lax.bitcast_convert_type(bits, F32)


def _t5_bucket(n):
    max_exact = N_BUCKETS // 2
    n = jnp.maximum(n, 0)
    nf = jnp.maximum(n, 1).astype(F32)
    large = max_exact + jnp.floor(jnp.log(nf / max_exact) / math.log(MAX_DISTANCE / max_exact)
                                  * (N_BUCKETS - max_exact)).astype(I32)
    large = jnp.minimum(large, N_BUCKETS - 1)
    return jnp.where(n < max_exact, n, large)


def _bias_from_bucket(bkt, rb_ref, h):
    out = jnp.zeros(bkt.shape, F32)
    for k in range(N_BUCKETS):
        out = jnp.where(bkt == k, rb_ref[k, h], out)
    return out


def _proj_kernel(x_ref, g_ref, w_ref, cos_ref, sin_ref,
                 qa_ref, kat_ref, vat_ref, k16_ref, v16_ref, qi_ref, ga_ref, gb_ref,
                 qb_ref, kb_ref, vb_ref, kw_ref, kw16_ref, kit_ref, h_scr):
    x = x_ref[...]
    ms = jnp.mean(x * x, axis=-1, keepdims=True)
    h_scr[...] = (x * lax.rsqrt(ms + EPS) * g_ref[...]).astype(BF16)

    def mm(lo, hi):
        return _dot(h_scr[...], w_ref[:, lo:hi])

    qa_ref[...] = (mm(_C_QA, _C_KA) * (HEAD_DIM ** -0.5 * LOG2E)).astype(BF16)
    ka = mm(_C_KA, _C_VA)
    kat_ref[0] = ka.T
    k16_ref[...] = ka.astype(BF16)
    va = mm(_C_VA, _C_QI)
    vat_ref[0] = va.T
    v16_ref[...] = va.astype(BF16)
    qi_ref[...] = mm(_C_QI, _C_KW).astype(BF16)
    ga_ref[...] = mm(_C_GA, _C_QB).astype(BF16)
    gb_ref[...] = mm(_C_GB, _C_END).astype(BF16)
    vb_ref[...] = mm(_C_VB, _C_GB).astype(BF16)

    cos = cos_ref[...]
    sin = sin_ref[...]
    lane = lax.broadcasted_iota(I32, cos.shape, 1)
    first_half = (lane % HEAD_DIM) < (HEAD_DIM // 2)

    def rope(z, scale):
        outs = []
        for g in range(z.shape[1] // LANES):
            zg = z[:, g * LANES:(g + 1) * LANES]
            partner = jnp.where(first_half, pltpu.roll(zg, LANES - HEAD_DIM // 2, 1),
                                pltpu.roll(zg, HEAD_DIM // 2, 1))
            r = zg * cos + partner * sin
            if scale != 1.0:
                r = r * scale
            outs.append(r.astype(BF16))
        return jnp.concatenate(outs, axis=1)

    qb_ref[...] = rope(mm(_C_QB, _C_KB), 1.0)
    kb_ref[...] = rope(mm(_C_KB, _C_VB), HEAD_DIM ** -0.5)

    kw = mm(_C_KW, _C_GA)
    wscale = (N_IDX_HEADS ** -0.5) * (D_IDX ** -0.5)
    kw = kw * jnp.where(lane >= D_IDX, wscale, 1.0)
    kw_ref[...] = kw
    kw16_ref[...] = kw.astype(BF16)
    kit_ref[0] = kw.T[:D_IDX]


def _project(x2d, g_pre, w_cat, cos_t, sin_t, tm, seq):
    rows = x2d.shape[0]
    d_model = x2d.shape[1]
    n_tab = cos_t.shape[0] // tm
    tiles = seq // tm
    row_spec = lambda w: pl.BlockSpec((tm, w), lambda i: (i, 0))
    t_spec = lambda w: pl.BlockSpec((1, w, tm), lambda i: (i // tiles, 0, i % tiles))
    outs = [
        ("qa", D_A, BF16), ("ka_t", LANES, F32), ("va_t", LANES, F32), ("k16", LANES, BF16), ("v16", LANES, BF16),
        ("qi", N_IDX_HEADS * D_IDX, BF16), ("ga", D_A, BF16), ("gb", D_B, BF16),
        ("qb", D_B, BF16), ("kb", D_B, BF16), ("vb", D_B, BF16), ("kw", LANES, F32), ("kw16", LANES, BF16),
        ("ki_t", D_IDX, F32),
    ]
    transposed = lambda n: n.endswith("_t")
    res = pl.pallas_call(
        _proj_kernel,
        grid=(rows // tm,),
        in_specs=[
            row_spec(d_model),
            pl.BlockSpec((1, d_model), lambda i: (0, 0)),
            pl.BlockSpec(w_cat.shape, lambda i: (0, 0), pipeline_mode=pl.Buffered(1)),
            pl.BlockSpec((tm, LANES), lambda i: (i % n_tab, 0)),
            pl.BlockSpec((tm, LANES), lambda i: (i % n_tab, 0)),
        ],
        out_specs=[t_spec(w) if transposed(n) else row_spec(w) for n, w, _ in outs],
        out_shape=[jax.ShapeDtypeStruct((rows // seq, w, seq) if transposed(n) else (rows, w), dt)
                   for n, w, dt in outs],
        scratch_shapes=[pltpu.VMEM((tm, d_model), BF16)],
        compiler_params=pltpu.CompilerParams(dimension_semantics=("arbitrary",), vmem_limit_bytes=VMEM_LIMIT),
        name="proj",
    )(x2d, g_pre, w_cat, cos_t, sin_t)
    return {n: r for (n, _, _), r in zip(outs, res)}


def _dsa_prompt_kernel(rb_ref, qi_ref, kwq_ref, qin_ref, kwqn_ref, kw16_ref, qa_ref, k16_ref, v16_ref, oa_ref,
                       sc2, bt, vt, lgs, acc_scr, cut_scr, key_scr, cnt_scr, *, topk, search_blk):
    b = pl.program_id(0)
    j = pl.program_id(1)
    nblk = pl.num_programs(1)
    sc = sc2.at[j % 2]
    sc_next = sc2.at[1 - j % 2]
    row = lax.broadcasted_iota(I32, (BLK, BLK), 0)
    col = lax.broadcasted_iota(I32, (BLK, BLK), 1)

    @pl.when((b == 0) & (j == 0))
    def _init_bias():
        for d in range(2):
            bkt = _t5_bucket(col - row + d * BLK)
            for h in range(N_HEADS_A):
                bt[d, h] = (_bias_from_bucket(bkt, rb_ref, h) - rb_ref[N_BUCKETS - 1, h]) * LOG2E
        bt[2] = jnp.zeros(bt.shape[1:], F32)

    @pl.when(j == 0)
    def _clear_values():
        vt[...] = jnp.zeros(vt.shape, BF16)

    vblk = v16_ref[pl.ds(pl.multiple_of(j * BLK, BLK), BLK), :]
    vtj = vblk.astype(F32).T.astype(BF16)
    ones = jnp.ones((ONES_ROWS, BLK), BF16)
    vt[j] = jnp.concatenate([piece for n in range(N_KV_A)
                             for piece in (vtj[n * HEAD_DIM:(n + 1) * HEAD_DIM], ones)], axis=0)

    def indexer_queries(qi_blk_ref, kw_blk_ref):
        qi = qi_blk_ref[...]
        qi_stack = jnp.concatenate([qi[:, h * D_IDX:(h + 1) * D_IDX] for h in range(N_IDX_HEADS)], axis=0)
        return qi_stack, kw_blk_ref[...].T

    def score_products(c, queries):
        out = []
        for u in range(search_blk):
            i = c * search_blk + u
            kib = kw16_ref[pl.ds(pl.multiple_of(i * BLK, BLK), BLK), :][:, :D_IDX]
            out.append(_nt_dot(kib, queries[0]))
        return out

    def score_finish(c, products, dst, queries, first_q):
        wi_t = queries[1]
        for u, s in enumerate(products):
            i = c * search_blk + u
            acc = jnp.zeros((BLK, BLK), F32)
            for h in range(N_IDX_HEADS):
                acc = acc + wi_t[D_IDX + h:D_IDX + h + 1, :] * jnp.maximum(s[:, h * BLK:(h + 1) * BLK], 0.0)
            dst[i] = jnp.where(row + i * BLK <= col + first_q, acc, -jnp.inf)

    def score_group(c, dst, queries, first_q):
        score_finish(c, score_products(c, queries), dst, queries, first_q)

    @pl.when(j == 0)
    def _own_scores():
        score_group(0, sc, indexer_queries(qi_ref, kwq_ref), 0)

    n_chunk = (j + search_blk) // search_blk

    def col_sum(w):
        return jnp.sum(w.reshape(BLK // SUBLANES, SUBLANES, BLK), axis=0)

    qa = qa_ref[...]
    g_per_kv = N_HEADS_A // N_KV_A
    q_stack = [jnp.concatenate([qa[:, (n * g_per_kv + g) * HEAD_DIM:(n * g_per_kv + g + 1) * HEAD_DIM]
                                for g in range(g_per_kv)], axis=0) for n in range(N_KV_A)]

    kf = float(topk)
    n_units = search_blk * N_KV_A

    def search(n_groups):
        def count_ge(cand, unit=None):
            raw = []
            if unit is not None:
                n = unit[1]
                for c in range(n_groups):
                    i = c * search_blk + unit[0]
                    raw.append((i, _nt_dot(k16_ref[i * BLK:(i + 1) * BLK, n * HEAD_DIM:(n + 1) * HEAD_DIM],
                                           q_stack[n])))
            acc = jnp.zeros((SUBLANES, BLK), F32)
            for c in range(n_groups):
                for u in range(search_blk):
                    acc = acc + col_sum(jnp.where(sc[c * search_blk + u] >= cand, 1.0, 0.0))
            for i, lg in raw:
                for g in range(g_per_kv):
                    lgs[i, unit[1] * g_per_kv + g] = lg[:, g * BLK:(g + 1) * BLK]
            return jnp.sum(acc, axis=0, keepdims=True)

        def bit_step(bit, carry, unit=None):
            key, cnt_key = carry
            cand = key | jnp.left_shift(jnp.int32(1), bit)
            cnt = count_ge(_key_to_float(cand), unit)
            return jnp.where(cnt >= kf, cand, key), jnp.where(cnt >= kf, cnt, cnt_key)

        c0 = count_ge(jnp.zeros((1, BLK), F32))
        carry = (jnp.where(c0 >= kf, 0, INT_MIN).astype(I32), jnp.where(c0 >= kf, c0, 0.0))
        for it in range(n_units):
            carry = bit_step(30 - it, carry, (it // N_KV_A, it % N_KV_A))
        n_rest = jnp.where((j + 1) * BLK <= topk, 0, 31 - n_units)
        key, cnt_key = lax.fori_loop(0, n_rest, lambda it, cr: bit_step(30 - n_units - it, cr), carry)
        key_scr[...] = jnp.broadcast_to(key, key_scr.shape)
        cnt_scr[...] = jnp.broadcast_to(cnt_key, cnt_scr.shape)

    for n_groups in range(1, sc.shape[0] // search_blk + 1):
        pl.when(n_chunk == n_groups)(functools.partial(search, n_groups))

    key = key_scr[0:1, :]
    cnt_ge = jnp.where(key < KEY_NEG_FLT_MAX, 0.0, cnt_scr[0:1, :])
    thr = _key_to_float(jnp.maximum(key, KEY_NEG_FLT_MAX))
    straddle = jnp.max(jnp.where(cnt_ge > kf, 1.0, 0.0))

    z8 = jnp.zeros((SUBLANES, BLK), F32)
    n_keys_max = sc.shape[0] * BLK
    cut_scr[...] = jnp.full(cut_scr.shape, 2 * n_keys_max, I32)

    @pl.when(straddle > 0.0)
    def _tie_cut():
        def gt_body(i, acc):
            return acc + col_sum(jnp.where(sc[i] > thr, 1.0, 0.0))
        cnt_gt = jnp.sum(lax.fori_loop(0, j + 1, gt_body, z8), axis=0, keepdims=True)
        need = kf - cnt_gt

        def count_tie(cand):
            def body(i, acc):
                t = sc[i]
                hit = jnp.where(t == thr, jnp.where(row + i * BLK < cand, 1.0, 0.0), 0.0)
                return acc + col_sum(hit)
            acc = lax.fori_loop(0, j + 1, body, z8)
            return jnp.sum(acc, axis=0, keepdims=True)

        def cut_body(it, cut):
            cand = cut | jnp.left_shift(jnp.int32(1), (n_keys_max.bit_length() - 1) - it)
            return jnp.where(count_tie(cand) <= need, cand, cut)

        cut = lax.fori_loop(0, n_keys_max.bit_length(), cut_body, jnp.zeros((1, BLK), I32))
        cut_scr[...] = jnp.broadcast_to(cut, cut_scr.shape)

    cut = cut_scr[0:1, :]

    def col_max(w):
        return jnp.max(w.reshape(BLK // SUBLANES, SUBLANES, BLK), axis=0)

    def mask_body(c, m8, biased):
        m8 = list(m8)
        for u in range(search_blk):
            i = c * search_blk + u
            t = sc[i]
            tie_keep = jnp.where(row + i * BLK < cut, 0.0, -jnp.inf)
            nm = jnp.where(t > thr, 0.0, jnp.where(t == thr, tie_keep, -jnp.inf))
            near = jnp.clip(j - i, 0, 2)
            for h in range(N_HEADS_A):
                x = lgs[i, h] + nm
                if biased:
                    x = x + bt[near, h]
                lgs[i, h] = x
                m8[h] = jnp.maximum(m8[h], col_max(x))
        return tuple(m8)

    n_far = jnp.maximum(n_chunk - 2, 0)
    neg8 = jnp.full((SUBLANES, BLK), NEG_BIG, F32)
    m8 = lax.fori_loop(0, n_far, functools.partial(mask_body, biased=False), (neg8,) * N_HEADS_A)
    m8 = lax.fori_loop(n_far, n_chunk, functools.partial(mask_body, biased=True), m8)
    m_row = [jnp.max(m8[h], axis=0, keepdims=True) for h in range(N_HEADS_A)]

    acc_scr[...] = jnp.zeros(acc_scr.shape, F32)

    vrows = HEAD_DIM + ONES_ROWS

    next_queries = indexer_queries(qin_ref, kwqn_ref)
    next_first_q = (j + 1) * BLK

    def pv_body(c, carry):
        pv = [None] * N_KV_A
        for u in range(search_blk):
            i = c * search_blk + u
            vti = vt[i]
            for n in range(N_KV_A):
                ps = [jnp.exp2(lgs[i, n * g_per_kv + g] - m_row[n * g_per_kv + g]).astype(BF16)
                      for g in range(g_per_kv)]
                t = _dot(vti[n * vrows:(n + 1) * vrows, :], jnp.concatenate(ps, axis=1))
                pv[n] = t if pv[n] is None else pv[n] + t
        for n in range(N_KV_A):
            acc_scr[n] += pv[n]
        score_group(c, sc_next, next_queries, next_first_q)
        return carry

    lax.fori_loop(0, n_chunk, pv_body, 0)

    @pl.when(((j + 1) % search_blk == 0) & (j + 1 < nblk))
    def _next_reaches_one_more_group():
        score_group(n_chunk, sc_next, next_queries, next_first_q)

    parts = []
    for n in range(N_KV_A):
        acc = acc_scr[n]
        out = acc[:HEAD_DIM] * (1.0 / acc[HEAD_DIM:HEAD_DIM + 1])
        parts += [out[:, g * BLK:(g + 1) * BLK] for g in range(g_per_kv)]
    oa_ref[...] = jnp.concatenate(parts, axis=0).T.astype(BF16)


def _dsa_prompt(p, rel_bias, batch, seq):
    nblk = seq // BLK
    topk = min(TOPK_MAX, seq // 4)
    search_blk = 4 if nblk % 4 == 0 else 1
    qrow = lambda w: pl.BlockSpec((BLK, w), lambda b, j: (b * nblk + j, 0))
    qnext = lambda w: pl.BlockSpec((BLK, w), lambda b, j: (b * nblk + jnp.minimum(j + 1, nblk - 1), 0))
    brow = lambda w: pl.BlockSpec((seq, w), lambda b, j: (b, 0))
    return pl.pallas_call(
        functools.partial(_dsa_prompt_kernel, topk=topk, search_blk=search_blk),
        grid=(batch, nblk),
        in_specs=[
            pl.BlockSpec(memory_space=pltpu.SMEM),
            qrow(N_IDX_HEADS * D_IDX), qrow(LANES), qnext(N_IDX_HEADS * D_IDX), qnext(LANES),
            brow(LANES), qrow(D_A), brow(LANES), brow(LANES),
        ],
        out_specs=qrow(D_A),
        out_shape=jax.ShapeDtypeStruct((batch * seq, D_A), BF16),
        scratch_shapes=[
            pltpu.VMEM((2, nblk, BLK, BLK), F32),
            pltpu.VMEM((3, N_HEADS_A, BLK, BLK), F32),
            pltpu.VMEM((nblk, N_KV_A * (HEAD_DIM + ONES_ROWS), BLK), BF16),
            pltpu.VMEM((nblk, N_HEADS_A, BLK, BLK), F32),
            pltpu.VMEM((N_KV_A, HEAD_DIM + ONES_ROWS, (N_HEADS_A // N_KV_A) * BLK), F32),
            pltpu.VMEM((SUBLANES, BLK), I32),
            pltpu.VMEM((SUBLANES, BLK), I32),
            pltpu.VMEM((SUBLANES, BLK), F32),
        ],
        compiler_params=pltpu.CompilerParams(dimension_semantics=("arbitrary", "arbitrary"),
                                             vmem_limit_bytes=VMEM_LIMIT),
        name="dsa_prompt",
    )(rel_bias, p["qi"], p["kw"], p["qi"], p["kw"], p["kw16"], p["qa"], p["k16"], p["v16"])


def _retention_kernel(q_ref, k_ref, v_ref, s0_ref, dmat_ref, din_ref, dout_ref, decm_ref, bdm_ref,
                      yn_ref, s_out_ref, s_scr, *, n_sub, pairs_per_phase):
    c = pl.program_id(1)
    chunk = din_ref.shape[0]

    @pl.when(c == 0)
    def _load_state():
        z = jnp.zeros((HEAD_DIM, HEAD_DIM), F32)
        for p in range(N_HEADS_B // 2):
            pair = jnp.concatenate([jnp.concatenate([s0_ref[0, 2 * p], z], axis=1),
                                    jnp.concatenate([z, s0_ref[0, 2 * p + 1]], axis=1)], axis=0)
            s_scr[p] = pair.T

    lo = lax.broadcasted_iota(I32, (chunk, LANES), 1) < HEAD_DIM
    zero = jnp.zeros((), BF16)
    n_pairs = N_HEADS_B // 2
    lanes = [slice(p * LANES, (p + 1) * LANES) for p in range(n_pairs)]
    rows = [slice(u * chunk, (u + 1) * chunk) for u in range(n_sub)]

    def run(pairs):
        units = [(p, u) for p in pairs for u in range(n_sub)]
        qd, kd, vt, s_a, s_b, intra, upd, inter = {}, {}, {}, {}, {}, {}, {}, {}
        for p, u in units:
            q, k, v = q_ref[0, rows[u], lanes[p]], k_ref[0, rows[u], lanes[p]], v_ref[0, rows[u], lanes[p]]
            qd[p, u] = (q.astype(F32) * din_ref[:, lanes[p]]).astype(BF16)
            kd[p, u] = (k.astype(F32) * dout_ref[:, lanes[p]]).astype(BF16)
            vt[p, u] = v.astype(F32).T.astype(BF16)
            s_a[p, u] = (_nt_dot(jnp.where(lo, k, zero), q) * dmat_ref[2 * p]).astype(BF16)
            s_b[p, u] = (_nt_dot(jnp.where(lo, zero, k), q) * dmat_ref[2 * p + 1]).astype(BF16)
        for p, u in units:
            intra[p, u] = jnp.concatenate([_dot(vt[p, u][:HEAD_DIM], s_a[p, u]),
                                           _dot(vt[p, u][HEAD_DIM:], s_b[p, u])], axis=0)
            upd[p, u] = _dot(vt[p, u], kd[p, u]) * bdm_ref[...]
        for p in pairs:
            st = s_scr[p]
            for u in range(n_sub):
                inter[p, u] = _nt_dot(st.astype(BF16), qd[p, u])
                st = st * decm_ref[p] + upd[p, u]
            s_scr[p] = st
            pair = st.T
            s_out_ref[0, 2 * p] = pair[:HEAD_DIM, :HEAD_DIM]
            s_out_ref[0, 2 * p + 1] = pair[HEAD_DIM:, HEAD_DIM:]
        for p, u in units:
            o = (inter[p, u] + intra[p, u]).reshape(2, HEAD_DIM, chunk)
            dlt = o - jnp.mean(o, axis=1, keepdims=True)
            var = jnp.mean(dlt * dlt, axis=1, keepdims=True)
            yn_ref[0, rows[u], lanes[p]] = (dlt * lax.rsqrt(var + EPS)).reshape(LANES, chunk).T.astype(BF16)

    for group in range(0, n_pairs, pairs_per_phase):
        run(range(group, group + pairs_per_phase))


def _ret_tables(chunk, n_real):
    h = N_HEADS_B
    log_g = jnp.log1p(-jnp.exp2(-5.0 - jnp.arange(h, dtype=F32)))
    i = jnp.arange(chunk, dtype=F32)
    diff = i[:, None] - i[None, :]
    dmat = jnp.where(diff >= 0, jnp.exp(jnp.maximum(diff, 0.0)[None] * log_g[:, None, None]), 0.0)
    dmat_t = jnp.swapaxes(dmat, 1, 2)
    dec_in = jnp.exp((i + 1.0)[:, None] * log_g[None, :])
    dec_out = jnp.exp((n_real - 1.0 - i)[:, None] * log_g[None, :])
    dec_chunk = jnp.exp(n_real * log_g)
    din = jnp.repeat(dec_in, HEAD_DIM, axis=1)
    dout = jnp.repeat(dec_out, HEAD_DIM, axis=1)
    head_of = np.arange(LANES) // HEAD_DIM
    bd = jnp.asarray((head_of[:, None] == head_of[None, :]).astype(np.float32))
    dc = jnp.repeat(dec_chunk, HEAD_DIM).reshape(h // 2, LANES)
    decm = dc[:, :, None] * bd[None]
    return dmat_t, din, dout, decm, bd


def _retention(q3, k3, v3, s0, n_real):
    batch, seq, _ = q3.shape
    chunk = min(RET_CHUNK, seq)
    n_sub = _largest_divisor(seq // chunk, 8)
    dmat_t, din, dout, decm, bd = _ret_tables(chunk, n_real)
    tok = pl.BlockSpec((1, n_sub * chunk, D_B), lambda b, c: (b, c, 0))
    st = pl.BlockSpec((1,) + s0.shape[1:], lambda b, c: (b, 0, 0, 0))
    full = lambda a: pl.BlockSpec(a.shape, lambda b, c: (0,) * a.ndim)
    return pl.pallas_call(
        functools.partial(_retention_kernel, n_sub=n_sub,
                          pairs_per_phase=max(1, min(N_HEADS_B // 2, 8 // n_sub))),
        grid=(batch, seq // (n_sub * chunk)),
        in_specs=[tok, tok, tok, st, full(dmat_t), full(din), full(dout), full(decm), full(bd)],
        out_specs=[tok, st],
        out_shape=[jax.ShapeDtypeStruct((batch, seq, D_B), BF16),
                   jax.ShapeDtypeStruct(s0.shape, F32)],
        scratch_shapes=[pltpu.VMEM((N_HEADS_B // 2, LANES, LANES), F32)],
        compiler_params=pltpu.CompilerParams(dimension_semantics=("arbitrary", "arbitrary"),
                                             vmem_limit_bytes=VMEM_LIMIT),
        name="retention",
    )(q3, k3, v3, s0, dmat_t, din, dout, decm, bd)


def _merge_kernel(x_ref, oa_ref, yn_ref, ga_ref, gb_ref, p_ref, wo_ref, gp_ref, wup_ref, wg_ref, out_ref):
    ga = ga_ref[...].astype(F32)
    gb = gb_ref[...].astype(F32)
    ya = (ga * _sigmoid(ga) * oa_ref[...].astype(F32)).astype(BF16)
    yb = (gb * _sigmoid(gb) * yn_ref[...].astype(F32)).astype(BF16)
    y = _dot(ya, wo_ref[0:D_A, :]) + _dot(yb, wo_ref[D_A:D_A + D_B, :])
    ms = jnp.mean(y * y, axis=-1, keepdims=True)
    x1 = x_ref[...] + y * lax.rsqrt(ms + EPS) * gp_ref[...]
    ple = _dot(p_ref[...].astype(BF16), wup_ref[...])
    gate = _sigmoid(_dot(x1.astype(BF16), wg_ref[...]))
    out_ref[...] = x1 + ple * gate


def _merge(x2d, oa, yn, ga, gb, p2d, wo16, g_post, wup16, wg16, tm):
    rows, d_model = x2d.shape
    row = lambda w: pl.BlockSpec((tm, w), lambda i: (i, 0))
    full = lambda a: pl.BlockSpec(a.shape, lambda i: (0, 0))
    return pl.pallas_call(
        _merge_kernel,
        grid=(rows // tm,),
        in_specs=[row(d_model), row(D_A), row(D_B), row(D_A), row(D_B), row(p2d.shape[1]),
                  full(wo16), full(g_post), full(wup16), full(wg16)],
        out_specs=row(d_model),
        out_shape=jax.ShapeDtypeStruct((rows, d_model), F32),
        compiler_params=pltpu.CompilerParams(dimension_semantics=("arbitrary",), vmem_limit_bytes=VMEM_LIMIT),
        name="merge",
    )(x2d, oa, yn, ga, gb, p2d, wo16, g_post, wup16, wg16)


def _page_copy(pages_hbm, buf, sem, pt_ref, b, p, slot):
    return pltpu.make_async_copy(pages_hbm.at[pt_ref[b, p]], buf.at[slot, p], sem.at[slot])


def _start_batch_pages(streams, pt_ref, b, slot, n_pages):
    def body(p, c):
        for pages_hbm, buf, sem in streams:
            _page_copy(pages_hbm, buf, sem, pt_ref, b, p, slot).start()
        return c
    lax.fori_loop(0, n_pages, body, 0)


def _wait_batch_pages(streams, pt_ref, b, slot, n_pages):
    for pages_hbm, buf, sem in streams:
        for p in range(n_pages):
            _page_copy(pages_hbm, buf, sem, pt_ref, b, p, slot).wait()


def _sample_index_kernel(pt_ref, qi_ref, w_ref, pages_hbm, out_ref, buf, sem, *, n_pages, unroll):
    b = pl.program_id(0)
    slot = b % 2
    streams = [(pages_hbm, buf, sem)]

    @pl.when(b == 0)
    def _first():
        _start_batch_pages(streams, pt_ref, 0, 0, n_pages)

    _wait_batch_pages(streams, pt_ref, b, slot, n_pages)

    qi = qi_ref[0]
    w = w_ref[0]
    n_q = qi.shape[0] // N_IDX_HEADS

    def run(prefetch):
        def body(c, carry):
            for u in range(unroll):
                p = c * unroll + u
                if prefetch:
                    _page_copy(pages_hbm, buf, sem, pt_ref, b + 1, p, 1 - slot).start()
                s = _dot(qi, buf[slot, p].astype(BF16))
                r = jnp.maximum(s, 0.0) * w
                out_ref[0, c, :, u * PAGE_SIZE:(u + 1) * PAGE_SIZE] = jnp.sum(
                    r.reshape(n_q, N_IDX_HEADS, PAGE_SIZE), axis=1)
            return carry
        lax.fori_loop(0, n_pages // unroll, body, 0)

    has_next = b + 1 < pl.num_programs(0)
    pl.when(has_next)(lambda: run(True))
    pl.when(jnp.logical_not(has_next))(lambda: run(False))


def _sample_index(page_table, qi_qh, w_qh, kidx_t, unroll):
    batch, n_pages = page_table.shape
    n_q = qi_qh.shape[1] // N_IDX_HEADS
    groups = n_pages // unroll
    grid_spec = pltpu.PrefetchScalarGridSpec(
        num_scalar_prefetch=1,
        grid=(batch,),
        in_specs=[pl.BlockSpec((1,) + qi_qh.shape[1:], lambda b, pt: (b, 0, 0)),
                  pl.BlockSpec((1,) + w_qh.shape[1:], lambda b, pt: (b, 0, 0)),
                  pl.BlockSpec(memory_space=pl.ANY)],
        out_specs=pl.BlockSpec((1, groups, n_q, unroll * PAGE_SIZE), lambda b, pt: (b, 0, 0, 0)),
        scratch_shapes=[pltpu.VMEM((2, n_pages) + kidx_t.shape[1:], kidx_t.dtype),
                        pltpu.SemaphoreType.DMA((2,))],
    )
    return pl.pallas_call(
        functools.partial(_sample_index_kernel, n_pages=n_pages, unroll=unroll),
        grid_spec=grid_spec,
        out_shape=jax.ShapeDtypeStruct((batch, groups, n_q, unroll * PAGE_SIZE), F32),
        compiler_params=pltpu.CompilerParams(dimension_semantics=("arbitrary",), vmem_limit_bytes=VMEM_LIMIT),
        name="sample_index",
    )(page_table, qi_qh, w_qh, kidx_t)


def _sample_select_kernel(sp_ref, qi_ref, w_ref, kw16_ref, thr_ref, cut_ref, nmn_ref, snew_scr,
                          *, topk, n_q, lane_chunk):
    rows, past = sp_ref.shape
    r_i = lax.broadcasted_iota(I32, (rows, LANES), 0)
    l_i = lax.broadcasted_iota(I32, (rows, LANES), 1)

    s = _nt_dot(qi_ref[...], kw16_ref[...][:, :D_IDX])
    r = jnp.maximum(s, 0.0) * w_ref[...]
    s_new = jnp.sum(r.reshape(rows, N_IDX_HEADS, LANES), axis=1)
    ok = ((l_i // n_q) == (r_i // n_q)) & ((l_i % n_q) <= (r_i % n_q))
    snew_scr[...] = jnp.where(ok, s_new, -jnp.inf)

    n_chunks = past // lane_chunk
    tiles = lane_chunk // LANES
    kf = float(topk)

    def count(ind):
        acc = ind(snew_scr[...], l_i + past)
        for ch in range(n_chunks):
            x = sp_ref[:, ch * lane_chunk:(ch + 1) * lane_chunk]
            for t in range(tiles):
                idx = l_i + (ch * lane_chunk + t * LANES)
                acc = acc + ind(x[:, t * LANES:(t + 1) * LANES], idx)
        return jnp.broadcast_to(jnp.sum(acc, axis=1, keepdims=True), (rows, LANES))

    c0 = count(lambda x, idx: jnp.where(x >= 0.0, 1.0, 0.0))
    key0 = jnp.where(c0 >= kf, 0, INT_MIN).astype(I32)

    def bit_body(it, key):
        cand = key | jnp.left_shift(jnp.int32(1), 30 - it)
        cf = _key_to_float(cand)
        return jnp.where(count(lambda x, idx: jnp.where(x >= cf, 1.0, 0.0)) >= kf, cand, key)

    key = lax.fori_loop(0, 31, bit_body, key0)
    key = jnp.maximum(key, KEY_NEG_FLT_MAX)
    thr = _key_to_float(key)
    cnt_gt = count(lambda x, idx: jnp.where(x > thr, 1.0, 0.0))
    cnt_ge = count(lambda x, idx: jnp.where(x >= thr, 1.0, 0.0))
    need = kf - cnt_gt
    straddle = jnp.max(jnp.where(cnt_ge > kf, 1.0, 0.0))

    idx_bits = (past + LANES).bit_length()
    thr_ref[...] = thr
    cut_ref[...] = jnp.full((rows, LANES), 1 << idx_bits, I32)

    @pl.when(straddle > 0.0)
    def _tie_cut():
        def cut_body(it, cut):
            cand = cut | jnp.left_shift(jnp.int32(1), (idx_bits - 1) - it)
            cnt = count(lambda x, idx: jnp.where(x == thr, jnp.where(idx < cand, 1.0, 0.0), 0.0))
            return jnp.where(cnt <= need, cand, cut)
        cut_ref[...] = lax.fori_loop(0, idx_bits, cut_body, jnp.zeros((rows, LANES), I32))

    cut = cut_ref[...]
    sn = snew_scr[...]
    tie_keep = jnp.where(l_i + past < cut, 0.0, -jnp.inf)
    nmn_ref[...] = jnp.where(sn > thr, 0.0, jnp.where(sn == thr, tie_keep, -jnp.inf))


def _sample_select(scores_past, qi_rows, w_rows, kw16_s, topk, n_q):
    rows, past = scores_past.shape
    lane_chunk = 2048 if past % 2048 == 0 else LANES
    full = lambda a: pl.BlockSpec(a.shape, lambda i: (0,) * a.ndim)
    o = jax.ShapeDtypeStruct((rows, LANES), F32)
    return pl.pallas_call(
        functools.partial(_sample_select_kernel, topk=topk, n_q=n_q, lane_chunk=lane_chunk),
        grid=(1,),
        in_specs=[full(scores_past), full(qi_rows), full(w_rows), full(kw16_s)],
        out_specs=[pl.BlockSpec((rows, LANES), lambda i: (0, 0))] * 3,
        out_shape=[o, jax.ShapeDtypeStruct((rows, LANES), I32), o],
        scratch_shapes=[pltpu.VMEM((rows, LANES), F32)],
        compiler_params=pltpu.CompilerParams(dimension_semantics=("arbitrary",), vmem_limit_bytes=VMEM_LIMIT),
        name="sample_select",
    )(scores_past, qi_rows, w_rows, kw16_s)


def _sample_attn_kernel(pt_ref, rbc_ref, q_ref, sp_ref, thr_ref, cut_ref, nmn_ref, k16n_ref, v16n_ref,
                        k_hbm, v_hbm, o_ref, kbuf, vbuf, ksem, vsem, lgs, bias_scr, mx_scr,
                        *, n_pages, n_q, unroll):
    b = pl.program_id(0)
    slot = b % 2
    rows = n_q * N_HEADS_A
    groups = n_pages // unroll
    streams = [(k_hbm, kbuf, ksem), (v_hbm, vbuf, vsem)]
    lane = lax.broadcasted_iota(I32, (rows, LANES), 1)
    rq = lax.broadcasted_iota(I32, (rows, LANES), 0) // N_HEADS_A
    expand = lambda a: jnp.concatenate(
        [jnp.broadcast_to(a[t:t + 1], (N_HEADS_A, LANES)) for t in range(n_q)], axis=0)

    @pl.when(b == 0)
    def _first():
        _start_batch_pages(streams, pt_ref, 0, 0, n_pages)
        far = jnp.concatenate([rbc_ref[N_BUCKETS - 1]] * n_q, axis=0)
        for t, dist in enumerate((PAGE_SIZE + rq - lane, rq - lane % n_q)):
            bkt = _t5_bucket(dist)
            tile = jnp.zeros((rows, LANES), F32)
            for k in range(N_BUCKETS - 1):
                tile = jnp.where(bkt == k, jnp.concatenate([rbc_ref[k]] * n_q, axis=0) - far, tile)
            bias_scr[t] = tile * LOG2E

    _wait_batch_pages(streams, pt_ref, b, slot, n_pages)

    thr = expand(thr_ref[0])
    cut = expand(cut_ref[0])
    q = q_ref[0]

    def logits_pass(prefetch):
        def body(c, mx):
            for u in range(unroll):
                p = c * unroll + u
                if prefetch:
                    for pages_hbm, buf, sem in streams:
                        _page_copy(pages_hbm, buf, sem, pt_ref, b + 1, p, 1 - slot).start()
                lg = _dot(q, kbuf[slot, p].astype(BF16))
                sc = expand(sp_ref[0, c, :, u * PAGE_SIZE:(u + 1) * PAGE_SIZE])
                tie_keep = jnp.where(lane + p * PAGE_SIZE < cut, 0.0, -jnp.inf)
                x = lg + jnp.where(sc > thr, 0.0, jnp.where(sc == thr, tie_keep, -jnp.inf))
                if u == unroll - 1:
                    x = x + bias_scr[0] * jnp.where(c == groups - 1, 1.0, 0.0)
                lgs[p] = x
                mx = jnp.maximum(mx, x)
            return mx
        mx_scr[...] = lax.fori_loop(0, groups, body, jnp.full((rows, LANES), NEG_BIG, F32))

    has_next = b + 1 < pl.num_programs(0)
    pl.when(has_next)(lambda: logits_pass(True))
    pl.when(jnp.logical_not(has_next))(lambda: logits_pass(False))

    xn = _nt_dot(q, k16n_ref[...]) + bias_scr[1] + expand(nmn_ref[0])
    m = jnp.max(jnp.maximum(mx_scr[...], xn), axis=1, keepdims=True)

    def pv_body(c, carry):
        lsum, acc = carry
        for u in range(unroll):
            p = c * unroll + u
            pr = jnp.exp2(lgs[p] - m)
            lsum = lsum + pr
            acc = acc + _nt_dot(pr.astype(BF16), vbuf[slot, p].astype(BF16))
        return lsum, acc

    pn = jnp.exp2(xn - m)
    lsum, acc = lax.fori_loop(0, groups, pv_body, (pn, _dot(pn.astype(BF16), v16n_ref[...])))
    out = acc * (1.0 / jnp.sum(lsum, axis=1, keepdims=True))
    head = lax.broadcasted_iota(I32, (rows, HEAD_DIM), 0) % N_HEADS_A
    o_ref[0] = jnp.where(head < N_HEADS_A // N_KV_A, out[:, :HEAD_DIM], out[:, HEAD_DIM:])


def _sample_attn(page_table, rb_col, q_bd, scores4, thr3, cut3, nmn3, k16n, v16n, k_pages_t, v_pages_t, unroll):
    batch, n_pages = page_table.shape
    n_q = scores4.shape[2]
    rows = q_bd.shape[1]
    per_b = lambda a: pl.BlockSpec((1,) + a.shape[1:], lambda b, pt: (b,) + (0,) * (a.ndim - 1))
    full = lambda a: pl.BlockSpec(a.shape, lambda b, pt: (0,) * a.ndim)
    page_buf = pltpu.VMEM((2, n_pages) + k_pages_t.shape[1:], k_pages_t.dtype)
    grid_spec = pltpu.PrefetchScalarGridSpec(
        num_scalar_prefetch=1,
        grid=(batch,),
        in_specs=[full(rb_col), per_b(q_bd), per_b(scores4), per_b(thr3), per_b(cut3), per_b(nmn3),
                  full(k16n), full(v16n), pl.BlockSpec(memory_space=pl.ANY), pl.BlockSpec(memory_space=pl.ANY)],
        out_specs=pl.BlockSpec((1, rows, HEAD_DIM), lambda b, pt: (b, 0, 0)),
        scratch_shapes=[page_buf, page_buf, pltpu.SemaphoreType.DMA((2,)), pltpu.SemaphoreType.DMA((2,)),
                        pltpu.VMEM((n_pages, rows, LANES), F32), pltpu.VMEM((2, rows, LANES), F32),
                        pltpu.VMEM((rows, LANES), F32)],
    )
    return pl.pallas_call(
        functools.partial(_sample_attn_kernel, n_pages=n_pages, n_q=n_q, unroll=unroll),
        grid_spec=grid_spec,
        out_shape=jax.ShapeDtypeStruct((batch, rows, HEAD_DIM), F32),
        compiler_params=pltpu.CompilerParams(dimension_semantics=("arbitrary",),
                                             vmem_limit_bytes=VMEM_LIMIT_PAGED),
        name="sample_attn",
    )(page_table, rb_col, q_bd, scores4, thr3, cut3, nmn3, k16n, v16n, k_pages_t, v_pages_t)


def _rope_tables(pos):
    half = HEAD_DIM // 2
    inv = ROPE_BASE ** (-jnp.arange(half, dtype=F32) / half)
    ang = pos.astype(F32)[:, None] * inv[None, :]
    cos = jnp.cos(ang)
    sin = jnp.sin(ang)
    reps = LANES // HEAD_DIM
    cos_t = jnp.tile(jnp.concatenate([cos, cos], axis=1), (1, reps))
    sin_t = jnp.tile(jnp.concatenate([-sin, sin], axis=1), (1, reps))
    return cos_t, sin_t


def _cat_weight(w_in):
    split = int(np.sum(SPLIT_SIZES[:6]))
    pad = jnp.zeros((w_in.shape[0], LANES - D_IDX - N_IDX_HEADS), BF16)
    return jnp.concatenate([w_in[:, :split].astype(BF16), pad, w_in[:, split:].astype(BF16)], axis=1)


def _pages_t(cache):
    pool, page = cache.shape[:2]
    return jnp.transpose(cache, (0, 2, 3, 1)).reshape(pool, -1, page)


def _largest_divisor(n, cap):
    d = cap
    while n % d:
        d //= 2
    return d


def kernel(x_prompt, x_sample, cache_k, cache_v, cache_kidx, state_ret, page_table, p_prompt, p_sample,
           rel_bias, w_in, w_out, g_pre, g_post, w_ple_up, w_ple_gate):
    batch, seq, d_model = x_prompt.shape
    dec_b, dec_t, _ = x_sample.shape
    depth = w_in.shape[0]
    n_pages = page_table.shape[1]
    past = n_pages * PAGE_SIZE
    rows_s = dec_b * dec_t
    assert depth == 1 and rows_s == LANES and seq % BLK == 0

    w_cat = _cat_weight(w_in[0])
    gpre = g_pre[0].reshape(1, d_model)
    gpost = g_post[0].reshape(1, d_model)
    wo16 = w_out[0].astype(BF16)
    wup16 = w_ple_up[0].astype(BF16)
    wg16 = w_ple_gate[0].astype(BF16)

    tm = _largest_divisor(seq, 1024)
    cos_p, sin_p = _rope_tables(jnp.arange(seq))
    xp2 = x_prompt.reshape(batch * seq, d_model)
    pp = _project(xp2, gpre, w_cat, cos_p, sin_p, tm, seq)
    oa_p = _dsa_prompt(pp, rel_bias, batch, seq)
    r3 = lambda a: a.reshape(batch, seq, D_B)
    s0_p = jnp.zeros((batch, N_HEADS_B, HEAD_DIM, HEAD_DIM), F32)
    yn_p, state_p = _retention(r3(pp["qb"]), r3(pp["kb"]), r3(pp["vb"]), s0_p, float(min(RET_CHUNK, seq)))
    y_prompt = _merge(xp2, oa_p, yn_p.reshape(batch * seq, D_B), pp["ga"], pp["gb"],
                      p_prompt[0].reshape(batch * seq, -1), wo16, gpost, wup16, wg16, tm)

    pos_s = past + jnp.arange(dec_t)
    cos_s, sin_s = _rope_tables(jnp.tile(pos_s, dec_b))
    xs2 = x_sample.reshape(rows_s, d_model)
    ps = _project(xs2, gpre, w_cat, cos_s, sin_s, rows_s, rows_s)
    topk_s = min(TOPK_MAX, (past + dec_t) // 4)

    qi_rows = ps["qi"].reshape(rows_s * N_IDX_HEADS, D_IDX)
    w_rows = jnp.broadcast_to(ps["kw"][:, D_IDX:D_IDX + N_IDX_HEADS].reshape(rows_s * N_IDX_HEADS, 1),
                              (rows_s * N_IDX_HEADS, LANES))
    unroll = _largest_divisor(n_pages, 32)
    scores4 = _sample_index(page_table, qi_rows.reshape(dec_b, dec_t * N_IDX_HEADS, D_IDX),
                            w_rows.reshape(dec_b, dec_t * N_IDX_HEADS, LANES),
                            jnp.transpose(cache_kidx[0], (0, 2, 1)), unroll)
    scores_past = scores4.transpose(0, 2, 1, 3).reshape(rows_s, past)
    thr, cut, nmn = _sample_select(scores_past, qi_rows, w_rows, ps["kw16"], topk_s, dec_t)

    kv_of_head = np.arange(N_HEADS_A) // (N_HEADS_A // N_KV_A)
    place = jnp.asarray((kv_of_head[:, None] == np.arange(N_KV_A)[None, :]).astype(np.float32)).astype(BF16)
    q_bd = (ps["qa"].reshape(dec_b, dec_t, N_HEADS_A, 1, HEAD_DIM) * place[None, None, :, :, None]).reshape(
        dec_b, dec_t * N_HEADS_A, N_KV_A * HEAD_DIM)
    rb_col = jnp.broadcast_to(rel_bias[:, :, None], rel_bias.shape + (LANES,))
    b3 = lambda a: a.reshape(dec_b, dec_t, LANES)
    o_qh = _sample_attn(page_table, rb_col, q_bd, scores4, b3(thr), b3(cut), b3(nmn),
                        ps["k16"], ps["v16"], _pages_t(cache_k[0]), _pages_t(cache_v[0]), unroll)
    oa_s = o_qh.reshape(rows_s, D_A)

    chunk_s = RET_CHUNK
    padt = lambda a: jnp.pad(a.reshape(dec_b, dec_t, D_B), ((0, 0), (0, chunk_s - dec_t), (0, 0)))
    yn_s, state_s = _retention(padt(ps["qb"]), padt(ps["kb"]), padt(ps["vb"]),
                               state_ret[0].astype(F32), float(math.gcd(dec_t, RET_CHUNK)))
    y_sample = _merge(xs2, oa_s, yn_s[:, :dec_t].reshape(rows_s, D_B), ps["ga"], ps["gb"],
                      p_sample[0].reshape(rows_s, -1), wo16, gpost, wup16, wg16, rows_s)

    def kv(a_t, b, t):
        a = a_t.reshape(a_t.shape[0], N_KV_A, HEAD_DIM, -1).transpose(0, 3, 1, 2)
        return a.reshape(1, b, t, N_KV_A, HEAD_DIM)

    def ki(a_t, b, t):
        return a_t.transpose(0, 2, 1).reshape(1, b, t, D_IDX)

    return (
        y_prompt.reshape(batch, seq, d_model),
        y_sample.reshape(dec_b, dec_t, d_model),
        kv(pp["ka_t"], batch, seq), kv(pp["va_t"], batch, seq),
        ki(pp["ki_t"], batch, seq),
        state_p[None].astype(state_ret.dtype),
        kv(ps["ka_t"], dec_b, dec_t), kv(ps["va_t"], dec_b, dec_t),
        ki(ps["ki_t"], dec_b, dec_t),
        state_s[None].astype(state_ret.dtype),
    )
```

```python
import functools
import math

import jax
import jax.numpy as jnp
import numpy as np
from jax import lax
from jax.experimental import pallas as pl
from jax.experimental.pallas import tpu as pltpu

F32 = jnp.float32
BF16 = jnp.bfloat16
I32 = jnp.int32

HEAD_DIM = 64
N_HEADS_A = 8
N_KV_A = 2
N_IDX_HEADS = 8
D_IDX = 64
TOPK_MAX = 256
N_HEADS_B = 8
D_A = N_HEADS_A * HEAD_DIM
D_B = N_HEADS_B * HEAD_DIM
N_BUCKETS = 32
MAX_DISTANCE = 128
ROPE_BASE = 10000.0
RET_CHUNK = 128
PAGE_SIZE = 128
EPS = 1e-6
SPLIT_SIZES = (D_A, N_KV_A * HEAD_DIM, N_KV_A * HEAD_DIM, N_IDX_HEADS * D_IDX, D_IDX, N_IDX_HEADS,
               D_A, D_B, D_B, D_B, D_B)

LANES = 128
SUBLANES = 8
BLK = 128
VMEM_LIMIT = 48 * 1024 * 1024
VMEM_LIMIT_PAGED = 56 * 1024 * 1024

INT_MIN = -(2 ** 31)
KEY_NEG_FLT_MAX = INT_MIN + 0x00800000
NEG_BIG = -1e30
LOG2E = math.log2(math.e)
ONES_ROWS = 16

_C_QA, _C_KA, _C_VA, _C_QI, _C_KW, _C_GA, _C_QB, _C_KB, _C_VB, _C_GB, _C_END = (
    0, 512, 640, 768, 1280, 1408, 1920, 2432, 2944, 3456, 3968)


def _nt_dot(a, b):
    return lax.dot_general(a, b, (((1,), (1,)), ((), ())), preferred_element_type=F32)


def _dot(a, b):
    return jnp.dot(a, b, preferred_element_type=F32)


def _sigmoid(x):
    return 1.0 / (1.0 + jnp.exp(-x))


def _key_to_float(k):
    bits = jnp.where(k >= 0, k, k ^ 0x7FFFFFFF)
    return lax.bitcast_convert_type(bits, F32)


def _t5_bucket(n):
    max_exact = N_BUCKETS // 2
    n = jnp.maximum(n, 0)
    nf = jnp.maximum(n, 1).astype(F32)
    large = max_exact + jnp.floor(jnp.log(nf / max_exact) / math.log(MAX_DISTANCE / max_exact)
                                  * (N_BUCKETS - max_exact)).astype(I32)
    large = jnp.minimum(large, N_BUCKETS - 1)
    return jnp.where(n < max_exact, n, large)


def _bias_from_bucket(bkt, rb_ref, h):
    out = jnp.zeros(bkt.shape, F32)
    for k in range(N_BUCKETS):
        out = jnp.where(bkt == k, rb_ref[k, h], out)
    return out


def _proj_kernel(x_ref, g_ref, w_ref, cos_ref, sin_ref,
                 qa_ref, kat_ref, vat_ref, k16_ref, v16_ref, qi_ref, ga_ref, gb_ref,
                 qb_ref, kb_ref, vb_ref, kw_ref, kw16_ref, kit_ref, h_scr):
    x = x_ref[...]
    ms = jnp.mean(x * x, axis=-1, keepdims=True)
    h_scr[...] = (x * lax.rsqrt(ms + EPS) * g_ref[...]).astype(BF16)

    def mm(lo, hi):
        return _dot(h_scr[...], w_ref[:, lo:hi])

    qa_ref[...] = (mm(_C_QA, _C_KA) * (HEAD_DIM ** -0.5 * LOG2E)).astype(BF16)
    ka = mm(_C_KA, _C_VA)
    kat_ref[0] = ka.T
    k16_ref[...] = ka.astype(BF16)
    va = mm(_C_VA, _C_QI)
    vat_ref[0] = va.T
    v16_ref[...] = va.astype(BF16)
    qi_ref[...] = mm(_C_QI, _C_KW).astype(BF16)
    ga_ref[...] = mm(_C_GA, _C_QB).astype(BF16)
    gb_ref[...] = mm(_C_GB, _C_END).astype(BF16)
    vb_ref[...] = mm(_C_VB, _C_GB).astype(BF16)

    cos = cos_ref[...]
    sin = sin_ref[...]
    lane = lax.broadcasted_iota(I32, cos.shape, 1)
    first_half = (lane % HEAD_DIM) < (HEAD_DIM // 2)

    def rope(z, scale):
        outs = []
        for g in range(z.shape[1] // LANES):
            zg = z[:, g * LANES:(g + 1) * LANES]
            partner = jnp.where(first_half, pltpu.roll(zg, LANES - HEAD_DIM // 2, 1),
                                pltpu.roll(zg, HEAD_DIM // 2, 1))
            r = zg * cos + partner * sin
            if scale != 1.0:
                r = r * scale
            outs.append(r.astype(BF16))
        return jnp.concatenate(outs, axis=1)

    qb_ref[...] = rope(mm(_C_QB, _C_KB), 1.0)
    kb_ref[...] = rope(mm(_C_KB, _C_VB), HEAD_DIM ** -0.5)

    kw = mm(_C_KW, _C_GA)
    wscale = (N_IDX_HEADS ** -0.5) * (D_IDX ** -0.5)
    kw = kw * jnp.where(lane >= D_IDX, wscale, 1.0)
    kw_ref[...] = kw
    kw16_ref[...] = kw.astype(BF16)
    kit_ref[0] = kw.T[:D_IDX]


def _project(x2d, g_pre, w_cat, cos_t, sin_t, tm, seq):
    rows = x2d.shape[0]
    d_model = x2d.shape[1]
    n_tab = cos_t.shape[0] // tm
    tiles = seq // tm
    row_spec = lambda w: pl.BlockSpec((tm, w), lambda i: (i, 0))
    t_spec = lambda w: pl.BlockSpec((1, w, tm), lambda i: (i // tiles, 0, i % tiles))
    outs = [
        ("qa", D_A, BF16), ("ka_t", LANES, F32), ("va_t", LANES, F32), ("k16", LANES, BF16), ("v16", LANES, BF16),
        ("qi", N_IDX_HEADS * D_IDX, BF16), ("ga", D_A, BF16), ("gb", D_B, BF16),
        ("qb", D_B, BF16), ("kb", D_B, BF16), ("vb", D_B, BF16), ("kw", LANES, F32), ("kw16", LANES, BF16),
        ("ki_t", D_IDX, F32),
    ]
    transposed = lambda n: n.endswith("_t")
    res = pl.pallas_call(
        _proj_kernel,
        grid=(rows // tm,),
        in_specs=[
            row_spec(d_model),
            pl.BlockSpec((1, d_model), lambda i: (0, 0)),
            pl.BlockSpec(w_cat.shape, lambda i: (0, 0), pipeline_mode=pl.Buffered(1)),
            pl.BlockSpec((tm, LANES), lambda i: (i % n_tab, 0)),
            pl.BlockSpec((tm, LANES), lambda i: (i % n_tab, 0)),
        ],
        out_specs=[t_spec(w) if transposed(n) else row_spec(w) for n, w, _ in outs],
        out_shape=[jax.ShapeDtypeStruct((rows // seq, w, seq) if transposed(n) else (rows, w), dt)
                   for n, w, dt in outs],
        scratch_shapes=[pltpu.VMEM((tm, d_model), BF16)],
        compiler_params=pltpu.CompilerParams(dimension_semantics=("arbitrary",), vmem_limit_bytes=VMEM_LIMIT),
        name="proj",
    )(x2d, g_pre, w_cat, cos_t, sin_t)
    return {n: r for (n, _, _), r in zip(outs, res)}


def _dsa_prompt_kernel(rb_ref, qi_ref, kwq_ref, qin_ref, kwqn_ref, kw16_ref, qa_ref, k16_ref, v16_ref, oa_ref,
                       sc2, bt, vt, lgs, acc_scr, cut_scr, key_scr, cnt_scr, *, topk, search_blk, pair):
    b = pl.program_id(0)
    s = pl.program_id(1)
    n_steps = pl.num_programs(1)
    blocks = range(pair)
    js = [s * pair + ab for ab in blocks]
    scs = [sc2.at[s % 2, ab] for ab in blocks]
    scs_next = [sc2.at[1 - s % 2, ab] for ab in blocks]
    row = lax.broadcasted_iota(I32, (BLK, BLK), 0)
    col = lax.broadcasted_iota(I32, (BLK, BLK), 1)
    rows_of = lambda ab: slice(ab * BLK, (ab + 1) * BLK)

    @pl.when((b == 0) & (s == 0))
    def _init_bias():
        for d in range(2):
            bkt = _t5_bucket(col - row + d * BLK)
            for h in range(N_HEADS_A):
                bt[d, h] = (_bias_from_bucket(bkt, rb_ref, h) - rb_ref[N_BUCKETS - 1, h]) * LOG2E
        bt[2] = jnp.zeros(bt.shape[1:], F32)

    @pl.when(s == 0)
    def _clear_values():
        vt[...] = jnp.zeros(vt.shape, BF16)

    ones = jnp.ones((ONES_ROWS, BLK), BF16)
    for ab in blocks:
        vblk = v16_ref[pl.ds(pl.multiple_of(js[ab] * BLK, BLK), BLK), :]
        vtj = vblk.astype(F32).T.astype(BF16)
        vt[js[ab]] = jnp.concatenate([piece for n in range(N_KV_A)
                                      for piece in (vtj[n * HEAD_DIM:(n + 1) * HEAD_DIM], ones)], axis=0)

    def indexer_queries(qi_blk_ref, kw_blk_ref, ab):
        qi = qi_blk_ref[rows_of(ab), :]
        qi_stack = jnp.concatenate([qi[:, h * D_IDX:(h + 1) * D_IDX] for h in range(N_IDX_HEADS)], axis=0)
        return qi_stack, kw_blk_ref[rows_of(ab), :].T

    def score_products(c, queries):
        out = []
        for u in range(search_blk):
            i = c * search_blk + u
            kib = kw16_ref[pl.ds(pl.multiple_of(i * BLK, BLK), BLK), :][:, :D_IDX]
            out.append(_nt_dot(kib, queries[0]))
        return out

    def score_finish(c, products, dst, queries, first_q):
        wi_t = queries[1]
        for u, s in enumerate(products):
            i = c * search_blk + u
            acc = jnp.zeros((BLK, BLK), F32)
            for h in range(N_IDX_HEADS):
                acc = acc + wi_t[D_IDX + h:D_IDX + h + 1, :] * jnp.maximum(s[:, h * BLK:(h + 1) * BLK], 0.0)
            dst[i] = jnp.where(row + i * BLK <= col + first_q, acc, -jnp.inf)

    def score_group(c, dst, queries, first_q):
        score_finish(c, score_products(c, queries), dst, queries, first_q)

    @pl.when(s == 0)
    def _own_scores():
        for ab in blocks:
            score_group(0, scs[ab], indexer_queries(qi_ref, kwq_ref, ab), ab * BLK)

    n_chunk = (js[0] + search_blk) // search_blk

    def col_sum(w):
        return jnp.sum(w.reshape(BLK // SUBLANES, SUBLANES, BLK), axis=0)

    g_per_kv = N_HEADS_A // N_KV_A
    q_stacks = []
    for ab in blocks:
        qa = qa_ref[rows_of(ab), :]
        q_stacks.append([jnp.concatenate([qa[:, (n * g_per_kv + g) * HEAD_DIM:(n * g_per_kv + g + 1) * HEAD_DIM]
                                          for g in range(g_per_kv)], axis=0) for n in range(N_KV_A)])

    kf = float(topk)
    n_units = search_blk * N_KV_A

    def search(n_groups):
        def count_ge(cands, unit=None):
            raw = []
            if unit is not None:
                n = unit[1]
                for ab in blocks:
                    for c in range(n_groups):
                        i = c * search_blk + unit[0]
                        raw.append((ab, i, _nt_dot(k16_ref[i * BLK:(i + 1) * BLK, n * HEAD_DIM:(n + 1) * HEAD_DIM],
                                                   q_stacks[ab][n])))
            counts = []
            for ab in blocks:
                acc = jnp.zeros((SUBLANES, BLK), F32)
                for c in range(n_groups):
                    for u in range(search_blk):
                        acc = acc + col_sum(jnp.where(scs[ab][c * search_blk + u] >= cands[ab], 1.0, 0.0))
                counts.append(jnp.sum(acc, axis=0, keepdims=True))
            for ab, i, lg in raw:
                for g in range(g_per_kv):
                    lgs[ab, i, unit[1] * g_per_kv + g] = lg[:, g * BLK:(g + 1) * BLK]
            return counts

        def bit_step(bit, carry, unit=None):
            cands = [key | jnp.left_shift(jnp.int32(1), bit) for key, _ in carry]
            counts = count_ge([_key_to_float(cand) for cand in cands], unit)
            return tuple((jnp.where(cnt >= kf, cand, key), jnp.where(cnt >= kf, cnt, cnt_key))
                         for (key, cnt_key), cand, cnt in zip(carry, cands, counts))

        c0s = count_ge([jnp.zeros((1, BLK), F32)] * pair)
        carry = tuple((jnp.where(c0 >= kf, 0, INT_MIN).astype(I32), jnp.where(c0 >= kf, c0, 0.0)) for c0 in c0s)
        for it in range(n_units):
            carry = bit_step(30 - it, carry, (it // N_KV_A, it % N_KV_A))
        n_rest = jnp.where((js[-1] + 1) * BLK <= topk, 0, 31 - n_units)
        carry = lax.fori_loop(0, n_rest, lambda it, cr: bit_step(30 - n_units - it, cr), carry)
        for ab in blocks:
            key_scr[ab] = jnp.broadcast_to(carry[ab][0], key_scr.shape[1:])
            cnt_scr[ab] = jnp.broadcast_to(carry[ab][1], cnt_scr.shape[1:])

    for n_groups in range(1, sc2.shape[2] // search_blk + 1):
        pl.when(n_chunk == n_groups)(functools.partial(search, n_groups))

    z8 = jnp.zeros((SUBLANES, BLK), F32)
    n_keys_max = sc2.shape[2] * BLK
    vrows = HEAD_DIM + ONES_ROWS

    def col_max(w):
        return jnp.max(w.reshape(BLK // SUBLANES, SUBLANES, BLK), axis=0)

    n_chunk_next = (js[0] + pair + search_blk) // search_blk

    def attend(ab):
        j, sc, sc_next = js[ab], scs[ab], scs_next[ab]
        lg_ab, acc_ab = lgs.at[ab], acc_scr.at[ab]
        key = key_scr[ab, 0:1, :]
        cnt_ge = jnp.where(key < KEY_NEG_FLT_MAX, 0.0, cnt_scr[ab, 0:1, :])
        thr = _key_to_float(jnp.maximum(key, KEY_NEG_FLT_MAX))
        straddle = jnp.max(jnp.where(cnt_ge > kf, 1.0, 0.0))
        cut_scr[ab] = jnp.full(cut_scr.shape[1:], 2 * n_keys_max, I32)

        @pl.when(straddle > 0.0)
        def _tie_cut():
            def gt_body(i, acc):
                return acc + col_sum(jnp.where(sc[i] > thr, 1.0, 0.0))
            cnt_gt = jnp.sum(lax.fori_loop(0, j + 1, gt_body, z8), axis=0, keepdims=True)
            need = kf - cnt_gt

            def count_tie(cand):
                def body(i, acc):
                    t = sc[i]
                    hit = jnp.where(t == thr, jnp.where(row + i * BLK < cand, 1.0, 0.0), 0.0)
                    return acc + col_sum(hit)
                acc = lax.fori_loop(0, j + 1, body, z8)
                return jnp.sum(acc, axis=0, keepdims=True)

            def cut_body(it, cut):
                cand = cut | jnp.left_shift(jnp.int32(1), (n_keys_max.bit_length() - 1) - it)
                return jnp.where(count_tie(cand) <= need, cand, cut)

            cut = lax.fori_loop(0, n_keys_max.bit_length(), cut_body, jnp.zeros((1, BLK), I32))
            cut_scr[ab] = jnp.broadcast_to(cut, cut_scr.shape[1:])

        cut = cut_scr[ab, 0:1, :]

        def mask_body(c, m8, biased):
            m8 = list(m8)
            for u in range(search_blk):
                i = c * search_blk + u
                t = sc[i]
                tie_keep = jnp.where(row + i * BLK < cut, 0.0, -jnp.inf)
                nm = jnp.where(t > thr, 0.0, jnp.where(t == thr, tie_keep, -jnp.inf))
                near = jnp.clip(j - i, 0, 2)
                for h in range(N_HEADS_A):
                    x = lg_ab[i, h] + nm
                    if biased:
                        x = x + bt[near, h]
                    lg_ab[i, h] = x
                    m8[h] = jnp.maximum(m8[h], col_max(x))
            return tuple(m8)

        n_far = jnp.maximum(n_chunk - 2, 0)
        neg8 = jnp.full((SUBLANES, BLK), NEG_BIG, F32)
        m8 = lax.fori_loop(0, n_far, functools.partial(mask_body, biased=False), (neg8,) * N_HEADS_A)
        m8 = lax.fori_loop(n_far, n_chunk, functools.partial(mask_body, biased=True), m8)
        m_row = [jnp.max(m8[h], axis=0, keepdims=True) for h in range(N_HEADS_A)]

        acc_ab[...] = jnp.zeros(acc_ab.shape, F32)
        next_queries = indexer_queries(qin_ref, kwqn_ref, ab)
        next_first_q = (j + pair) * BLK

        def pv_body(c, carry):
            pv = [None] * N_KV_A
            for u in range(search_blk):
                i = c * search_blk + u
                vti = vt[i]
                for n in range(N_KV_A):
                    ps = [jnp.exp2(lg_ab[i, n * g_per_kv + g] - m_row[n * g_per_kv + g]).astype(BF16)
                          for g in range(g_per_kv)]
                    t = _dot(vti[n * vrows:(n + 1) * vrows, :], jnp.concatenate(ps, axis=1))
                    pv[n] = t if pv[n] is None else pv[n] + t
            for n in range(N_KV_A):
                acc_ab[n] += pv[n]
            score_group(c, sc_next, next_queries, next_first_q)
            return carry

        lax.fori_loop(0, n_chunk, pv_body, 0)

        @pl.when((n_chunk_next > n_chunk) & (s + 1 < n_steps))
        def _next_reaches_one_more_group():
            score_group(n_chunk, sc_next, next_queries, next_first_q)

        parts = []
        for n in range(N_KV_A):
            acc = acc_ab[n]
            out = acc[:HEAD_DIM] * (1.0 / acc[HEAD_DIM:HEAD_DIM + 1])
            parts += [out[:, g * BLK:(g + 1) * BLK] for g in range(g_per_kv)]
        oa_ref[rows_of(ab), :] = jnp.concatenate(parts, axis=0).T.astype(BF16)

    for ab in blocks:
        attend(ab)


def _dsa_prompt(p, rel_bias, batch, seq):
    nblk = seq // BLK
    topk = min(TOPK_MAX, seq // 4)
    search_blk = 4 if nblk % 4 == 0 else 1
    pair = 2 if search_blk % 2 == 0 else 1
    n_steps = nblk // pair
    qrow = lambda w: pl.BlockSpec((pair * BLK, w), lambda b, s: (b * n_steps + s, 0))
    qnext = lambda w: pl.BlockSpec((pair * BLK, w),
                                   lambda b, s: (b * n_steps + jnp.minimum(s + 1, n_steps - 1), 0))
    brow = lambda w: pl.BlockSpec((seq, w), lambda b, s: (b, 0))
    return pl.pallas_call(
        functools.partial(_dsa_prompt_kernel, topk=topk, search_blk=search_blk, pair=pair),
        grid=(batch, n_steps),
        in_specs=[
            pl.BlockSpec(memory_space=pltpu.SMEM),
            qrow(N_IDX_HEADS * D_IDX), qrow(LANES), qnext(N_IDX_HEADS * D_IDX), qnext(LANES),
            brow(LANES), qrow(D_A), brow(LANES), brow(LANES),
        ],
        out_specs=qrow(D_A),
        out_shape=jax.ShapeDtypeStruct((batch * seq, D_A), BF16),
        scratch_shapes=[
            pltpu.VMEM((2, pair, nblk, BLK, BLK), F32),
            pltpu.VMEM((3, N_HEADS_A, BLK, BLK), F32),
            pltpu.VMEM((nblk, N_KV_A * (HEAD_DIM + ONES_ROWS), BLK), BF16),
            pltpu.VMEM((pair, nblk, N_HEADS_A, BLK, BLK), F32),
            pltpu.VMEM((pair, N_KV_A, HEAD_DIM + ONES_ROWS, (N_HEADS_A // N_KV_A) * BLK), F32),
            pltpu.VMEM((pair, SUBLANES, BLK), I32),
            pltpu.VMEM((pair, SUBLANES, BLK), I32),
            pltpu.VMEM((pair, SUBLANES, BLK), F32),
        ],
        compiler_params=pltpu.CompilerParams(dimension_semantics=("arbitrary", "arbitrary"),
                                             vmem_limit_bytes=VMEM_LIMIT),
        name="dsa_prompt",
    )(rel_bias, p["qi"], p["kw"], p["qi"], p["kw"], p["kw16"], p["qa"], p["k16"], p["v16"])


def _retention_kernel(q_ref, k_ref, v_ref, s0_ref, dmat_ref, din_ref, dout_ref, decm_ref, bdm_ref,
                      yn_ref, s_out_ref, s_scr, *, n_sub, pairs_per_phase):
    c = pl.program_id(1)
    chunk = din_ref.shape[0]

    @pl.when(c == 0)
    def _load_state():
        z = jnp.zeros((HEAD_DIM, HEAD_DIM), F32)
        for p in range(N_HEADS_B // 2):
            pair = jnp.concatenate([jnp.concatenate([s0_ref[0, 2 * p], z], axis=1),
                                    jnp.concatenate([z, s0_ref[0, 2 * p + 1]], axis=1)], axis=0)
            s_scr[p] = pair.T

    lo = lax.broadcasted_iota(I32, (chunk, LANES), 1) < HEAD_DIM
    zero = jnp.zeros((), BF16)
    n_pairs = N_HEADS_B // 2
    lanes = [slice(p * LANES, (p + 1) * LANES) for p in range(n_pairs)]
    rows = [slice(u * chunk, (u + 1) * chunk) for u in range(n_sub)]

    def run(pairs):
        units = [(p, u) for p in pairs for u in range(n_sub)]
        qd, kd, vt, s_a, s_b, intra, upd, inter = {}, {}, {}, {}, {}, {}, {}, {}
        for p, u in units:
            q, k, v = q_ref[0, rows[u], lanes[p]], k_ref[0, rows[u], lanes[p]], v_ref[0, rows[u], lanes[p]]
            qd[p, u] = (q.astype(F32) * din_ref[:, lanes[p]]).astype(BF16)
            kd[p, u] = (k.astype(F32) * dout_ref[:, lanes[p]]).astype(BF16)
            vt[p, u] = v.astype(F32).T.astype(BF16)
            s_a[p, u] = (_nt_dot(jnp.where(lo, k, zero), q) * dmat_ref[2 * p]).astype(BF16)
            s_b[p, u] = (_nt_dot(jnp.where(lo, zero, k), q) * dmat_ref[2 * p + 1]).astype(BF16)
        for p, u in units:
            intra[p, u] = jnp.concatenate([_dot(vt[p, u][:HEAD_DIM], s_a[p, u]),
                                           _dot(vt[p, u][HEAD_DIM:], s_b[p, u])], axis=0)
            upd[p, u] = _dot(vt[p, u], kd[p, u]) * bdm_ref[...]
        for p in pairs:
            st = s_scr[p]
            for u in range(n_sub):
                inter[p, u] = _nt_dot(st.astype(BF16), qd[p, u])
                st = st * decm_ref[p] + upd[p, u]
            s_scr[p] = st
            pair = st.T
            s_out_ref[0, 2 * p] = pair[:HEAD_DIM, :HEAD_DIM]
            s_out_ref[0, 2 * p + 1] = pair[HEAD_DIM:, HEAD_DIM:]
        for p, u in units:
            o = (inter[p, u] + intra[p, u]).reshape(2, HEAD_DIM, chunk)
            dlt = o - jnp.mean(o, axis=1, keepdims=True)
            var = jnp.mean(dlt * dlt, axis=1, keepdims=True)
            yn_ref[0, rows[u], lanes[p]] = (dlt * lax.rsqrt(var + EPS)).reshape(LANES, chunk).T.astype(BF16)

    for group in range(0, n_pairs, pairs_per_phase):
        run(range(group, group + pairs_per_phase))


def _ret_tables(chunk, n_real):
    h = N_HEADS_B
    log_g = jnp.log1p(-jnp.exp2(-5.0 - jnp.arange(h, dtype=F32)))
    i = jnp.arange(chunk, dtype=F32)
    diff = i[:, None] - i[None, :]
    dmat = jnp.where(diff >= 0, jnp.exp(jnp.maximum(diff, 0.0)[None] * log_g[:, None, None]), 0.0)
    dmat_t = jnp.swapaxes(dmat, 1, 2)
    dec_in = jnp.exp((i + 1.0)[:, None] * log_g[None, :])
    dec_out = jnp.exp((n_real - 1.0 - i)[:, None] * log_g[None, :])
    dec_chunk = jnp.exp(n_real * log_g)
    din = jnp.repeat(dec_in, HEAD_DIM, axis=1)
    dout = jnp.repeat(dec_out, HEAD_DIM, axis=1)
    head_of = np.arange(LANES) // HEAD_DIM
    bd = jnp.asarray((head_of[:, None] == head_of[None, :]).astype(np.float32))
    dc = jnp.repeat(dec_chunk, HEAD_DIM).reshape(h // 2, LANES)
    decm = dc[:, :, None] * bd[None]
    return dmat_t, din, dout, decm, bd


def _retention(q3, k3, v3, s0, n_real):
    batch, seq, _ = q3.shape
    chunk = min(RET_CHUNK, seq)
    n_sub = _largest_divisor(seq // chunk, 8)
    dmat_t, din, dout, decm, bd = _ret_tables(chunk, n_real)
    tok = pl.BlockSpec((1, n_sub * chunk, D_B), lambda b, c: (b, c, 0))
    st = pl.BlockSpec((1,) + s0.shape[1:], lambda b, c: (b, 0, 0, 0))
    full = lambda a: pl.BlockSpec(a.shape, lambda b, c: (0,) * a.ndim)
    return pl.pallas_call(
        functools.partial(_retention_kernel, n_sub=n_sub,
                          pairs_per_phase=max(1, min(N_HEADS_B // 2, 8 // n_sub))),
        grid=(batch, seq // (n_sub * chunk)),
        in_specs=[tok, tok, tok, st, full(dmat_t), full(din), full(dout), full(decm), full(bd)],
        out_specs=[tok, st],
        out_shape=[jax.ShapeDtypeStruct((batch, seq, D_B), BF16),
                   jax.ShapeDtypeStruct(s0.shape, F32)],
        scratch_shapes=[pltpu.VMEM((N_HEADS_B // 2, LANES, LANES), F32)],
        compiler_params=pltpu.CompilerParams(dimension_semantics=("arbitrary", "arbitrary"),
                                             vmem_limit_bytes=VMEM_LIMIT),
        name="retention",
    )(q3, k3, v3, s0, dmat_t, din, dout, decm, bd)


def _merge_kernel(x_ref, oa_ref, yn_ref, ga_ref, gb_ref, p_ref, wo_ref, gp_ref, wup_ref, wg_ref, out_ref):
    ga = ga_ref[...].astype(F32)
    gb = gb_ref[...].astype(F32)
    ya = (ga * _sigmoid(ga) * oa_ref[...].astype(F32)).astype(BF16)
    yb = (gb * _sigmoid(gb) * yn_ref[...].astype(F32)).astype(BF16)
    y = _dot(ya, wo_ref[0:D_A, :]) + _dot(yb, wo_ref[D_A:D_A + D_B, :])
    ms = jnp.mean(y * y, axis=-1, keepdims=True)
    x1 = x_ref[...] + y * lax.rsqrt(ms + EPS) * gp_ref[...]
    ple = _dot(p_ref[...].astype(BF16), wup_ref[...])
    gate = _sigmoid(_dot(x1.astype(BF16), wg_ref[...]))
    out_ref[...] = x1 + ple * gate


def _merge(x2d, oa, yn, ga, gb, p2d, wo16, g_post, wup16, wg16, tm):
    rows, d_model = x2d.shape
    row = lambda w: pl.BlockSpec((tm, w), lambda i: (i, 0))
    full = lambda a: pl.BlockSpec(a.shape, lambda i: (0, 0))
    return pl.pallas_call(
        _merge_kernel,
        grid=(rows // tm,),
        in_specs=[row(d_model), row(D_A), row(D_B), row(D_A), row(D_B), row(p2d.shape[1]),
                  full(wo16), full(g_post), full(wup16), full(wg16)],
        out_specs=row(d_model),
        out_shape=jax.ShapeDtypeStruct((rows, d_model), F32),
        compiler_params=pltpu.CompilerParams(dimension_semantics=("arbitrary",), vmem_limit_bytes=VMEM_LIMIT),
        name="merge",
    )(x2d, oa, yn, ga, gb, p2d, wo16, g_post, wup16, wg16)


def _page_copy(pages_hbm, buf, sem, pt_ref, b, p, slot):
    return pltpu.make_async_copy(pages_hbm.at[pt_ref[b, p]], buf.at[slot, p], sem.at[slot])


def _start_batch_pages(streams, pt_ref, b, slot, n_pages):
    def body(p, c):
        for pages_hbm, buf, sem in streams:
            _page_copy(pages_hbm, buf, sem, pt_ref, b, p, slot).start()
        return c
    lax.fori_loop(0, n_pages, body, 0)


def _wait_batch_pages(streams, pt_ref, b, slot, n_pages):
    for pages_hbm, buf, sem in streams:
        for p in range(n_pages):
            _page_copy(pages_hbm, buf, sem, pt_ref, b, p, slot).wait()


def _sample_index_kernel(pt_ref, qi_ref, w_ref, pages_hbm, out_ref, buf, sem, *, n_pages, unroll):
    b = pl.program_id(0)
    slot = b % 2
    streams = [(pages_hbm, buf, sem)]

    @pl.when(b == 0)
    def _first():
        _start_batch_pages(streams, pt_ref, 0, 0, n_pages)

    _wait_batch_pages(streams, pt_ref, b, slot, n_pages)

    qi = qi_ref[0]
    w = w_ref[0]
    n_q = qi.shape[0] // N_IDX_HEADS

    def run(prefetch):
        def body(c, carry):
            for u in range(unroll):
                p = c * unroll + u
                if prefetch:
                    _page_copy(pages_hbm, buf, sem, pt_ref, b + 1, p, 1 - slot).start()
                s = _dot(qi, buf[slot, p].astype(BF16))
                r = jnp.maximum(s, 0.0) * w
                out_ref[0, c, :, u * PAGE_SIZE:(u + 1) * PAGE_SIZE] = jnp.sum(
                    r.reshape(n_q, N_IDX_HEADS, PAGE_SIZE), axis=1)
            return carry
        lax.fori_loop(0, n_pages // unroll, body, 0)

    has_next = b + 1 < pl.num_programs(0)
    pl.when(has_next)(lambda: run(True))
    pl.when(jnp.logical_not(has_next))(lambda: run(False))


def _sample_index(page_table, qi_qh, w_qh, kidx_t, unroll):
    batch, n_pages = page_table.shape
    n_q = qi_qh.shape[1] // N_IDX_HEADS
    groups = n_pages // unroll
    grid_spec = pltpu.PrefetchScalarGridSpec(
        num_scalar_prefetch=1,
        grid=(batch,),
        in_specs=[pl.BlockSpec((1,) + qi_qh.shape[1:], lambda b, pt: (b, 0, 0)),
                  pl.BlockSpec((1,) + w_qh.shape[1:], lambda b, pt: (b, 0, 0)),
                  pl.BlockSpec(memory_space=pl.ANY)],
        out_specs=pl.BlockSpec((1, groups, n_q, unroll * PAGE_SIZE), lambda b, pt: (b, 0, 0, 0)),
        scratch_shapes=[pltpu.VMEM((2, n_pages) + kidx_t.shape[1:], kidx_t.dtype),
                        pltpu.SemaphoreType.DMA((2,))],
    )
    return pl.pallas_call(
        functools.partial(_sample_index_kernel, n_pages=n_pages, unroll=unroll),
        grid_spec=grid_spec,
        out_shape=jax.ShapeDtypeStruct((batch, groups, n_q, unroll * PAGE_SIZE), F32),
        compiler_params=pltpu.CompilerParams(dimension_semantics=("arbitrary",), vmem_limit_bytes=VMEM_LIMIT),
        name="sample_index",
    )(page_table, qi_qh, w_qh, kidx_t)


def _sample_select_kernel(sp_ref, qi_ref, w_ref, kw16_ref, thr_ref, cut_ref, nmn_ref, snew_scr,
                          *, topk, n_q, lane_chunk):
    rows, past = sp_ref.shape
    r_i = lax.broadcasted_iota(I32, (rows, LANES), 0)
    l_i = lax.broadcasted_iota(I32, (rows, LANES), 1)

    s = _nt_dot(qi_ref[...], kw16_ref[...][:, :D_IDX])
    r = jnp.maximum(s, 0.0) * w_ref[...]
    s_new = jnp.sum(r.reshape(rows, N_IDX_HEADS, LANES), axis=1)
    ok = ((l_i // n_q) == (r_i // n_q)) & ((l_i % n_q) <= (r_i % n_q))
    snew_scr[...] = jnp.where(ok, s_new, -jnp.inf)

    n_chunks = past // lane_chunk
    tiles = lane_chunk // LANES
    kf = float(topk)

    def count(ind):
        acc = ind(snew_scr[...], l_i + past)
        for ch in range(n_chunks):
            x = sp_ref[:, ch * lane_chunk:(ch + 1) * lane_chunk]
            for t in range(tiles):
                idx = l_i + (ch * lane_chunk + t * LANES)
                acc = acc + ind(x[:, t * LANES:(t + 1) * LANES], idx)
        return jnp.broadcast_to(jnp.sum(acc, axis=1, keepdims=True), (rows, LANES))

    c0 = count(lambda x, idx: jnp.where(x >= 0.0, 1.0, 0.0))
    key0 = jnp.where(c0 >= kf, 0, INT_MIN).astype(I32)

    def bit_body(it, key):
        cand = key | jnp.left_shift(jnp.int32(1), 30 - it)
        cf = _key_to_float(cand)
        return jnp.where(count(lambda x, idx: jnp.where(x >= cf, 1.0, 0.0)) >= kf, cand, key)

    key = lax.fori_loop(0, 31, bit_body, key0)
    key = jnp.maximum(key, KEY_NEG_FLT_MAX)
    thr = _key_to_float(key)
    cnt_gt = count(lambda x, idx: jnp.where(x > thr, 1.0, 0.0))
    cnt_ge = count(lambda x, idx: jnp.where(x >= thr, 1.0, 0.0))
    need = kf - cnt_gt
    straddle = jnp.max(jnp.where(cnt_ge > kf, 1.0, 0.0))

    idx_bits = (past + LANES).bit_length()
    thr_ref[...] = thr
    cut_ref[...] = jnp.full((rows, LANES), 1 << idx_bits, I32)

    @pl.when(straddle > 0.0)
    def _tie_cut():
        def cut_body(it, cut):
            cand = cut | jnp.left_shift(jnp.int32(1), (idx_bits - 1) - it)
            cnt = count(lambda x, idx: jnp.where(x == thr, jnp.where(idx < cand, 1.0, 0.0), 0.0))
            return jnp.where(cnt <= need, cand, cut)
        cut_ref[...] = lax.fori_loop(0, idx_bits, cut_body, jnp.zeros((rows, LANES), I32))

    cut = cut_ref[...]
    sn = snew_scr[...]
    tie_keep = jnp.where(l_i + past < cut, 0.0, -jnp.inf)
    nmn_ref[...] = jnp.where(sn > thr, 0.0, jnp.where(sn == thr, tie_keep, -jnp.inf))


def _sample_select(scores_past, qi_rows, w_rows, kw16_s, topk, n_q):
    rows, past = scores_past.shape
    lane_chunk = 2048 if past % 2048 == 0 else LANES
    full = lambda a: pl.BlockSpec(a.shape, lambda i: (0,) * a.ndim)
    o = jax.ShapeDtypeStruct((rows, LANES), F32)
    return pl.pallas_call(
        functools.partial(_sample_select_kernel, topk=topk, n_q=n_q, lane_chunk=lane_chunk),
        grid=(1,),
        in_specs=[full(scores_past), full(qi_rows), full(w_rows), full(kw16_s)],
        out_specs=[pl.BlockSpec((rows, LANES), lambda i: (0, 0))] * 3,
        out_shape=[o, jax.ShapeDtypeStruct((rows, LANES), I32), o],
        scratch_shapes=[pltpu.VMEM((rows, LANES), F32)],
        compiler_params=pltpu.CompilerParams(dimension_semantics=("arbitrary",), vmem_limit_bytes=VMEM_LIMIT),
        name="sample_select",
    )(scores_past, qi_rows, w_rows, kw16_s)


def _sample_attn_kernel(pt_ref, rbc_ref, q_ref, sp_ref, thr_ref, cut_ref, nmn_ref, k16n_ref, v16n_ref,
                        k_hbm, v_hbm, o_ref, kbuf, vbuf, ksem, vsem, lgs, bias_scr, mx_scr,
                        *, n_pages, n_q, unroll):
    b = pl.program_id(0)
    slot = b % 2
    rows = n_q * N_HEADS_A
    groups = n_pages // unroll
    streams = [(k_hbm, kbuf, ksem), (v_hbm, vbuf, vsem)]
    lane = lax.broadcasted_iota(I32, (rows, LANES), 1)
    rq = lax.broadcasted_iota(I32, (rows, LANES), 0) // N_HEADS_A
    expand = lambda a: jnp.concatenate(
        [jnp.broadcast_to(a[t:t + 1], (N_HEADS_A, LANES)) for t in range(n_q)], axis=0)

    @pl.when(b == 0)
    def _first():
        _start_batch_pages(streams, pt_ref, 0, 0, n_pages)
        far = jnp.concatenate([rbc_ref[N_BUCKETS - 1]] * n_q, axis=0)
        for t, dist in enumerate((PAGE_SIZE + rq - lane, rq - lane % n_q)):
            bkt = _t5_bucket(dist)
            tile = jnp.zeros((rows, LANES), F32)
            for k in range(N_BUCKETS - 1):
                tile = jnp.where(bkt == k, jnp.concatenate([rbc_ref[k]] * n_q, axis=0) - far, tile)
            bias_scr[t] = tile * LOG2E

    _wait_batch_pages(streams, pt_ref, b, slot, n_pages)

    thr = expand(thr_ref[0])
    cut = expand(cut_ref[0])
    q = q_ref[0]

    def logits_pass(prefetch):
        def body(c, mx):
            for u in range(unroll):
                p = c * unroll + u
                if prefetch:
                    for pages_hbm, buf, sem in streams:
                        _page_copy(pages_hbm, buf, sem, pt_ref, b + 1, p, 1 - slot).start()
                lg = _dot(q, kbuf[slot, p].astype(BF16))
                sc = expand(sp_ref[0, c, :, u * PAGE_SIZE:(u + 1) * PAGE_SIZE])
                tie_keep = jnp.where(lane + p * PAGE_SIZE < cut, 0.0, -jnp.inf)
                x = lg + jnp.where(sc > thr, 0.0, jnp.where(sc == thr, tie_keep, -jnp.inf))
                if u == unroll - 1:
                    x = x + bias_scr[0] * jnp.where(c == groups - 1, 1.0, 0.0)
                lgs[p] = x
                mx = jnp.maximum(mx, x)
            return mx
        mx_scr[...] = lax.fori_loop(0, groups, body, jnp.full((rows, LANES), NEG_BIG, F32))

    has_next = b + 1 < pl.num_programs(0)
    pl.when(has_next)(lambda: logits_pass(True))
    pl.when(jnp.logical_not(has_next))(lambda: logits_pass(False))

    xn = _nt_dot(q, k16n_ref[...]) + bias_scr[1] + expand(nmn_ref[0])
    m = jnp.max(jnp.maximum(mx_scr[...], xn), axis=1, keepdims=True)

    def pv_body(c, carry):
        lsum, acc = carry
        for u in range(unroll):
            p = c * unroll + u
            pr = jnp.exp2(lgs[p] - m)
            lsum = lsum + pr
            acc = acc + _nt_dot(pr.astype(BF16), vbuf[slot, p].astype(BF16))
        return lsum, acc

    pn = jnp.exp2(xn - m)
    lsum, acc = lax.fori_loop(0, groups, pv_body, (pn, _dot(pn.astype(BF16), v16n_ref[...])))
    out = acc * (1.0 / jnp.sum(lsum, axis=1, keepdims=True))
    head = lax.broadcasted_iota(I32, (rows, HEAD_DIM), 0) % N_HEADS_A
    o_ref[0] = jnp.where(head < N_HEADS_A // N_KV_A, out[:, :HEAD_DIM], out[:, HEAD_DIM:])


def _sample_attn(page_table, rb_col, q_bd, scores4, thr3, cut3, nmn3, k16n, v16n, k_pages_t, v_pages_t, unroll):
    batch, n_pages = page_table.shape
    n_q = scores4.shape[2]
    rows = q_bd.shape[1]
    per_b = lambda a: pl.BlockSpec((1,) + a.shape[1:], lambda b, pt: (b,) + (0,) * (a.ndim - 1))
    full = lambda a: pl.BlockSpec(a.shape, lambda b, pt: (0,) * a.ndim)
    page_buf = pltpu.VMEM((2, n_pages) + k_pages_t.shape[1:], k_pages_t.dtype)
    grid_spec = pltpu.PrefetchScalarGridSpec(
        num_scalar_prefetch=1,
        grid=(batch,),
        in_specs=[full(rb_col), per_b(q_bd), per_b(scores4), per_b(thr3), per_b(cut3), per_b(nmn3),
                  full(k16n), full(v16n), pl.BlockSpec(memory_space=pl.ANY), pl.BlockSpec(memory_space=pl.ANY)],
        out_specs=pl.BlockSpec((1, rows, HEAD_DIM), lambda b, pt: (b, 0, 0)),
        scratch_shapes=[page_buf, page_buf, pltpu.SemaphoreType.DMA((2,)), pltpu.SemaphoreType.DMA((2,)),
                        pltpu.VMEM((n_pages, rows, LANES), F32), pltpu.VMEM((2, rows, LANES), F32),
                        pltpu.VMEM((rows, LANES), F32)],
    )
    return pl.pallas_call(
        functools.partial(_sample_attn_kernel, n_pages=n_pages, n_q=n_q, unroll=unroll),
        grid_spec=grid_spec,
        out_shape=jax.ShapeDtypeStruct((batch, rows, HEAD_DIM), F32),
        compiler_params=pltpu.CompilerParams(dimension_semantics=("arbitrary",),
                                             vmem_limit_bytes=VMEM_LIMIT_PAGED),
        name="sample_attn",
    )(page_table, rb_col, q_bd, scores4, thr3, cut3, nmn3, k16n, v16n, k_pages_t, v_pages_t)


def _rope_tables(pos):
    half = HEAD_DIM // 2
    inv = ROPE_BASE ** (-jnp.arange(half, dtype=F32) / half)
    ang = pos.astype(F32)[:, None] * inv[None, :]
    cos = jnp.cos(ang)
    sin = jnp.sin(ang)
    reps = LANES // HEAD_DIM
    cos_t = jnp.tile(jnp.concatenate([cos, cos], axis=1), (1, reps))
    sin_t = jnp.tile(jnp.concatenate([-sin, sin], axis=1), (1, reps))
    return cos_t, sin_t


def _cat_weight(w_in):
    split = int(np.sum(SPLIT_SIZES[:6]))
    pad = jnp.zeros((w_in.shape[0], LANES - D_IDX - N_IDX_HEADS), BF16)
    return jnp.concatenate([w_in[:, :split].astype(BF16), pad, w_in[:, split:].astype(BF16)], axis=1)


def _pages_t(cache):
    pool, page = cache.shape[:2]
    return jnp.transpose(cache, (0, 2, 3, 1)).reshape(pool, -1, page)


def _largest_divisor(n, cap):
    d = cap
    while n % d:
        d //= 2
    return d


def kernel(x_prompt, x_sample, cache_k, cache_v, cache_kidx, state_ret, page_table, p_prompt, p_sample,
           rel_bias, w_in, w_out, g_pre, g_post, w_ple_up, w_ple_gate):
    batch, seq, d_model = x_prompt.shape
    dec_b, dec_t, _ = x_sample.shape
    depth = w_in.shape[0]
    n_pages = page_table.shape[1]
    past = n_pages * PAGE_SIZE
    rows_s = dec_b * dec_t
    assert depth == 1 and rows_s == LANES and seq % BLK == 0

    w_cat = _cat_weight(w_in[0])
    gpre = g_pre[0].reshape(1, d_model)
    gpost = g_post[0].reshape(1, d_model)
    wo16 = w_out[0].astype(BF16)
    wup16 = w_ple_up[0].astype(BF16)
    wg16 = w_ple_gate[0].astype(BF16)

    tm = _largest_divisor(seq, 1024)
    cos_p, sin_p = _rope_tables(jnp.arange(seq))
    xp2 = x_prompt.reshape(batch * seq, d_model)
    pp = _project(xp2, gpre, w_cat, cos_p, sin_p, tm, seq)
    oa_p = _dsa_prompt(pp, rel_bias, batch, seq)
    r3 = lambda a: a.reshape(batch, seq, D_B)
    s0_p = jnp.zeros((batch, N_HEADS_B, HEAD_DIM, HEAD_DIM), F32)
    yn_p, state_p = _retention(r3(pp["qb"]), r3(pp["kb"]), r3(pp["vb"]), s0_p, float(min(RET_CHUNK, seq)))
    y_prompt = _merge(xp2, oa_p, yn_p.reshape(batch * seq, D_B), pp["ga"], pp["gb"],
                      p_prompt[0].reshape(batch * seq, -1), wo16, gpost, wup16, wg16, tm)

    pos_s = past + jnp.arange(dec_t)
    cos_s, sin_s = _rope_tables(jnp.tile(pos_s, dec_b))
    xs2 = x_sample.reshape(rows_s, d_model)
    ps = _project(xs2, gpre, w_cat, cos_s, sin_s, rows_s, rows_s)
    topk_s = min(TOPK_MAX, (past + dec_t) // 4)

    qi_rows = ps["qi"].reshape(rows_s * N_IDX_HEADS, D_IDX)
    w_rows = jnp.broadcast_to(ps["kw"][:, D_IDX:D_IDX + N_IDX_HEADS].reshape(rows_s * N_IDX_HEADS, 1),
                              (rows_s * N_IDX_HEADS, LANES))
    unroll = _largest_divisor(n_pages, 32)
    scores4 = _sample_index(page_table, qi_rows.reshape(dec_b, dec_t * N_IDX_HEADS, D_IDX),
                            w_rows.reshape(dec_b, dec_t * N_IDX_HEADS, LANES),
                            jnp.transpose(cache_kidx[0], (0, 2, 1)), unroll)
    scores_past = scores4.transpose(0, 2, 1, 3).reshape(rows_s, past)
    thr, cut, nmn = _sample_select(scores_past, qi_rows, w_rows, ps["kw16"], topk_s, dec_t)

    kv_of_head = np.arange(N_HEADS_A) // (N_HEADS_A // N_KV_A)
    place = jnp.asarray((kv_of_head[:, None] == np.arange(N_KV_A)[None, :]).astype(np.float32)).astype(BF16)
    q_bd = (ps["qa"].reshape(dec_b, dec_t, N_HEADS_A, 1, HEAD_DIM) * place[None, None, :, :, None]).reshape(
        dec_b, dec_t * N_HEADS_A, N_KV_A * HEAD_DIM)
    rb_col = jnp.broadcast_to(rel_bias[:, :, None], rel_bias.shape + (LANES,))
    b3 = lambda a: a.reshape(dec_b, dec_t, LANES)
    o_qh = _sample_attn(page_table, rb_col, q_bd, scores4, b3(thr), b3(cut), b3(nmn),
                        ps["k16"], ps["v16"], _pages_t(cache_k[0]), _pages_t(cache_v[0]), unroll)
    oa_s = o_qh.reshape(rows_s, D_A)

    chunk_s = RET_CHUNK
    padt = lambda a: jnp.pad(a.reshape(dec_b, dec_t, D_B), ((0, 0), (0, chunk_s - dec_t), (0, 0)))
    yn_s, state_s = _retention(padt(ps["qb"]), padt(ps["kb"]), padt(ps["vb"]),
                               state_ret[0].astype(F32), float(math.gcd(dec_t, RET_CHUNK)))
    y_sample = _merge(xs2, oa_s, yn_s[:, :dec_t].reshape(rows_s, D_B), ps["ga"], ps["gb"],
                      p_sample[0].reshape(rows_s, -1), wo16, gpost, wup16, wg16, rows_s)

    def kv(a_t, b, t):
        a = a_t.reshape(a_t.shape[0], N_KV_A, HEAD_DIM, -1).transpose(0, 3, 1, 2)
        return a.reshape(1, b, t, N_KV_A, HEAD_DIM)

    def ki(a_t, b, t):
        return a_t.transpose(0, 2, 1).reshape(1, b, t, D_IDX)

    return (
        y_prompt.reshape(batch, seq, d_model),
        y_sample.reshape(dec_b, dec_t, d_model),
        kv(pp["ka_t"], batch, seq), kv(pp["va_t"], batch, seq),
        ki(pp["ki_t"], batch, seq),
        state_p[None].astype(state_ret.dtype),
        kv(ps["ka_t"], dec_b, dec_t), kv(ps["va_t"], dec_b, dec_t),
        ki(ps["ki_t"], dec_b, dec_t),
        state_s[None].astype(state_ret.dtype),
    )
```

```python
import functools
import math

import jax
import jax.numpy as jnp
import numpy as np
from jax import lax
from jax.experimental import pallas as pl
from jax.experimental.pallas import tpu as pltpu

F32 = jnp.float32
BF16 = jnp.bfloat16
I32 = jnp.int32

HEAD_DIM = 64
N_HEADS_A = 8
N_KV_A = 2
N_IDX_HEADS = 8
D_IDX = 64
TOPK_MAX = 256
N_HEADS_B = 8
D_A = N_HEADS_A * HEAD_DIM
D_B = N_HEADS_B * HEAD_DIM
N_BUCKETS = 32
MAX_DISTANCE = 128
ROPE_BASE = 10000.0
RET_CHUNK = 128
PAGE_SIZE = 128
EPS = 1e-6
SPLIT_SIZES = (D_A, N_KV_A * HEAD_DIM, N_KV_A * HEAD_DIM, N_IDX_HEADS * D_IDX, D_IDX, N_IDX_HEADS,
               D_A, D_B, D_B, D_B, D_B)

LANES = 128
SUBLANES = 8
BLK = 128
VMEM_LIMIT = 48 * 1024 * 1024
VMEM_LIMIT_PAGED = 56 * 1024 * 1024

INT_MIN = -(2 ** 31)
KEY_NEG_FLT_MAX = INT_MIN + 0x00800000
NEG_BIG = -1e30
LOG2E = math.log2(math.e)
ONES_ROWS = 16

_C_QA, _C_KA, _C_VA, _C_QI, _C_KW, _C_GA, _C_QB, _C_KB, _C_VB, _C_GB, _C_END = (
    0, 512, 640, 768, 1280, 1408, 1920, 2432, 2944, 3456, 3968)


def _nt_dot(a, b):
    return lax.dot_general(a, b, (((1,), (1,)), ((), ())), preferred_element_type=F32)


def _dot(a, b):
    return jnp.dot(a, b, preferred_element_type=F32)


def _sigmoid(x):
    return 1.0 / (1.0 + jnp.exp(-x))


def _key_to_float(k):
    bits = jnp.where(k >= 0, k, k ^ 0x7FFFFFFF)
    return lax.bitcast_convert_type(bits, F32)


def _t5_bucket(n):
    max_exact = N_BUCKETS // 2
    n = jnp.maximum(n, 0)
    nf = jnp.maximum(n, 1).astype(F32)
    large = max_exact + jnp.floor(jnp.log(nf / max_exact) / math.log(MAX_DISTANCE / max_exact)
                                  * (N_BUCKETS - max_exact)).astype(I32)
    large = jnp.minimum(large, N_BUCKETS - 1)
    return jnp.where(n < max_exact, n, large)


def _bias_from_bucket(bkt, rb_ref, h):
    out = jnp.zeros(bkt.shape, F32)
    for k in range(N_BUCKETS):
        out = jnp.where(bkt == k, rb_ref[k, h], out)
    return out


def _proj_kernel(x_ref, g_ref, w_ref, cos_ref, sin_ref,
                 qa_ref, kat_ref, vat_ref, k16_ref, v16_ref, qi_ref, ga_ref, gb_ref,
                 qb_ref, kb_ref, vb_ref, kw_ref, kw16_ref, kit_ref, h_scr):
    x = x_ref[...]
    ms = jnp.mean(x * x, axis=-1, keepdims=True)
    h_scr[...] = (x * lax.rsqrt(ms + EPS) * g_ref[...]).astype(BF16)

    def mm(lo, hi):
        return _dot(h_scr[...], w_ref[:, lo:hi])

    qa_ref[...] = (mm(_C_QA, _C_KA) * (HEAD_DIM ** -0.5 * LOG2E)).astype(BF16)
    ka = mm(_C_KA, _C_VA)
    kat_ref[0] = ka.T
    k16_ref[...] = ka.astype(BF16)
    va = mm(_C_VA, _C_QI)
    vat_ref[0] = va.T
    v16_ref[...] = va.astype(BF16)
    qi_ref[...] = mm(_C_QI, _C_KW).astype(BF16)
    ga_ref[...] = mm(_C_GA, _C_QB).astype(BF16)
    gb_ref[...] = mm(_C_GB, _C_END).astype(BF16)
    vb_ref[...] = mm(_C_VB, _C_GB).astype(BF16)

    cos = cos_ref[...]
    sin = sin_ref[...]
    lane = lax.broadcasted_iota(I32, cos.shape, 1)
    first_half = (lane % HEAD_DIM) < (HEAD_DIM // 2)

    def rope(z, scale):
        outs = []
        for g in range(z.shape[1] // LANES):
            zg = z[:, g * LANES:(g + 1) * LANES]
            partner = jnp.where(first_half, pltpu.roll(zg, LANES - HEAD_DIM // 2, 1),
                                pltpu.roll(zg, HEAD_DIM // 2, 1))
            r = zg * cos + partner * sin
            if scale != 1.0:
                r = r * scale
            outs.append(r.astype(BF16))
        return jnp.concatenate(outs, axis=1)

    qb_ref[...] = rope(mm(_C_QB, _C_KB), 1.0)
    kb_ref[...] = rope(mm(_C_KB, _C_VB), HEAD_DIM ** -0.5)

    kw = mm(_C_KW, _C_GA)
    wscale = (N_IDX_HEADS ** -0.5) * (D_IDX ** -0.5)
    kw = kw * jnp.where(lane >= D_IDX, wscale, 1.0)
    kw_ref[...] = kw
    kw16_ref[...] = kw.astype(BF16)
    kit_ref[0] = kw.T[:D_IDX]


def _project(x2d, g_pre, w_cat, cos_t, sin_t, tm, seq):
    rows = x2d.shape[0]
    d_model = x2d.shape[1]
    n_tab = cos_t.shape[0] // tm
    tiles = seq // tm
    row_spec = lambda w: pl.BlockSpec((tm, w), lambda i: (i, 0))
    t_spec = lambda w: pl.BlockSpec((1, w, tm), lambda i: (i // tiles, 0, i % tiles))
    outs = [
        ("qa", D_A, BF16), ("ka_t", LANES, F32), ("va_t", LANES, F32), ("k16", LANES, BF16), ("v16", LANES, BF16),
        ("qi", N_IDX_HEADS * D_IDX, BF16), ("ga", D_A, BF16), ("gb", D_B, BF16),
        ("qb", D_B, BF16), ("kb", D_B, BF16), ("vb", D_B, BF16), ("kw", LANES, F32), ("kw16", LANES, BF16),
        ("ki_t", D_IDX, F32),
    ]
    transposed = lambda n: n.endswith("_t")
    res = pl.pallas_call(
        _proj_kernel,
        grid=(rows // tm,),
        in_specs=[
            row_spec(d_model),
            pl.BlockSpec((1, d_model), lambda i: (0, 0)),
            pl.BlockSpec(w_cat.shape, lambda i: (0, 0), pipeline_mode=pl.Buffered(1)),
            pl.BlockSpec((tm, LANES), lambda i: (i % n_tab, 0)),
            pl.BlockSpec((tm, LANES), lambda i: (i % n_tab, 0)),
        ],
        out_specs=[t_spec(w) if transposed(n) else row_spec(w) for n, w, _ in outs],
        out_shape=[jax.ShapeDtypeStruct((rows // seq, w, seq) if transposed(n) else (rows, w), dt)
                   for n, w, dt in outs],
        scratch_shapes=[pltpu.VMEM((tm, d_model), BF16)],
        compiler_params=pltpu.CompilerParams(dimension_semantics=("arbitrary",), vmem_limit_bytes=VMEM_LIMIT),
        name="proj",
    )(x2d, g_pre, w_cat, cos_t, sin_t)
    return {n: r for (n, _, _), r in zip(outs, res)}


def _dsa_prompt_kernel(rb_ref, qi_ref, kwq_ref, qin_ref, kwqn_ref, kw16_ref, qa_ref, k16_ref, v16_ref, oa_ref,
                       sc2, bt, vt, lgs, acc_scr, cut_scr, key_scr, cnt_scr, *, topk, search_blk, pair):
    b = pl.program_id(0)
    s = pl.program_id(1)
    n_steps = pl.num_programs(1)
    blocks = range(pair)
    js = [s * pair + ab for ab in blocks]
    scs = [sc2.at[s % 2, ab] for ab in blocks]
    scs_next = [sc2.at[1 - s % 2, ab] for ab in blocks]
    row = lax.broadcasted_iota(I32, (BLK, BLK), 0)
    col = lax.broadcasted_iota(I32, (BLK, BLK), 1)
    rows_of = lambda ab: slice(ab * BLK, (ab + 1) * BLK)

    @pl.when((b == 0) & (s == 0))
    def _init_bias():
        for d in range(2):
            bkt = _t5_bucket(col - row + d * BLK)
            for h in range(N_HEADS_A):
                bt[d, h] = (_bias_from_bucket(bkt, rb_ref, h) - rb_ref[N_BUCKETS - 1, h]) * LOG2E
        bt[2] = jnp.zeros(bt.shape[1:], F32)
        lgs[...] = jnp.zeros(lgs.shape, F32)

    @pl.when(s == 0)
    def _clear_values():
        vt[...] = jnp.zeros(vt.shape, BF16)

    ones = jnp.ones((ONES_ROWS, BLK), BF16)
    for ab in blocks:
        vblk = v16_ref[pl.ds(pl.multiple_of(js[ab] * BLK, BLK), BLK), :]
        vtj = vblk.astype(F32).T.astype(BF16)
        vt[js[ab]] = jnp.concatenate([piece for n in range(N_KV_A)
                                      for piece in (vtj[n * HEAD_DIM:(n + 1) * HEAD_DIM], ones)], axis=0)

    def indexer_queries(qi_blk_ref, kw_blk_ref, ab):
        qi = qi_blk_ref[rows_of(ab), :]
        qi_stack = jnp.concatenate([qi[:, h * D_IDX:(h + 1) * D_IDX] for h in range(N_IDX_HEADS)], axis=0)
        return qi_stack, kw_blk_ref[rows_of(ab), :].T

    def score_products(c, queries):
        out = []
        for u in range(search_blk):
            i = c * search_blk + u
            kib = kw16_ref[pl.ds(pl.multiple_of(i * BLK, BLK), BLK), :][:, :D_IDX]
            out.append(_nt_dot(kib, queries[0]))
        return out

    def score_finish(c, products, dst, queries, first_q):
        wi_t = queries[1]
        for u, s in enumerate(products):
            i = c * search_blk + u
            acc = jnp.zeros((BLK, BLK), F32)
            for h in range(N_IDX_HEADS):
                acc = acc + wi_t[D_IDX + h:D_IDX + h + 1, :] * jnp.maximum(s[:, h * BLK:(h + 1) * BLK], 0.0)
            dst[i] = jnp.where(row + i * BLK <= col + first_q, acc, -jnp.inf)

    def score_group(c, dst, queries, first_q):
        score_finish(c, score_products(c, queries), dst, queries, first_q)

    @pl.when(s == 0)
    def _own_scores():
        for ab in blocks:
            score_group(0, scs[ab], indexer_queries(qi_ref, kwq_ref, ab), ab * BLK)

    n_chunk = (js[0] + search_blk) // search_blk

    def col_sum(w):
        return jnp.sum(w.reshape(BLK // SUBLANES, SUBLANES, BLK), axis=0)

    g_per_kv = N_HEADS_A // N_KV_A
    q_stacks = []
    for ab in blocks:
        qa = qa_ref[rows_of(ab), :]
        q_stacks.append([jnp.concatenate([qa[:, (n * g_per_kv + g) * HEAD_DIM:(n * g_per_kv + g + 1) * HEAD_DIM]
                                          for g in range(g_per_kv)], axis=0) for n in range(N_KV_A)])

    kf = float(topk)
    n_units = pair * N_KV_A

    def search(n_groups):
        def count_ge(cands, unit=None):
            raw = []
            if unit is not None:
                n = unit[1]
                for ab in blocks:
                    for c in range(n_groups):
                        i = c * pair + unit[0]
                        raw.append((ab, i, _nt_dot(k16_ref[i * BLK:(i + 1) * BLK, n * HEAD_DIM:(n + 1) * HEAD_DIM],
                                                   q_stacks[ab][n])))
            counts = []
            for ab in blocks:
                acc = jnp.zeros((SUBLANES, BLK), F32)
                for i in range(n_groups * pair):
                    acc = acc + col_sum(jnp.where(scs[ab][i] >= cands[ab], 1.0, 0.0))
                counts.append(jnp.sum(acc, axis=0, keepdims=True))
            for ab, i, lg in raw:
                for g in range(g_per_kv):
                    lgs[ab, i, unit[1] * g_per_kv + g] = lg[:, g * BLK:(g + 1) * BLK]
            return counts

        def bit_step(bit, carry, unit=None):
            cands = [key | jnp.left_shift(jnp.int32(1), bit) for key, _ in carry]
            counts = count_ge([_key_to_float(cand) for cand in cands], unit)
            return tuple((jnp.where(cnt >= kf, cand, key), jnp.where(cnt >= kf, cnt, cnt_key))
                         for (key, cnt_key), cand, cnt in zip(carry, cands, counts))

        c0s = count_ge([jnp.zeros((1, BLK), F32)] * pair)
        carry = tuple((jnp.where(c0 >= kf, 0, INT_MIN).astype(I32), jnp.where(c0 >= kf, c0, 0.0)) for c0 in c0s)
        for it in range(n_units):
            carry = bit_step(30 - it, carry, (it // N_KV_A, it % N_KV_A))
        n_rest = jnp.where((js[-1] + 1) * BLK <= topk, 0, 31 - n_units)
        carry = lax.fori_loop(0, n_rest, lambda it, cr: bit_step(30 - n_units - it, cr), carry)
        for ab in blocks:
            key_scr[ab] = jnp.broadcast_to(carry[ab][0], key_scr.shape[1:])
            cnt_scr[ab] = jnp.broadcast_to(carry[ab][1], cnt_scr.shape[1:])

    for n_groups in range(1, sc2.shape[2] // pair + 1):
        pl.when(s + 1 == n_groups)(functools.partial(search, n_groups))

    z8 = jnp.zeros((SUBLANES, BLK), F32)
    n_keys_max = sc2.shape[2] * BLK
    vrows = HEAD_DIM + ONES_ROWS

    def col_max(w):
        return jnp.max(w.reshape(BLK // SUBLANES, SUBLANES, BLK), axis=0)

    n_chunk_next = (js[0] + pair + search_blk) // search_blk

    def attend(ab):
        j, sc, sc_next = js[ab], scs[ab], scs_next[ab]
        lg_ab, acc_ab = lgs.at[ab], acc_scr.at[ab]
        key = key_scr[ab, 0:1, :]
        cnt_ge = jnp.where(key < KEY_NEG_FLT_MAX, 0.0, cnt_scr[ab, 0:1, :])
        thr = _key_to_float(jnp.maximum(key, KEY_NEG_FLT_MAX))
        straddle = jnp.max(jnp.where(cnt_ge > kf, 1.0, 0.0))
        cut_scr[ab] = jnp.full(cut_scr.shape[1:], 2 * n_keys_max, I32)

        @pl.when(straddle > 0.0)
        def _tie_cut():
            def gt_body(i, acc):
                return acc + col_sum(jnp.where(sc[i] > thr, 1.0, 0.0))
            cnt_gt = jnp.sum(lax.fori_loop(0, j + 1, gt_body, z8), axis=0, keepdims=True)
            need = kf - cnt_gt

            def count_tie(cand):
                def body(i, acc):
                    t = sc[i]
                    hit = jnp.where(t == thr, jnp.where(row + i * BLK < cand, 1.0, 0.0), 0.0)
                    return acc + col_sum(hit)
                acc = lax.fori_loop(0, j + 1, body, z8)
                return jnp.sum(acc, axis=0, keepdims=True)

            def cut_body(it, cut):
                cand = cut | jnp.left_shift(jnp.int32(1), (n_keys_max.bit_length() - 1) - it)
                return jnp.where(count_tie(cand) <= need, cand, cut)

            cut = lax.fori_loop(0, n_keys_max.bit_length(), cut_body, jnp.zeros((1, BLK), I32))
            cut_scr[ab] = jnp.broadcast_to(cut, cut_scr.shape[1:])

        cut = cut_scr[ab, 0:1, :]

        def mask_body(c, m8, biased):
            m8 = list(m8)
            for u in range(search_blk):
                i = c * search_blk + u
                t = sc[i]
                tie_keep = jnp.where(row + i * BLK < cut, 0.0, -jnp.inf)
                nm = jnp.where(t > thr, 0.0, jnp.where(t == thr, tie_keep, -jnp.inf))
                near = jnp.clip(j - i, 0, 2)
                for h in range(N_HEADS_A):
                    x = lg_ab[i, h] + nm
                    if biased:
                        x = x + bt[near, h]
                    lg_ab[i, h] = x
                    m8[h] = jnp.maximum(m8[h], col_max(x))
            return tuple(m8)

        n_far = jnp.maximum(n_chunk - 2, 0)
        neg8 = jnp.full((SUBLANES, BLK), NEG_BIG, F32)
        m8 = lax.fori_loop(0, n_far, functools.partial(mask_body, biased=False), (neg8,) * N_HEADS_A)
        m8 = lax.fori_loop(n_far, n_chunk, functools.partial(mask_body, biased=True), m8)
        m_row = [jnp.max(m8[h], axis=0, keepdims=True) for h in range(N_HEADS_A)]

        acc_ab[...] = jnp.zeros(acc_ab.shape, F32)
        next_queries = indexer_queries(qin_ref, kwqn_ref, ab)
        next_first_q = (j + pair) * BLK

        def pv_body(c, carry):
            pv = [None] * N_KV_A
            for u in range(search_blk):
                i = c * search_blk + u
                vti = vt[i]
                for n in range(N_KV_A):
                    ps = [jnp.exp2(lg_ab[i, n * g_per_kv + g] - m_row[n * g_per_kv + g]).astype(BF16)
                          for g in range(g_per_kv)]
                    t = _dot(vti[n * vrows:(n + 1) * vrows, :], jnp.concatenate(ps, axis=1))
                    pv[n] = t if pv[n] is None else pv[n] + t
            for n in range(N_KV_A):
                acc_ab[n] += pv[n]
            score_group(c, sc_next, next_queries, next_first_q)
            return carry

        lax.fori_loop(0, n_chunk, pv_body, 0)

        @pl.when((n_chunk_next > n_chunk) & (s + 1 < n_steps))
        def _next_reaches_one_more_group():
            score_group(n_chunk, sc_next, next_queries, next_first_q)

        parts = []
        for n in range(N_KV_A):
            acc = acc_ab[n]
            out = acc[:HEAD_DIM] * (1.0 / acc[HEAD_DIM:HEAD_DIM + 1])
            parts += [out[:, g * BLK:(g + 1) * BLK] for g in range(g_per_kv)]
        oa_ref[rows_of(ab), :] = jnp.concatenate(parts, axis=0).T.astype(BF16)

    for ab in blocks:
        attend(ab)


def _dsa_prompt(p, rel_bias, batch, seq):
    nblk = seq // BLK
    topk = min(TOPK_MAX, seq // 4)
    search_blk = 4 if nblk % 4 == 0 else 1
    pair = 2 if search_blk % 2 == 0 else 1
    n_steps = nblk // pair
    qrow = lambda w: pl.BlockSpec((pair * BLK, w), lambda b, s: (b * n_steps + s, 0))
    qnext = lambda w: pl.BlockSpec((pair * BLK, w),
                                   lambda b, s: (b * n_steps + jnp.minimum(s + 1, n_steps - 1), 0))
    brow = lambda w: pl.BlockSpec((seq, w), lambda b, s: (b, 0))
    return pl.pallas_call(
        functools.partial(_dsa_prompt_kernel, topk=topk, search_blk=search_blk, pair=pair),
        grid=(batch, n_steps),
        in_specs=[
            pl.BlockSpec(memory_space=pltpu.SMEM),
            qrow(N_IDX_HEADS * D_IDX), qrow(LANES), qnext(N_IDX_HEADS * D_IDX), qnext(LANES),
            brow(LANES), qrow(D_A), brow(LANES), brow(LANES),
        ],
        out_specs=qrow(D_A),
        out_shape=jax.ShapeDtypeStruct((batch * seq, D_A), BF16),
        scratch_shapes=[
            pltpu.VMEM((2, pair, nblk, BLK, BLK), F32),
            pltpu.VMEM((3, N_HEADS_A, BLK, BLK), F32),
            pltpu.VMEM((nblk, N_KV_A * (HEAD_DIM + ONES_ROWS), BLK), BF16),
            pltpu.VMEM((pair, nblk, N_HEADS_A, BLK, BLK), F32),
            pltpu.VMEM((pair, N_KV_A, HEAD_DIM + ONES_ROWS, (N_HEADS_A // N_KV_A) * BLK), F32),
            pltpu.VMEM((pair, SUBLANES, BLK), I32),
            pltpu.VMEM((pair, SUBLANES, BLK), I32),
            pltpu.VMEM((pair, SUBLANES, BLK), F32),
        ],
        compiler_params=pltpu.CompilerParams(dimension_semantics=("arbitrary", "arbitrary"),
                                             vmem_limit_bytes=VMEM_LIMIT),
        name="dsa_prompt",
    )(rel_bias, p["qi"], p["kw"], p["qi"], p["kw"], p["kw16"], p["qa"], p["k16"], p["v16"])


def _retention_kernel(q_ref, k_ref, v_ref, s0_ref, dmat_ref, din_ref, dout_ref, decm_ref, bdm_ref,
                      yn_ref, s_out_ref, s_scr, *, n_sub, pairs_per_phase):
    c = pl.program_id(1)
    chunk = din_ref.shape[0]

    @pl.when(c == 0)
    def _load_state():
        z = jnp.zeros((HEAD_DIM, HEAD_DIM), F32)
        for p in range(N_HEADS_B // 2):
            pair = jnp.concatenate([jnp.concatenate([s0_ref[0, 2 * p], z], axis=1),
                                    jnp.concatenate([z, s0_ref[0, 2 * p + 1]], axis=1)], axis=0)
            s_scr[p] = pair.T

    lo = lax.broadcasted_iota(I32, (chunk, LANES), 1) < HEAD_DIM
    zero = jnp.zeros((), BF16)
    n_pairs = N_HEADS_B // 2
    lanes = [slice(p * LANES, (p + 1) * LANES) for p in range(n_pairs)]
    rows = [slice(u * chunk, (u + 1) * chunk) for u in range(n_sub)]

    def run(pairs):
        units = [(p, u) for p in pairs for u in range(n_sub)]
        qd, kd, vt, s_a, s_b, intra, upd, inter = {}, {}, {}, {}, {}, {}, {}, {}
        for p, u in units:
            q, k, v = q_ref[0, rows[u], lanes[p]], k_ref[0, rows[u], lanes[p]], v_ref[0, rows[u], lanes[p]]
            qd[p, u] = (q.astype(F32) * din_ref[:, lanes[p]]).astype(BF16)
            kd[p, u] = (k.astype(F32) * dout_ref[:, lanes[p]]).astype(BF16)
            vt[p, u] = v.astype(F32).T.astype(BF16)
            s_a[p, u] = (_nt_dot(jnp.where(lo, k, zero), q) * dmat_ref[2 * p]).astype(BF16)
            s_b[p, u] = (_nt_dot(jnp.where(lo, zero, k), q) * dmat_ref[2 * p + 1]).astype(BF16)
        for p, u in units:
            intra[p, u] = jnp.concatenate([_dot(vt[p, u][:HEAD_DIM], s_a[p, u]),
                                           _dot(vt[p, u][HEAD_DIM:], s_b[p, u])], axis=0)
            upd[p, u] = _dot(vt[p, u], kd[p, u]) * bdm_ref[...]
        for p in pairs:
            st = s_scr[p]
            for u in range(n_sub):
                inter[p, u] = _nt_dot(st.astype(BF16), qd[p, u])
                st = st * decm_ref[p] + upd[p, u]
            s_scr[p] = st
            pair = st.T
            s_out_ref[0, 2 * p] = pair[:HEAD_DIM, :HEAD_DIM]
            s_out_ref[0, 2 * p + 1] = pair[HEAD_DIM:, HEAD_DIM:]
        for p, u in units:
            o = (inter[p, u] + intra[p, u]).reshape(2, HEAD_DIM, chunk)
            dlt = o - jnp.mean(o, axis=1, keepdims=True)
            var = jnp.mean(dlt * dlt, axis=1, keepdims=True)
            yn_ref[0, rows[u], lanes[p]] = (dlt * lax.rsqrt(var + EPS)).reshape(LANES, chunk).T.astype(BF16)

    for group in range(0, n_pairs, pairs_per_phase):
        run(range(group, group + pairs_per_phase))


def _ret_tables(chunk, n_real):
    h = N_HEADS_B
    log_g = jnp.log1p(-jnp.exp2(-5.0 - jnp.arange(h, dtype=F32)))
    i = jnp.arange(chunk, dtype=F32)
    diff = i[:, None] - i[None, :]
    dmat = jnp.where(diff >= 0, jnp.exp(jnp.maximum(diff, 0.0)[None] * log_g[:, None, None]), 0.0)
    dmat_t = jnp.swapaxes(dmat, 1, 2)
    dec_in = jnp.exp((i + 1.0)[:, None] * log_g[None, :])
    dec_out = jnp.exp((n_real - 1.0 - i)[:, None] * log_g[None, :])
    dec_chunk = jnp.exp(n_real * log_g)
    din = jnp.repeat(dec_in, HEAD_DIM, axis=1)
    dout = jnp.repeat(dec_out, HEAD_DIM, axis=1)
    head_of = np.arange(LANES) // HEAD_DIM
    bd = jnp.asarray((head_of[:, None] == head_of[None, :]).astype(np.float32))
    dc = jnp.repeat(dec_chunk, HEAD_DIM).reshape(h // 2, LANES)
    decm = dc[:, :, None] * bd[None]
    return dmat_t, din, dout, decm, bd


def _retention(q3, k3, v3, s0, n_real):
    batch, seq, _ = q3.shape
    chunk = min(RET_CHUNK, seq)
    n_sub = _largest_divisor(seq // chunk, 8)
    dmat_t, din, dout, decm, bd = _ret_tables(chunk, n_real)
    tok = pl.BlockSpec((1, n_sub * chunk, D_B), lambda b, c: (b, c, 0))
    st = pl.BlockSpec((1,) + s0.shape[1:], lambda b, c: (b, 0, 0, 0))
    full = lambda a: pl.BlockSpec(a.shape, lambda b, c: (0,) * a.ndim)
    return pl.pallas_call(
        functools.partial(_retention_kernel, n_sub=n_sub,
                          pairs_per_phase=max(1, min(N_HEADS_B // 2, 8 // n_sub))),
        grid=(batch, seq // (n_sub * chunk)),
        in_specs=[tok, tok, tok, st, full(dmat_t), full(din), full(dout), full(decm), full(bd)],
        out_specs=[tok, st],
        out_shape=[jax.ShapeDtypeStruct((batch, seq, D_B), BF16),
                   jax.ShapeDtypeStruct(s0.shape, F32)],
        scratch_shapes=[pltpu.VMEM((N_HEADS_B // 2, LANES, LANES), F32)],
        compiler_params=pltpu.CompilerParams(dimension_semantics=("arbitrary", "arbitrary"),
                                             vmem_limit_bytes=VMEM_LIMIT),
        name="retention",
    )(q3, k3, v3, s0, dmat_t, din, dout, decm, bd)


def _merge_kernel(x_ref, oa_ref, yn_ref, ga_ref, gb_ref, p_ref, wo_ref, gp_ref, wup_ref, wg_ref, out_ref):
    ga = ga_ref[...].astype(F32)
    gb = gb_ref[...].astype(F32)
    ya = (ga * _sigmoid(ga) * oa_ref[...].astype(F32)).astype(BF16)
    yb = (gb * _sigmoid(gb) * yn_ref[...].astype(F32)).astype(BF16)
    y = _dot(ya, wo_ref[0:D_A, :]) + _dot(yb, wo_ref[D_A:D_A + D_B, :])
    ms = jnp.mean(y * y, axis=-1, keepdims=True)
    x1 = x_ref[...] + y * lax.rsqrt(ms + EPS) * gp_ref[...]
    ple = _dot(p_ref[...].astype(BF16), wup_ref[...])
    gate = _sigmoid(_dot(x1.astype(BF16), wg_ref[...]))
    out_ref[...] = x1 + ple * gate


def _merge(x2d, oa, yn, ga, gb, p2d, wo16, g_post, wup16, wg16, tm):
    rows, d_model = x2d.shape
    row = lambda w: pl.BlockSpec((tm, w), lambda i: (i, 0))
    full = lambda a: pl.BlockSpec(a.shape, lambda i: (0, 0))
    return pl.pallas_call(
        _merge_kernel,
        grid=(rows // tm,),
        in_specs=[row(d_model), row(D_A), row(D_B), row(D_A), row(D_B), row(p2d.shape[1]),
                  full(wo16), full(g_post), full(wup16), full(wg16)],
        out_specs=row(d_model),
        out_shape=jax.ShapeDtypeStruct((rows, d_model), F32),
        compiler_params=pltpu.CompilerParams(dimension_semantics=("arbitrary",), vmem_limit_bytes=VMEM_LIMIT),
        name="merge",
    )(x2d, oa, yn, ga, gb, p2d, wo16, g_post, wup16, wg16)


def _page_copy(pages_hbm, buf, sem, pt_ref, b, p, slot):
    return pltpu.make_async_copy(pages_hbm.at[pt_ref[b, p]], buf.at[slot, p], sem.at[slot])


def _start_batch_pages(streams, pt_ref, b, slot, n_pages):
    def body(p, c):
        for pages_hbm, buf, sem in streams:
            _page_copy(pages_hbm, buf, sem, pt_ref, b, p, slot).start()
        return c
    lax.fori_loop(0, n_pages, body, 0)


def _wait_batch_pages(streams, pt_ref, b, slot, n_pages):
    for pages_hbm, buf, sem in streams:
        for p in range(n_pages):
            _page_copy(pages_hbm, buf, sem, pt_ref, b, p, slot).wait()


def _sample_index_kernel(pt_ref, qi_ref, w_ref, pages_hbm, out_ref, buf, sem, *, n_pages, unroll):
    b = pl.program_id(0)
    slot = b % 2
    streams = [(pages_hbm, buf, sem)]

    @pl.when(b == 0)
    def _first():
        _start_batch_pages(streams, pt_ref, 0, 0, n_pages)

    _wait_batch_pages(streams, pt_ref, b, slot, n_pages)

    qi = qi_ref[0]
    w = w_ref[0]
    n_q = qi.shape[0] // N_IDX_HEADS

    def run(prefetch):
        def body(c, carry):
            for u in range(unroll):
                p = c * unroll + u
                if prefetch:
                    _page_copy(pages_hbm, buf, sem, pt_ref, b + 1, p, 1 - slot).start()
                s = _dot(qi, buf[slot, p].astype(BF16))
                r = jnp.maximum(s, 0.0) * w
                out_ref[0, c, :, u * PAGE_SIZE:(u + 1) * PAGE_SIZE] = jnp.sum(
                    r.reshape(n_q, N_IDX_HEADS, PAGE_SIZE), axis=1)
            return carry
        lax.fori_loop(0, n_pages // unroll, body, 0)

    has_next = b + 1 < pl.num_programs(0)
    pl.when(has_next)(lambda: run(True))
    pl.when(jnp.logical_not(has_next))(lambda: run(False))


def _sample_index(page_table, qi_qh, w_qh, kidx_t, unroll):
    batch, n_pages = page_table.shape
    n_q = qi_qh.shape[1] // N_IDX_HEADS
    groups = n_pages // unroll
    grid_spec = pltpu.PrefetchScalarGridSpec(
        num_scalar_prefetch=1,
        grid=(batch,),
        in_specs=[pl.BlockSpec((1,) + qi_qh.shape[1:], lambda b, pt: (b, 0, 0)),
                  pl.BlockSpec((1,) + w_qh.shape[1:], lambda b, pt: (b, 0, 0)),
                  pl.BlockSpec(memory_space=pl.ANY)],
        out_specs=pl.BlockSpec((1, groups, n_q, unroll * PAGE_SIZE), lambda b, pt: (b, 0, 0, 0)),
        scratch_shapes=[pltpu.VMEM((2, n_pages) + kidx_t.shape[1:], kidx_t.dtype),
                        pltpu.SemaphoreType.DMA((2,))],
    )
    return pl.pallas_call(
        functools.partial(_sample_index_kernel, n_pages=n_pages, unroll=unroll),
        grid_spec=grid_spec,
        out_shape=jax.ShapeDtypeStruct((batch, groups, n_q, unroll * PAGE_SIZE), F32),
        compiler_params=pltpu.CompilerParams(dimension_semantics=("arbitrary",), vmem_limit_bytes=VMEM_LIMIT),
        name="sample_index",
    )(page_table, qi_qh, w_qh, kidx_t)


def _sample_select_kernel(sp_ref, qi_ref, w_ref, kw16_ref, thr_ref, cut_ref, nmn_ref, snew_scr,
                          *, topk, n_q, lane_chunk):
    rows, past = sp_ref.shape
    r_i = lax.broadcasted_iota(I32, (rows, LANES), 0)
    l_i = lax.broadcasted_iota(I32, (rows, LANES), 1)

    s = _nt_dot(qi_ref[...], kw16_ref[...][:, :D_IDX])
    r = jnp.maximum(s, 0.0) * w_ref[...]
    s_new = jnp.sum(r.reshape(rows, N_IDX_HEADS, LANES), axis=1)
    ok = ((l_i // n_q) == (r_i // n_q)) & ((l_i % n_q) <= (r_i % n_q))
    snew_scr[...] = jnp.where(ok, s_new, -jnp.inf)

    n_chunks = past // lane_chunk
    tiles = lane_chunk // LANES
    kf = float(topk)

    def count(ind):
        acc = ind(snew_scr[...], l_i + past)
        for ch in range(n_chunks):
            x = sp_ref[:, ch * lane_chunk:(ch + 1) * lane_chunk]
            for t in range(tiles):
                idx = l_i + (ch * lane_chunk + t * LANES)
                acc = acc + ind(x[:, t * LANES:(t + 1) * LANES], idx)
        return jnp.broadcast_to(jnp.sum(acc, axis=1, keepdims=True), (rows, LANES))

    c0 = count(lambda x, idx: jnp.where(x >= 0.0, 1.0, 0.0))
    key0 = jnp.where(c0 >= kf, 0, INT_MIN).astype(I32)

    def bit_body(it, key):
        cand = key | jnp.left_shift(jnp.int32(1), 30 - it)
        cf = _key_to_float(cand)
        return jnp.where(count(lambda x, idx: jnp.where(x >= cf, 1.0, 0.0)) >= kf, cand, key)

    key = lax.fori_loop(0, 31, bit_body, key0)
    key = jnp.maximum(key, KEY_NEG_FLT_MAX)
    thr = _key_to_float(key)
    cnt_gt = count(lambda x, idx: jnp.where(x > thr, 1.0, 0.0))
    cnt_ge = count(lambda x, idx: jnp.where(x >= thr, 1.0, 0.0))
    need = kf - cnt_gt
    straddle = jnp.max(jnp.where(cnt_ge > kf, 1.0, 0.0))

    idx_bits = (past + LANES).bit_length()
    thr_ref[...] = thr
    cut_ref[...] = jnp.full((rows, LANES), 1 << idx_bits, I32)

    @pl.when(straddle > 0.0)
    def _tie_cut():
        def cut_body(it, cut):
            cand = cut | jnp.left_shift(jnp.int32(1), (idx_bits - 1) - it)
            cnt = count(lambda x, idx: jnp.where(x == thr, jnp.where(idx < cand, 1.0, 0.0), 0.0))
            return jnp.where(cnt <= need, cand, cut)
        cut_ref[...] = lax.fori_loop(0, idx_bits, cut_body, jnp.zeros((rows, LANES), I32))

    cut = cut_ref[...]
    sn = snew_scr[...]
    tie_keep = jnp.where(l_i + past < cut, 0.0, -jnp.inf)
    nmn_ref[...] = jnp.where(sn > thr, 0.0, jnp.where(sn == thr, tie_keep, -jnp.inf))


def _sample_select(scores_past, qi_rows, w_rows, kw16_s, topk, n_q):
    rows, past = scores_past.shape
    lane_chunk = 2048 if past % 2048 == 0 else LANES
    full = lambda a: pl.BlockSpec(a.shape, lambda i: (0,) * a.ndim)
    o = jax.ShapeDtypeStruct((rows, LANES), F32)
    return pl.pallas_call(
        functools.partial(_sample_select_kernel, topk=topk, n_q=n_q, lane_chunk=lane_chunk),
        grid=(1,),
        in_specs=[full(scores_past), full(qi_rows), full(w_rows), full(kw16_s)],
        out_specs=[pl.BlockSpec((rows, LANES), lambda i: (0, 0))] * 3,
        out_shape=[o, jax.ShapeDtypeStruct((rows, LANES), I32), o],
        scratch_shapes=[pltpu.VMEM((rows, LANES), F32)],
        compiler_params=pltpu.CompilerParams(dimension_semantics=("arbitrary",), vmem_limit_bytes=VMEM_LIMIT),
        name="sample_select",
    )(scores_past, qi_rows, w_rows, kw16_s)


def _sample_attn_kernel(pt_ref, rbc_ref, q_ref, sp_ref, thr_ref, cut_ref, nmn_ref, k16n_ref, v16n_ref,
                        k_hbm, v_hbm, o_ref, kbuf, vbuf, ksem, vsem, lgs, bias_scr, mx_scr,
                        *, n_pages, n_q, unroll):
    b = pl.program_id(0)
    slot = b % 2
    rows = n_q * N_HEADS_A
    groups = n_pages // unroll
    streams = [(k_hbm, kbuf, ksem), (v_hbm, vbuf, vsem)]
    lane = lax.broadcasted_iota(I32, (rows, LANES), 1)
    rq = lax.broadcasted_iota(I32, (rows, LANES), 0) // N_HEADS_A
    expand = lambda a: jnp.concatenate(
        [jnp.broadcast_to(a[t:t + 1], (N_HEADS_A, LANES)) for t in range(n_q)], axis=0)

    @pl.when(b == 0)
    def _first():
        _start_batch_pages(streams, pt_ref, 0, 0, n_pages)
        far = jnp.concatenate([rbc_ref[N_BUCKETS - 1]] * n_q, axis=0)
        for t, dist in enumerate((PAGE_SIZE + rq - lane, rq - lane % n_q)):
            bkt = _t5_bucket(dist)
            tile = jnp.zeros((rows, LANES), F32)
            for k in range(N_BUCKETS - 1):
                tile = jnp.where(bkt == k, jnp.concatenate([rbc_ref[k]] * n_q, axis=0) - far, tile)
            bias_scr[t] = tile * LOG2E

    _wait_batch_pages(streams, pt_ref, b, slot, n_pages)

    thr = expand(thr_ref[0])
    cut = expand(cut_ref[0])
    q = q_ref[0]

    def logits_pass(prefetch):
        def body(c, mx):
            for u in range(unroll):
                p = c * unroll + u
                if prefetch:
                    for pages_hbm, buf, sem in streams:
                        _page_copy(pages_hbm, buf, sem, pt_ref, b + 1, p, 1 - slot).start()
                lg = _dot(q, kbuf[slot, p].astype(BF16))
                sc = expand(sp_ref[0, c, :, u * PAGE_SIZE:(u + 1) * PAGE_SIZE])
                tie_keep = jnp.where(lane + p * PAGE_SIZE < cut, 0.0, -jnp.inf)
                x = lg + jnp.where(sc > thr, 0.0, jnp.where(sc == thr, tie_keep, -jnp.inf))
                if u == unroll - 1:
                    x = x + bias_scr[0] * jnp.where(c == groups - 1, 1.0, 0.0)
                lgs[p] = x
                mx = jnp.maximum(mx, x)
            return mx
        mx_scr[...] = lax.fori_loop(0, groups, body, jnp.full((rows, LANES), NEG_BIG, F32))

    has_next = b + 1 < pl.num_programs(0)
    pl.when(has_next)(lambda: logits_pass(True))
    pl.when(jnp.logical_not(has_next))(lambda: logits_pass(False))

    xn = _nt_dot(q, k16n_ref[...]) + bias_scr[1] + expand(nmn_ref[0])
    m = jnp.max(jnp.maximum(mx_scr[...], xn), axis=1, keepdims=True)

    def pv_body(c, carry):
        lsum, acc = carry
        for u in range(unroll):
            p = c * unroll + u
            pr = jnp.exp2(lgs[p] - m)
            lsum = lsum + pr
            acc = acc + _nt_dot(pr.astype(BF16), vbuf[slot, p].astype(BF16))
        return lsum, acc

    pn = jnp.exp2(xn - m)
    lsum, acc = lax.fori_loop(0, groups, pv_body, (pn, _dot(pn.astype(BF16), v16n_ref[...])))
    out = acc * (1.0 / jnp.sum(lsum, axis=1, keepdims=True))
    head = lax.broadcasted_iota(I32, (rows, HEAD_DIM), 0) % N_HEADS_A
    o_ref[0] = jnp.where(head < N_HEADS_A // N_KV_A, out[:, :HEAD_DIM], out[:, HEAD_DIM:])


def _sample_attn(page_table, rb_col, q_bd, scores4, thr3, cut3, nmn3, k16n, v16n, k_pages_t, v_pages_t, unroll):
    batch, n_pages = page_table.shape
    n_q = scores4.shape[2]
    rows = q_bd.shape[1]
    per_b = lambda a: pl.BlockSpec((1,) + a.shape[1:], lambda b, pt: (b,) + (0,) * (a.ndim - 1))
    full = lambda a: pl.BlockSpec(a.shape, lambda b, pt: (0,) * a.ndim)
    page_buf = pltpu.VMEM((2, n_pages) + k_pages_t.shape[1:], k_pages_t.dtype)
    grid_spec = pltpu.PrefetchScalarGridSpec(
        num_scalar_prefetch=1,
        grid=(batch,),
        in_specs=[full(rb_col), per_b(q_bd), per_b(scores4), per_b(thr3), per_b(cut3), per_b(nmn3),
                  full(k16n), full(v16n), pl.BlockSpec(memory_space=pl.ANY), pl.BlockSpec(memory_space=pl.ANY)],
        out_specs=pl.BlockSpec((1, rows, HEAD_DIM), lambda b, pt: (b, 0, 0)),
        scratch_shapes=[page_buf, page_buf, pltpu.SemaphoreType.DMA((2,)), pltpu.SemaphoreType.DMA((2,)),
                        pltpu.VMEM((n_pages, rows, LANES), F32), pltpu.VMEM((2, rows, LANES), F32),
                        pltpu.VMEM((rows, LANES), F32)],
    )
    return pl.pallas_call(
        functools.partial(_sample_attn_kernel, n_pages=n_pages, n_q=n_q, unroll=unroll),
        grid_spec=grid_spec,
        out_shape=jax.ShapeDtypeStruct((batch, rows, HEAD_DIM), F32),
        compiler_params=pltpu.CompilerParams(dimension_semantics=("arbitrary",),
                                             vmem_limit_bytes=VMEM_LIMIT_PAGED),
        name="sample_attn",
    )(page_table, rb_col, q_bd, scores4, thr3, cut3, nmn3, k16n, v16n, k_pages_t, v_pages_t)


def _rope_tables(pos):
    half = HEAD_DIM // 2
    inv = ROPE_BASE ** (-jnp.arange(half, dtype=F32) / half)
    ang = pos.astype(F32)[:, None] * inv[None, :]
    cos = jnp.cos(ang)
    sin = jnp.sin(ang)
    reps = LANES // HEAD_DIM
    cos_t = jnp.tile(jnp.concatenate([cos, cos], axis=1), (1, reps))
    sin_t = jnp.tile(jnp.concatenate([-sin, sin], axis=1), (1, reps))
    return cos_t, sin_t


def _cat_weight(w_in):
    split = int(np.sum(SPLIT_SIZES[:6]))
    pad = jnp.zeros((w_in.shape[0], LANES - D_IDX - N_IDX_HEADS), BF16)
    return jnp.concatenate([w_in[:, :split].astype(BF16), pad, w_in[:, split:].astype(BF16)], axis=1)


def _pages_t(cache):
    pool, page = cache.shape[:2]
    return jnp.transpose(cache, (0, 2, 3, 1)).reshape(pool, -1, page)


def _largest_divisor(n, cap):
    d = cap
    while n % d:
        d //= 2
    return d


def kernel(x_prompt, x_sample, cache_k, cache_v, cache_kidx, state_ret, page_table, p_prompt, p_sample,
           rel_bias, w_in, w_out, g_pre, g_post, w_ple_up, w_ple_gate):
    batch, seq, d_model = x_prompt.shape
    dec_b, dec_t, _ = x_sample.shape
    depth = w_in.shape[0]
    n_pages = page_table.shape[1]
    past = n_pages * PAGE_SIZE
    rows_s = dec_b * dec_t
    assert depth == 1 and rows_s == LANES and seq % BLK == 0

    w_cat = _cat_weight(w_in[0])
    gpre = g_pre[0].reshape(1, d_model)
    gpost = g_post[0].reshape(1, d_model)
    wo16 = w_out[0].astype(BF16)
    wup16 = w_ple_up[0].astype(BF16)
    wg16 = w_ple_gate[0].astype(BF16)

    tm = _largest_divisor(seq, 1024)
    cos_p, sin_p = _rope_tables(jnp.arange(seq))
    xp2 = x_prompt.reshape(batch * seq, d_model)
    pp = _project(xp2, gpre, w_cat, cos_p, sin_p, tm, seq)
    oa_p = _dsa_prompt(pp, rel_bias, batch, seq)
    r3 = lambda a: a.reshape(batch, seq, D_B)
    s0_p = jnp.zeros((batch, N_HEADS_B, HEAD_DIM, HEAD_DIM), F32)
    yn_p, state_p = _retention(r3(pp["qb"]), r3(pp["kb"]), r3(pp["vb"]), s0_p, float(min(RET_CHUNK, seq)))
    y_prompt = _merge(xp2, oa_p, yn_p.reshape(batch * seq, D_B), pp["ga"], pp["gb"],
                      p_prompt[0].reshape(batch * seq, -1), wo16, gpost, wup16, wg16, tm)

    pos_s = past + jnp.arange(dec_t)
    cos_s, sin_s = _rope_tables(jnp.tile(pos_s, dec_b))
    xs2 = x_sample.reshape(rows_s, d_model)
    ps = _project(xs2, gpre, w_cat, cos_s, sin_s, rows_s, rows_s)
    topk_s = min(TOPK_MAX, (past + dec_t) // 4)

    qi_rows = ps["qi"].reshape(rows_s * N_IDX_HEADS, D_IDX)
    w_rows = jnp.broadcast_to(ps["kw"][:, D_IDX:D_IDX + N_IDX_HEADS].reshape(rows_s * N_IDX_HEADS, 1),
                              (rows_s * N_IDX_HEADS, LANES))
    unroll = _largest_divisor(n_pages, 32)
    scores4 = _sample_index(page_table, qi_rows.reshape(dec_b, dec_t * N_IDX_HEADS, D_IDX),
                            w_rows.reshape(dec_b, dec_t * N_IDX_HEADS, LANES),
                            jnp.transpose(cache_kidx[0], (0, 2, 1)), unroll)
    scores_past = scores4.transpose(0, 2, 1, 3).reshape(rows_s, past)
    thr, cut, nmn = _sample_select(scores_past, qi_rows, w_rows, ps["kw16"], topk_s, dec_t)

    kv_of_head = np.arange(N_HEADS_A) // (N_HEADS_A // N_KV_A)
    place = jnp.asarray((kv_of_head[:, None] == np.arange(N_KV_A)[None, :]).astype(np.float32)).astype(BF16)
    q_bd = (ps["qa"].reshape(dec_b, dec_t, N_HEADS_A, 1, HEAD_DIM) * place[None, None, :, :, None]).reshape(
        dec_b, dec_t * N_HEADS_A, N_KV_A * HEAD_DIM)
    rb_col = jnp.broadcast_to(rel_bias[:, :, None], rel_bias.shape + (LANES,))
    b3 = lambda a: a.reshape(dec_b, dec_t, LANES)
    o_qh = _sample_attn(page_table, rb_col, q_bd, scores4, b3(thr), b3(cut), b3(nmn),
                        ps["k16"], ps["v16"], _pages_t(cache_k[0]), _pages_t(cache_v[0]), unroll)
    oa_s = o_qh.reshape(rows_s, D_A)

    chunk_s = RET_CHUNK
    padt = lambda a: jnp.pad(a.reshape(dec_b, dec_t, D_B), ((0, 0), (0, chunk_s - dec_t), (0, 0)))
    yn_s, state_s = _retention(padt(ps["qb"]), padt(ps["kb"]), padt(ps["vb"]),
                               state_ret[0].astype(F32), float(math.gcd(dec_t, RET_CHUNK)))
    y_sample = _merge(xs2, oa_s, yn_s[:, :dec_t].reshape(rows_s, D_B), ps["ga"], ps["gb"],
                      p_sample[0].reshape(rows_s, -1), wo16, gpost, wup16, wg16, rows_s)

    def kv(a_t, b, t):
        a = a_t.reshape(a_t.shape[0], N_KV_A, HEAD_DIM, -1).transpose(0, 3, 1, 2)
        return a.reshape(1, b, t, N_KV_A, HEAD_DIM)

    def ki(a_t, b, t):
        return a_t.transpose(0, 2, 1).reshape(1, b, t, D_IDX)

    return (
        y_prompt.reshape(batch, seq, d_model),
        y_sample.reshape(dec_b, dec_t, d_model),
        kv(pp["ka_t"], batch, seq), kv(pp["va_t"], batch, seq),
        ki(pp["ki_t"], batch, seq),
        state_p[None].astype(state_ret.dtype),
        kv(ps["ka_t"], dec_b, dec_t), kv(ps["va_t"], dec_b, dec_t),
        ki(ps["ki_t"], dec_b, dec_t),
        state_s[None].astype(state_ret.dtype),
    )
```

```python
import functools
import math

import jax
import jax.numpy as jnp
import numpy as np
from jax import lax
from jax.experimental import pallas as pl
from jax.experimental.pallas import tpu as pltpu

F32 = jnp.float32
BF16 = jnp.bfloat16
I32 = jnp.int32

HEAD_DIM = 64
N_HEADS_A = 8
N_KV_A = 2
N_IDX_HEADS = 8
D_IDX = 64
TOPK_MAX = 256
N_HEADS_B = 8
D_A = N_HEADS_A * HEAD_DIM
D_B = N_HEADS_B * HEAD_DIM
N_BUCKETS = 32
MAX_DISTANCE = 128
ROPE_BASE = 10000.0
RET_CHUNK = 128
PAGE_SIZE = 128
EPS = 1e-6
SPLIT_SIZES = (D_A, N_KV_A * HEAD_DIM, N_KV_A * HEAD_DIM, N_IDX_HEADS * D_IDX, D_IDX, N_IDX_HEADS,
               D_A, D_B, D_B, D_B, D_B)

LANES = 128
SUBLANES = 8
BLK = 128
VMEM_LIMIT = 48 * 1024 * 1024
VMEM_LIMIT_PAGED = 56 * 1024 * 1024

INT_MIN = -(2 ** 31)
KEY_NEG_FLT_MAX = INT_MIN + 0x00800000
NEG_BIG = -1e30
LOG2E = math.log2(math.e)
ONES_ROWS = 16

_C_QA, _C_KA, _C_VA, _C_QI, _C_KW, _C_GA, _C_QB, _C_KB, _C_VB, _C_GB, _C_END = (
    0, 512, 640, 768, 1280, 1408, 1920, 2432, 2944, 3456, 3968)


def _nt_dot(a, b):
    return lax.dot_general(a, b, (((1,), (1,)), ((), ())), preferred_element_type=F32)


def _dot(a, b):
    return jnp.dot(a, b, preferred_element_type=F32)


def _sigmoid(x):
    return 1.0 / (1.0 + jnp.exp(-x))


def _key_to_float(k):
    bits = jnp.where(k >= 0, k, k ^ 0x7FFFFFFF)
    return lax.bitcast_convert_type(bits, F32)


def _t5_bucket(n):
    max_exact = N_BUCKETS // 2
    n = jnp.maximum(n, 0)
    nf = jnp.maximum(n, 1).astype(F32)
    large = max_exact + jnp.floor(jnp.log(nf / max_exact) / math.log(MAX_DISTANCE / max_exact)
                                  * (N_BUCKETS - max_exact)).astype(I32)
    large = jnp.minimum(large, N_BUCKETS - 1)
    return jnp.where(n < max_exact, n, large)


def _bias_from_bucket(bkt, rb_ref, h):
    out = jnp.zeros(bkt.shape, F32)
    for k in range(N_BUCKETS):
        out = jnp.where(bkt == k, rb_ref[k, h], out)
    return out


def _proj_kernel(x_ref, g_ref, w_ref, cos_ref, sin_ref,
                 qa_ref, kat_ref, vat_ref, k16_ref, v16_ref, qi_ref, ga_ref, gb_ref,
                 qb_ref, kb_ref, vb_ref, kw_ref, kw16_ref, kit_ref, h_scr):
    x = x_ref[...]
    ms = jnp.mean(x * x, axis=-1, keepdims=True)
    h_scr[...] = (x * lax.rsqrt(ms + EPS) * g_ref[...]).astype(BF16)

    def mm(lo, hi):
        return _dot(h_scr[...], w_ref[:, lo:hi])

    qa_ref[...] = (mm(_C_QA, _C_KA) * (HEAD_DIM ** -0.5 * LOG2E)).astype(BF16)
    ka = mm(_C_KA, _C_VA)
    kat_ref[0] = ka.T
    k16_ref[...] = ka.astype(BF16)
    va = mm(_C_VA, _C_QI)
    vat_ref[0] = va.T
    v16_ref[...] = va.astype(BF16)
    qi_ref[...] = mm(_C_QI, _C_KW).astype(BF16)
    ga_ref[...] = mm(_C_GA, _C_QB).astype(BF16)
    gb_ref[...] = mm(_C_GB, _C_END).astype(BF16)
    vb_ref[...] = mm(_C_VB, _C_GB).astype(BF16)

    cos = cos_ref[...]
    sin = sin_ref[...]
    lane = lax.broadcasted_iota(I32, cos.shape, 1)
    first_half = (lane % HEAD_DIM) < (HEAD_DIM // 2)

    def rope(z, scale):
        outs = []
        for g in range(z.shape[1] // LANES):
            zg = z[:, g * LANES:(g + 1) * LANES]
            partner = jnp.where(first_half, pltpu.roll(zg, LANES - HEAD_DIM // 2, 1),
                                pltpu.roll(zg, HEAD_DIM // 2, 1))
            r = zg * cos + partner * sin
            if scale != 1.0:
                r = r * scale
            outs.append(r.astype(BF16))
        return jnp.concatenate(outs, axis=1)

    qb_ref[...] = rope(mm(_C_QB, _C_KB), 1.0)
    kb_ref[...] = rope(mm(_C_KB, _C_VB), HEAD_DIM ** -0.5)

    kw = mm(_C_KW, _C_GA)
    wscale = (N_IDX_HEADS ** -0.5) * (D_IDX ** -0.5)
    kw = kw * jnp.where(lane >= D_IDX, wscale, 1.0)
    kw_ref[...] = kw
    kw16_ref[...] = kw.astype(BF16)
    kit_ref[0] = kw.T[:D_IDX]


def _project(x2d, g_pre, w_cat, cos_t, sin_t, tm, seq):
    rows = x2d.shape[0]
    d_model = x2d.shape[1]
    n_tab = cos_t.shape[0] // tm
    tiles = seq // tm
    row_spec = lambda w: pl.BlockSpec((tm, w), lambda i: (i, 0))
    t_spec = lambda w: pl.BlockSpec((1, w, tm), lambda i: (i // tiles, 0, i % tiles))
    outs = [
        ("qa", D_A, BF16), ("ka_t", LANES, F32), ("va_t", LANES, F32), ("k16", LANES, BF16), ("v16", LANES, BF16),
        ("qi", N_IDX_HEADS * D_IDX, BF16), ("ga", D_A, BF16), ("gb", D_B, BF16),
        ("qb", D_B, BF16), ("kb", D_B, BF16), ("vb", D_B, BF16), ("kw", LANES, F32), ("kw16", LANES, BF16),
        ("ki_t", D_IDX, F32),
    ]
    transposed = lambda n: n.endswith("_t")
    res = pl.pallas_call(
        _proj_kernel,
        grid=(rows // tm,),
        in_specs=[
            row_spec(d_model),
            pl.BlockSpec((1, d_model), lambda i: (0, 0)),
            pl.BlockSpec(w_cat.shape, lambda i: (0, 0), pipeline_mode=pl.Buffered(1)),
            pl.BlockSpec((tm, LANES), lambda i: (i % n_tab, 0)),
            pl.BlockSpec((tm, LANES), lambda i: (i % n_tab, 0)),
        ],
        out_specs=[t_spec(w) if transposed(n) else row_spec(w) for n, w, _ in outs],
        out_shape=[jax.ShapeDtypeStruct((rows // seq, w, seq) if transposed(n) else (rows, w), dt)
                   for n, w, dt in outs],
        scratch_shapes=[pltpu.VMEM((tm, d_model), BF16)],
        compiler_params=pltpu.CompilerParams(dimension_semantics=("arbitrary",), vmem_limit_bytes=VMEM_LIMIT),
        name="proj",
    )(x2d, g_pre, w_cat, cos_t, sin_t)
    return {n: r for (n, _, _), r in zip(outs, res)}


def _bias_tiles_kernel(rb_ref, bt):
    row = lax.broadcasted_iota(I32, (BLK, BLK), 0)
    col = lax.broadcasted_iota(I32, (BLK, BLK), 1)
    for d in range(2):
        bkt = _t5_bucket(col - row + d * BLK)
        for h in range(N_HEADS_A):
            bt[d, h] = (_bias_from_bucket(bkt, rb_ref, h) - rb_ref[N_BUCKETS - 1, h]) * LOG2E
    bt[2] = jnp.zeros(bt.shape[1:], F32)


def _bias_tiles(rel_bias):
    return pl.pallas_call(
        _bias_tiles_kernel,
        in_specs=[pl.BlockSpec(memory_space=pltpu.SMEM)],
        out_shape=jax.ShapeDtypeStruct((3, N_HEADS_A, BLK, BLK), F32),
        name="bias_tiles",
    )(rel_bias)


def _dsa_prompt_kernel(bt, qi_ref, kwq_ref, qin_ref, kwqn_ref, kw16_ref, qa_ref, k16_ref, v16_ref, oa_ref,
                       sc2, vt, lgs, acc_scr, cut_scr, key_scr, cnt_scr, *, topk, search_blk, pair):
    b = pl.program_id(0)
    s = pl.program_id(1)
    n_steps = pl.num_programs(1)
    blocks = range(pair)
    js = [s * pair + ab for ab in blocks]
    scs = [sc2.at[s % 2, ab] for ab in blocks]
    scs_next = [sc2.at[1 - s % 2, ab] for ab in blocks]
    row = lax.broadcasted_iota(I32, (BLK, BLK), 0)
    col = lax.broadcasted_iota(I32, (BLK, BLK), 1)
    rows_of = lambda ab: slice(ab * BLK, (ab + 1) * BLK)

    @pl.when(s == 0)
    def _clear_values():
        vt[...] = jnp.zeros(vt.shape, BF16)

    ones = jnp.ones((ONES_ROWS, BLK), BF16)
    for ab in blocks:
        vblk = v16_ref[pl.ds(pl.multiple_of(js[ab] * BLK, BLK), BLK), :]
        vtj = vblk.astype(F32).T.astype(BF16)
        vt[js[ab]] = jnp.concatenate([piece for n in range(N_KV_A)
                                      for piece in (vtj[n * HEAD_DIM:(n + 1) * HEAD_DIM], ones)], axis=0)

    def indexer_queries(qi_blk_ref, kw_blk_ref, ab):
        qi = qi_blk_ref[rows_of(ab), :]
        qi_stack = jnp.concatenate([qi[:, h * D_IDX:(h + 1) * D_IDX] for h in range(N_IDX_HEADS)], axis=0)
        return qi_stack, kw_blk_ref[rows_of(ab), :].T

    def score_products(c, queries):
        out = []
        for u in range(search_blk):
            i = c * search_blk + u
            kib = kw16_ref[pl.ds(pl.multiple_of(i * BLK, BLK), BLK), :][:, :D_IDX]
            out.append(_nt_dot(kib, queries[0]))
        return out

    def score_finish(c, products, dst, queries, first_q):
        wi_t = queries[1]
        for u, s in enumerate(products):
            i = c * search_blk + u
            acc = jnp.zeros((BLK, BLK), F32)
            for h in range(N_IDX_HEADS):
                acc = acc + wi_t[D_IDX + h:D_IDX + h + 1, :] * jnp.maximum(s[:, h * BLK:(h + 1) * BLK], 0.0)
            dst[i] = jnp.where(row + i * BLK <= col + first_q, acc, -jnp.inf)

    def score_group(c, dst, queries, first_q):
        score_finish(c, score_products(c, queries), dst, queries, first_q)

    @pl.when(s == 0)
    def _own_scores():
        for ab in blocks:
            score_group(0, scs[ab], indexer_queries(qi_ref, kwq_ref, ab), ab * BLK)

    n_chunk = (js[0] + search_blk) // search_blk

    def col_sum(w):
        return jnp.sum(w.reshape(BLK // SUBLANES, SUBLANES, BLK), axis=0)

    g_per_kv = N_HEADS_A // N_KV_A
    q_stacks = []
    for ab in blocks:
        qa = qa_ref[rows_of(ab), :]
        q_stacks.append([jnp.concatenate([qa[:, (n * g_per_kv + g) * HEAD_DIM:(n * g_per_kv + g + 1) * HEAD_DIM]
                                          for g in range(g_per_kv)], axis=0) for n in range(N_KV_A)])

    kf = float(topk)
    n_units = search_blk * N_KV_A

    def search(n_groups):
        def count_ge(cands, unit=None):
            raw = []
            if unit is not None:
                n = unit[1]
                for ab in blocks:
                    for c in range(n_groups):
                        i = c * search_blk + unit[0]
                        raw.append((ab, i, _nt_dot(k16_ref[i * BLK:(i + 1) * BLK, n * HEAD_DIM:(n + 1) * HEAD_DIM],
                                                   q_stacks[ab][n])))
            counts = []
            for ab in blocks:
                acc = jnp.zeros((SUBLANES, BLK), F32)
                for c in range(n_groups):
                    for u in range(search_blk):
                        acc = acc + col_sum(jnp.where(scs[ab][c * search_blk + u] >= cands[ab], 1.0, 0.0))
                counts.append(jnp.sum(acc, axis=0, keepdims=True))
            for ab, i, lg in raw:
                for g in range(g_per_kv):
                    lgs[ab, i, unit[1] * g_per_kv + g] = lg[:, g * BLK:(g + 1) * BLK]
            return counts

        def bit_step(bit, carry, unit=None):
            cands = [key | jnp.left_shift(jnp.int32(1), bit) for key, _ in carry]
            counts = count_ge([_key_to_float(cand) for cand in cands], unit)
            return tuple((jnp.where(cnt >= kf, cand, key), jnp.where(cnt >= kf, cnt, cnt_key))
                         for (key, cnt_key), cand, cnt in zip(carry, cands, counts))

        c0s = count_ge([jnp.zeros((1, BLK), F32)] * pair)
        carry = tuple((jnp.where(c0 >= kf, 0, INT_MIN).astype(I32), jnp.where(c0 >= kf, c0, 0.0)) for c0 in c0s)
        for it in range(n_units):
            carry = bit_step(30 - it, carry, (it // N_KV_A, it % N_KV_A))
        n_rest = jnp.where((js[-1] + 1) * BLK <= topk, 0, 31 - n_units)
        carry = lax.fori_loop(0, n_rest, lambda it, cr: bit_step(30 - n_units - it, cr), carry)
        for ab in blocks:
            key_scr[ab] = jnp.broadcast_to(carry[ab][0], key_scr.shape[1:])
            cnt_scr[ab] = jnp.broadcast_to(carry[ab][1], cnt_scr.shape[1:])

    for n_groups in range(1, sc2.shape[2] // search_blk + 1):
        pl.when(n_chunk == n_groups)(functools.partial(search, n_groups))

    z8 = jnp.zeros((SUBLANES, BLK), F32)
    n_keys_max = sc2.shape[2] * BLK
    vrows = HEAD_DIM + ONES_ROWS

    def col_max(w):
        return jnp.max(w.reshape(BLK // SUBLANES, SUBLANES, BLK), axis=0)

    n_chunk_next = (js[0] + pair + search_blk) // search_blk

    def attend(ab):
        j, sc, sc_next = js[ab], scs[ab], scs_next[ab]
        lg_ab, acc_ab = lgs.at[ab], acc_scr.at[ab]
        key = key_scr[ab, 0:1, :]
        cnt_ge = jnp.where(key < KEY_NEG_FLT_MAX, 0.0, cnt_scr[ab, 0:1, :])
        thr = _key_to_float(jnp.maximum(key, KEY_NEG_FLT_MAX))
        straddle = jnp.max(jnp.where(cnt_ge > kf, 1.0, 0.0))
        cut_scr[ab] = jnp.full(cut_scr.shape[1:], 2 * n_keys_max, I32)

        @pl.when(straddle > 0.0)
        def _tie_cut():
            def gt_body(i, acc):
                return acc + col_sum(jnp.where(sc[i] > thr, 1.0, 0.0))
            cnt_gt = jnp.sum(lax.fori_loop(0, j + 1, gt_body, z8), axis=0, keepdims=True)
            need = kf - cnt_gt

            def count_tie(cand):
                def body(i, acc):
                    t = sc[i]
                    hit = jnp.where(t == thr, jnp.where(row + i * BLK < cand, 1.0, 0.0), 0.0)
                    return acc + col_sum(hit)
                acc = lax.fori_loop(0, j + 1, body, z8)
                return jnp.sum(acc, axis=0, keepdims=True)

            def cut_body(it, cut):
                cand = cut | jnp.left_shift(jnp.int32(1), (n_keys_max.bit_length() - 1) - it)
                return jnp.where(count_tie(cand) <= need, cand, cut)

            cut = lax.fori_loop(0, n_keys_max.bit_length(), cut_body, jnp.zeros((1, BLK), I32))
            cut_scr[ab] = jnp.broadcast_to(cut, cut_scr.shape[1:])

        cut = cut_scr[ab, 0:1, :]

        def mask_body(c, m8, biased):
            m8 = list(m8)
            for u in range(search_blk):
                i = c * search_blk + u
                t = sc[i]
                tie_keep = jnp.where(row + i * BLK < cut, 0.0, -jnp.inf)
                nm = jnp.where(t > thr, 0.0, jnp.where(t == thr, tie_keep, -jnp.inf))
                near = jnp.clip(j - i, 0, 2)
                for h in range(N_HEADS_A):
                    x = lg_ab[i, h] + nm
                    if biased:
                        x = x + bt[near, h]
                    lg_ab[i, h] = x
                    m8[h] = jnp.maximum(m8[h], col_max(x))
            return tuple(m8)

        n_far = jnp.maximum(n_chunk - 2, 0)
        neg8 = jnp.full((SUBLANES, BLK), NEG_BIG, F32)
        m8 = lax.fori_loop(0, n_far, functools.partial(mask_body, biased=False), (neg8,) * N_HEADS_A)
        m8 = lax.fori_loop(n_far, n_chunk, functools.partial(mask_body, biased=True), m8)
        m_row = [jnp.max(m8[h], axis=0, keepdims=True) for h in range(N_HEADS_A)]

        acc_ab[...] = jnp.zeros(acc_ab.shape, F32)
        next_queries = indexer_queries(qin_ref, kwqn_ref, ab)
        next_first_q = (j + pair) * BLK

        def pv_body(c, carry):
            pv = [None] * N_KV_A
            for u in range(search_blk):
                i = c * search_blk + u
                vti = vt[i]
                for n in range(N_KV_A):
                    ps = [jnp.exp2(lg_ab[i, n * g_per_kv + g] - m_row[n * g_per_kv + g]).astype(BF16)
                          for g in range(g_per_kv)]
                    t = _dot(vti[n * vrows:(n + 1) * vrows, :], jnp.concatenate(ps, axis=1))
                    pv[n] = t if pv[n] is None else pv[n] + t
            for n in range(N_KV_A):
                acc_ab[n] += pv[n]
            score_group(c, sc_next, next_queries, next_first_q)
            return carry

        lax.fori_loop(0, n_chunk, pv_body, 0)

        @pl.when((n_chunk_next > n_chunk) & (s + 1 < n_steps))
        def _next_reaches_one_more_group():
            score_group(n_chunk, sc_next, next_queries, next_first_q)

        parts = []
        for n in range(N_KV_A):
            acc = acc_ab[n]
            out = acc[:HEAD_DIM] * (1.0 / acc[HEAD_DIM:HEAD_DIM + 1])
            parts += [out[:, g * BLK:(g + 1) * BLK] for g in range(g_per_kv)]
        oa_ref[rows_of(ab), :] = jnp.concatenate(parts, axis=0).T.astype(BF16)

    for ab in blocks:
        attend(ab)


def _dsa_prompt(p, rel_bias, batch, seq):
    nblk = seq // BLK
    topk = min(TOPK_MAX, seq // 4)
    search_blk = 4 if nblk % 4 == 0 else 1
    pair = 2 if search_blk % 2 == 0 else 1
    n_steps = nblk // pair
    qrow = lambda w: pl.BlockSpec((pair * BLK, w), lambda b, s: (b * n_steps + s, 0))
    qnext = lambda w: pl.BlockSpec((pair * BLK, w),
                                   lambda b, s: (b * n_steps + jnp.minimum(s + 1, n_steps - 1), 0))
    brow = lambda w: pl.BlockSpec((seq, w), lambda b, s: (b, 0))
    return pl.pallas_call(
        functools.partial(_dsa_prompt_kernel, topk=topk, search_blk=search_blk, pair=pair),
        grid=(batch, n_steps),
        in_specs=[
            pl.BlockSpec((3, N_HEADS_A, BLK, BLK), lambda b, s: (0, 0, 0, 0), pipeline_mode=pl.Buffered(1)),
            qrow(N_IDX_HEADS * D_IDX), qrow(LANES), qnext(N_IDX_HEADS * D_IDX), qnext(LANES),
            brow(LANES), qrow(D_A), brow(LANES), brow(LANES),
        ],
        out_specs=qrow(D_A),
        out_shape=jax.ShapeDtypeStruct((batch * seq, D_A), BF16),
        scratch_shapes=[
            pltpu.VMEM((2, pair, nblk, BLK, BLK), F32),
            pltpu.VMEM((nblk, N_KV_A * (HEAD_DIM + ONES_ROWS), BLK), BF16),
            pltpu.VMEM((pair, nblk, N_HEADS_A, BLK, BLK), F32),
            pltpu.VMEM((pair, N_KV_A, HEAD_DIM + ONES_ROWS, (N_HEADS_A // N_KV_A) * BLK), F32),
            pltpu.VMEM((pair, SUBLANES, BLK), I32),
            pltpu.VMEM((pair, SUBLANES, BLK), I32),
            pltpu.VMEM((pair, SUBLANES, BLK), F32),
        ],
        compiler_params=pltpu.CompilerParams(dimension_semantics=("arbitrary", "arbitrary"),
                                             vmem_limit_bytes=VMEM_LIMIT),
        name="dsa_prompt",
    )(_bias_tiles(rel_bias), p["qi"], p["kw"], p["qi"], p["kw"], p["kw16"], p["qa"], p["k16"], p["v16"])


def _retention_kernel(q_ref, k_ref, v_ref, s0_ref, dmat_ref, din_ref, dout_ref, decm_ref, bdm_ref,
                      yn_ref, s_out_ref, s_scr, *, n_sub, pairs_per_phase):
    c = pl.program_id(1)
    chunk = din_ref.shape[0]

    @pl.when(c == 0)
    def _load_state():
        z = jnp.zeros((HEAD_DIM, HEAD_DIM), F32)
        for p in range(N_HEADS_B // 2):
            pair = jnp.concatenate([jnp.concatenate([s0_ref[0, 2 * p], z], axis=1),
                                    jnp.concatenate([z, s0_ref[0, 2 * p + 1]], axis=1)], axis=0)
            s_scr[p] = pair.T

    lo = lax.broadcasted_iota(I32, (chunk, LANES), 1) < HEAD_DIM
    zero = jnp.zeros((), BF16)
    n_pairs = N_HEADS_B // 2
    lanes = [slice(p * LANES, (p + 1) * LANES) for p in range(n_pairs)]
    rows = [slice(u * chunk, (u + 1) * chunk) for u in range(n_sub)]

    def run(pairs):
        units = [(p, u) for p in pairs for u in range(n_sub)]
        qd, kd, vt, s_a, s_b, intra, upd, inter = {}, {}, {}, {}, {}, {}, {}, {}
        for p, u in units:
            q, k, v = q_ref[0, rows[u], lanes[p]], k_ref[0, rows[u], lanes[p]], v_ref[0, rows[u], lanes[p]]
            qd[p, u] = (q.astype(F32) * din_ref[:, lanes[p]]).astype(BF16)
            kd[p, u] = (k.astype(F32) * dout_ref[:, lanes[p]]).astype(BF16)
            vt[p, u] = v.astype(F32).T.astype(BF16)
            s_a[p, u] = (_nt_dot(jnp.where(lo, k, zero), q) * dmat_ref[2 * p]).astype(BF16)
            s_b[p, u] = (_nt_dot(jnp.where(lo, zero, k), q) * dmat_ref[2 * p + 1]).astype(BF16)
        for p, u in units:
            intra[p, u] = jnp.concatenate([_dot(vt[p, u][:HEAD_DIM], s_a[p, u]),
                                           _dot(vt[p, u][HEAD_DIM:], s_b[p, u])], axis=0)
            upd[p, u] = _dot(vt[p, u], kd[p, u]) * bdm_ref[...]
        for p in pairs:
            st = s_scr[p]
            for u in range(n_sub):
                inter[p, u] = _nt_dot(st.astype(BF16), qd[p, u])
                st = st * decm_ref[p] + upd[p, u]
            s_scr[p] = st
            pair = st.T
            s_out_ref[0, 2 * p] = pair[:HEAD_DIM, :HEAD_DIM]
            s_out_ref[0, 2 * p + 1] = pair[HEAD_DIM:, HEAD_DIM:]
        for p, u in units:
            o = (inter[p, u] + intra[p, u]).reshape(2, HEAD_DIM, chunk)
            dlt = o - jnp.mean(o, axis=1, keepdims=True)
            var = jnp.mean(dlt * dlt, axis=1, keepdims=True)
            yn_ref[0, rows[u], lanes[p]] = (dlt * lax.rsqrt(var + EPS)).reshape(LANES, chunk).T.astype(BF16)

    for group in range(0, n_pairs, pairs_per_phase):
        run(range(group, group + pairs_per_phase))


def _ret_tables(chunk, n_real):
    h = N_HEADS_B
    log_g = jnp.log1p(-jnp.exp2(-5.0 - jnp.arange(h, dtype=F32)))
    i = jnp.arange(chunk, dtype=F32)
    diff = i[:, None] - i[None, :]
    dmat = jnp.where(diff >= 0, jnp.exp(jnp.maximum(diff, 0.0)[None] * log_g[:, None, None]), 0.0)
    dmat_t = jnp.swapaxes(dmat, 1, 2)
    dec_in = jnp.exp((i + 1.0)[:, None] * log_g[None, :])
    dec_out = jnp.exp((n_real - 1.0 - i)[:, None] * log_g[None, :])
    dec_chunk = jnp.exp(n_real * log_g)
    din = jnp.repeat(dec_in, HEAD_DIM, axis=1)
    dout = jnp.repeat(dec_out, HEAD_DIM, axis=1)
    head_of = np.arange(LANES) // HEAD_DIM
    bd = jnp.asarray((head_of[:, None] == head_of[None, :]).astype(np.float32))
    dc = jnp.repeat(dec_chunk, HEAD_DIM).reshape(h // 2, LANES)
    decm = dc[:, :, None] * bd[None]
    return dmat_t, din, dout, decm, bd


def _retention(q3, k3, v3, s0, n_real):
    batch, seq, _ = q3.shape
    chunk = min(RET_CHUNK, seq)
    n_sub = _largest_divisor(seq // chunk, 8)
    dmat_t, din, dout, decm, bd = _ret_tables(chunk, n_real)
    tok = pl.BlockSpec((1, n_sub * chunk, D_B), lambda b, c: (b, c, 0))
    st = pl.BlockSpec((1,) + s0.shape[1:], lambda b, c: (b, 0, 0, 0))
    full = lambda a: pl.BlockSpec(a.shape, lambda b, c: (0,) * a.ndim)
    return pl.pallas_call(
        functools.partial(_retention_kernel, n_sub=n_sub,
                          pairs_per_phase=max(1, min(N_HEADS_B // 2, 8 // n_sub))),
        grid=(batch, seq // (n_sub * chunk)),
        in_specs=[tok, tok, tok, st, full(dmat_t), full(din), full(dout), full(decm), full(bd)],
        out_specs=[tok, st],
        out_shape=[jax.ShapeDtypeStruct((batch, seq, D_B), BF16),
                   jax.ShapeDtypeStruct(s0.shape, F32)],
        scratch_shapes=[pltpu.VMEM((N_HEADS_B // 2, LANES, LANES), F32)],
        compiler_params=pltpu.CompilerParams(dimension_semantics=("arbitrary", "arbitrary"),
                                             vmem_limit_bytes=VMEM_LIMIT),
        name="retention",
    )(q3, k3, v3, s0, dmat_t, din, dout, decm, bd)


def _merge_kernel(x_ref, oa_ref, yn_ref, ga_ref, gb_ref, p_ref, wo_ref, gp_ref, wup_ref, wg_ref, out_ref):
    ga = ga_ref[...].astype(F32)
    gb = gb_ref[...].astype(F32)
    ya = (ga * _sigmoid(ga) * oa_ref[...].astype(F32)).astype(BF16)
    yb = (gb * _sigmoid(gb) * yn_ref[...].astype(F32)).astype(BF16)
    y = _dot(ya, wo_ref[0:D_A, :]) + _dot(yb, wo_ref[D_A:D_A + D_B, :])
    ms = jnp.mean(y * y, axis=-1, keepdims=True)
    x1 = x_ref[...] + y * lax.rsqrt(ms + EPS) * gp_ref[...]
    ple = _dot(p_ref[...].astype(BF16), wup_ref[...])
    gate = _sigmoid(_dot(x1.astype(BF16), wg_ref[...]))
    out_ref[...] = x1 + ple * gate


def _merge(x2d, oa, yn, ga, gb, p2d, wo16, g_post, wup16, wg16, tm):
    rows, d_model = x2d.shape
    row = lambda w: pl.BlockSpec((tm, w), lambda i: (i, 0))
    full = lambda a: pl.BlockSpec(a.shape, lambda i: (0, 0))
    return pl.pallas_call(
        _merge_kernel,
        grid=(rows // tm,),
        in_specs=[row(d_model), row(D_A), row(D_B), row(D_A), row(D_B), row(p2d.shape[1]),
                  full(wo16), full(g_post), full(wup16), full(wg16)],
        out_specs=row(d_model),
        out_shape=jax.ShapeDtypeStruct((rows, d_model), F32),
        compiler_params=pltpu.CompilerParams(dimension_semantics=("arbitrary",), vmem_limit_bytes=VMEM_LIMIT),
        name="merge",
    )(x2d, oa, yn, ga, gb, p2d, wo16, g_post, wup16, wg16)


def _page_copy(pages_hbm, buf, sem, pt_ref, b, p, slot):
    return pltpu.make_async_copy(pages_hbm.at[pt_ref[b, p]], buf.at[slot, p], sem.at[slot])


def _start_batch_pages(streams, pt_ref, b, slot, n_pages):
    def body(p, c):
        for pages_hbm, buf, sem in streams:
            _page_copy(pages_hbm, buf, sem, pt_ref, b, p, slot).start()
        return c
    lax.fori_loop(0, n_pages, body, 0)


def _wait_batch_pages(streams, pt_ref, b, slot, n_pages):
    for pages_hbm, buf, sem in streams:
        for p in range(n_pages):
            _page_copy(pages_hbm, buf, sem, pt_ref, b, p, slot).wait()


def _sample_index_kernel(pt_ref, qi_ref, w_ref, pages_hbm, out_ref, buf, sem, *, n_pages, unroll):
    b = pl.program_id(0)
    slot = b % 2
    streams = [(pages_hbm, buf, sem)]

    @pl.when(b == 0)
    def _first():
        _start_batch_pages(streams, pt_ref, 0, 0, n_pages)

    _wait_batch_pages(streams, pt_ref, b, slot, n_pages)

    qi = qi_ref[0]
    w = w_ref[0]
    n_q = qi.shape[0] // N_IDX_HEADS

    def run(prefetch):
        def body(c, carry):
            for u in range(unroll):
                p = c * unroll + u
                if prefetch:
                    _page_copy(pages_hbm, buf, sem, pt_ref, b + 1, p, 1 - slot).start()
                s = _dot(qi, buf[slot, p].astype(BF16))
                r = jnp.maximum(s, 0.0) * w
                out_ref[0, c, :, u * PAGE_SIZE:(u + 1) * PAGE_SIZE] = jnp.sum(
                    r.reshape(n_q, N_IDX_HEADS, PAGE_SIZE), axis=1)
            return carry
        lax.fori_loop(0, n_pages // unroll, body, 0)

    has_next = b + 1 < pl.num_programs(0)
    pl.when(has_next)(lambda: run(True))
    pl.when(jnp.logical_not(has_next))(lambda: run(False))


def _sample_index(page_table, qi_qh, w_qh, kidx_t, unroll):
    batch, n_pages = page_table.shape
    n_q = qi_qh.shape[1] // N_IDX_HEADS
    groups = n_pages // unroll
    grid_spec = pltpu.PrefetchScalarGridSpec(
        num_scalar_prefetch=1,
        grid=(batch,),
        in_specs=[pl.BlockSpec((1,) + qi_qh.shape[1:], lambda b, pt: (b, 0, 0)),
                  pl.BlockSpec((1,) + w_qh.shape[1:], lambda b, pt: (b, 0, 0)),
                  pl.BlockSpec(memory_space=pl.ANY)],
        out_specs=pl.BlockSpec((1, groups, n_q, unroll * PAGE_SIZE), lambda b, pt: (b, 0, 0, 0)),
        scratch_shapes=[pltpu.VMEM((2, n_pages) + kidx_t.shape[1:], kidx_t.dtype),
                        pltpu.SemaphoreType.DMA((2,))],
    )
    return pl.pallas_call(
        functools.partial(_sample_index_kernel, n_pages=n_pages, unroll=unroll),
        grid_spec=grid_spec,
        out_shape=jax.ShapeDtypeStruct((batch, groups, n_q, unroll * PAGE_SIZE), F32),
        compiler_params=pltpu.CompilerParams(dimension_semantics=("arbitrary",), vmem_limit_bytes=VMEM_LIMIT),
        name="sample_index",
    )(page_table, qi_qh, w_qh, kidx_t)


def _sample_select_kernel(sp_ref, qi_ref, w_ref, kw16_ref, thr_ref, cut_ref, nmn_ref, snew_scr,
                          *, topk, n_q, lane_chunk):
    rows, past = sp_ref.shape
    r_i = lax.broadcasted_iota(I32, (rows, LANES), 0)
    l_i = lax.broadcasted_iota(I32, (rows, LANES), 1)

    s = _nt_dot(qi_ref[...], kw16_ref[...][:, :D_IDX])
    r = jnp.maximum(s, 0.0) * w_ref[...]
    s_new = jnp.sum(r.reshape(rows, N_IDX_HEADS, LANES), axis=1)
    ok = ((l_i // n_q) == (r_i // n_q)) & ((l_i % n_q) <= (r_i % n_q))
    snew_scr[...] = jnp.where(ok, s_new, -jnp.inf)

    n_chunks = past // lane_chunk
    tiles = lane_chunk // LANES
    kf = float(topk)

    def count(ind):
        acc = ind(snew_scr[...], l_i + past)
        for ch in range(n_chunks):
            x = sp_ref[:, ch * lane_chunk:(ch + 1) * lane_chunk]
            for t in range(tiles):
                idx = l_i + (ch * lane_chunk + t * LANES)
                acc = acc + ind(x[:, t * LANES:(t + 1) * LANES], idx)
        return jnp.broadcast_to(jnp.sum(acc, axis=1, keepdims=True), (rows, LANES))

    c0 = count(lambda x, idx: jnp.where(x >= 0.0, 1.0, 0.0))
    key0 = jnp.where(c0 >= kf, 0, INT_MIN).astype(I32)

    def bit_body(it, key):
        cand = key | jnp.left_shift(jnp.int32(1), 30 - it)
        cf = _key_to_float(cand)
        return jnp.where(count(lambda x, idx: jnp.where(x >= cf, 1.0, 0.0)) >= kf, cand, key)

    key = lax.fori_loop(0, 31, bit_body, key0)
    key = jnp.maximum(key, KEY_NEG_FLT_MAX)
    thr = _key_to_float(key)
    cnt_gt = count(lambda x, idx: jnp.where(x > thr, 1.0, 0.0))
    cnt_ge = count(lambda x, idx: jnp.where(x >= thr, 1.0, 0.0))
    need = kf - cnt_gt
    straddle = jnp.max(jnp.where(cnt_ge > kf, 1.0, 0.0))

    idx_bits = (past + LANES).bit_length()
    thr_ref[...] = thr
    cut_ref[...] = jnp.full((rows, LANES), 1 << idx_bits, I32)

    @pl.when(straddle > 0.0)
    def _tie_cut():
        def cut_body(it, cut):
            cand = cut | jnp.left_shift(jnp.int32(1), (idx_bits - 1) - it)
            cnt = count(lambda x, idx: jnp.where(x == thr, jnp.where(idx < cand, 1.0, 0.0), 0.0))
            return jnp.where(cnt <= need, cand, cut)
        cut_ref[...] = lax.fori_loop(0, idx_bits, cut_body, jnp.zeros((rows, LANES), I32))

    cut = cut_ref[...]
    sn = snew_scr[...]
    tie_keep = jnp.where(l_i + past < cut, 0.0, -jnp.inf)
    nmn_ref[...] = jnp.where(sn > thr, 0.0, jnp.where(sn == thr, tie_keep, -jnp.inf))


def _sample_select(scores_past, qi_rows, w_rows, kw16_s, topk, n_q):
    rows, past = scores_past.shape
    lane_chunk = 2048 if past % 2048 == 0 else LANES
    full = lambda a: pl.BlockSpec(a.shape, lambda i: (0,) * a.ndim)
    o = jax.ShapeDtypeStruct((rows, LANES), F32)
    return pl.pallas_call(
        functools.partial(_sample_select_kernel, topk=topk, n_q=n_q, lane_chunk=lane_chunk),
        grid=(1,),
        in_specs=[full(scores_past), full(qi_rows), full(w_rows), full(kw16_s)],
        out_specs=[pl.BlockSpec((rows, LANES), lambda i: (0, 0))] * 3,
        out_shape=[o, jax.ShapeDtypeStruct((rows, LANES), I32), o],
        scratch_shapes=[pltpu.VMEM((rows, LANES), F32)],
        compiler_params=pltpu.CompilerParams(dimension_semantics=("arbitrary",), vmem_limit_bytes=VMEM_LIMIT),
        name="sample_select",
    )(scores_past, qi_rows, w_rows, kw16_s)


def _sample_attn_kernel(pt_ref, rbc_ref, q_ref, sp_ref, thr_ref, cut_ref, nmn_ref, k16n_ref, v16n_ref,
                        k_hbm, v_hbm, o_ref, kbuf, vbuf, ksem, vsem, lgs, bias_scr, mx_scr,
                        *, n_pages, n_q, unroll):
    b = pl.program_id(0)
    slot = b % 2
    rows = n_q * N_HEADS_A
    groups = n_pages // unroll
    streams = [(k_hbm, kbuf, ksem), (v_hbm, vbuf, vsem)]
    lane = lax.broadcasted_iota(I32, (rows, LANES), 1)
    rq = lax.broadcasted_iota(I32, (rows, LANES), 0) // N_HEADS_A
    expand = lambda a: jnp.concatenate(
        [jnp.broadcast_to(a[t:t + 1], (N_HEADS_A, LANES)) for t in range(n_q)], axis=0)

    @pl.when(b == 0)
    def _first():
        _start_batch_pages(streams, pt_ref, 0, 0, n_pages)
        far = jnp.concatenate([rbc_ref[N_BUCKETS - 1]] * n_q, axis=0)
        for t, dist in enumerate((PAGE_SIZE + rq - lane, rq - lane % n_q)):
            bkt = _t5_bucket(dist)
            tile = jnp.zeros((rows, LANES), F32)
            for k in range(N_BUCKETS - 1):
                tile = jnp.where(bkt == k, jnp.concatenate([rbc_ref[k]] * n_q, axis=0) - far, tile)
            bias_scr[t] = tile * LOG2E

    _wait_batch_pages(streams, pt_ref, b, slot, n_pages)

    thr = expand(thr_ref[0])
    cut = expand(cut_ref[0])
    q = q_ref[0]

    def logits_pass(prefetch):
        def body(c, mx):
            for u in range(unroll):
                p = c * unroll + u
                if prefetch:
                    for pages_hbm, buf, sem in streams:
                        _page_copy(pages_hbm, buf, sem, pt_ref, b + 1, p, 1 - slot).start()
                lg = _dot(q, kbuf[slot, p].astype(BF16))
                sc = expand(sp_ref[0, c, :, u * PAGE_SIZE:(u + 1) * PAGE_SIZE])
                tie_keep = jnp.where(lane + p * PAGE_SIZE < cut, 0.0, -jnp.inf)
                x = lg + jnp.where(sc > thr, 0.0, jnp.where(sc == thr, tie_keep, -jnp.inf))
                if u == unroll - 1:
                    x = x + bias_scr[0] * jnp.where(c == groups - 1, 1.0, 0.0)
                lgs[p] = x
                mx = jnp.maximum(mx, x)
            return mx
        mx_scr[...] = lax.fori_loop(0, groups, body, jnp.full((rows, LANES), NEG_BIG, F32))

    has_next = b + 1 < pl.num_programs(0)
    pl.when(has_next)(lambda: logits_pass(True))
    pl.when(jnp.logical_not(has_next))(lambda: logits_pass(False))

    xn = _nt_dot(q, k16n_ref[...]) + bias_scr[1] + expand(nmn_ref[0])
    m = jnp.max(jnp.maximum(mx_scr[...], xn), axis=1, keepdims=True)

    def pv_body(c, carry):
        lsum, acc = carry
        for u in range(unroll):
            p = c * unroll + u
            pr = jnp.exp2(lgs[p] - m)
            lsum = lsum + pr
            acc = acc + _nt_dot(pr.astype(BF16), vbuf[slot, p].astype(BF16))
        return lsum, acc

    pn = jnp.exp2(xn - m)
    lsum, acc = lax.fori_loop(0, groups, pv_body, (pn, _dot(pn.astype(BF16), v16n_ref[...])))
    out = acc * (1.0 / jnp.sum(lsum, axis=1, keepdims=True))
    head = lax.broadcasted_iota(I32, (rows, HEAD_DIM), 0) % N_HEADS_A
    o_ref[0] = jnp.where(head < N_HEADS_A // N_KV_A, out[:, :HEAD_DIM], out[:, HEAD_DIM:])


def _sample_attn(page_table, rb_col, q_bd, scores4, thr3, cut3, nmn3, k16n, v16n, k_pages_t, v_pages_t, unroll):
    batch, n_pages = page_table.shape
    n_q = scores4.shape[2]
    rows = q_bd.shape[1]
    per_b = lambda a: pl.BlockSpec((1,) + a.shape[1:], lambda b, pt: (b,) + (0,) * (a.ndim - 1))
    full = lambda a: pl.BlockSpec(a.shape, lambda b, pt: (0,) * a.ndim)
    page_buf = pltpu.VMEM((2, n_pages) + k_pages_t.shape[1:], k_pages_t.dtype)
    grid_spec = pltpu.PrefetchScalarGridSpec(
        num_scalar_prefetch=1,
        grid=(batch,),
        in_specs=[full(rb_col), per_b(q_bd), per_b(scores4), per_b(thr3), per_b(cut3), per_b(nmn3),
                  full(k16n), full(v16n), pl.BlockSpec(memory_space=pl.ANY), pl.BlockSpec(memory_space=pl.ANY)],
        out_specs=pl.BlockSpec((1, rows, HEAD_DIM), lambda b, pt: (b, 0, 0)),
        scratch_shapes=[page_buf, page_buf, pltpu.SemaphoreType.DMA((2,)), pltpu.SemaphoreType.DMA((2,)),
                        pltpu.VMEM((n_pages, rows, LANES), F32), pltpu.VMEM((2, rows, LANES), F32),
                        pltpu.VMEM((rows, LANES), F32)],
    )
    return pl.pallas_call(
        functools.partial(_sample_attn_kernel, n_pages=n_pages, n_q=n_q, unroll=unroll),
        grid_spec=grid_spec,
        out_shape=jax.ShapeDtypeStruct((batch, rows, HEAD_DIM), F32),
        compiler_params=pltpu.CompilerParams(dimension_semantics=("arbitrary",),
                                             vmem_limit_bytes=VMEM_LIMIT_PAGED),
        name="sample_attn",
    )(page_table, rb_col, q_bd, scores4, thr3, cut3, nmn3, k16n, v16n, k_pages_t, v_pages_t)


def _rope_tables(pos):
    half = HEAD_DIM // 2
    inv = ROPE_BASE ** (-jnp.arange(half, dtype=F32) / half)
    ang = pos.astype(F32)[:, None] * inv[None, :]
    cos = jnp.cos(ang)
    sin = jnp.sin(ang)
    reps = LANES // HEAD_DIM
    cos_t = jnp.tile(jnp.concatenate([cos, cos], axis=1), (1, reps))
    sin_t = jnp.tile(jnp.concatenate([-sin, sin], axis=1), (1, reps))
    return cos_t, sin_t


def _cat_weight(w_in):
    split = int(np.sum(SPLIT_SIZES[:6]))
    pad = jnp.zeros((w_in.shape[0], LANES - D_IDX - N_IDX_HEADS), BF16)
    return jnp.concatenate([w_in[:, :split].astype(BF16), pad, w_in[:, split:].astype(BF16)], axis=1)


def _pages_t(cache):
    pool, page = cache.shape[:2]
    return jnp.transpose(cache, (0, 2, 3, 1)).reshape(pool, -1, page)


def _largest_divisor(n, cap):
    d = cap
    while n % d:
        d //= 2
    return d


def kernel(x_prompt, x_sample, cache_k, cache_v, cache_kidx, state_ret, page_table, p_prompt, p_sample,
           rel_bias, w_in, w_out, g_pre, g_post, w_ple_up, w_ple_gate):
    batch, seq, d_model = x_prompt.shape
    dec_b, dec_t, _ = x_sample.shape
    depth = w_in.shape[0]
    n_pages = page_table.shape[1]
    past = n_pages * PAGE_SIZE
    rows_s = dec_b * dec_t
    assert depth == 1 and rows_s == LANES and seq % BLK == 0

    w_cat = _cat_weight(w_in[0])
    gpre = g_pre[0].reshape(1, d_model)
    gpost = g_post[0].reshape(1, d_model)
    wo16 = w_out[0].astype(BF16)
    wup16 = w_ple_up[0].astype(BF16)
    wg16 = w_ple_gate[0].astype(BF16)

    tm = _largest_divisor(seq, 1024)
    cos_p, sin_p = _rope_tables(jnp.arange(seq))
    xp2 = x_prompt.reshape(batch * seq, d_model)
    pp = _project(xp2, gpre, w_cat, cos_p, sin_p, tm, seq)
    oa_p = _dsa_prompt(pp, rel_bias, batch, seq)
    r3 = lambda a: a.reshape(batch, seq, D_B)
    s0_p = jnp.zeros((batch, N_HEADS_B, HEAD_DIM, HEAD_DIM), F32)
    yn_p, state_p = _retention(r3(pp["qb"]), r3(pp["kb"]), r3(pp["vb"]), s0_p, float(min(RET_CHUNK, seq)))
    y_prompt = _merge(xp2, oa_p, yn_p.reshape(batch * seq, D_B), pp["ga"], pp["gb"],
                      p_prompt[0].reshape(batch * seq, -1), wo16, gpost, wup16, wg16, tm)

    pos_s = past + jnp.arange(dec_t)
    cos_s, sin_s = _rope_tables(jnp.tile(pos_s, dec_b))
    xs2 = x_sample.reshape(rows_s, d_model)
    ps = _project(xs2, gpre, w_cat, cos_s, sin_s, rows_s, rows_s)
    topk_s = min(TOPK_MAX, (past + dec_t) // 4)

    qi_rows = ps["qi"].reshape(rows_s * N_IDX_HEADS, D_IDX)
    w_rows = jnp.broadcast_to(ps["kw"][:, D_IDX:D_IDX + N_IDX_HEADS].reshape(rows_s * N_IDX_HEADS, 1),
                              (rows_s * N_IDX_HEADS, LANES))
    unroll = _largest_divisor(n_pages, 32)
    scores4 = _sample_index(page_table, qi_rows.reshape(dec_b, dec_t * N_IDX_HEADS, D_IDX),
                            w_rows.reshape(dec_b, dec_t * N_IDX_HEADS, LANES),
                            jnp.transpose(cache_kidx[0], (0, 2, 1)), unroll)
    scores_past = scores4.transpose(0, 2, 1, 3).reshape(rows_s, past)
    thr, cut, nmn = _sample_select(scores_past, qi_rows, w_rows, ps["kw16"], topk_s, dec_t)

    kv_of_head = np.arange(N_HEADS_A) // (N_HEADS_A // N_KV_A)
    place = jnp.asarray((kv_of_head[:, None] == np.arange(N_KV_A)[None, :]).astype(np.float32)).astype(BF16)
    q_bd = (ps["qa"].reshape(dec_b, dec_t, N_HEADS_A, 1, HEAD_DIM) * place[None, None, :, :, None]).reshape(
        dec_b, dec_t * N_HEADS_A, N_KV_A * HEAD_DIM)
    rb_col = jnp.broadcast_to(rel_bias[:, :, None], rel_bias.shape + (LANES,))
    b3 = lambda a: a.reshape(dec_b, dec_t, LANES)
    o_qh = _sample_attn(page_table, rb_col, q_bd, scores4, b3(thr), b3(cut), b3(nmn),
                        ps["k16"], ps["v16"], _pages_t(cache_k[0]), _pages_t(cache_v[0]), unroll)
    oa_s = o_qh.reshape(rows_s, D_A)

    chunk_s = RET_CHUNK
    padt = lambda a: jnp.pad(a.reshape(dec_b, dec_t, D_B), ((0, 0), (0, chunk_s - dec_t), (0, 0)))
    yn_s, state_s = _retention(padt(ps["qb"]), padt(ps["kb"]), padt(ps["vb"]),
                               state_ret[0].astype(F32), float(math.gcd(dec_t, RET_CHUNK)))
    y_sample = _merge(xs2, oa_s, yn_s[:, :dec_t].reshape(rows_s, D_B), ps["ga"], ps["gb"],
                      p_sample[0].reshape(rows_s, -1), wo16, gpost, wup16, wg16, rows_s)

    def kv(a_t, b, t):
        a = a_t.reshape(a_t.shape[0], N_KV_A, HEAD_DIM, -1).transpose(0, 3, 1, 2)
        return a.reshape(1, b, t, N_KV_A, HEAD_DIM)

    def ki(a_t, b, t):
        return a_t.transpose(0, 2, 1).reshape(1, b, t, D_IDX)

    return (
        y_prompt.reshape(batch, seq, d_model),
        y_sample.reshape(dec_b, dec_t, d_model),
        kv(pp["ka_t"], batch, seq), kv(pp["va_t"], batch, seq),
        ki(pp["ki_t"], batch, seq),
        state_p[None].astype(state_ret.dtype),
        kv(ps["ka_t"], dec_b, dec_t), kv(ps["va_t"], dec_b, dec_t),
        ki(ps["ki_t"], dec_b, dec_t),
        state_s[None].astype(state_ret.dtype),
    )
```

```python
import functools
import math

import jax
import jax.numpy as jnp
import numpy as np
from jax import lax
from jax.experimental import pallas as pl
from jax.experimental.pallas import tpu as pltpu

F32 = jnp.float32
BF16 = jnp.bfloat16
I32 = jnp.int32

HEAD_DIM = 64
N_HEADS_A = 8
N_KV_A = 2
N_IDX_HEADS = 8
D_IDX = 64
TOPK_MAX = 256
N_HEADS_B = 8
D_A = N_HEADS_A * HEAD_DIM
D_B = N_HEADS_B * HEAD_DIM
N_BUCKETS = 32
MAX_DISTANCE = 128
ROPE_BASE = 10000.0
RET_CHUNK = 128
PAGE_SIZE = 128
EPS = 1e-6
SPLIT_SIZES = (D_A, N_KV_A * HEAD_DIM, N_KV_A * HEAD_DIM, N_IDX_HEADS * D_IDX, D_IDX, N_IDX_HEADS,
               D_A, D_B, D_B, D_B, D_B)

LANES = 128
SUBLANES = 8
BLK = 128
VMEM_LIMIT = 48 * 1024 * 1024
VMEM_LIMIT_PAGED = 56 * 1024 * 1024

INT_MIN = -(2 ** 31)
KEY_NEG_FLT_MAX = INT_MIN + 0x00800000
NEG_BIG = -1e30
LOG2E = math.log2(math.e)
ONES_ROWS = 16

_C_QA, _C_KA, _C_VA, _C_QI, _C_KW, _C_GA, _C_QB, _C_KB, _C_VB, _C_GB, _C_END = (
    0, 512, 640, 768, 1280, 1408, 1920, 2432, 2944, 3456, 3968)


def _nt_dot(a, b):
    return lax.dot_general(a, b, (((1,), (1,)), ((), ())), preferred_element_type=F32)


def _dot(a, b):
    return jnp.dot(a, b, preferred_element_type=F32)


def _sigmoid(x):
    return 1.0 / (1.0 + jnp.exp(-x))


def _key_to_float(k):
    bits = jnp.where(k >= 0, k, k ^ 0x7FFFFFFF)
    return lax.bitcast_convert_type(bits, F32)


def _t5_bucket(n):
    max_exact = N_BUCKETS // 2
    n = jnp.maximum(n, 0)
    nf = jnp.maximum(n, 1).astype(F32)
    large = max_exact + jnp.floor(jnp.log(nf / max_exact) / math.log(MAX_DISTANCE / max_exact)
                                  * (N_BUCKETS - max_exact)).astype(I32)
    large = jnp.minimum(large, N_BUCKETS - 1)
    return jnp.where(n < max_exact, n, large)


def _bias_from_bucket(bkt, rb_ref, h):
    out = jnp.zeros(bkt.shape, F32)
    for k in range(N_BUCKETS):
        out = jnp.where(bkt == k, rb_ref[k, h], out)
    return out


def _proj_kernel(x_ref, g_ref, w_ref, cos_ref, sin_ref,
                 qa_ref, kat_ref, vat_ref, k16_ref, v16_ref, qi_ref, ga_ref, gb_ref,
                 qb_ref, kb_ref, vb_ref, kw_ref, kw16_ref, kit_ref, h_scr):
    x = x_ref[...]
    ms = jnp.mean(x * x, axis=-1, keepdims=True)
    h_scr[...] = (x * lax.rsqrt(ms + EPS) * g_ref[...]).astype(BF16)

    def mm(lo, hi):
        return _dot(h_scr[...], w_ref[:, lo:hi])

    qa_ref[...] = (mm(_C_QA, _C_KA) * (HEAD_DIM ** -0.5 * LOG2E)).astype(BF16)
    ka = mm(_C_KA, _C_VA)
    kat_ref[0] = ka.T
    k16_ref[...] = ka.astype(BF16)
    va = mm(_C_VA, _C_QI)
    vat_ref[0] = va.T
    v16_ref[...] = va.astype(BF16)
    qi_ref[...] = mm(_C_QI, _C_KW).astype(BF16)
    ga_ref[...] = mm(_C_GA, _C_QB).astype(BF16)
    gb_ref[...] = mm(_C_GB, _C_END).astype(BF16)
    vb_ref[...] = mm(_C_VB, _C_GB).astype(BF16)

    cos = cos_ref[...]
    sin = sin_ref[...]
    lane = lax.broadcasted_iota(I32, cos.shape, 1)
    first_half = (lane % HEAD_DIM) < (HEAD_DIM // 2)

    def rope(z, scale):
        outs = []
        for g in range(z.shape[1] // LANES):
            zg = z[:, g * LANES:(g + 1) * LANES]
            partner = jnp.where(first_half, pltpu.roll(zg, LANES - HEAD_DIM // 2, 1),
                                pltpu.roll(zg, HEAD_DIM // 2, 1))
            r = zg * cos + partner * sin
            if scale != 1.0:
                r = r * scale
            outs.append(r.astype(BF16))
        return jnp.concatenate(outs, axis=1)

    qb_ref[...] = rope(mm(_C_QB, _C_KB), 1.0)
    kb_ref[...] = rope(mm(_C_KB, _C_VB), HEAD_DIM ** -0.5)

    kw = mm(_C_KW, _C_GA)
    wscale = (N_IDX_HEADS ** -0.5) * (D_IDX ** -0.5)
    kw = kw * jnp.where(lane >= D_IDX, wscale, 1.0)
    kw_ref[...] = kw
    kw16_ref[...] = kw.astype(BF16)
    kit_ref[0] = kw.T[:D_IDX]


def _project(x2d, g_pre, w_cat, cos_t, sin_t, tm, seq):
    rows = x2d.shape[0]
    d_model = x2d.shape[1]
    n_tab = cos_t.shape[0] // tm
    tiles = seq // tm
    row_spec = lambda w: pl.BlockSpec((tm, w), lambda i: (i, 0))
    t_spec = lambda w: pl.BlockSpec((1, w, tm), lambda i: (i // tiles, 0, i % tiles))
    outs = [
        ("qa", D_A, BF16), ("ka_t", LANES, F32), ("va_t", LANES, F32), ("k16", LANES, BF16), ("v16", LANES, BF16),
        ("qi", N_IDX_HEADS * D_IDX, BF16), ("ga", D_A, BF16), ("gb", D_B, BF16),
        ("qb", D_B, BF16), ("kb", D_B, BF16), ("vb", D_B, BF16), ("kw", LANES, F32), ("kw16", LANES, BF16),
        ("ki_t", D_IDX, F32),
    ]
    transposed = lambda n: n.endswith("_t")
    res = pl.pallas_call(
        _proj_kernel,
        grid=(rows // tm,),
        in_specs=[
            row_spec(d_model),
            pl.BlockSpec((1, d_model), lambda i: (0, 0)),
            pl.BlockSpec(w_cat.shape, lambda i: (0, 0), pipeline_mode=pl.Buffered(1)),
            pl.BlockSpec((tm, LANES), lambda i: (i % n_tab, 0)),
            pl.BlockSpec((tm, LANES), lambda i: (i % n_tab, 0)),
        ],
        out_specs=[t_spec(w) if transposed(n) else row_spec(w) for n, w, _ in outs],
        out_shape=[jax.ShapeDtypeStruct((rows // seq, w, seq) if transposed(n) else (rows, w), dt)
                   for n, w, dt in outs],
        scratch_shapes=[pltpu.VMEM((tm, d_model), BF16)],
        compiler_params=pltpu.CompilerParams(dimension_semantics=("arbitrary",), vmem_limit_bytes=VMEM_LIMIT),
        name="proj",
    )(x2d, g_pre, w_cat, cos_t, sin_t)
    return {n: r for (n, _, _), r in zip(outs, res)}


def _dsa_prompt_kernel(rb_ref, qi_ref, kwq_ref, qin_ref, kwqn_ref, kw16_ref, qa_ref, k16_ref, v16_ref, oa_ref,
                       sc2, bt, vt, lgs, acc_scr, cut_scr, key_scr, cnt_scr, *, topk, search_blk, pair):
    b = pl.program_id(0)
    s = pl.program_id(1)
    n_steps = pl.num_programs(1)
    blocks = range(pair)
    js = [s * pair + ab for ab in blocks]
    scs = [sc2.at[s % 2, ab] for ab in blocks]
    scs_next = [sc2.at[1 - s % 2, ab] for ab in blocks]
    row = lax.broadcasted_iota(I32, (BLK, BLK), 0)
    col = lax.broadcasted_iota(I32, (BLK, BLK), 1)
    rows_of = lambda ab: slice(ab * BLK, (ab + 1) * BLK)

    @pl.when((b == 0) & (s == 0))
    def _init_bias():
        for d in range(2):
            bkt = _t5_bucket(col - row + d * BLK)
            for h in range(N_HEADS_A):
                bt[d, h] = (_bias_from_bucket(bkt, rb_ref, h) - rb_ref[N_BUCKETS - 1, h]) * LOG2E
        bt[2] = jnp.zeros(bt.shape[1:], F32)

    @pl.when(s == 0)
    def _clear_values():
        vt[...] = jnp.zeros(vt.shape, BF16)

    ones = jnp.ones((ONES_ROWS, BLK), BF16)
    for ab in blocks:
        vblk = v16_ref[pl.ds(pl.multiple_of(js[ab] * BLK, BLK), BLK), :]
        vtj = vblk.astype(F32).T.astype(BF16)
        vt[js[ab]] = jnp.concatenate([piece for n in range(N_KV_A)
                                      for piece in (vtj[n * HEAD_DIM:(n + 1) * HEAD_DIM], ones)], axis=0)

    def indexer_queries(qi_blk_ref, kw_blk_ref, ab):
        qi = qi_blk_ref[rows_of(ab), :]
        qi_stack = jnp.concatenate([qi[:, h * D_IDX:(h + 1) * D_IDX] for h in range(N_IDX_HEADS)], axis=0)
        return qi_stack, kw_blk_ref[rows_of(ab), :].T

    def score_products(c, queries):
        out = []
        for u in range(search_blk):
            i = c * search_blk + u
            kib = kw16_ref[pl.ds(pl.multiple_of(i * BLK, BLK), BLK), :][:, :D_IDX]
            out.append(_nt_dot(kib, queries[0]))
        return out

    def score_finish(c, products, dst, queries, first_q):
        wi_t = queries[1]
        for u, s in enumerate(products):
            i = c * search_blk + u
            acc = jnp.zeros((BLK, BLK), F32)
            for h in range(N_IDX_HEADS):
                acc = acc + wi_t[D_IDX + h:D_IDX + h + 1, :] * jnp.maximum(s[:, h * BLK:(h + 1) * BLK], 0.0)
            dst[i] = jnp.where(row + i * BLK <= col + first_q, acc, -jnp.inf)

    def score_group(c, dst, queries, first_q):
        score_finish(c, score_products(c, queries), dst, queries, first_q)

    @pl.when(s == 0)
    def _own_scores():
        for ab in blocks:
            score_group(0, scs[ab], indexer_queries(qi_ref, kwq_ref, ab), ab * BLK)

    n_chunk = (js[0] + search_blk) // search_blk

    def col_sum(w):
        return jnp.sum(w.reshape(BLK // SUBLANES, SUBLANES, BLK), axis=0)

    g_per_kv = N_HEADS_A // N_KV_A
    q_stacks = []
    for ab in blocks:
        qa = qa_ref[rows_of(ab), :]
        q_stacks.append([jnp.concatenate([qa[:, (n * g_per_kv + g) * HEAD_DIM:(n * g_per_kv + g + 1) * HEAD_DIM]
                                          for g in range(g_per_kv)], axis=0) for n in range(N_KV_A)])

    kf = float(topk)
    n_units = search_blk * N_KV_A * pair

    def search(n_groups):
        def count_ge(cands, unit=None):
            raw = []
            if unit is not None:
                n, ab = unit[1], unit[2]
                for c in range(n_groups):
                    i = c * search_blk + unit[0]
                    raw.append((ab, i, _nt_dot(k16_ref[i * BLK:(i + 1) * BLK, n * HEAD_DIM:(n + 1) * HEAD_DIM],
                                               q_stacks[ab][n])))
            counts = []
            for ab in blocks:
                acc = jnp.zeros((SUBLANES, BLK), F32)
                for c in range(n_groups):
                    for u in range(search_blk):
                        acc = acc + col_sum(jnp.where(scs[ab][c * search_blk + u] >= cands[ab], 1.0, 0.0))
                counts.append(jnp.sum(acc, axis=0, keepdims=True))
            for ab, i, lg in raw:
                for g in range(g_per_kv):
                    lgs[ab, i, unit[1] * g_per_kv + g] = lg[:, g * BLK:(g + 1) * BLK]
            return counts

        def bit_step(bit, carry, unit=None):
            cands = [key | jnp.left_shift(jnp.int32(1), bit) for key, _ in carry]
            counts = count_ge([_key_to_float(cand) for cand in cands], unit)
            return tuple((jnp.where(cnt >= kf, cand, key), jnp.where(cnt >= kf, cnt, cnt_key))
                         for (key, cnt_key), cand, cnt in zip(carry, cands, counts))

        c0s = count_ge([jnp.zeros((1, BLK), F32)] * pair)
        carry = tuple((jnp.where(c0 >= kf, 0, INT_MIN).astype(I32), jnp.where(c0 >= kf, c0, 0.0)) for c0 in c0s)
        for it in range(n_units):
            carry = bit_step(30 - it, carry, (it // (N_KV_A * pair), (it // pair) % N_KV_A, it % pair))
        n_rest = jnp.where((js[-1] + 1) * BLK <= topk, 0, 31 - n_units)
        carry = lax.fori_loop(0, n_rest, lambda it, cr: bit_step(30 - n_units - it, cr), carry)
        for ab in blocks:
            key_scr[ab] = jnp.broadcast_to(carry[ab][0], key_scr.shape[1:])
            cnt_scr[ab] = jnp.broadcast_to(carry[ab][1], cnt_scr.shape[1:])

    for n_groups in range(1, sc2.shape[2] // search_blk + 1):
        pl.when(n_chunk == n_groups)(functools.partial(search, n_groups))

    z8 = jnp.zeros((SUBLANES, BLK), F32)
    n_keys_max = sc2.shape[2] * BLK
    vrows = HEAD_DIM + ONES_ROWS

    def col_max(w):
        return jnp.max(w.reshape(BLK // SUBLANES, SUBLANES, BLK), axis=0)

    n_chunk_next = (js[0] + pair + search_blk) // search_blk

    def attend(ab):
        j, sc, sc_next = js[ab], scs[ab], scs_next[ab]
        lg_ab, acc_ab = lgs.at[ab], acc_scr.at[ab]
        key = key_scr[ab, 0:1, :]
        cnt_ge = jnp.where(key < KEY_NEG_FLT_MAX, 0.0, cnt_scr[ab, 0:1, :])
        thr = _key_to_float(jnp.maximum(key, KEY_NEG_FLT_MAX))
        straddle = jnp.max(jnp.where(cnt_ge > kf, 1.0, 0.0))
        cut_scr[ab] = jnp.full(cut_scr.shape[1:], 2 * n_keys_max, I32)

        @pl.when(straddle > 0.0)
        def _tie_cut():
            def gt_body(i, acc):
                return acc + col_sum(jnp.where(sc[i] > thr, 1.0, 0.0))
            cnt_gt = jnp.sum(lax.fori_loop(0, j + 1, gt_body, z8), axis=0, keepdims=True)
            need = kf - cnt_gt

            def count_tie(cand):
                def body(i, acc):
                    t = sc[i]
                    hit = jnp.where(t == thr, jnp.where(row + i * BLK < cand, 1.0, 0.0), 0.0)
                    return acc + col_sum(hit)
                acc = lax.fori_loop(0, j + 1, body, z8)
                return jnp.sum(acc, axis=0, keepdims=True)

            def cut_body(it, cut):
                cand = cut | jnp.left_shift(jnp.int32(1), (n_keys_max.bit_length() - 1) - it)
                return jnp.where(count_tie(cand) <= need, cand, cut)

            cut = lax.fori_loop(0, n_keys_max.bit_length(), cut_body, jnp.zeros((1, BLK), I32))
            cut_scr[ab] = jnp.broadcast_to(cut, cut_scr.shape[1:])

        cut = cut_scr[ab, 0:1, :]

        def mask_body(c, m8, biased):
            m8 = list(m8)
            for u in range(search_blk):
                i = c * search_blk + u
                t = sc[i]
                tie_keep = jnp.where(row + i * BLK < cut, 0.0, -jnp.inf)
                nm = jnp.where(t > thr, 0.0, jnp.where(t == thr, tie_keep, -jnp.inf))
                near = jnp.clip(j - i, 0, 2)
                for h in range(N_HEADS_A):
                    x = lg_ab[i, h] + nm
                    if biased:
                        x = x + bt[near, h]
                    lg_ab[i, h] = x
                    m8[h] = jnp.maximum(m8[h], col_max(x))
            return tuple(m8)

        n_far = jnp.maximum(n_chunk - 2, 0)
        neg8 = jnp.full((SUBLANES, BLK), NEG_BIG, F32)
        m8 = lax.fori_loop(0, n_far, functools.partial(mask_body, biased=False), (neg8,) * N_HEADS_A)
        m8 = lax.fori_loop(n_far, n_chunk, functools.partial(mask_body, biased=True), m8)
        m_row = [jnp.max(m8[h], axis=0, keepdims=True) for h in range(N_HEADS_A)]

        acc_ab[...] = jnp.zeros(acc_ab.shape, F32)
        next_queries = indexer_queries(qin_ref, kwqn_ref, ab)
        next_first_q = (j + pair) * BLK

        def pv_body(c, carry):
            pv = [None] * N_KV_A
            for u in range(search_blk):
                i = c * search_blk + u
                vti = vt[i]
                for n in range(N_KV_A):
                    ps = [jnp.exp2(lg_ab[i, n * g_per_kv + g] - m_row[n * g_per_kv + g]).astype(BF16)
                          for g in range(g_per_kv)]
                    t = _dot(vti[n * vrows:(n + 1) * vrows, :], jnp.concatenate(ps, axis=1))
                    pv[n] = t if pv[n] is None else pv[n] + t
            for n in range(N_KV_A):
                acc_ab[n] += pv[n]
            score_group(c, sc_next, next_queries, next_first_q)
            return carry

        lax.fori_loop(0, n_chunk, pv_body, 0)

        @pl.when((n_chunk_next > n_chunk) & (s + 1 < n_steps))
        def _next_reaches_one_more_group():
            score_group(n_chunk, sc_next, next_queries, next_first_q)

        parts = []
        for n in range(N_KV_A):
            acc = acc_ab[n]
            out = acc[:HEAD_DIM] * (1.0 / acc[HEAD_DIM:HEAD_DIM + 1])
            parts += [out[:, g * BLK:(g + 1) * BLK] for g in range(g_per_kv)]
        oa_ref[rows_of(ab), :] = jnp.concatenate(parts, axis=0).T.astype(BF16)

    for ab in blocks:
        attend(ab)


def _dsa_prompt(p, rel_bias, batch, seq):
    nblk = seq // BLK
    topk = min(TOPK_MAX, seq // 4)
    search_blk = 4 if nblk % 4 == 0 else 1
    pair = 2 if search_blk % 2 == 0 else 1
    n_steps = nblk // pair
    qrow = lambda w: pl.BlockSpec((pair * BLK, w), lambda b, s: (b * n_steps + s, 0))
    qnext = lambda w: pl.BlockSpec((pair * BLK, w),
                                   lambda b, s: (b * n_steps + jnp.minimum(s + 1, n_steps - 1), 0))
    brow = lambda w: pl.BlockSpec((seq, w), lambda b, s: (b, 0))
    return pl.pallas_call(
        functools.partial(_dsa_prompt_kernel, topk=topk, search_blk=search_blk, pair=pair),
        grid=(batch, n_steps),
        in_specs=[
            pl.BlockSpec(memory_space=pltpu.SMEM),
            qrow(N_IDX_HEADS * D_IDX), qrow(LANES), qnext(N_IDX_HEADS * D_IDX), qnext(LANES),
            brow(LANES), qrow(D_A), brow(LANES), brow(LANES),
        ],
        out_specs=qrow(D_A),
        out_shape=jax.ShapeDtypeStruct((batch * seq, D_A), BF16),
        scratch_shapes=[
            pltpu.VMEM((2, pair, nblk, BLK, BLK), F32),
            pltpu.VMEM((3, N_HEADS_A, BLK, BLK), F32),
            pltpu.VMEM((nblk, N_KV_A * (HEAD_DIM + ONES_ROWS), BLK), BF16),
            pltpu.VMEM((pair, nblk, N_HEADS_A, BLK, BLK), F32),
            pltpu.VMEM((pair, N_KV_A, HEAD_DIM + ONES_ROWS, (N_HEADS_A // N_KV_A) * BLK), F32),
            pltpu.VMEM((pair, SUBLANES, BLK), I32),
            pltpu.VMEM((pair, SUBLANES, BLK), I32),
            pltpu.VMEM((pair, SUBLANES, BLK), F32),
        ],
        compiler_params=pltpu.CompilerParams(dimension_semantics=("arbitrary", "arbitrary"),
                                             vmem_limit_bytes=VMEM_LIMIT),
        name="dsa_prompt",
    )(rel_bias, p["qi"], p["kw"], p["qi"], p["kw"], p["kw16"], p["qa"], p["k16"], p["v16"])


def _retention_kernel(q_ref, k_ref, v_ref, s0_ref, dmat_ref, din_ref, dout_ref, decm_ref, bdm_ref,
                      yn_ref, s_out_ref, s_scr, *, n_sub, pairs_per_phase):
    c = pl.program_id(1)
    chunk = din_ref.shape[0]

    @pl.when(c == 0)
    def _load_state():
        z = jnp.zeros((HEAD_DIM, HEAD_DIM), F32)
        for p in range(N_HEADS_B // 2):
            pair = jnp.concatenate([jnp.concatenate([s0_ref[0, 2 * p], z], axis=1),
                                    jnp.concatenate([z, s0_ref[0, 2 * p + 1]], axis=1)], axis=0)
            s_scr[p] = pair.T

    lo = lax.broadcasted_iota(I32, (chunk, LANES), 1) < HEAD_DIM
    zero = jnp.zeros((), BF16)
    n_pairs = N_HEADS_B // 2
    lanes = [slice(p * LANES, (p + 1) * LANES) for p in range(n_pairs)]
    rows = [slice(u * chunk, (u + 1) * chunk) for u in range(n_sub)]

    def run(pairs):
        units = [(p, u) for p in pairs for u in range(n_sub)]
        qd, kd, vt, s_a, s_b, intra, upd, inter = {}, {}, {}, {}, {}, {}, {}, {}
        for p, u in units:
            q, k, v = q_ref[0, rows[u], lanes[p]], k_ref[0, rows[u], lanes[p]], v_ref[0, rows[u], lanes[p]]
            qd[p, u] = (q.astype(F32) * din_ref[:, lanes[p]]).astype(BF16)
            kd[p, u] = (k.astype(F32) * dout_ref[:, lanes[p]]).astype(BF16)
            vt[p, u] = v.astype(F32).T.astype(BF16)
            s_a[p, u] = (_nt_dot(jnp.where(lo, k, zero), q) * dmat_ref[2 * p]).astype(BF16)
            s_b[p, u] = (_nt_dot(jnp.where(lo, zero, k), q) * dmat_ref[2 * p + 1]).astype(BF16)
        for p, u in units:
            intra[p, u] = jnp.concatenate([_dot(vt[p, u][:HEAD_DIM], s_a[p, u]),
                                           _dot(vt[p, u][HEAD_DIM:], s_b[p, u])], axis=0)
            upd[p, u] = _dot(vt[p, u], kd[p, u]) * bdm_ref[...]
        for p in pairs:
            st = s_scr[p]
            for u in range(n_sub):
                inter[p, u] = _nt_dot(st.astype(BF16), qd[p, u])
                st = st * decm_ref[p] + upd[p, u]
            s_scr[p] = st
            pair = st.T
            s_out_ref[0, 2 * p] = pair[:HEAD_DIM, :HEAD_DIM]
            s_out_ref[0, 2 * p + 1] = pair[HEAD_DIM:, HEAD_DIM:]
        for p, u in units:
            o = (inter[p, u] + intra[p, u]).reshape(2, HEAD_DIM, chunk)
            dlt = o - jnp.mean(o, axis=1, keepdims=True)
            var = jnp.mean(dlt * dlt, axis=1, keepdims=True)
            yn_ref[0, rows[u], lanes[p]] = (dlt * lax.rsqrt(var + EPS)).reshape(LANES, chunk).T.astype(BF16)

    for group in range(0, n_pairs, pairs_per_phase):
        run(range(group, group + pairs_per_phase))


def _ret_tables(chunk, n_real):
    h = N_HEADS_B
    log_g = jnp.log1p(-jnp.exp2(-5.0 - jnp.arange(h, dtype=F32)))
    i = jnp.arange(chunk, dtype=F32)
    diff = i[:, None] - i[None, :]
    dmat = jnp.where(diff >= 0, jnp.exp(jnp.maximum(diff, 0.0)[None] * log_g[:, None, None]), 0.0)
    dmat_t = jnp.swapaxes(dmat, 1, 2)
    dec_in = jnp.exp((i + 1.0)[:, None] * log_g[None, :])
    dec_out = jnp.exp((n_real - 1.0 - i)[:, None] * log_g[None, :])
    dec_chunk = jnp.exp(n_real * log_g)
    din = jnp.repeat(dec_in, HEAD_DIM, axis=1)
    dout = jnp.repeat(dec_out, HEAD_DIM, axis=1)
    head_of = np.arange(LANES) // HEAD_DIM
    bd = jnp.asarray((head_of[:, None] == head_of[None, :]).astype(np.float32))
    dc = jnp.repeat(dec_chunk, HEAD_DIM).reshape(h // 2, LANES)
    decm = dc[:, :, None] * bd[None]
    return dmat_t, din, dout, decm, bd


def _retention(q3, k3, v3, s0, n_real):
    batch, seq, _ = q3.shape
    chunk = min(RET_CHUNK, seq)
    n_sub = _largest_divisor(seq // chunk, 8)
    dmat_t, din, dout, decm, bd = _ret_tables(chunk, n_real)
    tok = pl.BlockSpec((1, n_sub * chunk, D_B), lambda b, c: (b, c, 0))
    st = pl.BlockSpec((1,) + s0.shape[1:], lambda b, c: (b, 0, 0, 0))
    full = lambda a: pl.BlockSpec(a.shape, lambda b, c: (0,) * a.ndim)
    return pl.pallas_call(
        functools.partial(_retention_kernel, n_sub=n_sub,
                          pairs_per_phase=max(1, min(N_HEADS_B // 2, 8 // n_sub))),
        grid=(batch, seq // (n_sub * chunk)),
        in_specs=[tok, tok, tok, st, full(dmat_t), full(din), full(dout), full(decm), full(bd)],
        out_specs=[tok, st],
        out_shape=[jax.ShapeDtypeStruct((batch, seq, D_B), BF16),
                   jax.ShapeDtypeStruct(s0.shape, F32)],
        scratch_shapes=[pltpu.VMEM((N_HEADS_B // 2, LANES, LANES), F32)],
        compiler_params=pltpu.CompilerParams(dimension_semantics=("arbitrary", "arbitrary"),
                                             vmem_limit_bytes=VMEM_LIMIT),
        name="retention",
    )(q3, k3, v3, s0, dmat_t, din, dout, decm, bd)


def _merge_kernel(x_ref, oa_ref, yn_ref, ga_ref, gb_ref, p_ref, wo_ref, gp_ref, wup_ref, wg_ref, out_ref):
    ga = ga_ref[...].astype(F32)
    gb = gb_ref[...].astype(F32)
    ya = (ga * _sigmoid(ga) * oa_ref[...].astype(F32)).astype(BF16)
    yb = (gb * _sigmoid(gb) * yn_ref[...].astype(F32)).astype(BF16)
    y = _dot(ya, wo_ref[0:D_A, :]) + _dot(yb, wo_ref[D_A:D_A + D_B, :])
    ms = jnp.mean(y * y, axis=-1, keepdims=True)
    x1 = x_ref[...] + y * lax.rsqrt(ms + EPS) * gp_ref[...]
    ple = _dot(p_ref[...].astype(BF16), wup_ref[...])
    gate = _sigmoid(_dot(x1.astype(BF16), wg_ref[...]))
    out_ref[...] = x1 + ple * gate


def _merge(x2d, oa, yn, ga, gb, p2d, wo16, g_post, wup16, wg16, tm):
    rows, d_model = x2d.shape
    row = lambda w: pl.BlockSpec((tm, w), lambda i: (i, 0))
    full = lambda a: pl.BlockSpec(a.shape, lambda i: (0, 0))
    return pl.pallas_call(
        _merge_kernel,
        grid=(rows // tm,),
        in_specs=[row(d_model), row(D_A), row(D_B), row(D_A), row(D_B), row(p2d.shape[1]),
                  full(wo16), full(g_post), full(wup16), full(wg16)],
        out_specs=row(d_model),
        out_shape=jax.ShapeDtypeStruct((rows, d_model), F32),
        compiler_params=pltpu.CompilerParams(dimension_semantics=("arbitrary",), vmem_limit_bytes=VMEM_LIMIT),
        name="merge",
    )(x2d, oa, yn, ga, gb, p2d, wo16, g_post, wup16, wg16)


def _page_copy(pages_hbm, buf, sem, pt_ref, b, p, slot):
    return pltpu.make_async_copy(pages_hbm.at[pt_ref[b, p]], buf.at[slot, p], sem.at[slot])


def _start_batch_pages(streams, pt_ref, b, slot, n_pages):
    def body(p, c):
        for pages_hbm, buf, sem in streams:
            _page_copy(pages_hbm, buf, sem, pt_ref, b, p, slot).start()
        return c
    lax.fori_loop(0, n_pages, body, 0)


def _wait_batch_pages(streams, pt_ref, b, slot, n_pages):
    for pages_hbm, buf, sem in streams:
        for p in range(n_pages):
            _page_copy(pages_hbm, buf, sem, pt_ref, b, p, slot).wait()


def _sample_index_kernel(pt_ref, qi_ref, w_ref, pages_hbm, out_ref, buf, sem, *, n_pages, unroll):
    b = pl.program_id(0)
    slot = b % 2
    streams = [(pages_hbm, buf, sem)]

    @pl.when(b == 0)
    def _first():
        _start_batch_pages(streams, pt_ref, 0, 0, n_pages)

    _wait_batch_pages(streams, pt_ref, b, slot, n_pages)

    qi = qi_ref[0]
    w = w_ref[0]
    n_q = qi.shape[0] // N_IDX_HEADS

    def run(prefetch):
        def body(c, carry):
            for u in range(unroll):
                p = c * unroll + u
                if prefetch:
                    _page_copy(pages_hbm, buf, sem, pt_ref, b + 1, p, 1 - slot).start()
                s = _dot(qi, buf[slot, p].astype(BF16))
                r = jnp.maximum(s, 0.0) * w
                out_ref[0, c, :, u * PAGE_SIZE:(u + 1) * PAGE_SIZE] = jnp.sum(
                    r.reshape(n_q, N_IDX_HEADS, PAGE_SIZE), axis=1)
            return carry
        lax.fori_loop(0, n_pages // unroll, body, 0)

    has_next = b + 1 < pl.num_programs(0)
    pl.when(has_next)(lambda: run(True))
    pl.when(jnp.logical_not(has_next))(lambda: run(False))


def _sample_index(page_table, qi_qh, w_qh, kidx_t, unroll):
    batch, n_pages = page_table.shape
    n_q = qi_qh.shape[1] // N_IDX_HEADS
    groups = n_pages // unroll
    grid_spec = pltpu.PrefetchScalarGridSpec(
        num_scalar_prefetch=1,
        grid=(batch,),
        in_specs=[pl.BlockSpec((1,) + qi_qh.shape[1:], lambda b, pt: (b, 0, 0)),
                  pl.BlockSpec((1,) + w_qh.shape[1:], lambda b, pt: (b, 0, 0)),
                  pl.BlockSpec(memory_space=pl.ANY)],
        out_specs=pl.BlockSpec((1, groups, n_q, unroll * PAGE_SIZE), lambda b, pt: (b, 0, 0, 0)),
        scratch_shapes=[pltpu.VMEM((2, n_pages) + kidx_t.shape[1:], kidx_t.dtype),
                        pltpu.SemaphoreType.DMA((2,))],
    )
    return pl.pallas_call(
        functools.partial(_sample_index_kernel, n_pages=n_pages, unroll=unroll),
        grid_spec=grid_spec,
        out_shape=jax.ShapeDtypeStruct((batch, groups, n_q, unroll * PAGE_SIZE), F32),
        compiler_params=pltpu.CompilerParams(dimension_semantics=("arbitrary",), vmem_limit_bytes=VMEM_LIMIT),
        name="sample_index",
    )(page_table, qi_qh, w_qh, kidx_t)


def _sample_select_kernel(sp_ref, qi_ref, w_ref, kw16_ref, thr_ref, cut_ref, nmn_ref, snew_scr,
                          *, topk, n_q, lane_chunk):
    rows, past = sp_ref.shape
    r_i = lax.broadcasted_iota(I32, (rows, LANES), 0)
    l_i = lax.broadcasted_iota(I32, (rows, LANES), 1)

    s = _nt_dot(qi_ref[...], kw16_ref[...][:, :D_IDX])
    r = jnp.maximum(s, 0.0) * w_ref[...]
    s_new = jnp.sum(r.reshape(rows, N_IDX_HEADS, LANES), axis=1)
    ok = ((l_i // n_q) == (r_i // n_q)) & ((l_i % n_q) <= (r_i % n_q))
    snew_scr[...] = jnp.where(ok, s_new, -jnp.inf)

    n_chunks = past // lane_chunk
    tiles = lane_chunk // LANES
    kf = float(topk)

    def count(ind):
        acc = ind(snew_scr[...], l_i + past)
        for ch in range(n_chunks):
            x = sp_ref[:, ch * lane_chunk:(ch + 1) * lane_chunk]
            for t in range(tiles):
                idx = l_i + (ch * lane_chunk + t * LANES)
                acc = acc + ind(x[:, t * LANES:(t + 1) * LANES], idx)
        return jnp.broadcast_to(jnp.sum(acc, axis=1, keepdims=True), (rows, LANES))

    c0 = count(lambda x, idx: jnp.where(x >= 0.0, 1.0, 0.0))
    key0 = jnp.where(c0 >= kf, 0, INT_MIN).astype(I32)

    def bit_body(it, key):
        cand = key | jnp.left_shift(jnp.int32(1), 30 - it)
        cf = _key_to_float(cand)
        return jnp.where(count(lambda x, idx: jnp.where(x >= cf, 1.0, 0.0)) >= kf, cand, key)

    key = lax.fori_loop(0, 31, bit_body, key0)
    key = jnp.maximum(key, KEY_NEG_FLT_MAX)
    thr = _key_to_float(key)
    cnt_gt = count(lambda x, idx: jnp.where(x > thr, 1.0, 0.0))
    cnt_ge = count(lambda x, idx: jnp.where(x >= thr, 1.0, 0.0))
    need = kf - cnt_gt
    straddle = jnp.max(jnp.where(cnt_ge > kf, 1.0, 0.0))

    idx_bits = (past + LANES).bit_length()
    thr_ref[...] = thr
    cut_ref[...] = jnp.full((rows, LANES), 1 << idx_bits, I32)

    @pl.when(straddle > 0.0)
    def _tie_cut():
        def cut_body(it, cut):
            cand = cut | jnp.left_shift(jnp.int32(1), (idx_bits - 1) - it)
            cnt = count(lambda x, idx: jnp.where(x == thr, jnp.where(idx < cand, 1.0, 0.0), 0.0))
            return jnp.where(cnt <= need, cand, cut)
        cut_ref[...] = lax.fori_loop(0, idx_bits, cut_body, jnp.zeros((rows, LANES), I32))

    cut = cut_ref[...]
    sn = snew_scr[...]
    tie_keep = jnp.where(l_i + past < cut, 0.0, -jnp.inf)
    nmn_ref[...] = jnp.where(sn > thr, 0.0, jnp.where(sn == thr, tie_keep, -jnp.inf))


def _sample_select(scores_past, qi_rows, w_rows, kw16_s, topk, n_q):
    rows, past = scores_past.shape
    lane_chunk = 2048 if past % 2048 == 0 else LANES
    full = lambda a: pl.BlockSpec(a.shape, lambda i: (0,) * a.ndim)
    o = jax.ShapeDtypeStruct((rows, LANES), F32)
    return pl.pallas_call(
        functools.partial(_sample_select_kernel, topk=topk, n_q=n_q, lane_chunk=lane_chunk),
        grid=(1,),
        in_specs=[full(scores_past), full(qi_rows), full(w_rows), full(kw16_s)],
        out_specs=[pl.BlockSpec((rows, LANES), lambda i: (0, 0))] * 3,
        out_shape=[o, jax.ShapeDtypeStruct((rows, LANES), I32), o],
        scratch_shapes=[pltpu.VMEM((rows, LANES), F32)],
        compiler_params=pltpu.CompilerParams(dimension_semantics=("arbitrary",), vmem_limit_bytes=VMEM_LIMIT),
        name="sample_select",
    )(scores_past, qi_rows, w_rows, kw16_s)


def _sample_attn_kernel(pt_ref, rbc_ref, q_ref, sp_ref, thr_ref, cut_ref, nmn_ref, k16n_ref, v16n_ref,
                        k_hbm, v_hbm, o_ref, kbuf, vbuf, ksem, vsem, lgs, bias_scr, mx_scr,
                        *, n_pages, n_q, unroll):
    b = pl.program_id(0)
    slot = b % 2
    rows = n_q * N_HEADS_A
    groups = n_pages // unroll
    streams = [(k_hbm, kbuf, ksem), (v_hbm, vbuf, vsem)]
    lane = lax.broadcasted_iota(I32, (rows, LANES), 1)
    rq = lax.broadcasted_iota(I32, (rows, LANES), 0) // N_HEADS_A
    expand = lambda a: jnp.concatenate(
        [jnp.broadcast_to(a[t:t + 1], (N_HEADS_A, LANES)) for t in range(n_q)], axis=0)

    @pl.when(b == 0)
    def _first():
        _start_batch_pages(streams, pt_ref, 0, 0, n_pages)
        far = jnp.concatenate([rbc_ref[N_BUCKETS - 1]] * n_q, axis=0)
        for t, dist in enumerate((PAGE_SIZE + rq - lane, rq - lane % n_q)):
            bkt = _t5_bucket(dist)
            tile = jnp.zeros((rows, LANES), F32)
            for k in range(N_BUCKETS - 1):
                tile = jnp.where(bkt == k, jnp.concatenate([rbc_ref[k]] * n_q, axis=0) - far, tile)
            bias_scr[t] = tile * LOG2E

    _wait_batch_pages(streams, pt_ref, b, slot, n_pages)

    thr = expand(thr_ref[0])
    cut = expand(cut_ref[0])
    q = q_ref[0]

    def logits_pass(prefetch):
        def body(c, mx):
            for u in range(unroll):
                p = c * unroll + u
                if prefetch:
                    for pages_hbm, buf, sem in streams:
                        _page_copy(pages_hbm, buf, sem, pt_ref, b + 1, p, 1 - slot).start()
                lg = _dot(q, kbuf[slot, p].astype(BF16))
                sc = expand(sp_ref[0, c, :, u * PAGE_SIZE:(u + 1) * PAGE_SIZE])
                tie_keep = jnp.where(lane + p * PAGE_SIZE < cut, 0.0, -jnp.inf)
                x = lg + jnp.where(sc > thr, 0.0, jnp.where(sc == thr, tie_keep, -jnp.inf))
                if u == unroll - 1:
                    x = x + bias_scr[0] * jnp.where(c == groups - 1, 1.0, 0.0)
                lgs[p] = x
                mx = jnp.maximum(mx, x)
            return mx
        mx_scr[...] = lax.fori_loop(0, groups, body, jnp.full((rows, LANES), NEG_BIG, F32))

    has_next = b + 1 < pl.num_programs(0)
    pl.when(has_next)(lambda: logits_pass(True))
    pl.when(jnp.logical_not(has_next))(lambda: logits_pass(False))

    xn = _nt_dot(q, k16n_ref[...]) + bias_scr[1] + expand(nmn_ref[0])
    m = jnp.max(jnp.maximum(mx_scr[...], xn), axis=1, keepdims=True)

    def pv_body(c, carry):
        lsum, acc = carry
        for u in range(unroll):
            p = c * unroll + u
            pr = jnp.exp2(lgs[p] - m)
            lsum = lsum + pr
            acc = acc + _nt_dot(pr.astype(BF16), vbuf[slot, p].astype(BF16))
        return lsum, acc

    pn = jnp.exp2(xn - m)
    lsum, acc = lax.fori_loop(0, groups, pv_body, (pn, _dot(pn.astype(BF16), v16n_ref[...])))
    out = acc * (1.0 / jnp.sum(lsum, axis=1, keepdims=True))
    head = lax.broadcasted_iota(I32, (rows, HEAD_DIM), 0) % N_HEADS_A
    o_ref[0] = jnp.where(head < N_HEADS_A // N_KV_A, out[:, :HEAD_DIM], out[:, HEAD_DIM:])


def _sample_attn(page_table, rb_col, q_bd, scores4, thr3, cut3, nmn3, k16n, v16n, k_pages_t, v_pages_t, unroll):
    batch, n_pages = page_table.shape
    n_q = scores4.shape[2]
    rows = q_bd.shape[1]
    per_b = lambda a: pl.BlockSpec((1,) + a.shape[1:], lambda b, pt: (b,) + (0,) * (a.ndim - 1))
    full = lambda a: pl.BlockSpec(a.shape, lambda b, pt: (0,) * a.ndim)
    page_buf = pltpu.VMEM((2, n_pages) + k_pages_t.shape[1:], k_pages_t.dtype)
    grid_spec = pltpu.PrefetchScalarGridSpec(
        num_scalar_prefetch=1,
        grid=(batch,),
        in_specs=[full(rb_col), per_b(q_bd), per_b(scores4), per_b(thr3), per_b(cut3), per_b(nmn3),
                  full(k16n), full(v16n), pl.BlockSpec(memory_space=pl.ANY), pl.BlockSpec(memory_space=pl.ANY)],
        out_specs=pl.BlockSpec((1, rows, HEAD_DIM), lambda b, pt: (b, 0, 0)),
        scratch_shapes=[page_buf, page_buf, pltpu.SemaphoreType.DMA((2,)), pltpu.SemaphoreType.DMA((2,)),
                        pltpu.VMEM((n_pages, rows, LANES), F32), pltpu.VMEM((2, rows, LANES), F32),
                        pltpu.VMEM((rows, LANES), F32)],
    )
    return pl.pallas_call(
        functools.partial(_sample_attn_kernel, n_pages=n_pages, n_q=n_q, unroll=unroll),
        grid_spec=grid_spec,
        out_shape=jax.ShapeDtypeStruct((batch, rows, HEAD_DIM), F32),
        compiler_params=pltpu.CompilerParams(dimension_semantics=("arbitrary",),
                                             vmem_limit_bytes=VMEM_LIMIT_PAGED),
        name="sample_attn",
    )(page_table, rb_col, q_bd, scores4, thr3, cut3, nmn3, k16n, v16n, k_pages_t, v_pages_t)


def _rope_tables(pos):
    half = HEAD_DIM // 2
    inv = ROPE_BASE ** (-jnp.arange(half, dtype=F32) / half)
    ang = pos.astype(F32)[:, None] * inv[None, :]
    cos = jnp.cos(ang)
    sin = jnp.sin(ang)
    reps = LANES // HEAD_DIM
    cos_t = jnp.tile(jnp.concatenate([cos, cos], axis=1), (1, reps))
    sin_t = jnp.tile(jnp.concatenate([-sin, sin], axis=1), (1, reps))
    return cos_t, sin_t


def _cat_weight(w_in):
    split = int(np.sum(SPLIT_SIZES[:6]))
    pad = jnp.zeros((w_in.shape[0], LANES - D_IDX - N_IDX_HEADS), BF16)
    return jnp.concatenate([w_in[:, :split].astype(BF16), pad, w_in[:, split:].astype(BF16)], axis=1)


def _pages_t(cache):
    pool, page = cache.shape[:2]
    return jnp.transpose(cache, (0, 2, 3, 1)).reshape(pool, -1, page)


def _largest_divisor(n, cap):
    d = cap
    while n % d:
        d //= 2
    return d


def kernel(x_prompt, x_sample, cache_k, cache_v, cache_kidx, state_ret, page_table, p_prompt, p_sample,
           rel_bias, w_in, w_out, g_pre, g_post, w_ple_up, w_ple_gate):
    batch, seq, d_model = x_prompt.shape
    dec_b, dec_t, _ = x_sample.shape
    depth = w_in.shape[0]
    n_pages = page_table.shape[1]
    past = n_pages * PAGE_SIZE
    rows_s = dec_b * dec_t
    assert depth == 1 and rows_s == LANES and seq % BLK == 0

    w_cat = _cat_weight(w_in[0])
    gpre = g_pre[0].reshape(1, d_model)
    gpost = g_post[0].reshape(1, d_model)
    wo16 = w_out[0].astype(BF16)
    wup16 = w_ple_up[0].astype(BF16)
    wg16 = w_ple_gate[0].astype(BF16)

    tm = _largest_divisor(seq, 1024)
    cos_p, sin_p = _rope_tables(jnp.arange(seq))
    xp2 = x_prompt.reshape(batch * seq, d_model)
    pp = _project(xp2, gpre, w_cat, cos_p, sin_p, tm, seq)
    oa_p = _dsa_prompt(pp, rel_bias, batch, seq)
    r3 = lambda a: a.reshape(batch, seq, D_B)
    s0_p = jnp.zeros((batch, N_HEADS_B, HEAD_DIM, HEAD_DIM), F32)
    yn_p, state_p = _retention(r3(pp["qb"]), r3(pp["kb"]), r3(pp["vb"]), s0_p, float(min(RET_CHUNK, seq)))
    y_prompt = _merge(xp2, oa_p, yn_p.reshape(batch * seq, D_B), pp["ga"], pp["gb"],
                      p_prompt[0].reshape(batch * seq, -1), wo16, gpost, wup16, wg16, tm)

    pos_s = past + jnp.arange(dec_t)
    cos_s, sin_s = _rope_tables(jnp.tile(pos_s, dec_b))
    xs2 = x_sample.reshape(rows_s, d_model)
    ps = _project(xs2, gpre, w_cat, cos_s, sin_s, rows_s, rows_s)
    topk_s = min(TOPK_MAX, (past + dec_t) // 4)

    qi_rows = ps["qi"].reshape(rows_s * N_IDX_HEADS, D_IDX)
    w_rows = jnp.broadcast_to(ps["kw"][:, D_IDX:D_IDX + N_IDX_HEADS].reshape(rows_s * N_IDX_HEADS, 1),
                              (rows_s * N_IDX_HEADS, LANES))
    unroll = _largest_divisor(n_pages, 32)
    scores4 = _sample_index(page_table, qi_rows.reshape(dec_b, dec_t * N_IDX_HEADS, D_IDX),
                            w_rows.reshape(dec_b, dec_t * N_IDX_HEADS, LANES),
                            jnp.transpose(cache_kidx[0], (0, 2, 1)), unroll)
    scores_past = scores4.transpose(0, 2, 1, 3).reshape(rows_s, past)
    thr, cut, nmn = _sample_select(scores_past, qi_rows, w_rows, ps["kw16"], topk_s, dec_t)

    kv_of_head = np.arange(N_HEADS_A) // (N_HEADS_A // N_KV_A)
    place = jnp.asarray((kv_of_head[:, None] == np.arange(N_KV_A)[None, :]).astype(np.float32)).astype(BF16)
    q_bd = (ps["qa"].reshape(dec_b, dec_t, N_HEADS_A, 1, HEAD_DIM) * place[None, None, :, :, None]).reshape(
        dec_b, dec_t * N_HEADS_A, N_KV_A * HEAD_DIM)
    rb_col = jnp.broadcast_to(rel_bias[:, :, None], rel_bias.shape + (LANES,))
    b3 = lambda a: a.reshape(dec_b, dec_t, LANES)
    o_qh = _sample_attn(page_table, rb_col, q_bd, scores4, b3(thr), b3(cut), b3(nmn),
                        ps["k16"], ps["v16"], _pages_t(cache_k[0]), _pages_t(cache_v[0]), unroll)
    oa_s = o_qh.reshape(rows_s, D_A)

    chunk_s = RET_CHUNK
    padt = lambda a: jnp.pad(a.reshape(dec_b, dec_t, D_B), ((0, 0), (0, chunk_s - dec_t), (0, 0)))
    yn_s, state_s = _retention(padt(ps["qb"]), padt(ps["kb"]), padt(ps["vb"]),
                               state_ret[0].astype(F32), float(math.gcd(dec_t, RET_CHUNK)))
    y_sample = _merge(xs2, oa_s, yn_s[:, :dec_t].reshape(rows_s, D_B), ps["ga"], ps["gb"],
                      p_sample[0].reshape(rows_s, -1), wo16, gpost, wup16, wg16, rows_s)

    def kv(a_t, b, t):
        a = a_t.reshape(a_t.shape[0], N_KV_A, HEAD_DIM, -1).transpose(0, 3, 1, 2)
        return a.reshape(1, b, t, N_KV_A, HEAD_DIM)

    def ki(a_t, b, t):
        return a_t.transpose(0, 2, 1).reshape(1, b, t, D_IDX)

    return (
        y_prompt.reshape(batch, seq, d_model),
        y_sample.reshape(dec_b, dec_t, d_model),
        kv(pp["ka_t"], batch, seq), kv(pp["va_t"], batch, seq),
        ki(pp["ki_t"], batch, seq),
        state_p[None].astype(state_ret.dtype),
        kv(ps["ka_t"], dec_b, dec_t), kv(ps["va_t"], dec_b, dec_t),
        ki(ps["ki_t"], dec_b, dec_t),
        state_s[None].astype(state_ret.dtype),
    )
```
